```python
import numpy as np
import jax, jax.numpy as jnp
from jax import lax

D_MODEL = 2048
BATCH = 4
SEQ = 2048
DEPTH = 1

MEM_LEN = 256
NSA_HEADS = 16
NSA_GROUPS = 4
NSA_REP = NSA_HEADS // NSA_GROUPS
NSA_DK = 128
NSA_DV = 128
CMP_LEN = 32
CMP_STRIDE = 16
CMP_HIDDEN = 1024
SEL_LEN = 64
SEL_TOPK = 16
WIN = 512
WIN_QB = 128
SEL_QB = 32
RET_HEADS = 8
RET_DK = 128
RET_DV = 256
RET_CHUNK = 128
ROPE_BASE = 10000.0
X_HEADS = 4
X_DH = 128
D_FF = 4 * D_MODEL
EPS = 1e-6
NEG = -1e30
NSA_Q = NSA_HEADS * NSA_DK
NSA_KW = NSA_GROUPS * NSA_DK
NSA_VW = NSA_GROUPS * NSA_DV
NSA_GATE = NSA_HEADS * 3
RET_QK = RET_HEADS * RET_DK
RET_V = RET_HEADS * RET_DV
IN_SIZES = (NSA_Q, NSA_KW, NSA_VW, NSA_KW, NSA_VW, NSA_KW, NSA_VW, NSA_GATE,
            RET_QK, RET_QK, RET_V, RET_V, D_MODEL, D_MODEL)
IN_WIDTH = sum(IN_SIZES)

kernel_name = "hybrid_nsa_retention_gated_block"


def rmsnorm(x, w):
    xf = x.astype(jnp.float32)
    y = xf * lax.rsqrt(jnp.mean(xf * xf, axis=-1, keepdims=True) + EPS)
    return (y * w.astype(jnp.float32)).astype(x.dtype)


def masked_softmax(s, mask):
    p = jax.nn.softmax(jnp.where(mask, s, NEG), axis=-1)
    return p * mask


def split_in(z):
    offs = np.cumsum(np.array(IN_SIZES))[:-1]
    return jnp.split(z, offs, axis=-1)


def nsa_compress(k, pe, w1, w2):
    B, T, G, d = k.shape
    n_cmp = (T - CMP_LEN) // CMP_STRIDE + 1
    idx = np.arange(n_cmp)[:, None] * CMP_STRIDE + np.arange(CMP_LEN)[None, :]
    blk = k[:, idx] + pe[None, None, :, None, :]
    blk = blk.transpose(0, 1, 3, 2, 4).reshape(B, n_cmp, G, CMP_LEN * d)
    return jax.nn.silu(blk @ w1) @ w2


def cmp_attention(q, kc, vc):
    T, n_cmp = q.shape[1], kc.shape[1]
    s = jnp.einsum('btgrd,bcgd->bgrtc', q, kc).astype(jnp.float32) * (NSA_DK ** -0.5)
    t = jnp.arange(T)
    end = jnp.arange(n_cmp) * CMP_STRIDE + CMP_LEN - 1
    mask = end[None, :] <= t[:, None]
    p = masked_softmax(s, mask)
    o = jnp.einsum('bgrtc,bcgd->btgrd', p.astype(vc.dtype), vc)
    return o, p


def select_blocks(p_cmp, T):
    n_cmp = p_cmp.shape[-1]
    n_sel = T // SEL_LEN
    cs = np.arange(n_cmp) * CMP_STRIDE
    js = np.arange(n_sel) * SEL_LEN
    ov = ((cs[:, None] < js[None, :] + SEL_LEN) & (cs[:, None] + CMP_LEN > js[None, :])).astype(np.float32)
    imp = jnp.einsum('bgrtc,cj->bgtj', p_cmp, jnp.asarray(ov))
    t = jnp.arange(T)
    cur = t // SEL_LEN
    j = jnp.arange(n_sel)
    forced = (j[None, :] == 0) | (j[None, :] == cur[:, None]) | (j[None, :] == cur[:, None] - 1)
    future = j[None, :] > cur[:, None]
    imp = jnp.where(forced, jnp.inf, jnp.where(future, -jnp.inf, imp))
    _, idx = lax.top_k(imp, min(SEL_TOPK, n_sel))
    return idx


def sel_attention(q, ks, vs, idx):
    B, T, G, R, d = q.shape
    dv = vs.shape[-1]
    n_sel = T // SEL_LEN
    n = idx.shape[-1]
    kb = ks.reshape(B, n_sel, SEL_LEN, G, d).transpose(0, 3, 1, 2, 4).reshape(B * G, n_sel, SEL_LEN, d)
    vb = vs.reshape(B, n_sel, SEL_LEN, G, dv).transpose(0, 3, 1, 2, 4).reshape(B * G, n_sel, SEL_LEN, dv)
    nq = T // SEL_QB
    qx = q.reshape(B, nq, SEL_QB, G, R, d).transpose(1, 0, 2, 3, 4, 5)
    ix = idx.reshape(B * G, nq, SEL_QB, n).transpose(1, 0, 2, 3)
    gather = jax.vmap(lambda tab, ii: tab[ii])

    def one(args):
        qb, ib, start = args
        kg = gather(kb, ib).reshape(B, G, SEL_QB, n * SEL_LEN, d)
        vg = gather(vb, ib).reshape(B, G, SEL_QB, n * SEL_LEN, dv)
        s = jnp.einsum('bqgrd,bgqkd->bgrqk', qb, kg).astype(jnp.float32) * (NSA_DK ** -0.5)
        tq = start + jnp.arange(SEL_QB)
        tk = (ib.reshape(B, G, SEL_QB, n)[..., None] * SEL_LEN + jnp.arange(SEL_LEN)).reshape(B, G, SEL_QB, n * SEL_LEN)
        mask = (tk <= tq[None, None, :, None])[:, :, None]
        p = masked_softmax(s, mask)
        return jnp.einsum('bgrqk,bgqkd->bqgrd', p.astype(vg.dtype), vg)

    o = lax.map(one, (qx, ix, jnp.arange(nq) * SEL_QB))
    return o.transpose(1, 0, 2, 3, 4, 5).reshape(B, T, G, R, dv)


def win_attention(q, kw, vw):
    B, T, G, R, d = q.shape
    nb = T // WIN_QB
    P = WIN // WIN_QB

    def band(x):
        xp = jnp.pad(x, ((0, 0), (WIN, 0), (0, 0), (0, 0))).reshape(B, nb + P, WIN_QB, G, x.shape[-1])
        return jnp.concatenate([xp[:, j:j + nb] for j in range(P + 1)], axis=2)

    kb, vb = band(kw), band(vw)
    qb = q.reshape(B, nb, WIN_QB, G, R, d)
    s = jnp.einsum('bnqgrd,bnkgd->bgrnqk', qb, kb).astype(jnp.float32) * (NSA_DK ** -0.5)
    blk = jnp.arange(nb)[:, None] * WIN_QB
    tq = blk + jnp.arange(WIN_QB)[None, :]
    tk = blk - WIN + jnp.arange((P + 1) * WIN_QB)[None, :]
    diff = tq[:, :, None] - tk[:, None, :]
    mask = (diff >= 0) & (diff < WIN) & (tk[:, None, :] >= 0)
    p = masked_softmax(s, mask)
    o = jnp.einsum('bgrnqk,bnkgd->bnqgrd', p.astype(vb.dtype), vb)
    return o.reshape(B, T, G, R, vw.shape[-1])


def rotary(x):
    T, d = x.shape[1], x.shape[-1]
    inv = ROPE_BASE ** (-jnp.arange(0, d, 2, dtype=jnp.float32) / d)
    ang = jnp.arange(T, dtype=jnp.float32)[:, None] * inv[None, :]
    cos, sin = jnp.cos(ang)[None, :, None, :], jnp.sin(ang)[None, :, None, :]
    xf = x.astype(jnp.float32)
    x1, x2 = xf[..., 0::2], xf[..., 1::2]
    return jnp.stack([x1 * cos - x2 * sin, x1 * sin + x2 * cos], axis=-1).reshape(xf.shape)


def retention(q, k, v, gn_w):
    B, T, H, dk = q.shape
    dv = v.shape[-1]
    C = RET_CHUNK
    nc = T // C
    qf = rotary(q)
    kf = rotary(k) * (dk ** -0.5)
    log_g = jnp.log1p(-jnp.exp2(-5.0 - jnp.arange(H, dtype=jnp.float32)))
    i = jnp.arange(C, dtype=jnp.float32)
    rel = i[:, None] - i[None, :]
    decay = jnp.where(rel >= 0, jnp.exp(log_g[:, None, None] * jnp.maximum(rel, 0.0)), 0.0)
    qc = qf.reshape(B, nc, C, H, dk)
    kc = kf.reshape(B, nc, C, H, dk)
    vc = v.astype(jnp.float32).reshape(B, nc, C, H, dv)
    s = jnp.einsum('bnchd,bnmhd->bhncm', qc, kc) * decay[:, None]
    o_intra = jnp.einsum('bhncm,bnmhe->bnche', s, vc)
    w_k = jnp.exp(log_g[:, None] * (C - 1 - i)[None, :])
    kv = jnp.einsum('bnmhd,bnmhe,hm->nbhde', kc, vc, w_k)
    g_chunk = jnp.exp(log_g * C)

    def step(state, kv_n):
        return state * g_chunk[None, :, None, None] + kv_n, state

    _, prev = lax.scan(step, jnp.zeros((B, H, dk, dv), jnp.float32), kv)
    w_q = jnp.exp(log_g[:, None] * (i + 1.0)[None, :])
    o_inter = jnp.einsum('bnchd,nbhde->bnche', qc, prev) * w_q.T[None, None, :, :, None]
    o = (o_intra + o_inter).reshape(B, T, H, dv)
    mu = jnp.mean(o, axis=-1, keepdims=True)
    var = jnp.mean(jnp.square(o - mu), axis=-1, keepdims=True)
    o = ((o - mu) * lax.rsqrt(var + EPS)).reshape(B, T, H * dv) * gn_w.astype(jnp.float32)
    return o.astype(v.dtype)


def hybrid_layer(h, mem, attn_norm_w, w_in, cmp_pe_k, cmp_w1_k, cmp_w2_k, cmp_pe_v, cmp_w1_v, cmp_w2_v,
                 w_a, ret_gn_w, w_b, w_out, x_norm_w, mem_norm_w, wq_x, wk_x, wv_x, wo_x,
                 mlp_norm_w, w_up, w_down):
    B, T, _ = h.shape
    G, R = NSA_GROUPS, NSA_REP
    n = rmsnorm(h, attn_norm_w)
    (q_n, k_c, v_c, k_s, v_s, k_w, v_w, g_nsa,
     q_r, k_r, v_r, g_r, gate_a, gate_b) = split_in(n @ w_in)
    q_n = q_n.reshape(B, T, G, R, NSA_DK)
    kv = lambda a, d: a.reshape(B, T, G, d)
    kc = nsa_compress(kv(k_c, NSA_DK), cmp_pe_k, cmp_w1_k, cmp_w2_k)
    vc = nsa_compress(kv(v_c, NSA_DV), cmp_pe_v, cmp_w1_v, cmp_w2_v)
    o_cmp, p_cmp = cmp_attention(q_n, kc, vc)
    idx = select_blocks(p_cmp, T)
    o_sel = sel_attention(q_n, kv(k_s, NSA_DK), kv(v_s, NSA_DV), idx)
    o_win = win_attention(q_n, kv(k_w, NSA_DK), kv(v_w, NSA_DV))
    g3 = jax.nn.sigmoid(g_nsa).reshape(B, T, G, R, 3)
    o_nsa = (g3[..., 0:1] * o_cmp + g3[..., 1:2] * o_sel + g3[..., 2:3] * o_win).reshape(B, T, NSA_HEADS * NSA_DV)
    o_ret = retention(q_r.reshape(B, T, RET_HEADS, RET_DK), k_r.reshape(B, T, RET_HEADS, RET_DK),
                      v_r.reshape(B, T, RET_HEADS, RET_DV), ret_gn_w)
    o_ret = jax.nn.silu(g_r) * o_ret
    merged = jax.nn.sigmoid(gate_a) * (o_nsa @ w_a) + jax.nn.sigmoid(gate_b) * (o_ret @ w_b)
    h = h + merged @ w_out
    nx = rmsnorm(h, x_norm_w)
    m = rmsnorm(mem, mem_norm_w)
    qx = (nx @ wq_x).reshape(B, T, X_HEADS, X_DH)
    kx = (m @ wk_x).reshape(B, m.shape[1], X_HEADS, X_DH)
    vx = (m @ wv_x).reshape(B, m.shape[1], X_HEADS, X_DH)
    sx = jnp.einsum('bthd,bmhd->bhtm', qx, kx).astype(jnp.float32) * (X_DH ** -0.5)
    px = jax.nn.softmax(sx, axis=-1).astype(vx.dtype)
    ox = jnp.einsum('bhtm,bmhd->bthd', px, vx).reshape(B, T, X_HEADS * X_DH)
    h = h + ox @ wo_x
    nm = rmsnorm(h, mlp_norm_w)
    h = h + jnp.square(jax.nn.relu(nm @ w_up)) @ w_down
    return h


def setup_inputs(seed: int = 0) -> dict:
    key = jax.random.key(seed)
    ks = iter(jax.random.split(key, 32))
    L = DEPTH
    f32 = jnp.float32

    def w(shape, fan_in):
        return jax.random.normal(next(ks), shape, f32) * (fan_in ** -0.5)

    def gain(shape):
        return 1.0 + 0.1 * jax.random.normal(next(ks), shape, f32)

    return {
        "x": jax.random.normal(next(ks), (BATCH, SEQ, D_MODEL), f32),
        "mem": jax.random.normal(next(ks), (BATCH, MEM_LEN, D_MODEL), f32),
        "attn_norm_w": gain((L, D_MODEL)),
        "w_in": w((L, D_MODEL, IN_WIDTH), D_MODEL),
        "cmp_pe_k": 0.02 * jax.random.normal(next(ks), (L, CMP_LEN, NSA_DK), f32),
        "cmp_w1_k": w((L, CMP_LEN * NSA_DK, CMP_HIDDEN), CMP_LEN * NSA_DK),
        "cmp_w2_k": w((L, CMP_HIDDEN, NSA_DK), CMP_HIDDEN),
        "cmp_pe_v": 0.02 * jax.random.normal(next(ks), (L, CMP_LEN, NSA_DV), f32),
        "cmp_w1_v": w((L, CMP_LEN * NSA_DV, CMP_HIDDEN), CMP_LEN * NSA_DV),
        "cmp_w2_v": w((L, CMP_HIDDEN, NSA_DV), CMP_HIDDEN),
        "w_a": w((L, NSA_HEADS * NSA_DV, D_MODEL), NSA_HEADS * NSA_DV),
        "ret_gn_w": gain((L, RET_V)),
        "w_b": w((L, RET_V, D_MODEL), RET_V),
        "w_out": w((L, D_MODEL, D_MODEL), D_MODEL),
        "x_norm_w": gain((L, D_MODEL)),
        "mem_norm_w": gain((L, D_MODEL)),
        "wq_x": w((L, D_MODEL, X_HEADS * X_DH), D_MODEL),
        "wk_x": w((L, D_MODEL, X_HEADS * X_DH), D_MODEL),
        "wv_x": w((L, D_MODEL, X_HEADS * X_DH), D_MODEL),
        "wo_x": w((L, X_HEADS * X_DH, D_MODEL), X_HEADS * X_DH),
        "mlp_norm_w": gain((L, D_MODEL)),
        "w_up": w((L, D_MODEL, D_FF), D_MODEL),
        "w_down": w((L, D_FF, D_MODEL), D_FF),
        "final_norm_w": gain((D_MODEL,)),
    }


def reference(x, mem, attn_norm_w, w_in, cmp_pe_k, cmp_w1_k, cmp_w2_k, cmp_pe_v, cmp_w1_v, cmp_w2_v,
              w_a, ret_gn_w, w_b, w_out, x_norm_w, mem_norm_w, wq_x, wk_x, wv_x, wo_x,
              mlp_norm_w, w_up, w_down, final_norm_w):
    h = x
    for l in range(DEPTH):
        h = hybrid_layer(h, mem, attn_norm_w[l], w_in[l], cmp_pe_k[l], cmp_w1_k[l], cmp_w2_k[l],
                         cmp_pe_v[l], cmp_w1_v[l], cmp_w2_v[l], w_a[l], ret_gn_w[l], w_b[l], w_out[l],
                         x_norm_w[l], mem_norm_w[l], wq_x[l], wk_x[l], wv_x[l], wo_x[l],
                         mlp_norm_w[l], w_up[l], w_down[l])
    return rmsnorm(h, final_norm_w)
```

```python
import functools

import jax
import jax.numpy as jnp
import numpy as np
from jax import lax
from jax.experimental import pallas as pl
from jax.experimental.pallas import tpu as pltpu

F32 = jnp.float32
BF16 = jnp.bfloat16

D_MODEL = 2048
MEM_LEN = 256
NSA_HEADS = 16
NSA_GROUPS = 4
NSA_REP = NSA_HEADS // NSA_GROUPS
NSA_DK = 128
NSA_DV = 128
CMP_LEN = 32
CMP_STRIDE = 16
CMP_HIDDEN = 1024
SEL_LEN = 64
SEL_TOPK = 16
WIN = 512
RET_HEADS = 8
RET_DK = 128
RET_DV = 256
RET_CHUNK = 128
ROPE_BASE = 10000.0
X_HEADS = 4
X_DH = 128
D_FF = 4 * D_MODEL
EPS = 1e-6
NEG = -1e30

LANES = 128

A_QN = 0
A_VR = 2048
A_KC = 4096
A_VC = 4608
A_KS = 5120
A_VS = 5632
A_KW = 6144
A_VW = 6656
A_WIDTH = 7168
B_QR = 0
B_KR = 1024
B_GR = 2048
B_GA = 4096
B_GB = 6144
B_GN = 8192
B_WIDTH = 8192 + NSA_GROUPS * LANES

NT_DIMS = (((1,), (1,)), ((), ()))


def _params(sem, vmem_mb=None):
    kw = {"dimension_semantics": sem}
    if vmem_mb is not None:
        kw["vmem_limit_bytes"] = vmem_mb * 1024 * 1024
    return pltpu.CompilerParams(**kw)


def _rms(x, w):
    return x * lax.rsqrt(jnp.mean(x * x, axis=-1, keepdims=True) + EPS) * w


def _dot(a, b):
    return jnp.dot(a, b, preferred_element_type=F32)


def _dot_nt(a, b):
    return lax.dot_general(a, b, NT_DIMS, preferred_element_type=F32)


def _rmsnorm_kernel(x_ref, w_ref, o_ref):
    o_ref[...] = _rms(x_ref[...].astype(F32), w_ref[...]).astype(o_ref.dtype)


def _rmsnorm(x, w, out_dtype, tm=512):
    m, d = x.shape
    return pl.pallas_call(
        _rmsnorm_kernel,
        grid=(m // tm,),
        in_specs=[pl.BlockSpec((tm, d), lambda i: (i, 0)), pl.BlockSpec((1, d), lambda i: (0, 0))],
        out_specs=pl.BlockSpec((tm, d), lambda i: (i, 0)),
        out_shape=jax.ShapeDtypeStruct((m, d), out_dtype),
        compiler_params=_params(("parallel",)),
        name="rmsnorm",
    )(x, w.reshape(1, d))


def _mm_kernel(a_ref, b_ref, o_ref):
    o_ref[...] = _dot(a_ref[...], b_ref[...]).astype(o_ref.dtype)


def _matmul(a, b, out_dtype, tm=512, tn=512, name="matmul"):
    m, k = a.shape
    n = b.shape[1]
    return pl.pallas_call(
        _mm_kernel,
        grid=(m // tm, n // tn),
        in_specs=[pl.BlockSpec((tm, k), lambda i, j: (i, 0)), pl.BlockSpec((k, tn), lambda i, j: (0, j))],
        out_specs=pl.BlockSpec((tm, tn), lambda i, j: (i, j)),
        out_shape=jax.ShapeDtypeStruct((m, n), out_dtype),
        compiler_params=_params(("parallel", "arbitrary"), 40),
        name=name,
    )(a, b)


def _compress_kernel(k2_ref, pe_ref, w1_ref, w2_ref, o_ref):
    rows, half = k2_ref.shape
    k2 = k2_ref[...].astype(F32)
    a_lo = (k2 + pe_ref[:, :half]).astype(BF16)
    a_hi = (k2 + pe_ref[:, half:]).astype(BF16)
    lo = _dot(a_lo, w1_ref[:half, :])
    hi = _dot(a_hi, w1_ref[half:, :])
    h = lo + pltpu.roll(hi, rows - 1, axis=0)
    act = h * jax.nn.sigmoid(h)
    o_ref[...] = _dot(act.astype(BF16), w2_ref[...]).astype(o_ref.dtype)


def _compress(k2, pe, w1, w2, rows=256):
    m, half = k2.shape
    hid = w1.shape[1]
    dout = w2.shape[1]
    return pl.pallas_call(
        _compress_kernel,
        grid=(m // rows,),
        in_specs=[
            pl.BlockSpec((rows, half), lambda i: (i, 0)),
            pl.BlockSpec((1, 2 * half), lambda i: (0, 0)),
            pl.BlockSpec((2 * half, hid), lambda i: (0, 0)),
            pl.BlockSpec((hid, dout), lambda i: (0, 0)),
        ],
        out_specs=pl.BlockSpec((rows, dout), lambda i: (i, 0)),
        out_shape=jax.ShapeDtypeStruct((m, dout), BF16),
        compiler_params=_params(("parallel",), 40),
        name="nsa_compress",
    )(k2, pe, w1, w2)


def _nsa_kernel(q_ref, kc_ref, vc_ref, ks_ref, vs_ref, kw_ref, vw_ref, g_ref, o_ref,
                m_sc, l_sc, acc_sc, *, seq, tq, kblk):
    i = pl.program_id(2)
    t0 = i * tq
    rep = NSA_REP
    scale = NSA_DK ** -0.5
    n_cmp = (seq - CMP_LEN) // CMP_STRIDE + 1
    n_sel = seq // SEL_LEN
    topk = min(SEL_TOPK, n_sel)

    q = q_ref[...]
    q4 = jnp.concatenate([q[:, r * NSA_DK:(r + 1) * NSA_DK] for r in range(rep)], axis=0)
    tcol = t0 + lax.broadcasted_iota(jnp.int32, (tq, 1), 0)

    def masked(s, mask1):
        return jnp.concatenate(
            [jnp.where(mask1, s[r * tq:(r + 1) * tq], NEG) for r in range(rep)], axis=0)

    c_idx = lax.broadcasted_iota(jnp.int32, (tq, LANES), 1)
    mask_c = ((c_idx * CMP_STRIDE + (CMP_LEN - 1)) <= tcol) & (c_idx < n_cmp)
    sm = masked(_dot_nt(q4, kc_ref[...]) * scale, mask_c)
    e = jnp.exp(sm - jnp.max(sm, axis=-1, keepdims=True))
    p = e / jnp.sum(e, axis=-1, keepdims=True)
    mask_cf = jnp.where(mask_c, 1.0, 0.0)
    p = jnp.concatenate([p[r * tq:(r + 1) * tq] * mask_cf for r in range(rep)], axis=0)
    o_cmp = _dot(p.astype(BF16), vc_ref[...])
    psum = p[0:tq]
    for r in range(1, rep):
        psum = psum + p[r * tq:(r + 1) * tq]

    ci = lax.broadcasted_iota(jnp.int32, (LANES, LANES), 0)
    ji = lax.broadcasted_iota(jnp.int32, (LANES, LANES), 1)
    ov = jnp.where((ci * CMP_STRIDE < ji * SEL_LEN + SEL_LEN) & (ci * CMP_STRIDE + CMP_LEN > ji * SEL_LEN)
                   & (ci < n_cmp) & (ji < n_sel), 1.0, 0.0).astype(BF16)
    p_hi = psum.astype(BF16)
    p_lo = (psum - p_hi.astype(F32)).astype(BF16)
    imp = _dot(p_hi, ov) + _dot(p_lo, ov)
    j_idx = c_idx
    cur = lax.shift_right_logical(tcol, int(np.log2(SEL_LEN)))
    forced = (j_idx == 0) | (j_idx == cur) | (j_idx == cur - 1)
    impm = jnp.where(forced, jnp.inf, jnp.where(j_idx > cur, -jnp.inf, imp))
    rank = jnp.zeros((tq, LANES), F32)
    for ii in range(n_sel):
        col = impm[:, ii:ii + 1]
        beats = (col > impm) | ((col == impm) & (j_idx > ii))
        rank = rank + jnp.where(beats, 1.0, 0.0)
    sel = jnp.where((rank < topk) & (j_idx < n_sel), 1.0, 0.0).astype(BF16)

    m_sc[...] = jnp.full(m_sc.shape, NEG, F32)
    l_sc[...] = jnp.zeros(l_sc.shape, F32)
    acc_sc[...] = jnp.zeros(acc_sc.shape, F32)
    shift = int(np.log2(SEL_LEN))

    def sel_body(kb, carry):
        k0 = pl.multiple_of(kb * kblk, kblk)
        k = ks_ref[pl.ds(k0, kblk), :]
        v = vs_ref[pl.ds(k0, kblk), :]
        s = _dot_nt(q4, k) * scale
        jrow = lax.broadcasted_iota(jnp.int32, (LANES, kblk), 0)
        tkk = k0 + lax.broadcasted_iota(jnp.int32, (LANES, kblk), 1)
        expand = jnp.where(jrow == lax.shift_right_logical(tkk, shift), 1.0, 0.0).astype(BF16)
        in_sel = _dot(sel, expand)
        tk = k0 + lax.broadcasted_iota(jnp.int32, (tq, kblk), 1)
        sm_ = masked(s, (in_sel > 0.5) & (tk <= tcol))
        m_old = m_sc[...]
        m_new = jnp.maximum(m_old, jnp.max(sm_, axis=-1, keepdims=True))
        alpha = jnp.exp(m_old - m_new)
        pk = jnp.exp(sm_ - m_new)
        l_sc[...] = alpha * l_sc[...] + jnp.sum(pk, axis=-1, keepdims=True)
        acc_sc[...] = alpha * acc_sc[...] + _dot(pk.astype(BF16), v)
        m_sc[...] = m_new
        return carry

    n_chunks = lax.div(t0 + tq + (kblk - 1), kblk)
    lax.fori_loop(0, n_chunks, sel_body, 0)
    o_sel = acc_sc[...] / l_sc[...]

    wlen = WIN + tq
    start = pl.multiple_of(jnp.maximum(i - WIN // tq, 0) * tq, tq)
    kw = kw_ref[pl.ds(start, wlen), :]
    vw = vw_ref[pl.ds(start, wlen), :]
    dlt = tcol - (start + lax.broadcasted_iota(jnp.int32, (tq, wlen), 1))
    sm = masked(_dot_nt(q4, kw) * scale, (dlt >= 0) & (dlt < WIN))
    e = jnp.exp(sm - jnp.max(sm, axis=-1, keepdims=True))
    pw = e / jnp.sum(e, axis=-1, keepdims=True)
    o_win = _dot(pw.astype(BF16), vw)

    gs = jax.nn.sigmoid(g_ref[...])
    for r in range(rep):
        rows = slice(r * tq, (r + 1) * tq)
        o = (gs[:, 3 * r:3 * r + 1] * o_cmp[rows] + gs[:, 3 * r + 1:3 * r + 2] * o_sel[rows]
             + gs[:, 3 * r + 2:3 * r + 3] * o_win[rows])
        o_ref[:, r * NSA_DV:(r + 1) * NSA_DV] = o.astype(o_ref.dtype)


def _nsa_attention(za, zb, kc, vc, batch, seq, tq=128, kblk=256):
    assert seq % kblk == 0 and seq >= WIN + tq and WIN % tq == 0
    nq = seq // tq
    gw = NSA_REP * NSA_DK
    kern = functools.partial(_nsa_kernel, seq=seq, tq=tq, kblk=kblk)

    def kv_spec(off):
        return pl.BlockSpec((seq, LANES), lambda b, g, i: (b, off // LANES + g))

    return pl.pallas_call(
        kern,
        grid=(batch, NSA_GROUPS, nq),
        in_specs=[
            pl.BlockSpec((tq, gw), lambda b, g, i: (b * nq + i, A_QN // gw + g)),
            pl.BlockSpec((LANES, NSA_DK), lambda b, g, i: (b * NSA_GROUPS + g, 0)),
            pl.BlockSpec((LANES, NSA_DV), lambda b, g, i: (b * NSA_GROUPS + g, 0)),
            kv_spec(A_KS), kv_spec(A_VS), kv_spec(A_KW), kv_spec(A_VW),
            pl.BlockSpec((tq, LANES), lambda b, g, i: (b * nq + i, B_GN // LANES + g)),
        ],
        out_specs=pl.BlockSpec((tq, gw), lambda b, g, i: (b * nq + i, g)),
        out_shape=jax.ShapeDtypeStruct((batch * seq, NSA_HEADS * NSA_DV), BF16),
        scratch_shapes=[
            pltpu.VMEM((NSA_REP * tq, 1), F32),
            pltpu.VMEM((NSA_REP * tq, 1), F32),
            pltpu.VMEM((NSA_REP * tq, NSA_DV), F32),
        ],
        compiler_params=_params(("parallel", "parallel", "arbitrary"), 40),
        name="nsa_attention",
    )(za, kc, vc, za, za, za, za, zb)


def _retention_kernel(q_ref, k_ref, v_ref, g_ref, cos_ref, sin_ref, dec_ref, wq_ref, wk_ref, gc_ref,
                      gn_ref, o_ref, st_ref):
    @pl.when(pl.program_id(1) == 0)
    def _():
        st_ref[...] = jnp.zeros(st_ref.shape, F32)

    cos = cos_ref[...]
    sin = sin_ref[...]
    for h in range(RET_HEADS):
        qh = q_ref[:, h * RET_DK:(h + 1) * RET_DK]
        kh = k_ref[:, h * RET_DK:(h + 1) * RET_DK]
        qf = qh * cos + pltpu.roll(qh, RET_DK // 2, axis=1) * sin
        kf = (kh * cos + pltpu.roll(kh, RET_DK // 2, axis=1) * sin) * (RET_DK ** -0.5)
        qb = qf.astype(BF16)
        cols = slice(h * RET_DV, (h + 1) * RET_DV)
        v = v_ref[:, cols]
        s = _dot_nt(qb, kf.astype(BF16)) * dec_ref[h]
        st = st_ref[h]
        o = _dot(s.astype(BF16), v) + _dot(qb, st.astype(BF16)) * wq_ref[h]
        kv = _dot((kf * wk_ref[h]).T.astype(BF16), v)
        st_ref[h] = st * gc_ref[h] + kv
        mu = jnp.mean(o, axis=-1, keepdims=True)
        d = o - mu
        var = jnp.mean(d * d, axis=-1, keepdims=True)
        on = d * lax.rsqrt(var + EPS) * gn_ref[:, cols]
        gr = g_ref[:, cols]
        o_ref[:, cols] = (gr * jax.nn.sigmoid(gr) * on).astype(o_ref.dtype)


def _retention(za, zb, gn_w, batch, seq):
    c = RET_CHUNK
    nc = seq // c
    hq = RET_HEADS * RET_DK
    hv = RET_HEADS * RET_DV
    inv = ROPE_BASE ** (-jnp.arange(0, RET_DK, 2, dtype=F32) / RET_DK)
    ang = jnp.arange(seq, dtype=F32)[:, None] * inv[None, :]
    cos2 = jnp.concatenate([jnp.cos(ang), jnp.cos(ang)], axis=-1)
    sin2 = jnp.concatenate([-jnp.sin(ang), jnp.sin(ang)], axis=-1)
    log_g = jnp.log1p(-jnp.exp2(-5.0 - jnp.arange(RET_HEADS, dtype=F32)))
    idx = jnp.arange(c, dtype=F32)
    rel = idx[:, None] - idx[None, :]
    decay = jnp.where(rel >= 0, jnp.exp(log_g[:, None, None] * jnp.maximum(rel, 0.0)), 0.0)
    w_k = jnp.exp(log_g[:, None] * (c - 1 - idx)[None, :])[:, :, None]
    w_q = jnp.exp(log_g[:, None] * (idx + 1.0)[None, :])[:, :, None]
    g_chunk = jnp.broadcast_to(jnp.exp(log_g * c)[:, None, None], (RET_HEADS, 1, RET_DV))

    row = lambda b, n: b * nc + n
    return pl.pallas_call(
        _retention_kernel,
        grid=(batch, nc),
        in_specs=[
            pl.BlockSpec((c, hq), lambda b, n: (row(b, n), B_QR // hq)),
            pl.BlockSpec((c, hq), lambda b, n: (row(b, n), B_KR // hq)),
            pl.BlockSpec((c, hv), lambda b, n: (row(b, n), A_VR // hv)),
            pl.BlockSpec((c, hv), lambda b, n: (row(b, n), B_GR // hv)),
            pl.BlockSpec((c, RET_DK), lambda b, n: (n, 0)),
            pl.BlockSpec((c, RET_DK), lambda b, n: (n, 0)),
            pl.BlockSpec((RET_HEADS, c, c), lambda b, n: (0, 0, 0)),
            pl.BlockSpec((RET_HEADS, c, 1), lambda b, n: (0, 0, 0)),
            pl.BlockSpec((RET_HEADS, c, 1), lambda b, n: (0, 0, 0)),
            pl.BlockSpec((RET_HEADS, 1, RET_DV), lambda b, n: (0, 0, 0)),
            pl.BlockSpec((1, hv), lambda b, n: (0, 0)),
        ],
        out_specs=pl.BlockSpec((c, hv), lambda b, n: (row(b, n), 0)),
        out_shape=jax.ShapeDtypeStruct((batch * seq, hv), BF16),
        scratch_shapes=[pltpu.VMEM((RET_HEADS, RET_DK, RET_DV), F32)],
        compiler_params=_params(("parallel", "arbitrary"), 40),
        name="retention",
    )(zb, zb, za, zb, cos2, sin2, decay, w_q, w_k, g_chunk, gn_w.reshape(1, hv))


def _merge_kernel(on_ref, or_ref, ga_ref, gb_ref, x_ref, wa_ref, wb_ref, wo_ref, h_ref):
    a = _dot(on_ref[...], wa_ref[...])
    b = _dot(or_ref[...], wb_ref[...])
    merged = jax.nn.sigmoid(ga_ref[...]) * a + jax.nn.sigmoid(gb_ref[...]) * b
    h_ref[...] = x_ref[...] + _dot(merged.astype(BF16), wo_ref[...])


def _merge(o_nsa, o_ret, zb, x, w_a, w_b, w_out, tm=256):
    m, d = x.shape
    resident = lambda shape: pl.BlockSpec(shape, lambda i: (0, 0), pipeline_mode=pl.Buffered(1))
    return pl.pallas_call(
        _merge_kernel,
        grid=(m // tm,),
        in_specs=[
            pl.BlockSpec((tm, d), lambda i: (i, 0)),
            pl.BlockSpec((tm, d), lambda i: (i, 0)),
            pl.BlockSpec((tm, d), lambda i: (i, B_GA // d)),
            pl.BlockSpec((tm, d), lambda i: (i, B_GB // d)),
            pl.BlockSpec((tm, d), lambda i: (i, 0)),
            resident(w_a.shape), resident(w_b.shape), resident(w_out.shape),
        ],
        out_specs=pl.BlockSpec((tm, d), lambda i: (i, 0)),
        out_shape=jax.ShapeDtypeStruct((m, d), F32),
        compiler_params=_params(("parallel",), 56),
        name="merge_out_proj",
    )(o_nsa, o_ret, zb, zb, x, w_a, w_b, w_out)


def _mem_kv_kernel(m_ref, nw_ref, wk_ref, wv_ref, k_ref, v_ref):
    mn = _rms(m_ref[...], nw_ref[...]).astype(BF16)
    k_ref[...] = _dot(mn, wk_ref[...]).astype(k_ref.dtype)
    v_ref[...] = _dot(mn, wv_ref[...]).astype(v_ref.dtype)


def _mem_kv(mem2, nw, wk, wv, tm=256):
    m, d = mem2.shape
    n = wk.shape[1]
    out = jax.ShapeDtypeStruct((m, n), BF16)
    return pl.pallas_call(
        _mem_kv_kernel,
        grid=(m // tm,),
        in_specs=[
            pl.BlockSpec((tm, d), lambda i: (i, 0)),
            pl.BlockSpec((1, d), lambda i: (0, 0)),
            pl.BlockSpec((d, n), lambda i: (0, 0)),
            pl.BlockSpec((d, n), lambda i: (0, 0)),
        ],
        out_specs=[pl.BlockSpec((tm, n), lambda i: (i, 0))] * 2,
        out_shape=[out, out],
        compiler_params=_params(("parallel",), 40),
        name="mem_kv_proj",
    )(mem2, nw.reshape(1, d), wk, wv)


def _cross_kernel(h_ref, xw_ref, mw_ref, wq_ref, kx_ref, vx_ref, wo_ref, h2_ref, nm_ref):
    h = h_ref[...]
    nx = _rms(h, xw_ref[...]).astype(BF16)
    qx = _dot(nx, wq_ref[...]).astype(BF16)
    outs = []
    for hh in range(X_HEADS):
        cols = slice(hh * X_DH, (hh + 1) * X_DH)
        s = _dot_nt(qx[:, cols], kx_ref[:, cols]) * (X_DH ** -0.5)
        e = jnp.exp(s - jnp.max(s, axis=-1, keepdims=True))
        p = e / jnp.sum(e, axis=-1, keepdims=True)
        outs.append(_dot(p.astype(BF16), vx_ref[:, cols]))
    ox = jnp.concatenate(outs, axis=-1).astype(BF16)
    h2 = h + _dot(ox, wo_ref[...])
    h2_ref[...] = h2
    nm_ref[...] = _rms(h2, mw_ref[...]).astype(nm_ref.dtype)


def _cross_attention(h1, x_norm_w, mlp_norm_w, wq, kx, vx, wo, seq, tm=256):
    m, d = h1.shape
    n = wq.shape[1]
    per_batch = seq // tm
    vec = lambda: pl.BlockSpec((1, d), lambda i: (0, 0))
    return pl.pallas_call(
        _cross_kernel,
        grid=(m // tm,),
        in_specs=[
            pl.BlockSpec((tm, d), lambda i: (i, 0)),
            vec(), vec(),
            pl.BlockSpec((d, n), lambda i: (0, 0)),
            pl.BlockSpec((MEM_LEN, n), lambda i: (i // per_batch, 0)),
            pl.BlockSpec((MEM_LEN, n), lambda i: (i // per_batch, 0)),
            pl.BlockSpec((n, d), lambda i: (0, 0)),
        ],
        out_specs=[pl.BlockSpec((tm, d), lambda i: (i, 0))] * 2,
        out_shape=[jax.ShapeDtypeStruct((m, d), F32), jax.ShapeDtypeStruct((m, d), BF16)],
        compiler_params=_params(("parallel",), 40),
        name="cross_attention",
    )(h1, x_norm_w.reshape(1, d), mlp_norm_w.reshape(1, d), wq, kx, vx, wo)


def _mlp_kernel(nm_ref, wu_ref, wd_ref, h_ref, fw_ref, o_ref, acc_ref):
    j = pl.program_id(1)

    @pl.when(j == 0)
    def _():
        acc_ref[...] = jnp.zeros(acc_ref.shape, F32)

    u = jnp.maximum(_dot(nm_ref[...], wu_ref[...]), 0.0)
    acc_ref[...] += _dot((u * u).astype(BF16), wd_ref[...])

    @pl.when(j == pl.num_programs(1) - 1)
    def _():
        o_ref[...] = _rms(h_ref[...] + acc_ref[...], fw_ref[...])


def _mlp(nm, w_up, w_down, h2, final_w, tm=512, tf=512):
    m, d = nm.shape
    f = w_up.shape[1]
    return pl.pallas_call(
        _mlp_kernel,
        grid=(m // tm, f // tf),
        in_specs=[
            pl.BlockSpec((tm, d), lambda i, j: (i, 0)),
            pl.BlockSpec((d, tf), lambda i, j: (0, j)),
            pl.BlockSpec((tf, d), lambda i, j: (j, 0)),
            pl.BlockSpec((tm, d), lambda i, j: (i, 0)),
            pl.BlockSpec((1, d), lambda i, j: (0, 0)),
        ],
        out_specs=pl.BlockSpec((tm, d), lambda i, j: (i, 0)),
        out_shape=jax.ShapeDtypeStruct((m, d), F32),
        scratch_shapes=[pltpu.VMEM((tm, d), F32)],
        compiler_params=_params(("parallel", "arbitrary"), 48),
        name="mlp_final_norm",
    )(nm, w_up, w_down, h2, final_w.reshape(1, d))


def _split_w_in(w):
    d = w.shape[0]
    sizes = (2048, 512, 512, 512, 512, 512, 512, 48, 1024, 1024, 2048, 2048, 2048, 2048)
    offs = np.concatenate([[0], np.cumsum(sizes)])
    seg = lambda n: w[:, offs[n]:offs[n + 1]]
    (q_n, k_c, v_c, k_s, v_s, k_w, v_w, g_nsa, q_r, k_r, v_r, g_r, gate_a, gate_b) = [seg(n) for n in range(14)]

    def deinterleave(a):
        return a.reshape(d, RET_HEADS, RET_DK // 2, 2).transpose(0, 1, 3, 2).reshape(d, RET_HEADS * RET_DK)

    per_group = NSA_REP * 3
    gn = jnp.pad(g_nsa.reshape(d, NSA_GROUPS, per_group), ((0, 0), (0, 0), (0, LANES - per_group)))
    wa = jnp.concatenate([q_n, v_r, k_c, v_c, k_s, v_s, k_w, v_w], axis=1)
    wb = jnp.concatenate([deinterleave(q_r), deinterleave(k_r), g_r, gate_a, gate_b,
                          gn.reshape(d, NSA_GROUPS * LANES)], axis=1)
    return wa.astype(BF16), wb.astype(BF16)


def _layer(h, mem, attn_norm_w, w_in, cmp_pe_k, cmp_w1_k, cmp_w2_k, cmp_pe_v, cmp_w1_v, cmp_w2_v,
           w_a, ret_gn_w, w_b, w_out, x_norm_w, mem_norm_w, wq_x, wk_x, wv_x, wo_x,
           mlp_norm_w, w_up, w_down, out_norm_w, batch, seq):
    bf = lambda a: a.astype(BF16)
    w_in_a, w_in_b = _split_w_in(w_in)

    n = _rmsnorm(h, attn_norm_w, BF16)
    za = _matmul(n, w_in_a, BF16, name="in_proj_a")
    zb = _matmul(n, w_in_b, F32, name="in_proj_b")

    def blocks16(off):
        a = za[:, off:off + NSA_GROUPS * NSA_DK].reshape(batch, seq // CMP_STRIDE, CMP_STRIDE, NSA_GROUPS, NSA_DK)
        return a.transpose(0, 3, 1, 2, 4).reshape(batch * NSA_GROUPS * (seq // CMP_STRIDE), CMP_STRIDE * NSA_DK)

    assert seq // CMP_STRIDE == LANES
    kc = _compress(blocks16(A_KC), cmp_pe_k.reshape(1, -1), bf(cmp_w1_k), bf(cmp_w2_k))
    vc = _compress(blocks16(A_VC), cmp_pe_v.reshape(1, -1), bf(cmp_w1_v), bf(cmp_w2_v))

    o_nsa = _nsa_attention(za, zb, kc, vc, batch, seq)
    o_ret = _retention(za, zb, ret_gn_w, batch, seq)
    h1 = _merge(o_nsa, o_ret, zb, h, bf(w_a), bf(w_b), bf(w_out))

    kx, vx = _mem_kv(mem.reshape(batch * MEM_LEN, D_MODEL), mem_norm_w, bf(wk_x), bf(wv_x))
    h2, nm = _cross_attention(h1, x_norm_w, mlp_norm_w, bf(wq_x), kx, vx, bf(wo_x), seq)
    return _mlp(nm, bf(w_up), bf(w_down), h2, out_norm_w)


def kernel(x, mem, attn_norm_w, w_in, cmp_pe_k, cmp_w1_k, cmp_w2_k, cmp_pe_v, cmp_w1_v, cmp_w2_v, w_a, ret_gn_w,
           w_b, w_out, x_norm_w, mem_norm_w, wq_x, wk_x, wv_x, wo_x, mlp_norm_w, w_up, w_down, final_norm_w):
    batch, seq, d = x.shape
    depth = w_in.shape[0]
    assert depth == 1
    h = x.reshape(batch * seq, d)
    out = _layer(h, mem, attn_norm_w[0], w_in[0], cmp_pe_k[0], cmp_w1_k[0], cmp_w2_k[0],
                 cmp_pe_v[0], cmp_w1_v[0], cmp_w2_v[0], w_a[0], ret_gn_w[0], w_b[0], w_out[0],
                 x_norm_w[0], mem_norm_w[0], wq_x[0], wk_x[0], wv_x[0], wo_x[0],
                 mlp_norm_w[0], w_up[0], w_down[0], final_norm_w, batch, seq)
    return out.reshape(batch, seq, d)
```

```python
import functools

import jax
import jax.numpy as jnp
import numpy as np
from jax import lax
from jax.experimental import pallas as pl
from jax.experimental.pallas import tpu as pltpu

F32 = jnp.float32
BF16 = jnp.bfloat16

D_MODEL = 2048
MEM_LEN = 256
NSA_HEADS = 16
NSA_GROUPS = 4
NSA_REP = NSA_HEADS // NSA_GROUPS
NSA_DK = 128
NSA_DV = 128
CMP_LEN = 32
CMP_STRIDE = 16
CMP_HIDDEN = 1024
SEL_LEN = 64
SEL_TOPK = 16
WIN = 512
RET_HEADS = 8
RET_DK = 128
RET_DV = 256
RET_CHUNK = 128
ROPE_BASE = 10000.0
X_HEADS = 4
X_DH = 128
D_FF = 4 * D_MODEL
EPS = 1e-6
NEG = -1e30

LANES = 128

W_IN_NSA_GATE = 5120
A_QN = 0
A_KC = 2048
A_VC = 2560
A_KS = 3072
A_VS = 3584
A_KW = 4096
A_VW = 4608
A_WIDTH = 5120
B_QR = 0
B_KR = 1024
B_VR = 2048
B_GR = 4096
B_GA = 6144
B_GB = 8192
B_GN = 10240
B_WIDTH = B_GN + NSA_GROUPS * LANES

NT_DIMS = (((1,), (1,)), ((), ()))


def _params(sem, vmem_mb=None):
    kw = {"dimension_semantics": sem}
    if vmem_mb is not None:
        kw["vmem_limit_bytes"] = vmem_mb * 1024 * 1024
    return pltpu.CompilerParams(**kw)


def _rms(x, w):
    return x * lax.rsqrt(jnp.mean(x * x, axis=-1, keepdims=True) + EPS) * w


def _dot(a, b):
    return jnp.dot(a, b, preferred_element_type=F32)


def _dot_nt(a, b):
    return lax.dot_general(a, b, NT_DIMS, preferred_element_type=F32)


def _rmsnorm_kernel(x_ref, w_ref, o_ref):
    o_ref[...] = _rms(x_ref[...].astype(F32), w_ref[...]).astype(o_ref.dtype)


def _rmsnorm(x, w, out_dtype, tm=512):
    m, d = x.shape
    return pl.pallas_call(
        _rmsnorm_kernel,
        grid=(m // tm,),
        in_specs=[pl.BlockSpec((tm, d), lambda i: (i, 0)), pl.BlockSpec((1, d), lambda i: (0, 0))],
        out_specs=pl.BlockSpec((tm, d), lambda i: (i, 0)),
        out_shape=jax.ShapeDtypeStruct((m, d), out_dtype),
        compiler_params=_params(("parallel",)),
        name="rmsnorm",
    )(x, w.reshape(1, d))


def _mm_kernel(a_ref, b_ref, o_ref):
    o_ref[...] = _dot(a_ref[...], b_ref[...]).astype(o_ref.dtype)


def _matmul(a, b, out_dtype, tm=512, tn=512, name="matmul"):
    m, k = a.shape
    n = b.shape[1]
    return pl.pallas_call(
        _mm_kernel,
        grid=(m // tm, n // tn),
        in_specs=[pl.BlockSpec((tm, k), lambda i, j: (i, 0)), pl.BlockSpec((k, tn), lambda i, j: (0, j))],
        out_specs=pl.BlockSpec((tm, tn), lambda i, j: (i, j)),
        out_shape=jax.ShapeDtypeStruct((m, n), out_dtype),
        compiler_params=_params(("parallel", "arbitrary"), 40),
        name=name,
    )(a, b)


def _compress_kernel(k2_ref, pe_ref, w1_ref, w2_ref, o_ref):
    rows, half = k2_ref.shape
    k2 = k2_ref[...].astype(F32)
    a_lo = (k2 + pe_ref[:, :half]).astype(BF16)
    a_hi = (k2 + pe_ref[:, half:]).astype(BF16)
    lo = _dot(a_lo, w1_ref[:half, :])
    hi = _dot(a_hi, w1_ref[half:, :])
    h = lo + pltpu.roll(hi, rows - 1, axis=0)
    act = h * jax.nn.sigmoid(h)
    o_ref[...] = _dot(act.astype(BF16), w2_ref[...]).astype(o_ref.dtype)


def _compress(k2, pe, w1, w2, rows=256):
    m, half = k2.shape
    hid = w1.shape[1]
    dout = w2.shape[1]
    return pl.pallas_call(
        _compress_kernel,
        grid=(m // rows,),
        in_specs=[
            pl.BlockSpec((rows, half), lambda i: (i, 0)),
            pl.BlockSpec((1, 2 * half), lambda i: (0, 0)),
            pl.BlockSpec((2 * half, hid), lambda i: (0, 0)),
            pl.BlockSpec((hid, dout), lambda i: (0, 0)),
        ],
        out_specs=pl.BlockSpec((rows, dout), lambda i: (i, 0)),
        out_shape=jax.ShapeDtypeStruct((m, dout), BF16),
        compiler_params=_params(("parallel",), 40),
        name="nsa_compress",
    )(k2, pe, w1, w2)


def _nsa_kernel(q_ref, kc_ref, vc_ref, ks_ref, vs_ref, kw_ref, vw_ref, g_ref, e_ref, o_ref,
                s_sc, mx_sc, l_sc, acc_sc, *, seq, tq, kblk):
    i = pl.program_id(2)
    t0 = i * tq
    rep = NSA_REP
    scale = NSA_DK ** -0.5
    n_cmp = (seq - CMP_LEN) // CMP_STRIDE + 1
    n_sel = seq // SEL_LEN
    topk = min(SEL_TOPK, n_sel)

    q = q_ref[...]
    q4 = jnp.concatenate([q[:, r * NSA_DK:(r + 1) * NSA_DK] for r in range(rep)], axis=0)
    tcol = t0 + lax.broadcasted_iota(jnp.int32, (tq, 1), 0)

    def masked(s, mask1):
        return jnp.concatenate(
            [jnp.where(mask1, s[r * tq:(r + 1) * tq], NEG) for r in range(rep)], axis=0)

    c_idx = lax.broadcasted_iota(jnp.int32, (tq, LANES), 1)
    mask_c = ((c_idx * CMP_STRIDE + (CMP_LEN - 1)) <= tcol) & (c_idx < n_cmp)
    sm = masked(_dot_nt(q4, kc_ref[...]) * scale, mask_c)
    e = jnp.exp(sm - jnp.max(sm, axis=-1, keepdims=True))
    p = e / jnp.sum(e, axis=-1, keepdims=True)
    mask_cf = jnp.where(mask_c, 1.0, 0.0)
    p = jnp.concatenate([p[r * tq:(r + 1) * tq] * mask_cf for r in range(rep)], axis=0)
    o_cmp = _dot(p.astype(BF16), vc_ref[...])
    psum = p[0:tq]
    for r in range(1, rep):
        psum = psum + p[r * tq:(r + 1) * tq]

    jo = lax.broadcasted_iota(jnp.int32, (n_sel, LANES), 0)
    co = lax.broadcasted_iota(jnp.int32, (n_sel, LANES), 1)
    ov_t = jnp.where((co * CMP_STRIDE < jo * SEL_LEN + SEL_LEN) & (co * CMP_STRIDE + CMP_LEN > jo * SEL_LEN)
                     & (co < n_cmp), 1.0, 0.0).astype(BF16)
    p_hi = psum.astype(BF16)
    p_lo = (psum - p_hi.astype(F32)).astype(BF16)
    imp = _dot_nt(ov_t, p_hi) + _dot_nt(ov_t, p_lo)
    j_idx = lax.broadcasted_iota(jnp.int32, (n_sel, tq), 0)
    cur = lax.shift_right_logical(t0 + lax.broadcasted_iota(jnp.int32, (n_sel, tq), 1), int(np.log2(SEL_LEN)))
    forced = (j_idx == 0) | (j_idx == cur) | (j_idx == cur - 1)
    impm = jnp.where(forced, jnp.inf, jnp.where(j_idx > cur, -jnp.inf, imp))
    rank = jnp.zeros((n_sel, tq), F32)
    for ii in range(n_sel):
        row = impm[ii:ii + 1, :]
        beats = (row > impm) | ((row == impm) & (j_idx > ii))
        rank = rank + jnp.where(beats, 1.0, 0.0)
    sel_t = jnp.where(rank < topk, 1.0, 0.0)
    sel = jnp.concatenate([sel_t, jnp.zeros((LANES - n_sel, tq), F32)], axis=0).T.astype(BF16)

    n_chunks = lax.div(t0 + tq + (kblk - 1), kblk)
    lane_groups = kblk // LANES
    mx_sc[...] = jnp.full(mx_sc.shape, NEG, F32)

    def score_pass(kb, carry):
        k0 = pl.multiple_of(kb * kblk, kblk)
        s = _dot_nt(q4, ks_ref[pl.ds(k0, kblk), :]) * scale
        in_sel = _dot(sel, e_ref[kb])
        tk = k0 + lax.broadcasted_iota(jnp.int32, (tq, kblk), 1)
        sm_ = masked(s, (in_sel > 0.5) & (tk <= tcol))
        s_sc[kb] = sm_
        mx = mx_sc[...]
        for c in range(lane_groups):
            mx = jnp.maximum(mx, sm_[:, c * LANES:(c + 1) * LANES])
        mx_sc[...] = mx
        return carry

    lax.fori_loop(0, n_chunks, score_pass, 0)
    m_sel = jnp.max(mx_sc[...], axis=-1, keepdims=True)
    l_sc[...] = jnp.zeros(l_sc.shape, F32)
    acc_sc[...] = jnp.zeros(acc_sc.shape, F32)

    def value_pass(kb, carry):
        k0 = pl.multiple_of(kb * kblk, kblk)
        pk = jnp.exp(s_sc[kb] - m_sel)
        lsum = l_sc[...]
        for c in range(lane_groups):
            lsum = lsum + pk[:, c * LANES:(c + 1) * LANES]
        l_sc[...] = lsum
        acc_sc[...] += _dot(pk.astype(BF16), vs_ref[pl.ds(k0, kblk), :])
        return carry

    lax.fori_loop(0, n_chunks, value_pass, 0)
    o_sel = acc_sc[...] / jnp.sum(l_sc[...], axis=-1, keepdims=True)

    wlen = WIN + tq
    start = pl.multiple_of(jnp.maximum(i - WIN // tq, 0) * tq, tq)
    kw = kw_ref[pl.ds(start, wlen), :]
    vw = vw_ref[pl.ds(start, wlen), :]
    dlt = tcol - (start + lax.broadcasted_iota(jnp.int32, (tq, wlen), 1))
    sm = masked(_dot_nt(q4, kw) * scale, (dlt >= 0) & (dlt < WIN))
    e = jnp.exp(sm - jnp.max(sm, axis=-1, keepdims=True))
    pw = e / jnp.sum(e, axis=-1, keepdims=True)
    o_win = _dot(pw.astype(BF16), vw)

    gs = jax.nn.sigmoid(g_ref[...])
    for r in range(rep):
        rows = slice(r * tq, (r + 1) * tq)
        o = (gs[:, 3 * r:3 * r + 1] * o_cmp[rows] + gs[:, 3 * r + 1:3 * r + 2] * o_sel[rows]
             + gs[:, 3 * r + 2:3 * r + 3] * o_win[rows])
        o_ref[:, r * NSA_DV:(r + 1) * NSA_DV] = o.astype(o_ref.dtype)


def _nsa_attention(za, zb, kc, vc, batch, seq, tq=128, kblk=256):
    assert seq % kblk == 0 and seq >= WIN + tq and WIN % tq == 0
    nq = seq // tq
    gw = NSA_REP * NSA_DK
    kern = functools.partial(_nsa_kernel, seq=seq, tq=tq, kblk=kblk)
    key = np.arange(seq).reshape(seq // kblk, 1, kblk)
    expand = jnp.asarray(key // SEL_LEN == np.arange(LANES).reshape(1, LANES, 1), BF16)

    def kv_spec(off):
        return pl.BlockSpec((seq, LANES), lambda b, g, i: (b, off // LANES + g))

    return pl.pallas_call(
        kern,
        grid=(batch, NSA_GROUPS, nq),
        in_specs=[
            pl.BlockSpec((tq, gw), lambda b, g, i: (b * nq + i, A_QN // gw + g)),
            pl.BlockSpec((LANES, NSA_DK), lambda b, g, i: (b * NSA_GROUPS + g, 0)),
            pl.BlockSpec((LANES, NSA_DV), lambda b, g, i: (b * NSA_GROUPS + g, 0)),
            kv_spec(A_KS), kv_spec(A_VS), kv_spec(A_KW), kv_spec(A_VW),
            pl.BlockSpec((tq, LANES), lambda b, g, i: (b * nq + i, B_GN // LANES + g)),
            pl.BlockSpec(expand.shape, lambda b, g, i: (0, 0, 0)),
        ],
        out_specs=pl.BlockSpec((tq, gw), lambda b, g, i: (b * nq + i, g)),
        out_shape=jax.ShapeDtypeStruct((batch * seq, NSA_HEADS * NSA_DV), BF16),
        scratch_shapes=[
            pltpu.VMEM((seq // kblk, NSA_REP * tq, kblk), F32),
            pltpu.VMEM((NSA_REP * tq, LANES), F32),
            pltpu.VMEM((NSA_REP * tq, LANES), F32),
            pltpu.VMEM((NSA_REP * tq, NSA_DV), F32),
        ],
        compiler_params=_params(("parallel", "parallel", "arbitrary"), 40),
        name="nsa_attention",
    )(za, kc, vc, za, za, za, za, zb, expand)


def _retention_kernel(q_ref, k_ref, v_ref, g_ref, cos_ref, sin_e_ref, sin_o_ref, dec_ref, wq_ref, wk_ref,
                      gc_ref, gn_ref, o_ref, st_ref):
    @pl.when(pl.program_id(1) == 0)
    def _():
        st_ref[...] = jnp.zeros(st_ref.shape, F32)

    cos = cos_ref[...]
    sin_e = sin_e_ref[...]
    sin_o = sin_o_ref[...]

    def rotate(x):
        return x * cos + pltpu.roll(x, RET_DK - 1, axis=1) * sin_e + pltpu.roll(x, 1, axis=1) * sin_o

    for h in range(RET_HEADS):
        qf = rotate(q_ref[:, h * RET_DK:(h + 1) * RET_DK])
        kf = rotate(k_ref[:, h * RET_DK:(h + 1) * RET_DK]) * (RET_DK ** -0.5)
        qb = qf.astype(BF16)
        cols = slice(h * RET_DV, (h + 1) * RET_DV)
        v = v_ref[:, cols].astype(BF16)
        s = _dot_nt(qb, kf.astype(BF16)) * dec_ref[h]
        st = st_ref[h]
        o = _dot(s.astype(BF16), v) + _dot(qb, st.astype(BF16)) * wq_ref[h]
        kv = _dot((kf * wk_ref[h]).T.astype(BF16), v)
        st_ref[h] = st * gc_ref[h] + kv
        mu = jnp.mean(o, axis=-1, keepdims=True)
        d = o - mu
        var = jnp.mean(d * d, axis=-1, keepdims=True)
        on = d * lax.rsqrt(var + EPS) * gn_ref[:, cols]
        gr = g_ref[:, cols]
        o_ref[:, cols] = (gr * jax.nn.sigmoid(gr) * on).astype(o_ref.dtype)


def _retention(zb, gn_w, batch, seq):
    c = RET_CHUNK
    nc = seq // c
    hq = RET_HEADS * RET_DK
    hv = RET_HEADS * RET_DV
    f32 = np.float32
    inv = f32(ROPE_BASE) ** (-np.arange(0, RET_DK, 2, dtype=f32) / f32(RET_DK))
    ang = np.arange(seq, dtype=f32)[:, None] * inv[None, :]
    zero = np.zeros_like(ang)
    pairs = lambda even, odd: np.stack([even, odd], axis=-1).reshape(seq, RET_DK)
    cos = pairs(np.cos(ang), np.cos(ang))
    sin_e = pairs(-np.sin(ang), zero)
    sin_o = pairs(zero, np.sin(ang))
    log_g = np.log1p(-np.exp2(f32(-5.0) - np.arange(RET_HEADS, dtype=f32)))
    idx = np.arange(c, dtype=f32)
    rel = idx[:, None] - idx[None, :]
    decay = np.where(rel >= 0, np.exp(log_g[:, None, None] * np.maximum(rel, f32(0.0))), f32(0.0)).astype(f32)
    w_k = np.exp(log_g[:, None] * (f32(c - 1) - idx)[None, :])[:, :, None].astype(f32)
    w_q = np.exp(log_g[:, None] * (idx + f32(1.0))[None, :])[:, :, None].astype(f32)
    g_chunk = np.broadcast_to(np.exp(log_g * f32(c))[:, None, None], (RET_HEADS, 1, RET_DV)).astype(f32)

    row = lambda b, n: b * nc + n
    return pl.pallas_call(
        _retention_kernel,
        grid=(batch, nc),
        in_specs=[
            pl.BlockSpec((c, hq), lambda b, n: (row(b, n), B_QR // hq)),
            pl.BlockSpec((c, hq), lambda b, n: (row(b, n), B_KR // hq)),
            pl.BlockSpec((c, hv), lambda b, n: (row(b, n), B_VR // hv)),
            pl.BlockSpec((c, hv), lambda b, n: (row(b, n), B_GR // hv)),
            pl.BlockSpec((c, RET_DK), lambda b, n: (n, 0)),
            pl.BlockSpec((c, RET_DK), lambda b, n: (n, 0)),
            pl.BlockSpec((c, RET_DK), lambda b, n: (n, 0)),
            pl.BlockSpec((RET_HEADS, c, c), lambda b, n: (0, 0, 0)),
            pl.BlockSpec((RET_HEADS, c, 1), lambda b, n: (0, 0, 0)),
            pl.BlockSpec((RET_HEADS, c, 1), lambda b, n: (0, 0, 0)),
            pl.BlockSpec((RET_HEADS, 1, RET_DV), lambda b, n: (0, 0, 0)),
            pl.BlockSpec((1, hv), lambda b, n: (0, 0)),
        ],
        out_specs=pl.BlockSpec((c, hv), lambda b, n: (row(b, n), 0)),
        out_shape=jax.ShapeDtypeStruct((batch * seq, hv), BF16),
        scratch_shapes=[pltpu.VMEM((RET_HEADS, RET_DK, RET_DV), F32)],
        compiler_params=_params(("parallel", "arbitrary"), 40),
        name="retention",
    )(zb, zb, zb, zb, cos, sin_e, sin_o, decay, w_q, w_k, g_chunk, gn_w.reshape(1, hv))


def _merge_kernel(on_ref, or_ref, ga_ref, gb_ref, x_ref, wa_ref, wb_ref, wo_ref, h_ref):
    a = _dot(on_ref[...], wa_ref[...])
    b = _dot(or_ref[...], wb_ref[...])
    merged = jax.nn.sigmoid(ga_ref[...]) * a + jax.nn.sigmoid(gb_ref[...]) * b
    h_ref[...] = x_ref[...] + _dot(merged.astype(BF16), wo_ref[...])


def _merge(o_nsa, o_ret, zb, x, w_a, w_b, w_out, tm=256):
    m, d = x.shape
    resident = lambda shape: pl.BlockSpec(shape, lambda i: (0, 0), pipeline_mode=pl.Buffered(1))
    return pl.pallas_call(
        _merge_kernel,
        grid=(m // tm,),
        in_specs=[
            pl.BlockSpec((tm, d), lambda i: (i, 0)),
            pl.BlockSpec((tm, d), lambda i: (i, 0)),
            pl.BlockSpec((tm, d), lambda i: (i, B_GA // d)),
            pl.BlockSpec((tm, d), lambda i: (i, B_GB // d)),
            pl.BlockSpec((tm, d), lambda i: (i, 0)),
            resident(w_a.shape), resident(w_b.shape), resident(w_out.shape),
        ],
        out_specs=pl.BlockSpec((tm, d), lambda i: (i, 0)),
        out_shape=jax.ShapeDtypeStruct((m, d), F32),
        compiler_params=_params(("parallel",), 56),
        name="merge_out_proj",
    )(o_nsa, o_ret, zb, zb, x, w_a, w_b, w_out)


def _mem_kv_kernel(m_ref, nw_ref, wk_ref, wv_ref, k_ref, v_ref):
    mn = _rms(m_ref[...], nw_ref[...]).astype(BF16)
    k_ref[...] = _dot(mn, wk_ref[...]).astype(k_ref.dtype)
    v_ref[...] = _dot(mn, wv_ref[...]).astype(v_ref.dtype)


def _mem_kv(mem2, nw, wk, wv, tm=256):
    m, d = mem2.shape
    n = wk.shape[1]
    out = jax.ShapeDtypeStruct((m, n), BF16)
    return pl.pallas_call(
        _mem_kv_kernel,
        grid=(m // tm,),
        in_specs=[
            pl.BlockSpec((tm, d), lambda i: (i, 0)),
            pl.BlockSpec((1, d), lambda i: (0, 0)),
            pl.BlockSpec((d, n), lambda i: (0, 0)),
            pl.BlockSpec((d, n), lambda i: (0, 0)),
        ],
        out_specs=[pl.BlockSpec((tm, n), lambda i: (i, 0))] * 2,
        out_shape=[out, out],
        compiler_params=_params(("parallel",), 40),
        name="mem_kv_proj",
    )(mem2, nw.reshape(1, d), wk, wv)


def _cross_kernel(h_ref, xw_ref, mw_ref, wq_ref, kx_ref, vx_ref, wo_ref, h2_ref, nm_ref):
    h = h_ref[...]
    nx = _rms(h, xw_ref[...]).astype(BF16)
    qx = _dot(nx, wq_ref[...]).astype(BF16)
    outs = []
    for hh in range(X_HEADS):
        cols = slice(hh * X_DH, (hh + 1) * X_DH)
        s = _dot_nt(qx[:, cols], kx_ref[:, cols]) * (X_DH ** -0.5)
        e = jnp.exp(s - jnp.max(s, axis=-1, keepdims=True))
        p = e / jnp.sum(e, axis=-1, keepdims=True)
        outs.append(_dot(p.astype(BF16), vx_ref[:, cols]))
    ox = jnp.concatenate(outs, axis=-1).astype(BF16)
    h2 = h + _dot(ox, wo_ref[...])
    h2_ref[...] = h2
    nm_ref[...] = _rms(h2, mw_ref[...]).astype(nm_ref.dtype)


def _cross_attention(h1, x_norm_w, mlp_norm_w, wq, kx, vx, wo, seq, tm=256):
    m, d = h1.shape
    n = wq.shape[1]
    per_batch = seq // tm
    vec = lambda: pl.BlockSpec((1, d), lambda i: (0, 0))
    return pl.pallas_call(
        _cross_kernel,
        grid=(m // tm,),
        in_specs=[
            pl.BlockSpec((tm, d), lambda i: (i, 0)),
            vec(), vec(),
            pl.BlockSpec((d, n), lambda i: (0, 0)),
            pl.BlockSpec((MEM_LEN, n), lambda i: (i // per_batch, 0)),
            pl.BlockSpec((MEM_LEN, n), lambda i: (i // per_batch, 0)),
            pl.BlockSpec((n, d), lambda i: (0, 0)),
        ],
        out_specs=[pl.BlockSpec((tm, d), lambda i: (i, 0))] * 2,
        out_shape=[jax.ShapeDtypeStruct((m, d), F32), jax.ShapeDtypeStruct((m, d), BF16)],
        compiler_params=_params(("parallel",), 40),
        name="cross_attention",
    )(h1, x_norm_w.reshape(1, d), mlp_norm_w.reshape(1, d), wq, kx, vx, wo)


def _mlp_kernel(nm_ref, wu_ref, wd_ref, h_ref, fw_ref, o_ref, acc_ref):
    j = pl.program_id(1)

    @pl.when(j == 0)
    def _():
        acc_ref[...] = jnp.zeros(acc_ref.shape, F32)

    u = jnp.maximum(_dot(nm_ref[...], wu_ref[...]), 0.0)
    acc_ref[...] += _dot((u * u).astype(BF16), wd_ref[...])

    @pl.when(j == pl.num_programs(1) - 1)
    def _():
        o_ref[...] = _rms(h_ref[...] + acc_ref[...], fw_ref[...])


def _mlp(nm, w_up, w_down, h2, final_w, tm=512, tf=1024):
    m, d = nm.shape
    f = w_up.shape[1]
    return pl.pallas_call(
        _mlp_kernel,
        grid=(m // tm, f // tf),
        in_specs=[
            pl.BlockSpec((tm, d), lambda i, j: (i, 0)),
            pl.BlockSpec((d, tf), lambda i, j: (0, j)),
            pl.BlockSpec((tf, d), lambda i, j: (j, 0)),
            pl.BlockSpec((tm, d), lambda i, j: (i, 0)),
            pl.BlockSpec((1, d), lambda i, j: (0, 0)),
        ],
        out_specs=pl.BlockSpec((tm, d), lambda i, j: (i, 0)),
        out_shape=jax.ShapeDtypeStruct((m, d), F32),
        scratch_shapes=[pltpu.VMEM((tm, d), F32)],
        compiler_params=_params(("parallel", "arbitrary"), 56),
        name="mlp_final_norm",
    )(nm, w_up, w_down, h2, final_w.reshape(1, d))


def _split_w_in(w):
    d = w.shape[0]
    per_group = NSA_REP * 3
    n_gate = NSA_GROUPS * per_group
    g_nsa = w[:, W_IN_NSA_GATE:W_IN_NSA_GATE + n_gate].reshape(d, NSA_GROUPS, per_group)
    gn = jnp.pad(g_nsa, ((0, 0), (0, 0), (0, LANES - per_group))).reshape(d, NSA_GROUPS * LANES)
    wa = w[:, :W_IN_NSA_GATE]
    wb = jnp.concatenate([w[:, W_IN_NSA_GATE + n_gate:], gn], axis=1)
    assert wa.shape[1] == A_WIDTH and wb.shape[1] == B_WIDTH
    return wa.astype(BF16), wb.astype(BF16)


def _layer(h, mem, attn_norm_w, w_in, cmp_pe_k, cmp_w1_k, cmp_w2_k, cmp_pe_v, cmp_w1_v, cmp_w2_v,
           w_a, ret_gn_w, w_b, w_out, x_norm_w, mem_norm_w, wq_x, wk_x, wv_x, wo_x,
           mlp_norm_w, w_up, w_down, out_norm_w, batch, seq):
    bf = lambda a: a.astype(BF16)
    w_in_a, w_in_b = _split_w_in(w_in)

    n = _rmsnorm(h, attn_norm_w, BF16)
    za = _matmul(n, w_in_a, BF16, tm=1024, name="in_proj_a")
    zb = _matmul(n, w_in_b, F32, tm=1024, name="in_proj_b")

    def blocks16(off):
        a = za[:, off:off + NSA_GROUPS * NSA_DK].reshape(batch, seq // CMP_STRIDE, CMP_STRIDE, NSA_GROUPS, NSA_DK)
        return a.transpose(0, 3, 1, 2, 4).reshape(batch * NSA_GROUPS * (seq // CMP_STRIDE), CMP_STRIDE * NSA_DK)

    assert seq // CMP_STRIDE == LANES
    kc = _compress(blocks16(A_KC), cmp_pe_k.reshape(1, -1), bf(cmp_w1_k), bf(cmp_w2_k))
    vc = _compress(blocks16(A_VC), cmp_pe_v.reshape(1, -1), bf(cmp_w1_v), bf(cmp_w2_v))

    o_nsa = _nsa_attention(za, zb, kc, vc, batch, seq)
    o_ret = _retention(zb, ret_gn_w, batch, seq)
    h1 = _merge(o_nsa, o_ret, zb, h, bf(w_a), bf(w_b), bf(w_out))

    kx, vx = _mem_kv(mem.reshape(batch * MEM_LEN, D_MODEL), mem_norm_w, bf(wk_x), bf(wv_x))
    h2, nm = _cross_attention(h1, x_norm_w, mlp_norm_w, bf(wq_x), kx, vx, bf(wo_x), seq)
    return _mlp(nm, bf(w_up), bf(w_down), h2, out_norm_w)


def kernel(x, mem, attn_norm_w, w_in, cmp_pe_k, cmp_w1_k, cmp_w2_k, cmp_pe_v, cmp_w1_v, cmp_w2_v, w_a, ret_gn_w,
           w_b, w_out, x_norm_w, mem_norm_w, wq_x, wk_x, wv_x, wo_x, mlp_norm_w, w_up, w_down, final_norm_w):
    batch, seq, d = x.shape
    depth = w_in.shape[0]
    assert depth == 1
    h = x.reshape(batch * seq, d)
    out = _layer(h, mem, attn_norm_w[0], w_in[0], cmp_pe_k[0], cmp_w1_k[0], cmp_w2_k[0],
                 cmp_pe_v[0], cmp_w1_v[0], cmp_w2_v[0], w_a[0], ret_gn_w[0], w_b[0], w_out[0],
                 x_norm_w[0], mem_norm_w[0], wq_x[0], wk_x[0], wv_x[0], wo_x[0],
                 mlp_norm_w[0], w_up[0], w_down[0], final_norm_w, batch, seq)
    return out.reshape(batch, seq, d)
```

```python
import functools

import jax
import jax.numpy as jnp
import numpy as np
from jax import lax
from jax.experimental import pallas as pl
from jax.experimental.pallas import tpu as pltpu

F32 = jnp.float32
BF16 = jnp.bfloat16

D_MODEL = 2048
MEM_LEN = 256
NSA_HEADS = 16
NSA_GROUPS = 4
NSA_REP = NSA_HEADS // NSA_GROUPS
NSA_DK = 128
NSA_DV = 128
CMP_LEN = 32
CMP_STRIDE = 16
CMP_HIDDEN = 1024
SEL_LEN = 64
SEL_TOPK = 16
WIN = 512
RET_HEADS = 8
RET_DK = 128
RET_DV = 256
RET_CHUNK = 128
ROPE_BASE = 10000.0
X_HEADS = 4
X_DH = 128
D_FF = 4 * D_MODEL
EPS = 1e-6
NEG = -1e30
LOG2E = 1.4426950408889634

LANES = 128

W_IN_BLOCK = 1024
W_IN_KV = 2048
W_IN_NSA_GATE = 5120
A_QN = 0
A_KS = 2048
A_VS = 2560
A_KW = 3072
A_VW = 3584
B_QR = 0
B_KR = 1024
B_VR = 2048
B_GR = 4096
B_GA = 6144
B_GB = 8192
B_WIDTH = 10240

NT_DIMS = (((1,), (1,)), ((), ()))


def _params(sem, vmem_mb=None):
    kw = {"dimension_semantics": sem}
    if vmem_mb is not None:
        kw["vmem_limit_bytes"] = vmem_mb * 1024 * 1024
    return pltpu.CompilerParams(**kw)


def _rms(x, w):
    return x * lax.rsqrt(jnp.mean(x * x, axis=-1, keepdims=True) + EPS) * w


def _dot(a, b):
    return jnp.dot(a, b, preferred_element_type=F32)


def _dot_nt(a, b):
    return lax.dot_general(a, b, NT_DIMS, preferred_element_type=F32)


def _rmsnorm_kernel(x_ref, w_ref, o_ref):
    o_ref[...] = _rms(x_ref[...].astype(F32), w_ref[...]).astype(o_ref.dtype)


def _rmsnorm(x, w, out_dtype, tm=512):
    m, d = x.shape
    return pl.pallas_call(
        _rmsnorm_kernel,
        grid=(m // tm,),
        in_specs=[pl.BlockSpec((tm, d), lambda i: (i, 0)), pl.BlockSpec((1, d), lambda i: (0, 0))],
        out_specs=pl.BlockSpec((tm, d), lambda i: (i, 0)),
        out_shape=jax.ShapeDtypeStruct((m, d), out_dtype),
        compiler_params=_params(("parallel",)),
        name="rmsnorm",
    )(x, w.reshape(1, d))


def _mm_kernel(a_ref, b_ref, o_ref):
    o_ref[...] = _dot(a_ref[...], b_ref[...]).astype(o_ref.dtype)


def _matmul(a, b, out_dtype, tm=512, tn=512, name="matmul"):
    m, k = a.shape
    n = b.shape[1]
    return pl.pallas_call(
        _mm_kernel,
        grid=(m // tm, n // tn),
        in_specs=[pl.BlockSpec((tm, k), lambda i, j: (i, 0)), pl.BlockSpec((k, tn), lambda i, j: (0, j))],
        out_specs=pl.BlockSpec((tm, tn), lambda i, j: (i, j)),
        out_shape=jax.ShapeDtypeStruct((m, n), out_dtype),
        compiler_params=_params(("parallel", "arbitrary"), 40),
        name=name,
    )(a, b)


CAST_ROWS = 16


def _cast_rows(dst_ref, dst0, src_ref, src0, nrows):
    def body(r, carry):
        off = r * CAST_ROWS
        dst_ref[pl.ds(pl.multiple_of(dst0 + off, CAST_ROWS), CAST_ROWS), :] = (
            src_ref[pl.ds(pl.multiple_of(src0 + off, 8), CAST_ROWS), :].astype(BF16))
        return carry

    lax.fori_loop(0, nrows // CAST_ROWS, body, 0)


def _in_proj_kernel(n_ref, w_ref, wnext_ref, o_ref, wb_sc, *rest, shift, regroup):
    tn = wb_sc.shape[0]

    @pl.when(pl.program_id(1) == 0)
    def _():
        _cast_rows(wb_sc, 0, w_ref, shift, tn - shift)
        if shift:
            _cast_rows(wb_sc, tn - shift, wnext_ref, 0, shift)

    res = _dot_nt(n_ref[...], wb_sc[...])
    if regroup:
        (r_sc,) = rest
        tm = res.shape[0]
        for c in range(tn // LANES):
            r_sc[c] = res[:, c * LANES:(c + 1) * LANES]
        for c in range(tn // LANES):
            for j in range(regroup):
                o_ref[c, :, j * LANES:(j + 1) * LANES] = (
                    r_sc[c, pl.ds(j, tm // regroup, stride=regroup), :].astype(o_ref.dtype))
    else:
        o_ref[...] = res.astype(o_ref.dtype)


def _in_proj(n, wt, src_block, n_blocks, out_dtype, *, shift=0, regroup=0, tm=1024, tn=1024, name):
    m, k = n.shape
    per = tn // LANES
    assert shift % CAST_ROWS == 0 and shift <= LANES
    scratch = [pltpu.VMEM((tn, k), BF16)]
    if regroup:
        assert n_blocks == 1
        out_shape = jax.ShapeDtypeStruct((per, m // regroup, regroup * LANES), out_dtype)
        out_spec = pl.BlockSpec((per, tm // regroup, regroup * LANES), lambda j, i: (0, i, 0))
        scratch.append(pltpu.VMEM((per, tm, LANES), F32))
    else:
        out_shape = jax.ShapeDtypeStruct((m, n_blocks * tn), out_dtype)
        out_spec = pl.BlockSpec((tm, tn), lambda j, i: (i, j))
    return pl.pallas_call(
        functools.partial(_in_proj_kernel, shift=shift, regroup=regroup),
        grid=(n_blocks, m // tm),
        in_specs=[
            pl.BlockSpec((tm, k), lambda j, i: (i, 0)),
            pl.BlockSpec((tn, k), lambda j, i: (src_block(j), 0)),
            pl.BlockSpec((LANES, k), lambda j, i: ((src_block(j) + 1) * per, 0)),
        ],
        out_specs=out_spec,
        out_shape=out_shape,
        scratch_shapes=scratch,
        compiler_params=_params(("parallel", "arbitrary"), 56),
        name=name,
    )(n, wt, wt)


def _compress_kernel(k2_ref, pe_ref, w1_ref, w2_ref, o_ref):
    rows, half = k2_ref.shape
    k2 = k2_ref[...].astype(F32)
    a_lo = (k2 + pe_ref[:, :half]).astype(BF16)
    a_hi = (k2 + pe_ref[:, half:]).astype(BF16)
    lo = _dot(a_lo, w1_ref[:half, :])
    hi = _dot(a_hi, w1_ref[half:, :])
    h = lo + pltpu.roll(hi, rows - 1, axis=0)
    act = h * jax.nn.sigmoid(h)
    o_ref[...] = _dot(act.astype(BF16), w2_ref[...]).astype(o_ref.dtype)


def _compress(k2, first_row, m, pe, w1, w2, rows=256):
    half = k2.shape[1]
    hid = w1.shape[1]
    dout = w2.shape[1]
    first = first_row // rows
    return pl.pallas_call(
        _compress_kernel,
        grid=(m // rows,),
        in_specs=[
            pl.BlockSpec((rows, half), lambda i: (first + i, 0)),
            pl.BlockSpec((1, 2 * half), lambda i: (0, 0)),
            pl.BlockSpec((2 * half, hid), lambda i: (0, 0)),
            pl.BlockSpec((hid, dout), lambda i: (0, 0)),
        ],
        out_specs=pl.BlockSpec((rows, dout), lambda i: (i, 0)),
        out_shape=jax.ShapeDtypeStruct((m, dout), BF16),
        compiler_params=_params(("parallel",), 40),
        name="nsa_compress",
    )(k2, pe, w1, w2)


def _nsa_kernel(q_ref, kc_ref, vc_ref, ks_ref, vs_ref, kw_ref, vw_ref, g_ref, e_ref, o_ref,
                s_sc, mx_sc, l_sc, acc_sc, po_sc, *, seq, tq, kblk):
    i = pl.program_id(2)
    t0 = i * tq
    rep = NSA_REP
    c1 = NSA_DK ** -0.5 * LOG2E
    n_cmp = (seq - CMP_LEN) // CMP_STRIDE + 1
    n_sel = seq // SEL_LEN
    topk = min(SEL_TOPK, n_sel)

    q = q_ref[...]
    q4 = jnp.concatenate([q[:, r * NSA_DK:(r + 1) * NSA_DK] for r in range(rep)], axis=0)
    tcol = t0 + lax.broadcasted_iota(jnp.int32, (tq, 1), 0)
    head = lambda r: slice(r * tq, (r + 1) * tq)
    per_group = rep * 3
    gs = pltpu.roll(jax.nn.sigmoid(g_ref[...]), lax.rem(LANES - pl.program_id(1) * per_group, LANES), axis=1)
    gate = lambda r, branch: gs[:, 3 * r + branch:3 * r + branch + 1]

    def masked(s, mask1):
        return jnp.concatenate([jnp.where(mask1, s[head(r)], NEG) for r in range(rep)], axis=0)

    def exp_rows(sm):
        return jnp.exp2((sm - jnp.max(sm, axis=-1, keepdims=True)) * c1)

    c_idx = lax.broadcasted_iota(jnp.int32, (tq, LANES), 1)
    mask_c = ((c_idx * CMP_STRIDE + (CMP_LEN - 1)) <= tcol) & (c_idx < n_cmp)
    e = exp_rows(masked(_dot_nt(q4, kc_ref[...]), mask_c))
    p = e / jnp.sum(e, axis=-1, keepdims=True)
    mask_cf = jnp.where(mask_c, 1.0, 0.0)
    p = jnp.concatenate([p[head(r)] * mask_cf for r in range(rep)], axis=0)
    o_cmp = _dot(p.astype(BF16), vc_ref[...])
    psum = p[head(0)]
    for r in range(1, rep):
        psum = psum + p[head(r)]

    wlen = WIN + tq
    start = pl.multiple_of(jnp.maximum(i - WIN // tq, 0) * tq, tq)
    dlt = tcol - (start + lax.broadcasted_iota(jnp.int32, (tq, wlen), 1))
    e = exp_rows(masked(_dot_nt(q4, kw_ref[pl.ds(start, wlen), :]), (dlt >= 0) & (dlt < WIN)))
    o_win = _dot(e.astype(BF16), vw_ref[pl.ds(start, wlen), :]) / jnp.sum(e, axis=-1, keepdims=True)
    for r in range(rep):
        po_sc[head(r), :] = gate(r, 0) * o_cmp[head(r)] + gate(r, 2) * o_win[head(r)]

    jo = lax.broadcasted_iota(jnp.int32, (n_sel, LANES), 0)
    co = lax.broadcasted_iota(jnp.int32, (n_sel, LANES), 1)
    ov_t = jnp.where((co * CMP_STRIDE < jo * SEL_LEN + SEL_LEN) & (co * CMP_STRIDE + CMP_LEN > jo * SEL_LEN)
                     & (co < n_cmp), 1.0, 0.0).astype(BF16)
    p_hi = psum.astype(BF16)
    p_lo = (psum - p_hi.astype(F32)).astype(BF16)
    imp = _dot_nt(ov_t, p_hi) + _dot_nt(ov_t, p_lo)
    j_idx = lax.broadcasted_iota(jnp.int32, (n_sel, tq), 0)
    cur = lax.shift_right_logical(t0 + lax.broadcasted_iota(jnp.int32, (n_sel, tq), 1), int(np.log2(SEL_LEN)))
    forced = (j_idx == 0) | (j_idx == cur) | (j_idx == cur - 1)
    impm = jnp.where(forced, jnp.inf, jnp.where(j_idx > cur, -jnp.inf, imp))
    rank = jnp.zeros((n_sel, tq), F32)
    for ii in range(n_sel):
        row = impm[ii:ii + 1, :]
        beats = (row > impm) | ((row == impm) & (j_idx > ii))
        rank = rank + jnp.where(beats, 1.0, 0.0)
    sel_t = jnp.where(rank < topk, 1.0, 0.0)
    sel = jnp.concatenate([sel_t, jnp.zeros((LANES - n_sel, tq), F32)], axis=0).T.astype(BF16)

    n_chunks = lax.div(t0 + tq + (kblk - 1), kblk)

    def lane_fold(x, op):
        out = x[:, :LANES]
        for c in range(1, kblk // LANES):
            out = op(out, x[:, c * LANES:(c + 1) * LANES])
        return out

    def score_chunk(kb):
        k0 = kb * kblk if isinstance(kb, int) else pl.multiple_of(kb * kblk, kblk)
        s = _dot_nt(q4, ks_ref[pl.ds(k0, kblk), :])
        in_sel = _dot(sel, e_ref[kb])
        tk = k0 + lax.broadcasted_iota(jnp.int32, (tq, kblk), 1)
        sm_ = masked(s, (in_sel > 0.5) & (tk <= tcol))
        s_sc[kb] = sm_
        return lane_fold(sm_, jnp.maximum)

    def score_pass(kb, carry):
        mx_sc[...] = jnp.maximum(mx_sc[...], score_chunk(kb))
        return carry

    mx_sc[...] = score_chunk(0)
    lax.fori_loop(1, n_chunks, score_pass, 0)
    m_sel = jnp.max(mx_sc[...], axis=-1, keepdims=True)

    def value_chunk(kb):
        k0 = kb * kblk if isinstance(kb, int) else pl.multiple_of(kb * kblk, kblk)
        pk = jnp.exp2((s_sc[kb] - m_sel) * c1)
        return lane_fold(pk, jnp.add), _dot(pk.astype(BF16), vs_ref[pl.ds(k0, kblk), :])

    def value_pass(kb, carry):
        lsum, pv = value_chunk(kb)
        l_sc[...] += lsum
        acc_sc[...] += pv
        return carry

    l_sc[...], acc_sc[...] = value_chunk(0)
    lax.fori_loop(1, n_chunks, value_pass, 0)
    o_sel = acc_sc[...] / jnp.sum(l_sc[...], axis=-1, keepdims=True)

    for r in range(rep):
        o = po_sc[head(r), :] + gate(r, 1) * o_sel[head(r)]
        o_ref[:, r * NSA_DV:(r + 1) * NSA_DV] = o.astype(o_ref.dtype)


def _nsa_attention(za, zg, kc, vc, batch, seq, tq=256, kblk=512):
    assert seq % kblk == 0 and seq >= WIN + tq and WIN % tq == 0
    nq = seq // tq
    gw = NSA_REP * NSA_DK
    kern = functools.partial(_nsa_kernel, seq=seq, tq=tq, kblk=kblk)
    key = np.arange(seq).reshape(seq // kblk, 1, kblk)
    expand = jnp.asarray(key // SEL_LEN == np.arange(LANES).reshape(1, LANES, 1), BF16)

    def kv_spec(off):
        return pl.BlockSpec((seq, LANES), lambda b, g, i: (b, off // LANES + g))

    return pl.pallas_call(
        kern,
        grid=(batch, NSA_GROUPS, nq),
        in_specs=[
            pl.BlockSpec((tq, gw), lambda b, g, i: (b * nq + i, A_QN // gw + g)),
            pl.BlockSpec((LANES, NSA_DK), lambda b, g, i: (g * batch + b, 0)),
            pl.BlockSpec((LANES, NSA_DV), lambda b, g, i: (g * batch + b, 0)),
            kv_spec(A_KS), kv_spec(A_VS), kv_spec(A_KW), kv_spec(A_VW),
            pl.BlockSpec((tq, LANES), lambda b, g, i: (b * nq + i, 0)),
            pl.BlockSpec(expand.shape, lambda b, g, i: (0, 0, 0)),
        ],
        out_specs=pl.BlockSpec((tq, gw), lambda b, g, i: (b * nq + i, g)),
        out_shape=jax.ShapeDtypeStruct((batch * seq, NSA_HEADS * NSA_DV), BF16),
        scratch_shapes=[
            pltpu.VMEM((seq // kblk, NSA_REP * tq, kblk), F32),
            pltpu.VMEM((NSA_REP * tq, LANES), F32),
            pltpu.VMEM((NSA_REP * tq, LANES), F32),
            pltpu.VMEM((NSA_REP * tq, NSA_DV), F32),
            pltpu.VMEM((NSA_REP * tq, NSA_DV), F32),
        ],
        compiler_params=_params(("parallel", "parallel", "arbitrary"), 40),
        name="nsa_attention",
    )(za, kc, vc, za, za, za, za, zg, expand)


def _retention_kernel(q_ref, k_ref, v_ref, g_ref, cos_ref, sin_e_ref, sin_o_ref, dec_ref, wq_ref, wk_ref,
                      gc_ref, gn_ref, o_ref, st_ref):
    @pl.when(pl.program_id(1) == 0)
    def _():
        st_ref[...] = jnp.zeros(st_ref.shape, F32)

    cos = cos_ref[...]
    sin_e = sin_e_ref[...]
    sin_o = sin_o_ref[...]

    def rotate(x):
        return x * cos + pltpu.roll(x, RET_DK - 1, axis=1) * sin_e + pltpu.roll(x, 1, axis=1) * sin_o

    for h in range(RET_HEADS):
        qf = rotate(q_ref[:, h * RET_DK:(h + 1) * RET_DK])
        kf = rotate(k_ref[:, h * RET_DK:(h + 1) * RET_DK]) * (RET_DK ** -0.5)
        qb = qf.astype(BF16)
        cols = slice(h * RET_DV, (h + 1) * RET_DV)
        v = v_ref[:, cols].astype(BF16)
        s = _dot_nt(qb, kf.astype(BF16)) * dec_ref[h]
        st = st_ref[h]
        o = _dot(s.astype(BF16), v) + _dot(qb, st.astype(BF16)) * wq_ref[h]
        kv = _dot((kf * wk_ref[h]).T.astype(BF16), v)
        st_ref[h] = st * gc_ref[h] + kv
        mu = jnp.mean(o, axis=-1, keepdims=True)
        d = o - mu
        var = jnp.mean(d * d, axis=-1, keepdims=True)
        on = d * lax.rsqrt(var + EPS) * gn_ref[:, cols]
        gr = g_ref[:, cols]
        o_ref[:, cols] = (gr * jax.nn.sigmoid(gr) * on).astype(o_ref.dtype)


def _retention(zb, gn_w, batch, seq):
    c = RET_CHUNK
    nc = seq // c
    hq = RET_HEADS * RET_DK
    hv = RET_HEADS * RET_DV
    f32 = np.float32
    inv = f32(ROPE_BASE) ** (-np.arange(0, RET_DK, 2, dtype=f32) / f32(RET_DK))
    ang = np.arange(seq, dtype=f32)[:, None] * inv[None, :]
    zero = np.zeros_like(ang)
    pairs = lambda even, odd: np.stack([even, odd], axis=-1).reshape(seq, RET_DK)
    cos = pairs(np.cos(ang), np.cos(ang))
    sin_e = pairs(-np.sin(ang), zero)
    sin_o = pairs(zero, np.sin(ang))
    log_g = np.log1p(-np.exp2(f32(-5.0) - np.arange(RET_HEADS, dtype=f32)))
    idx = np.arange(c, dtype=f32)
    rel = idx[:, None] - idx[None, :]
    decay = np.where(rel >= 0, np.exp(log_g[:, None, None] * np.maximum(rel, f32(0.0))), f32(0.0)).astype(f32)
    w_k = np.exp(log_g[:, None] * (f32(c - 1) - idx)[None, :])[:, :, None].astype(f32)
    w_q = np.exp(log_g[:, None] * (idx + f32(1.0))[None, :])[:, :, None].astype(f32)
    g_chunk = np.broadcast_to(np.exp(log_g * f32(c))[:, None, None], (RET_HEADS, 1, RET_DV)).astype(f32)

    row = lambda b, n: b * nc + n
    return pl.pallas_call(
        _retention_kernel,
        grid=(batch, nc),
        in_specs=[
            pl.BlockSpec((c, hq), lambda b, n: (row(b, n), B_QR // hq)),
            pl.BlockSpec((c, hq), lambda b, n: (row(b, n), B_KR // hq)),
            pl.BlockSpec((c, hv), lambda b, n: (row(b, n), B_VR // hv)),
            pl.BlockSpec((c, hv), lambda b, n: (row(b, n), B_GR // hv)),
            pl.BlockSpec((c, RET_DK), lambda b, n: (n, 0)),
            pl.BlockSpec((c, RET_DK), lambda b, n: (n, 0)),
            pl.BlockSpec((c, RET_DK), lambda b, n: (n, 0)),
            pl.BlockSpec((RET_HEADS, c, c), lambda b, n: (0, 0, 0)),
            pl.BlockSpec((RET_HEADS, c, 1), lambda b, n: (0, 0, 0)),
            pl.BlockSpec((RET_HEADS, c, 1), lambda b, n: (0, 0, 0)),
            pl.BlockSpec((RET_HEADS, 1, RET_DV), lambda b, n: (0, 0, 0)),
            pl.BlockSpec((1, hv), lambda b, n: (0, 0)),
        ],
        out_specs=pl.BlockSpec((c, hv), lambda b, n: (row(b, n), 0)),
        out_shape=jax.ShapeDtypeStruct((batch * seq, hv), BF16),
        scratch_shapes=[pltpu.VMEM((RET_HEADS, RET_DK, RET_DV), F32)],
        compiler_params=_params(("parallel", "arbitrary"), 40),
        name="retention",
    )(zb, zb, zb, zb, cos, sin_e, sin_o, decay, w_q, w_k, g_chunk, gn_w.reshape(1, hv))


def _merge_kernel(on_ref, or_ref, ga_ref, gb_ref, x_ref, wa_ref, wb_ref, wo_ref, h_ref):
    a = _dot(on_ref[...], wa_ref[...])
    b = _dot(or_ref[...], wb_ref[...])
    merged = jax.nn.sigmoid(ga_ref[...]) * a + jax.nn.sigmoid(gb_ref[...]) * b
    h_ref[...] = x_ref[...] + _dot(merged.astype(BF16), wo_ref[...])


def _merge(o_nsa, o_ret, zb, x, w_a, w_b, w_out, tm=256):
    m, d = x.shape
    resident = lambda shape: pl.BlockSpec(shape, lambda i: (0, 0), pipeline_mode=pl.Buffered(1))
    return pl.pallas_call(
        _merge_kernel,
        grid=(m // tm,),
        in_specs=[
            pl.BlockSpec((tm, d), lambda i: (i, 0)),
            pl.BlockSpec((tm, d), lambda i: (i, 0)),
            pl.BlockSpec((tm, d), lambda i: (i, B_GA // d)),
            pl.BlockSpec((tm, d), lambda i: (i, B_GB // d)),
            pl.BlockSpec((tm, d), lambda i: (i, 0)),
            resident(w_a.shape), resident(w_b.shape), resident(w_out.shape),
        ],
        out_specs=pl.BlockSpec((tm, d), lambda i: (i, 0)),
        out_shape=jax.ShapeDtypeStruct((m, d), F32),
        compiler_params=_params(("parallel",), 56),
        name="merge_out_proj",
    )(o_nsa, o_ret, zb, zb, x, w_a, w_b, w_out)


def _mem_kv_kernel(m_ref, nw_ref, wk_ref, wv_ref, k_ref, v_ref):
    mn = _rms(m_ref[...], nw_ref[...]).astype(BF16)
    k_ref[...] = _dot(mn, wk_ref[...]).astype(k_ref.dtype)
    v_ref[...] = _dot(mn, wv_ref[...]).astype(v_ref.dtype)


def _mem_kv(mem2, nw, wk, wv, tm=256):
    m, d = mem2.shape
    n = wk.shape[1]
    out = jax.ShapeDtypeStruct((m, n), BF16)
    return pl.pallas_call(
        _mem_kv_kernel,
        grid=(m // tm,),
        in_specs=[
            pl.BlockSpec((tm, d), lambda i: (i, 0)),
            pl.BlockSpec((1, d), lambda i: (0, 0)),
            pl.BlockSpec((d, n), lambda i: (0, 0)),
            pl.BlockSpec((d, n), lambda i: (0, 0)),
        ],
        out_specs=[pl.BlockSpec((tm, n), lambda i: (i, 0))] * 2,
        out_shape=[out, out],
        compiler_params=_params(("parallel",), 40),
        name="mem_kv_proj",
    )(mem2, nw.reshape(1, d), wk, wv)


def _cross_kernel(h_ref, xw_ref, mw_ref, wq_ref, kx_ref, vx_ref, wo_ref, h2_ref, nm_ref):
    h = h_ref[...]
    nx = _rms(h, xw_ref[...]).astype(BF16)
    qx = _dot(nx, wq_ref[...]).astype(BF16)
    outs = []
    for hh in range(X_HEADS):
        cols = slice(hh * X_DH, (hh + 1) * X_DH)
        s = _dot_nt(qx[:, cols], kx_ref[:, cols]) * (X_DH ** -0.5)
        e = jnp.exp(s - jnp.max(s, axis=-1, keepdims=True))
        p = e / jnp.sum(e, axis=-1, keepdims=True)
        outs.append(_dot(p.astype(BF16), vx_ref[:, cols]))
    ox = jnp.concatenate(outs, axis=-1).astype(BF16)
    h2 = h + _dot(ox, wo_ref[...])
    h2_ref[...] = h2
    nm_ref[...] = _rms(h2, mw_ref[...]).astype(nm_ref.dtype)


def _cross_attention(h1, x_norm_w, mlp_norm_w, wq, kx, vx, wo, seq, tm=256):
    m, d = h1.shape
    n = wq.shape[1]
    per_batch = seq // tm
    vec = lambda: pl.BlockSpec((1, d), lambda i: (0, 0))
    return pl.pallas_call(
        _cross_kernel,
        grid=(m // tm,),
        in_specs=[
            pl.BlockSpec((tm, d), lambda i: (i, 0)),
            vec(), vec(),
            pl.BlockSpec((d, n), lambda i: (0, 0)),
            pl.BlockSpec((MEM_LEN, n), lambda i: (i // per_batch, 0)),
            pl.BlockSpec((MEM_LEN, n), lambda i: (i // per_batch, 0)),
            pl.BlockSpec((n, d), lambda i: (0, 0)),
        ],
        out_specs=[pl.BlockSpec((tm, d), lambda i: (i, 0))] * 2,
        out_shape=[jax.ShapeDtypeStruct((m, d), F32), jax.ShapeDtypeStruct((m, d), BF16)],
        compiler_params=_params(("parallel",), 40),
        name="cross_attention",
    )(h1, x_norm_w.reshape(1, d), mlp_norm_w.reshape(1, d), wq, kx, vx, wo)


def _mlp_kernel(nm_ref, wu_ref, wd_ref, h_ref, fw_ref, o_ref, acc_ref):
    j = pl.program_id(1)

    @pl.when(j == 0)
    def _():
        acc_ref[...] = jnp.zeros(acc_ref.shape, F32)

    u = jnp.maximum(_dot(nm_ref[...], wu_ref[...]), 0.0)
    acc_ref[...] += _dot((u * u).astype(BF16), wd_ref[...])

    @pl.when(j == pl.num_programs(1) - 1)
    def _():
        o_ref[...] = _rms(h_ref[...] + acc_ref[...], fw_ref[...])


def _mlp(nm, w_up, w_down, h2, final_w, tm=512, tf=1024):
    m, d = nm.shape
    f = w_up.shape[1]
    return pl.pallas_call(
        _mlp_kernel,
        grid=(m // tm, f // tf),
        in_specs=[
            pl.BlockSpec((tm, d), lambda i, j: (i, 0)),
            pl.BlockSpec((d, tf), lambda i, j: (0, j)),
            pl.BlockSpec((tf, d), lambda i, j: (j, 0)),
            pl.BlockSpec((tm, d), lambda i, j: (i, 0)),
            pl.BlockSpec((1, d), lambda i, j: (0, 0)),
        ],
        out_specs=pl.BlockSpec((tm, d), lambda i, j: (i, 0)),
        out_shape=jax.ShapeDtypeStruct((m, d), F32),
        scratch_shapes=[pltpu.VMEM((tm, d), F32)],
        compiler_params=_params(("parallel", "arbitrary"), 56),
        name="mlp_final_norm",
    )(nm, w_up, w_down, h2, final_w.reshape(1, d))


def _layer(h, mem, attn_norm_w, w_in, cmp_pe_k, cmp_w1_k, cmp_w2_k, cmp_pe_v, cmp_w1_v, cmp_w2_v,
           w_a, ret_gn_w, w_b, w_out, x_norm_w, mem_norm_w, wq_x, wk_x, wv_x, wo_x,
           mlp_norm_w, w_up, w_down, out_norm_w, batch, seq):
    bf = lambda a: a.astype(BF16)
    blk = W_IN_BLOCK
    kv_block = W_IN_KV // blk
    gate_shift = NSA_HEADS * 3

    n = _rmsnorm(h, attn_norm_w, BF16)
    wt = w_in.T
    skip_kv = lambda j: j + (j >= kv_block).astype(jnp.int32)
    za = _in_proj(n, wt, skip_kv, 4, BF16, name="in_proj_a")
    kv = _in_proj(n, wt, lambda j: kv_block, 1, BF16, regroup=CMP_STRIDE, name="in_proj_kv")
    zb = _in_proj(n, wt, lambda j: W_IN_NSA_GATE // blk + j, B_WIDTH // blk, F32, shift=gate_shift,
                  name="in_proj_b")
    zg = _in_proj(n, wt, lambda j: W_IN_NSA_GATE // LANES, 1, F32, tn=LANES, name="in_proj_gate")

    assert seq // CMP_STRIDE == LANES
    rows_kv = NSA_GROUPS * batch * LANES
    k2 = kv.reshape(2 * rows_kv, CMP_STRIDE * NSA_DK)
    kc = _compress(k2, 0, rows_kv, cmp_pe_k.reshape(1, -1), bf(cmp_w1_k), bf(cmp_w2_k))
    vc = _compress(k2, rows_kv, rows_kv, cmp_pe_v.reshape(1, -1), bf(cmp_w1_v), bf(cmp_w2_v))

    o_nsa = _nsa_attention(za, zg, kc, vc, batch, seq)
    o_ret = _retention(zb, ret_gn_w, batch, seq)
    h1 = _merge(o_nsa, o_ret, zb, h, bf(w_a), bf(w_b), bf(w_out))

    kx, vx = _mem_kv(mem.reshape(batch * MEM_LEN, D_MODEL), mem_norm_w, bf(wk_x), bf(wv_x))
    h2, nm = _cross_attention(h1, x_norm_w, mlp_norm_w, bf(wq_x), kx, vx, bf(wo_x), seq)
    return _mlp(nm, bf(w_up), bf(w_down), h2, out_norm_w)


def kernel(x, mem, attn_norm_w, w_in, cmp_pe_k, cmp_w1_k, cmp_w2_k, cmp_pe_v, cmp_w1_v, cmp_w2_v, w_a, ret_gn_w,
           w_b, w_out, x_norm_w, mem_norm_w, wq_x, wk_x, wv_x, wo_x, mlp_norm_w, w_up, w_down, final_norm_w):
    batch, seq, d = x.shape
    depth = w_in.shape[0]
    assert depth == 1
    h = x.reshape(batch * seq, d)
    out = _layer(h, mem, attn_norm_w[0], w_in[0], cmp_pe_k[0], cmp_w1_k[0], cmp_w2_k[0],
                 cmp_pe_v[0], cmp_w1_v[0], cmp_w2_v[0], w_a[0], ret_gn_w[0], w_b[0], w_out[0],
                 x_norm_w[0], mem_norm_w[0], wq_x[0], wk_x[0], wv_x[0], wo_x[0],
                 mlp_norm_w[0], w_up[0], w_down[0], final_norm_w, batch, seq)
    return out.reshape(batch, seq, d)
```

```python
import functools

import jax
import jax.numpy as jnp
import numpy as np
from jax import lax
from jax.experimental import pallas as pl
from jax.experimental.pallas import tpu as pltpu

F32 = jnp.float32
BF16 = jnp.bfloat16

D_MODEL = 2048
MEM_LEN = 256
NSA_HEADS = 16
NSA_GROUPS = 4
NSA_REP = NSA_HEADS // NSA_GROUPS
NSA_DK = 128
NSA_DV = 128
CMP_LEN = 32
CMP_STRIDE = 16
CMP_HIDDEN = 1024
SEL_LEN = 64
SEL_TOPK = 16
WIN = 512
RET_HEADS = 8
RET_DK = 128
RET_DV = 256
RET_CHUNK = 128
ROPE_BASE = 10000.0
X_HEADS = 4
X_DH = 128
D_FF = 4 * D_MODEL
EPS = 1e-6
NEG = -1e30
LOG2E = 1.4426950408889634

LANES = 128

W_IN_BLOCK = 1024
W_IN_KV = 2048
W_IN_NSA_GATE = 5120
A_QN = 0
A_KS = 2048
A_VS = 2560
A_KW = 3072
A_VW = 3584
B_QR = 0
B_KR = 1024
B_VR = 2048
B_GR = 4096
B_GA = 6144
B_GB = 8192
B_WIDTH = 10240

NT_DIMS = (((1,), (1,)), ((), ()))


def _params(sem, vmem_mb=None):
    kw = {"dimension_semantics": sem}
    if vmem_mb is not None:
        kw["vmem_limit_bytes"] = vmem_mb * 1024 * 1024
    return pltpu.CompilerParams(**kw)


def _rms(x, w):
    return x * lax.rsqrt(jnp.mean(x * x, axis=-1, keepdims=True) + EPS) * w


def _dot(a, b):
    return jnp.dot(a, b, preferred_element_type=F32)


def _dot_nt(a, b):
    return lax.dot_general(a, b, NT_DIMS, preferred_element_type=F32)


def _rmsnorm_kernel(x_ref, w_ref, o_ref):
    o_ref[...] = _rms(x_ref[...].astype(F32), w_ref[...]).astype(o_ref.dtype)


def _rmsnorm(x, w, out_dtype, tm=512):
    m, d = x.shape
    return pl.pallas_call(
        _rmsnorm_kernel,
        grid=(m // tm,),
        in_specs=[pl.BlockSpec((tm, d), lambda i: (i, 0)), pl.BlockSpec((1, d), lambda i: (0, 0))],
        out_specs=pl.BlockSpec((tm, d), lambda i: (i, 0)),
        out_shape=jax.ShapeDtypeStruct((m, d), out_dtype),
        compiler_params=_params(("parallel",)),
        name="rmsnorm",
    )(x, w.reshape(1, d))


def _mm_kernel(a_ref, b_ref, o_ref):
    o_ref[...] = _dot(a_ref[...], b_ref[...]).astype(o_ref.dtype)


def _matmul(a, b, out_dtype, tm=512, tn=512, name="matmul"):
    m, k = a.shape
    n = b.shape[1]
    return pl.pallas_call(
        _mm_kernel,
        grid=(m // tm, n // tn),
        in_specs=[pl.BlockSpec((tm, k), lambda i, j: (i, 0)), pl.BlockSpec((k, tn), lambda i, j: (0, j))],
        out_specs=pl.BlockSpec((tm, tn), lambda i, j: (i, j)),
        out_shape=jax.ShapeDtypeStruct((m, n), out_dtype),
        compiler_params=_params(("parallel", "arbitrary"), 40),
        name=name,
    )(a, b)


CAST_ROWS = 16


def _cast_rows(dst_ref, dst0, src_ref, src0, nrows):
    def body(r, carry):
        off = r * CAST_ROWS
        dst_ref[pl.ds(pl.multiple_of(dst0 + off, CAST_ROWS), CAST_ROWS), :] = (
            src_ref[pl.ds(pl.multiple_of(src0 + off, 8), CAST_ROWS), :].astype(BF16))
        return carry

    lax.fori_loop(0, nrows // CAST_ROWS, body, 0)


def _in_proj_kernel(n_ref, w_ref, wnext_ref, o_ref, wb_sc, *rest, shift, regroup):
    tn = wb_sc.shape[0]

    @pl.when(pl.program_id(1) == 0)
    def _():
        _cast_rows(wb_sc, 0, w_ref, shift, tn - shift)
        if shift:
            _cast_rows(wb_sc, tn - shift, wnext_ref, 0, shift)

    res = _dot_nt(n_ref[...], wb_sc[...])
    if regroup:
        (r_sc,) = rest
        tm = res.shape[0]
        for c in range(tn // LANES):
            r_sc[c] = res[:, c * LANES:(c + 1) * LANES]
        for c in range(tn // LANES):
            for j in range(regroup):
                o_ref[c, :, j * LANES:(j + 1) * LANES] = (
                    r_sc[c, pl.ds(j, tm // regroup, stride=regroup), :].astype(o_ref.dtype))
    else:
        o_ref[...] = res.astype(o_ref.dtype)


def _in_proj(n, wt, src_block, n_blocks, out_dtype, *, shift=0, regroup=0, tm=1024, tn=1024, name):
    m, k = n.shape
    per = tn // LANES
    assert shift % CAST_ROWS == 0 and shift <= LANES
    scratch = [pltpu.VMEM((tn, k), BF16)]
    if regroup:
        assert n_blocks == 1
        out_shape = jax.ShapeDtypeStruct((per, m // regroup, regroup * LANES), out_dtype)
        out_spec = pl.BlockSpec((per, tm // regroup, regroup * LANES), lambda j, i: (0, i, 0))
        scratch.append(pltpu.VMEM((per, tm, LANES), F32))
    else:
        out_shape = jax.ShapeDtypeStruct((m, n_blocks * tn), out_dtype)
        out_spec = pl.BlockSpec((tm, tn), lambda j, i: (i, j))
    return pl.pallas_call(
        functools.partial(_in_proj_kernel, shift=shift, regroup=regroup),
        grid=(n_blocks, m // tm),
        in_specs=[
            pl.BlockSpec((tm, k), lambda j, i: (i, 0)),
            pl.BlockSpec((tn, k), lambda j, i: (src_block(j), 0)),
            pl.BlockSpec((LANES, k), lambda j, i: ((src_block(j) + 1) * per, 0)),
        ],
        out_specs=out_spec,
        out_shape=out_shape,
        scratch_shapes=scratch,
        compiler_params=_params(("parallel", "arbitrary"), 56),
        name=name,
    )(n, wt, wt)


def _compress_kernel(k2_ref, pe_ref, w1_ref, w2_ref, o_ref):
    rows, half = k2_ref.shape
    k2 = k2_ref[...].astype(F32)
    a_lo = (k2 + pe_ref[:, :half]).astype(BF16)
    a_hi = (k2 + pe_ref[:, half:]).astype(BF16)
    lo = _dot(a_lo, w1_ref[:half, :])
    hi = _dot(a_hi, w1_ref[half:, :])
    h = lo + pltpu.roll(hi, rows - 1, axis=0)
    act = h * jax.nn.sigmoid(h)
    o_ref[...] = _dot(act.astype(BF16), w2_ref[...]).astype(o_ref.dtype)


def _compress(k2, first_row, m, pe, w1, w2, rows=256):
    half = k2.shape[1]
    hid = w1.shape[1]
    dout = w2.shape[1]
    first = first_row // rows
    return pl.pallas_call(
        _compress_kernel,
        grid=(m // rows,),
        in_specs=[
            pl.BlockSpec((rows, half), lambda i: (first + i, 0)),
            pl.BlockSpec((1, 2 * half), lambda i: (0, 0)),
            pl.BlockSpec((2 * half, hid), lambda i: (0, 0)),
            pl.BlockSpec((hid, dout), lambda i: (0, 0)),
        ],
        out_specs=pl.BlockSpec((rows, dout), lambda i: (i, 0)),
        out_shape=jax.ShapeDtypeStruct((m, dout), BF16),
        compiler_params=_params(("parallel",), 40),
        name="nsa_compress",
    )(k2, pe, w1, w2)


def _nsa_kernel(q_ref, kc_ref, vc_ref, ks_ref, vs_ref, kw_ref, vw_ref, g_ref, e_ref, o_ref,
                s_sc, mx_sc, acc_sc, po_sc, *, seq, tq, kblk):
    i = pl.program_id(2)
    t0 = i * tq
    rep = NSA_REP
    c1 = NSA_DK ** -0.5 * LOG2E
    n_cmp = (seq - CMP_LEN) // CMP_STRIDE + 1
    n_sel = seq // SEL_LEN
    topk = min(SEL_TOPK, n_sel)

    q = q_ref[...]
    q4 = jnp.concatenate([q[:, r * NSA_DK:(r + 1) * NSA_DK] for r in range(rep)], axis=0)
    tcol = t0 + lax.broadcasted_iota(jnp.int32, (tq, 1), 0)
    head = lambda r: slice(r * tq, (r + 1) * tq)
    per_group = rep * 3
    gs = pltpu.roll(jax.nn.sigmoid(g_ref[...]), lax.rem(LANES - pl.program_id(1) * per_group, LANES), axis=1)
    gate = lambda r, branch: gs[:, 3 * r + branch:3 * r + branch + 1]

    def masked(s, mask1):
        return jnp.concatenate([jnp.where(mask1, s[head(r)], NEG) for r in range(rep)], axis=0)

    def exp_rows(sm):
        return jnp.exp2((sm - jnp.max(sm, axis=-1, keepdims=True)) * c1)

    def with_ones(v):
        return jnp.concatenate([v, jnp.ones(v.shape, v.dtype)], axis=1)

    def normalised(ev):
        return ev[:, :NSA_DV] / ev[:, NSA_DV:]

    c_idx = lax.broadcasted_iota(jnp.int32, (tq, LANES), 1)
    mask_c = ((c_idx * CMP_STRIDE + (CMP_LEN - 1)) <= tcol) & (c_idx < n_cmp)
    e = exp_rows(masked(_dot_nt(q4, kc_ref[...]), mask_c))
    p = e / jnp.sum(e, axis=-1, keepdims=True)
    mask_cf = jnp.where(mask_c, 1.0, 0.0)
    p = jnp.concatenate([p[head(r)] * mask_cf for r in range(rep)], axis=0)
    o_cmp = _dot(p.astype(BF16), vc_ref[...])
    psum = p[head(0)]
    for r in range(1, rep):
        psum = psum + p[head(r)]

    wlen = WIN + tq
    start = pl.multiple_of(jnp.maximum(i - WIN // tq, 0) * tq, tq)
    dlt = tcol - (start + lax.broadcasted_iota(jnp.int32, (tq, wlen), 1))
    mask_w = (dlt >= 0) & (dlt < WIN)
    kw = kw_ref[pl.ds(start, wlen), :]
    vw1 = with_ones(vw_ref[pl.ds(start, wlen), :])
    for r in range(rep):
        e = exp_rows(jnp.where(mask_w, _dot_nt(q4[head(r)], kw), NEG))
        o_win = normalised(_dot(e.astype(BF16), vw1))
        po_sc[head(r), :] = gate(r, 0) * o_cmp[head(r)] + gate(r, 2) * o_win

    jo = lax.broadcasted_iota(jnp.int32, (n_sel, LANES), 0)
    co = lax.broadcasted_iota(jnp.int32, (n_sel, LANES), 1)
    ov_t = jnp.where((co * CMP_STRIDE < jo * SEL_LEN + SEL_LEN) & (co * CMP_STRIDE + CMP_LEN > jo * SEL_LEN)
                     & (co < n_cmp), 1.0, 0.0).astype(BF16)
    p_hi = psum.astype(BF16)
    p_lo = (psum - p_hi.astype(F32)).astype(BF16)
    imp = _dot_nt(ov_t, p_hi) + _dot_nt(ov_t, p_lo)
    j_idx = lax.broadcasted_iota(jnp.int32, (n_sel, tq), 0)
    cur = lax.shift_right_logical(t0 + lax.broadcasted_iota(jnp.int32, (n_sel, tq), 1), int(np.log2(SEL_LEN)))
    forced = (j_idx == 0) | (j_idx == cur) | (j_idx == cur - 1)
    impm = jnp.where(forced, jnp.inf, jnp.where(j_idx > cur, -jnp.inf, imp))
    rank = jnp.zeros((n_sel, tq), F32)
    for ii in range(n_sel):
        row = impm[ii:ii + 1, :]
        beats = (row > impm) | ((row == impm) & (j_idx > ii))
        rank = rank + jnp.where(beats, 1.0, 0.0)
    sel_t = jnp.where(rank < topk, 1.0, 0.0)
    sel = jnp.concatenate([sel_t, jnp.zeros((LANES - n_sel, tq), F32)], axis=0).T.astype(BF16)

    n_chunks = lax.div(t0 + tq + (kblk - 1), kblk)

    def lane_fold(x, op):
        out = x[:, :LANES]
        for c in range(1, kblk // LANES):
            out = op(out, x[:, c * LANES:(c + 1) * LANES])
        return out

    def score_chunk(kb):
        k0 = kb * kblk if isinstance(kb, int) else pl.multiple_of(kb * kblk, kblk)
        k = ks_ref[pl.ds(k0, kblk), :]
        in_sel = _dot(sel, e_ref[kb])
        tk = k0 + lax.broadcasted_iota(jnp.int32, (tq, kblk), 1)
        mask1 = (in_sel > 0.5) & (tk <= tcol)
        folds = []
        for r in range(rep):
            sm_ = jnp.where(mask1, _dot_nt(q4[head(r)], k), NEG)
            s_sc[kb, head(r), :] = sm_
            folds.append(lane_fold(sm_, jnp.maximum))
        return jnp.concatenate(folds, axis=0)

    def score_pass(kb, carry):
        mx_sc[...] = jnp.maximum(mx_sc[...], score_chunk(kb))
        return carry

    mx_sc[...] = score_chunk(0)
    lax.fori_loop(1, n_chunks, score_pass, 0)
    m_sel = jnp.max(mx_sc[...], axis=-1, keepdims=True)

    def value_chunk(kb):
        k0 = kb * kblk if isinstance(kb, int) else pl.multiple_of(kb * kblk, kblk)
        v1 = with_ones(vs_ref[pl.ds(k0, kblk), :])
        pvs = []
        for r in range(rep):
            pk = jnp.exp2((s_sc[kb, head(r), :] - m_sel[head(r)]) * c1)
            pvs.append(_dot(pk.astype(BF16), v1))
        return jnp.concatenate(pvs, axis=0)

    def value_pass(kb, carry):
        acc_sc[...] += value_chunk(kb)
        return carry

    acc_sc[...] = value_chunk(0)
    lax.fori_loop(1, n_chunks, value_pass, 0)
    o_sel = normalised(acc_sc[...])

    for r in range(rep):
        o = po_sc[head(r), :] + gate(r, 1) * o_sel[head(r)]
        o_ref[:, r * NSA_DV:(r + 1) * NSA_DV] = o.astype(o_ref.dtype)


def _nsa_attention(za, zg, kc, vc, batch, seq, tq=256, kblk=512):
    assert seq % kblk == 0 and seq >= WIN + tq and WIN % tq == 0
    nq = seq // tq
    gw = NSA_REP * NSA_DK
    kern = functools.partial(_nsa_kernel, seq=seq, tq=tq, kblk=kblk)
    key = np.arange(seq).reshape(seq // kblk, 1, kblk)
    expand = jnp.asarray(key // SEL_LEN == np.arange(LANES).reshape(1, LANES, 1), BF16)

    def kv_spec(off):
        return pl.BlockSpec((seq, LANES), lambda b, g, i: (b, off // LANES + g))

    return pl.pallas_call(
        kern,
        grid=(batch, NSA_GROUPS, nq),
        in_specs=[
            pl.BlockSpec((tq, gw), lambda b, g, i: (b * nq + i, A_QN // gw + g)),
            pl.BlockSpec((LANES, NSA_DK), lambda b, g, i: (g * batch + b, 0)),
            pl.BlockSpec((LANES, NSA_DV), lambda b, g, i: (g * batch + b, 0)),
            kv_spec(A_KS), kv_spec(A_VS), kv_spec(A_KW), kv_spec(A_VW),
            pl.BlockSpec((tq, LANES), lambda b, g, i: (b * nq + i, 0)),
            pl.BlockSpec(expand.shape, lambda b, g, i: (0, 0, 0)),
        ],
        out_specs=pl.BlockSpec((tq, gw), lambda b, g, i: (b * nq + i, g)),
        out_shape=jax.ShapeDtypeStruct((batch * seq, NSA_HEADS * NSA_DV), BF16),
        scratch_shapes=[
            pltpu.VMEM((seq // kblk, NSA_REP * tq, kblk), F32),
            pltpu.VMEM((NSA_REP * tq, LANES), F32),
            pltpu.VMEM((NSA_REP * tq, 2 * NSA_DV), F32),
            pltpu.VMEM((NSA_REP * tq, NSA_DV), F32),
        ],
        compiler_params=_params(("parallel", "parallel", "arbitrary"), 40),
        name="nsa_attention",
    )(za, kc, vc, za, za, za, za, zg, expand)


def _retention_kernel(q_ref, k_ref, v_ref, g_ref, cos_ref, sin_e_ref, sin_o_ref, dec_ref, wq_ref, wk_ref,
                      gc_ref, gn_ref, o_ref, st_ref):
    @pl.when(pl.program_id(1) == 0)
    def _():
        st_ref[...] = jnp.zeros(st_ref.shape, F32)

    cos = cos_ref[...]
    sin_e = sin_e_ref[...]
    sin_o = sin_o_ref[...]

    def rotate(x):
        return x * cos + pltpu.roll(x, RET_DK - 1, axis=1) * sin_e + pltpu.roll(x, 1, axis=1) * sin_o

    for h in range(RET_HEADS):
        qf = rotate(q_ref[:, h * RET_DK:(h + 1) * RET_DK])
        kf = rotate(k_ref[:, h * RET_DK:(h + 1) * RET_DK]) * (RET_DK ** -0.5)
        qb = qf.astype(BF16)
        cols = slice(h * RET_DV, (h + 1) * RET_DV)
        v = v_ref[:, cols].astype(BF16)
        s = _dot_nt(qb, kf.astype(BF16)) * dec_ref[h]
        st = st_ref[h]
        wq = wq_ref[h]
        o = _dot(s.astype(BF16), v) + _dot(qb, st.astype(BF16)) * jnp.concatenate([wq, wq], axis=1)
        kv = _dot((kf * wk_ref[h]).T.astype(BF16), v)
        st_ref[h] = st * gc_ref[h] + kv
        mu = jnp.mean(o, axis=-1, keepdims=True)
        d = o - mu
        var = jnp.mean(d * d, axis=-1, keepdims=True)
        on = d * lax.rsqrt(var + EPS) * gn_ref[:, cols]
        gr = g_ref[:, cols]
        o_ref[:, cols] = (gr * jax.nn.sigmoid(gr) * on).astype(o_ref.dtype)


def _retention(zb, gn_w, batch, seq):
    c = RET_CHUNK
    nc = seq // c
    hq = RET_HEADS * RET_DK
    hv = RET_HEADS * RET_DV
    f32 = np.float32
    inv = f32(ROPE_BASE) ** (-np.arange(0, RET_DK, 2, dtype=f32) / f32(RET_DK))
    ang = np.arange(seq, dtype=f32)[:, None] * inv[None, :]
    zero = np.zeros_like(ang)
    pairs = lambda even, odd: np.stack([even, odd], axis=-1).reshape(seq, RET_DK)
    cos = pairs(np.cos(ang), np.cos(ang))
    sin_e = pairs(-np.sin(ang), zero)
    sin_o = pairs(zero, np.sin(ang))
    log_g = np.log1p(-np.exp2(f32(-5.0) - np.arange(RET_HEADS, dtype=f32)))
    idx = np.arange(c, dtype=f32)
    rel = idx[:, None] - idx[None, :]
    decay = np.where(rel >= 0, np.exp(log_g[:, None, None] * np.maximum(rel, f32(0.0))), f32(0.0)).astype(f32)
    lanes = lambda a: np.ascontiguousarray(np.broadcast_to(a[:, :, None], (RET_HEADS, c, RET_DK)), dtype=f32)
    w_k = lanes(np.exp(log_g[:, None] * (f32(c - 1) - idx)[None, :]))
    w_q = lanes(np.exp(log_g[:, None] * (idx + f32(1.0))[None, :]))
    g_chunk = np.broadcast_to(np.exp(log_g * f32(c))[:, None, None], (RET_HEADS, 1, RET_DV)).astype(f32)

    row = lambda b, n: b * nc + n
    return pl.pallas_call(
        _retention_kernel,
        grid=(batch, nc),
        in_specs=[
            pl.BlockSpec((c, hq), lambda b, n: (row(b, n), B_QR // hq)),
            pl.BlockSpec((c, hq), lambda b, n: (row(b, n), B_KR // hq)),
            pl.BlockSpec((c, hv), lambda b, n: (row(b, n), B_VR // hv)),
            pl.BlockSpec((c, hv), lambda b, n: (row(b, n), B_GR // hv)),
            pl.BlockSpec((c, RET_DK), lambda b, n: (n, 0)),
            pl.BlockSpec((c, RET_DK), lambda b, n: (n, 0)),
            pl.BlockSpec((c, RET_DK), lambda b, n: (n, 0)),
            pl.BlockSpec((RET_HEADS, c, c), lambda b, n: (0, 0, 0)),
            pl.BlockSpec((RET_HEADS, c, RET_DK), lambda b, n: (0, 0, 0)),
            pl.BlockSpec((RET_HEADS, c, RET_DK), lambda b, n: (0, 0, 0)),
            pl.BlockSpec((RET_HEADS, 1, RET_DV), lambda b, n: (0, 0, 0)),
            pl.BlockSpec((1, hv), lambda b, n: (0, 0)),
        ],
        out_specs=pl.BlockSpec((c, hv), lambda b, n: (row(b, n), 0)),
        out_shape=jax.ShapeDtypeStruct((batch * seq, hv), BF16),
        scratch_shapes=[pltpu.VMEM((RET_HEADS, RET_DK, RET_DV), F32)],
        compiler_params=_params(("parallel", "arbitrary"), 40),
        name="retention",
    )(zb, zb, zb, zb, cos, sin_e, sin_o, decay, w_q, w_k, g_chunk, gn_w.reshape(1, hv))


def _merge_kernel(on_ref, or_ref, ga_ref, gb_ref, x_ref, wa_ref, wb_ref, wo_ref, h_ref):
    a = _dot(on_ref[...], wa_ref[...])
    b = _dot(or_ref[...], wb_ref[...])
    merged = jax.nn.sigmoid(ga_ref[...]) * a + jax.nn.sigmoid(gb_ref[...]) * b
    h_ref[...] = x_ref[...] + _dot(merged.astype(BF16), wo_ref[...])


def _merge(o_nsa, o_ret, zb, x, w_a, w_b, w_out, tm=256):
    m, d = x.shape
    resident = lambda shape: pl.BlockSpec(shape, lambda i: (0, 0), pipeline_mode=pl.Buffered(1))
    return pl.pallas_call(
        _merge_kernel,
        grid=(m // tm,),
        in_specs=[
            pl.BlockSpec((tm, d), lambda i: (i, 0)),
            pl.BlockSpec((tm, d), lambda i: (i, 0)),
            pl.BlockSpec((tm, d), lambda i: (i, B_GA // d)),
            pl.BlockSpec((tm, d), lambda i: (i, B_GB // d)),
            pl.BlockSpec((tm, d), lambda i: (i, 0)),
            resident(w_a.shape), resident(w_b.shape), resident(w_out.shape),
        ],
        out_specs=pl.BlockSpec((tm, d), lambda i: (i, 0)),
        out_shape=jax.ShapeDtypeStruct((m, d), F32),
        compiler_params=_params(("parallel",), 56),
        name="merge_out_proj",
    )(o_nsa, o_ret, zb, zb, x, w_a, w_b, w_out)


def _mem_kv_kernel(m_ref, nw_ref, wk_ref, wv_ref, k_ref, v_ref):
    mn = _rms(m_ref[...], nw_ref[...]).astype(BF16)
    k_ref[...] = _dot(mn, wk_ref[...]).astype(k_ref.dtype)
    v_ref[...] = _dot(mn, wv_ref[...]).astype(v_ref.dtype)


def _mem_kv(mem2, nw, wk, wv, tm=256):
    m, d = mem2.shape
    n = wk.shape[1]
    out = jax.ShapeDtypeStruct((m, n), BF16)
    return pl.pallas_call(
        _mem_kv_kernel,
        grid=(m // tm,),
        in_specs=[
            pl.BlockSpec((tm, d), lambda i: (i, 0)),
            pl.BlockSpec((1, d), lambda i: (0, 0)),
            pl.BlockSpec((d, n), lambda i: (0, 0)),
            pl.BlockSpec((d, n), lambda i: (0, 0)),
        ],
        out_specs=[pl.BlockSpec((tm, n), lambda i: (i, 0))] * 2,
        out_shape=[out, out],
        compiler_params=_params(("parallel",), 40),
        name="mem_kv_proj",
    )(mem2, nw.reshape(1, d), wk, wv)


def _cross_kernel(h_ref, xw_ref, mw_ref, wq_ref, kx_ref, vx_ref, wo_ref, h2_ref, nm_ref):
    h = h_ref[...]
    nx = _rms(h, xw_ref[...]).astype(BF16)
    qx = _dot(nx, wq_ref[...]).astype(BF16)
    outs = []
    for hh in range(X_HEADS):
        cols = slice(hh * X_DH, (hh + 1) * X_DH)
        s = _dot_nt(qx[:, cols], kx_ref[:, cols]) * (X_DH ** -0.5)
        e = jnp.exp(s - jnp.max(s, axis=-1, keepdims=True))
        p = e / jnp.sum(e, axis=-1, keepdims=True)
        outs.append(_dot(p.astype(BF16), vx_ref[:, cols]))
    ox = jnp.concatenate(outs, axis=-1).astype(BF16)
    h2 = h + _dot(ox, wo_ref[...])
    h2_ref[...] = h2
    nm_ref[...] = _rms(h2, mw_ref[...]).astype(nm_ref.dtype)


def _cross_attention(h1, x_norm_w, mlp_norm_w, wq, kx, vx, wo, seq, tm=512):
    m, d = h1.shape
    n = wq.shape[1]
    per_batch = seq // tm
    vec = lambda: pl.BlockSpec((1, d), lambda i: (0, 0))
    return pl.pallas_call(
        _cross_kernel,
        grid=(m // tm,),
        in_specs=[
            pl.BlockSpec((tm, d), lambda i: (i, 0)),
            vec(), vec(),
            pl.BlockSpec((d, n), lambda i: (0, 0)),
            pl.BlockSpec((MEM_LEN, n), lambda i: (i // per_batch, 0)),
            pl.BlockSpec((MEM_LEN, n), lambda i: (i // per_batch, 0)),
            pl.BlockSpec((n, d), lambda i: (0, 0)),
        ],
        out_specs=[pl.BlockSpec((tm, d), lambda i: (i, 0))] * 2,
        out_shape=[jax.ShapeDtypeStruct((m, d), F32), jax.ShapeDtypeStruct((m, d), BF16)],
        compiler_params=_params(("parallel",), 40),
        name="cross_attention",
    )(h1, x_norm_w.reshape(1, d), mlp_norm_w.reshape(1, d), wq, kx, vx, wo)


def _mlp_kernel(nm_ref, wu_ref, wd_ref, h_ref, fw_ref, o_ref, acc_ref):
    j = pl.program_id(1)

    @pl.when(j == 0)
    def _():
        acc_ref[...] = jnp.zeros(acc_ref.shape, F32)

    u = jnp.maximum(_dot(nm_ref[...], wu_ref[...]), 0.0)
    acc_ref[...] += _dot((u * u).astype(BF16), wd_ref[...])

    @pl.when(j == pl.num_programs(1) - 1)
    def _():
        o_ref[...] = _rms(h_ref[...] + acc_ref[...], fw_ref[...])


def _mlp(nm, w_up, w_down, h2, final_w, tm=512, tf=1024):
    m, d = nm.shape
    f = w_up.shape[1]
    return pl.pallas_call(
        _mlp_kernel,
        grid=(m // tm, f // tf),
        in_specs=[
            pl.BlockSpec((tm, d), lambda i, j: (i, 0)),
            pl.BlockSpec((d, tf), lambda i, j: (0, j)),
            pl.BlockSpec((tf, d), lambda i, j: (j, 0)),
            pl.BlockSpec((tm, d), lambda i, j: (i, 0)),
            pl.BlockSpec((1, d), lambda i, j: (0, 0)),
        ],
        out_specs=pl.BlockSpec((tm, d), lambda i, j: (i, 0)),
        out_shape=jax.ShapeDtypeStruct((m, d), F32),
        scratch_shapes=[pltpu.VMEM((tm, d), F32)],
        compiler_params=_params(("parallel", "arbitrary"), 56),
        name="mlp_final_norm",
    )(nm, w_up, w_down, h2, final_w.reshape(1, d))


def _layer(h, mem, attn_norm_w, w_in, cmp_pe_k, cmp_w1_k, cmp_w2_k, cmp_pe_v, cmp_w1_v, cmp_w2_v,
           w_a, ret_gn_w, w_b, w_out, x_norm_w, mem_norm_w, wq_x, wk_x, wv_x, wo_x,
           mlp_norm_w, w_up, w_down, out_norm_w, batch, seq):
    bf = lambda a: a.astype(BF16)
    blk = W_IN_BLOCK
    kv_block = W_IN_KV // blk
    gate_shift = NSA_HEADS * 3

    n = _rmsnorm(h, attn_norm_w, BF16)
    wt = w_in.T
    skip_kv = lambda j: j + (j >= kv_block).astype(jnp.int32)
    za = _in_proj(n, wt, skip_kv, 4, BF16, name="in_proj_a")
    kv = _in_proj(n, wt, lambda j: kv_block, 1, BF16, regroup=CMP_STRIDE, name="in_proj_kv")
    zb = _in_proj(n, wt, lambda j: W_IN_NSA_GATE // blk + j, B_WIDTH // blk, F32, shift=gate_shift,
                  name="in_proj_b")
    zg = _in_proj(n, wt, lambda j: W_IN_NSA_GATE // LANES, 1, F32, tn=LANES, name="in_proj_gate")

    assert seq // CMP_STRIDE == LANES
    rows_kv = NSA_GROUPS * batch * LANES
    k2 = kv.reshape(2 * rows_kv, CMP_STRIDE * NSA_DK)
    kc = _compress(k2, 0, rows_kv, cmp_pe_k.reshape(1, -1), bf(cmp_w1_k), bf(cmp_w2_k))
    vc = _compress(k2, rows_kv, rows_kv, cmp_pe_v.reshape(1, -1), bf(cmp_w1_v), bf(cmp_w2_v))

    o_nsa = _nsa_attention(za, zg, kc, vc, batch, seq)
    o_ret = _retention(zb, ret_gn_w, batch, seq)
    h1 = _merge(o_nsa, o_ret, zb, h, bf(w_a), bf(w_b), bf(w_out))

    kx, vx = _mem_kv(mem.reshape(batch * MEM_LEN, D_MODEL), mem_norm_w, bf(wk_x), bf(wv_x))
    h2, nm = _cross_attention(h1, x_norm_w, mlp_norm_w, bf(wq_x), kx, vx, bf(wo_x), seq)
    return _mlp(nm, bf(w_up), bf(w_down), h2, out_norm_w)


def kernel(x, mem, attn_norm_w, w_in, cmp_pe_k, cmp_w1_k, cmp_w2_k, cmp_pe_v, cmp_w1_v, cmp_w2_v, w_a, ret_gn_w,
           w_b, w_out, x_norm_w, mem_norm_w, wq_x, wk_x, wv_x, wo_x, mlp_norm_w, w_up, w_down, final_norm_w):
    batch, seq, d = x.shape
    depth = w_in.shape[0]
    assert depth == 1
    h = x.reshape(batch * seq, d)
    out = _layer(h, mem, attn_norm_w[0], w_in[0], cmp_pe_k[0], cmp_w1_k[0], cmp_w2_k[0],
                 cmp_pe_v[0], cmp_w1_v[0], cmp_w2_v[0], w_a[0], ret_gn_w[0], w_b[0], w_out[0],
                 x_norm_w[0], mem_norm_w[0], wq_x[0], wk_x[0], wv_x[0], wo_x[0],
                 mlp_norm_w[0], w_up[0], w_down[0], final_norm_w, batch, seq)
    return out.reshape(batch, seq, d)
```

```python
import functools

import jax
import jax.numpy as jnp
import numpy as np
from jax import lax
from jax.experimental import pallas as pl
from jax.experimental.pallas import tpu as pltpu

F32 = jnp.float32
BF16 = jnp.bfloat16

D_MODEL = 2048
MEM_LEN = 256
NSA_HEADS = 16
NSA_GROUPS = 4
NSA_REP = NSA_HEADS // NSA_GROUPS
NSA_DK = 128
NSA_DV = 128
CMP_LEN = 32
CMP_STRIDE = 16
CMP_HIDDEN = 1024
SEL_LEN = 64
SEL_TOPK = 16
WIN = 512
RET_HEADS = 8
RET_DK = 128
RET_DV = 256
RET_CHUNK = 128
ROPE_BASE = 10000.0
X_HEADS = 4
X_DH = 128
D_FF = 4 * D_MODEL
EPS = 1e-6
NEG = -1e30
LOG2E = 1.4426950408889634

LANES = 128
MXU_ROWS = 256

W_IN_BLOCK = 1024
W_IN_KV = 2048
W_IN_NSA_GATE = 5120
A_QN = 0
A_KS = 2048
A_VS = 2560
A_KW = 3072
A_VW = 3584
B_QR = 0
B_KR = 1024
B_VR = 2048
B_GR = 4096
B_GA = 6144
B_GB = 8192
B_WIDTH = 10240

NT_DIMS = (((1,), (1,)), ((), ()))


def _params(sem, vmem_mb=None):
    kw = {"dimension_semantics": sem}
    if vmem_mb is not None:
        kw["vmem_limit_bytes"] = vmem_mb * 1024 * 1024
    return pltpu.CompilerParams(**kw)


def _rms(x, w):
    return x * lax.rsqrt(jnp.mean(x * x, axis=-1, keepdims=True) + EPS) * w


def _dot(a, b):
    return jnp.dot(a, b, preferred_element_type=F32)


def _dot_nt(a, b):
    return lax.dot_general(a, b, NT_DIMS, preferred_element_type=F32)


def _rmsnorm_kernel(x_ref, w_ref, o_ref):
    o_ref[...] = _rms(x_ref[...].astype(F32), w_ref[...]).astype(o_ref.dtype)


def _rmsnorm(x, w, out_dtype, tm=512):
    m, d = x.shape
    return pl.pallas_call(
        _rmsnorm_kernel,
        grid=(m // tm,),
        in_specs=[pl.BlockSpec((tm, d), lambda i: (i, 0)), pl.BlockSpec((1, d), lambda i: (0, 0))],
        out_specs=pl.BlockSpec((tm, d), lambda i: (i, 0)),
        out_shape=jax.ShapeDtypeStruct((m, d), out_dtype),
        compiler_params=_params(("parallel",)),
        name="rmsnorm",
    )(x, w.reshape(1, d))


def _mm_kernel(a_ref, b_ref, o_ref):
    o_ref[...] = _dot(a_ref[...], b_ref[...]).astype(o_ref.dtype)


def _matmul(a, b, out_dtype, tm=512, tn=512, name="matmul"):
    m, k = a.shape
    n = b.shape[1]
    return pl.pallas_call(
        _mm_kernel,
        grid=(m // tm, n // tn),
        in_specs=[pl.BlockSpec((tm, k), lambda i, j: (i, 0)), pl.BlockSpec((k, tn), lambda i, j: (0, j))],
        out_specs=pl.BlockSpec((tm, tn), lambda i, j: (i, j)),
        out_shape=jax.ShapeDtypeStruct((m, n), out_dtype),
        compiler_params=_params(("parallel", "arbitrary"), 40),
        name=name,
    )(a, b)


CAST_ROWS = 16


def _cast_rows(dst_ref, dst0, src_ref, src0, nrows):
    def body(r, carry):
        off = r * CAST_ROWS
        dst_ref[pl.ds(pl.multiple_of(dst0 + off, CAST_ROWS), CAST_ROWS), :] = (
            src_ref[pl.ds(pl.multiple_of(src0 + off, 8), CAST_ROWS), :].astype(BF16))
        return carry

    lax.fori_loop(0, nrows // CAST_ROWS, body, 0)


def _in_proj_kernel(n_ref, w_ref, wnext_ref, o_ref, wb_sc, *rest, shift, regroup):
    tn = wb_sc.shape[0]

    @pl.when(pl.program_id(1) == 0)
    def _():
        _cast_rows(wb_sc, 0, w_ref, shift, tn - shift)
        if shift:
            _cast_rows(wb_sc, tn - shift, wnext_ref, 0, shift)

    res = _dot_nt(n_ref[...], wb_sc[...])
    if regroup:
        (r_sc,) = rest
        tm = res.shape[0]
        for c in range(tn // LANES):
            r_sc[c] = res[:, c * LANES:(c + 1) * LANES]
        for c in range(tn // LANES):
            for j in range(regroup):
                o_ref[c, :, j * LANES:(j + 1) * LANES] = (
                    r_sc[c, pl.ds(j, tm // regroup, stride=regroup), :].astype(o_ref.dtype))
    else:
        o_ref[...] = res.astype(o_ref.dtype)


def _in_proj(n, wt, src_block, n_blocks, out_dtype, *, shift=0, regroup=0, tm=1024, tn=1024, name):
    m, k = n.shape
    per = tn // LANES
    assert shift % CAST_ROWS == 0 and shift <= LANES
    scratch = [pltpu.VMEM((tn, k), BF16)]
    if regroup:
        assert n_blocks == 1
        out_shape = jax.ShapeDtypeStruct((per, m // regroup, regroup * LANES), out_dtype)
        out_spec = pl.BlockSpec((per, tm // regroup, regroup * LANES), lambda j, i: (0, i, 0))
        scratch.append(pltpu.VMEM((per, tm, LANES), F32))
    else:
        out_shape = jax.ShapeDtypeStruct((m, n_blocks * tn), out_dtype)
        out_spec = pl.BlockSpec((tm, tn), lambda j, i: (i, j))
    return pl.pallas_call(
        functools.partial(_in_proj_kernel, shift=shift, regroup=regroup),
        grid=(n_blocks, m // tm),
        in_specs=[
            pl.BlockSpec((tm, k), lambda j, i: (i, 0)),
            pl.BlockSpec((tn, k), lambda j, i: (src_block(j), 0)),
            pl.BlockSpec((LANES, k), lambda j, i: ((src_block(j) + 1) * per, 0)),
        ],
        out_specs=out_spec,
        out_shape=out_shape,
        scratch_shapes=scratch,
        compiler_params=_params(("parallel", "arbitrary"), 56),
        name=name,
    )(n, wt, wt)


def _compress_kernel(k2_ref, pe_ref, w1_ref, w2_ref, o_ref):
    rows, half = k2_ref.shape
    k2 = k2_ref[...].astype(F32)
    a_lo = (k2 + pe_ref[:, :half]).astype(BF16)
    a_hi = (k2 + pe_ref[:, half:]).astype(BF16)
    lo = _dot(a_lo, w1_ref[:half, :])
    hi = _dot(a_hi, w1_ref[half:, :])
    h = lo + pltpu.roll(hi, rows - 1, axis=0)
    act = h * jax.nn.sigmoid(h)
    o_ref[...] = _dot(act.astype(BF16), w2_ref[...]).astype(o_ref.dtype)


def _compress(k2, first_row, m, pe, w1, w2, rows=256):
    half = k2.shape[1]
    hid = w1.shape[1]
    dout = w2.shape[1]
    first = first_row // rows
    return pl.pallas_call(
        _compress_kernel,
        grid=(m // rows,),
        in_specs=[
            pl.BlockSpec((rows, half), lambda i: (first + i, 0)),
            pl.BlockSpec((1, 2 * half), lambda i: (0, 0)),
            pl.BlockSpec((2 * half, hid), lambda i: (0, 0)),
            pl.BlockSpec((hid, dout), lambda i: (0, 0)),
        ],
        out_specs=pl.BlockSpec((rows, dout), lambda i: (i, 0)),
        out_shape=jax.ShapeDtypeStruct((m, dout), BF16),
        compiler_params=_params(("parallel",), 40),
        name="nsa_compress",
    )(k2, pe, w1, w2)


def _nsa_kernel(*refs, seq, tq, kblk, gp, hm):
    q_ref = refs[0]
    kc_refs = refs[1:1 + gp]
    vc_refs = refs[1 + gp:1 + 2 * gp]
    ks_ref, vs_ref, kw_ref, vw_ref, g_ref, e_ref, o_ref, s_sc, mx_sc, acc_sc, po_sc = refs[1 + 2 * gp:]
    i = pl.program_id(2)
    t0 = i * tq
    rep = NSA_REP
    c1 = NSA_DK ** -0.5 * LOG2E
    n_cmp = (seq - CMP_LEN) // CMP_STRIDE + 1
    n_sel = seq // SEL_LEN
    topk = min(SEL_TOPK, n_sel)
    groups = range(gp)

    q = q_ref[...]
    units = range(rep // hm)
    rows = lambda gg, r: slice((gg * rep + r) * tq, (gg * rep + r + 1) * tq)
    urows = lambda gg, u: slice((gg * rep + u * hm) * tq, (gg * rep + (u + 1) * hm) * tq)
    part = lambda x, h: x[h * tq:(h + 1) * tq]
    head_q = lambda gg, r: q[:, (gg * rep + r) * NSA_DK:(gg * rep + r + 1) * NSA_DK]
    unit_q = {(gg, u): jnp.concatenate([head_q(gg, u * hm + h) for h in range(hm)], axis=0)
              for gg in groups for u in units}
    group_cols = lambda ref, gg: ref.at[:, gg * LANES:(gg + 1) * LANES]

    def masked(s, mask1):
        return jnp.concatenate([jnp.where(mask1, part(s, h), NEG) for h in range(hm)], axis=0)
    tcol = t0 + lax.broadcasted_iota(jnp.int32, (tq, 1), 0)
    per_group = rep * 3
    g_sig = jax.nn.sigmoid(g_ref[...])
    first_group = pl.program_id(1) * gp
    gs = [pltpu.roll(g_sig, lax.rem(LANES - (first_group + gg) * per_group, LANES), axis=1) for gg in groups]
    gate = lambda gg, r, branch: gs[gg][:, 3 * r + branch:3 * r + branch + 1]

    def exp_rows(sm):
        return jnp.exp2((sm - jnp.max(sm, axis=-1, keepdims=True)) * c1)

    def with_ones(v):
        return jnp.concatenate([v, jnp.ones(v.shape, v.dtype)], axis=1)

    def normalised(ev):
        return ev[:, :NSA_DV] / ev[:, NSA_DV:]

    c_idx = lax.broadcasted_iota(jnp.int32, (tq, LANES), 1)
    mask_c = ((c_idx * CMP_STRIDE + (CMP_LEN - 1)) <= tcol) & (c_idx < n_cmp)
    mask_cf = jnp.where(mask_c, 1.0, 0.0)
    o_cmp, psum = {}, []
    for gg in groups:
        kc = kc_refs[gg][...]
        vc = vc_refs[gg][...]
        tot = None
        for u in units:
            e = exp_rows(masked(_dot_nt(unit_q[gg, u], kc), mask_c))
            p = e / jnp.sum(e, axis=-1, keepdims=True)
            p = jnp.concatenate([part(p, h) * mask_cf for h in range(hm)], axis=0)
            o_cmp[gg, u] = _dot(p.astype(BF16), vc)
            for h in range(hm):
                tot = part(p, h) if tot is None else tot + part(p, h)
        psum.append(tot)

    wlen = WIN + tq
    start = pl.multiple_of(jnp.maximum(i - WIN // tq, 0) * tq, tq)
    dlt = tcol - (start + lax.broadcasted_iota(jnp.int32, (tq, wlen), 1))
    mask_w = (dlt >= 0) & (dlt < WIN)
    for gg in groups:
        kw = group_cols(kw_ref, gg)[pl.ds(start, wlen), :]
        vw1 = with_ones(group_cols(vw_ref, gg)[pl.ds(start, wlen), :])
        for u in units:
            e = exp_rows(masked(_dot_nt(unit_q[gg, u], kw), mask_w))
            o_win = normalised(_dot(e.astype(BF16), vw1))
            for h in range(hm):
                r = u * hm + h
                po_sc[rows(gg, r), :] = (gate(gg, r, 0) * part(o_cmp[gg, u], h)
                                         + gate(gg, r, 2) * part(o_win, h))

    jo = lax.broadcasted_iota(jnp.int32, (n_sel, LANES), 0)
    co = lax.broadcasted_iota(jnp.int32, (n_sel, LANES), 1)
    ov_t = jnp.where((co * CMP_STRIDE < jo * SEL_LEN + SEL_LEN) & (co * CMP_STRIDE + CMP_LEN > jo * SEL_LEN)
                     & (co < n_cmp), 1.0, 0.0).astype(BF16)
    j_idx = lax.broadcasted_iota(jnp.int32, (n_sel, tq), 0)
    cur = lax.shift_right_logical(t0 + lax.broadcasted_iota(jnp.int32, (n_sel, tq), 1), int(np.log2(SEL_LEN)))
    forced = (j_idx == 0) | (j_idx == cur) | (j_idx == cur - 1)
    future = j_idx > cur
    sel = []
    for gg in groups:
        p_hi = psum[gg].astype(BF16)
        p_lo = (psum[gg] - p_hi.astype(F32)).astype(BF16)
        imp = _dot_nt(ov_t, p_hi) + _dot_nt(ov_t, p_lo)
        impm = jnp.where(forced, jnp.inf, jnp.where(future, -jnp.inf, imp))
        rank = jnp.zeros((n_sel, tq), F32)
        for ii in range(n_sel):
            row = impm[ii:ii + 1, :]
            beats = (row > impm) | ((row == impm) & (j_idx > ii))
            rank = rank + jnp.where(beats, 1.0, 0.0)
        sel_t = jnp.where(rank < topk, 1.0, 0.0)
        sel.append(jnp.concatenate([sel_t, jnp.zeros((LANES - n_sel, tq), F32)], axis=0).T.astype(BF16))

    n_chunks = lax.div(t0 + tq + (kblk - 1), kblk)

    def lane_fold_max(x):
        out = x[:, :LANES]
        for c in range(1, kblk // LANES):
            out = jnp.maximum(out, x[:, c * LANES:(c + 1) * LANES])
        return out

    def chunk_start(kb):
        return kb * kblk if isinstance(kb, int) else pl.multiple_of(kb * kblk, kblk)

    def score_chunk(kb):
        k0 = chunk_start(kb)
        causal = (k0 + lax.broadcasted_iota(jnp.int32, (tq, kblk), 1)) <= tcol
        folds = []
        for gg in groups:
            k = group_cols(ks_ref, gg)[pl.ds(k0, kblk), :]
            mask1 = (_dot(sel[gg], e_ref[kb]) > 0.5) & causal
            for u in units:
                sm_ = masked(_dot_nt(unit_q[gg, u], k), mask1)
                s_sc[kb, urows(gg, u), :] = sm_
                folds.append(lane_fold_max(sm_))
        return jnp.concatenate(folds, axis=0)

    def score_pass(kb, carry):
        mx_sc[...] = jnp.maximum(mx_sc[...], score_chunk(kb))
        return carry

    mx_sc[...] = score_chunk(0)
    lax.fori_loop(1, n_chunks, score_pass, 0)
    m_sel = jnp.max(mx_sc[...], axis=-1, keepdims=True)

    def value_chunk(kb):
        k0 = chunk_start(kb)
        pvs = []
        for gg in groups:
            v1 = with_ones(group_cols(vs_ref, gg)[pl.ds(k0, kblk), :])
            for u in units:
                pk = jnp.exp2((s_sc[kb, urows(gg, u), :] - m_sel[urows(gg, u)]) * c1)
                pvs.append(_dot(pk.astype(BF16), v1))
        return jnp.concatenate(pvs, axis=0)

    def value_pass(kb, carry):
        acc_sc[...] += value_chunk(kb)
        return carry

    acc_sc[...] = value_chunk(0)
    lax.fori_loop(1, n_chunks, value_pass, 0)

    for gg in groups:
        for r in range(rep):
            o = po_sc[rows(gg, r), :] + gate(gg, r, 1) * normalised(acc_sc[rows(gg, r), :])
            col = (gg * rep + r) * NSA_DV
            o_ref[:, col:col + NSA_DV] = o.astype(o_ref.dtype)


def _nsa_attention(za, zg, kc, vc, batch, seq, tq=256, kblk=512, gp=2):
    assert seq % kblk == 0 and seq >= WIN + tq and WIN % tq == 0 and NSA_GROUPS % gp == 0
    nq = seq // tq
    gw = gp * NSA_REP * NSA_DK
    streams = gp * NSA_REP * tq
    hm = max(1, MXU_ROWS // tq)
    assert NSA_REP % hm == 0
    kern = functools.partial(_nsa_kernel, seq=seq, tq=tq, kblk=kblk, gp=gp, hm=hm)
    key = np.arange(seq).reshape(seq // kblk, 1, kblk)
    expand = jnp.asarray(key // SEL_LEN == np.arange(LANES).reshape(1, LANES, 1), BF16)

    def kv_spec(off):
        return pl.BlockSpec((seq, gp * LANES), lambda b, g, i: (b, off // (gp * LANES) + g))

    def cmp_spec(gg):
        return pl.BlockSpec((LANES, NSA_DK), lambda b, g, i: ((g * gp + gg) * batch + b, 0))

    return pl.pallas_call(
        kern,
        grid=(batch, NSA_GROUPS // gp, nq),
        in_specs=[
            pl.BlockSpec((tq, gw), lambda b, g, i: (b * nq + i, A_QN // gw + g)),
            *[cmp_spec(gg) for gg in range(gp)],
            *[cmp_spec(gg) for gg in range(gp)],
            kv_spec(A_KS), kv_spec(A_VS), kv_spec(A_KW), kv_spec(A_VW),
            pl.BlockSpec((tq, LANES), lambda b, g, i: (b * nq + i, 0)),
            pl.BlockSpec(expand.shape, lambda b, g, i: (0, 0, 0)),
        ],
        out_specs=pl.BlockSpec((tq, gw), lambda b, g, i: (b * nq + i, g)),
        out_shape=jax.ShapeDtypeStruct((batch * seq, NSA_HEADS * NSA_DV), BF16),
        scratch_shapes=[
            pltpu.VMEM((seq // kblk, streams, kblk), F32),
            pltpu.VMEM((streams, LANES), F32),
            pltpu.VMEM((streams, 2 * NSA_DV), F32),
            pltpu.VMEM((streams, NSA_DV), F32),
        ],
        compiler_params=_params(("parallel", "parallel", "arbitrary"), 56),
        name="nsa_attention",
    )(za, *([kc] * gp), *([vc] * gp), za, za, za, za, zg, expand)


def _retention_kernel(q_ref, k_ref, v_ref, g_ref, cos_ref, sin_e_ref, sin_o_ref, dec_ref, wq_ref, wk_ref,
                      gc_ref, gn_ref, o_ref, st_ref):
    @pl.when(pl.program_id(1) == 0)
    def _():
        st_ref[...] = jnp.zeros(st_ref.shape, F32)

    cos = cos_ref[...]
    sin_e = sin_e_ref[...]
    sin_o = sin_o_ref[...]

    def rotate(x):
        return x * cos + pltpu.roll(x, RET_DK - 1, axis=1) * sin_e + pltpu.roll(x, 1, axis=1) * sin_o

    for h in range(RET_HEADS):
        qf = rotate(q_ref[:, h * RET_DK:(h + 1) * RET_DK])
        kf = rotate(k_ref[:, h * RET_DK:(h + 1) * RET_DK]) * (RET_DK ** -0.5)
        qb = qf.astype(BF16)
        cols = slice(h * RET_DV, (h + 1) * RET_DV)
        v = v_ref[:, cols].astype(BF16)
        s = _dot_nt(qb, kf.astype(BF16)) * dec_ref[h]
        st = st_ref[h]
        wq = wq_ref[h]
        o = _dot(s.astype(BF16), v) + _dot(qb, st.astype(BF16)) * jnp.concatenate([wq, wq], axis=1)
        kv = _dot((kf * wk_ref[h]).T.astype(BF16), v)
        st_ref[h] = st * gc_ref[h] + kv
        mu = jnp.mean(o, axis=-1, keepdims=True)
        d = o - mu
        var = jnp.mean(d * d, axis=-1, keepdims=True)
        on = d * lax.rsqrt(var + EPS) * gn_ref[:, cols]
        gr = g_ref[:, cols]
        o_ref[:, cols] = (gr * jax.nn.sigmoid(gr) * on).astype(o_ref.dtype)


def _retention(zb, gn_w, batch, seq):
    c = RET_CHUNK
    nc = seq // c
    hq = RET_HEADS * RET_DK
    hv = RET_HEADS * RET_DV
    f32 = np.float32
    inv = f32(ROPE_BASE) ** (-np.arange(0, RET_DK, 2, dtype=f32) / f32(RET_DK))
    ang = np.arange(seq, dtype=f32)[:, None] * inv[None, :]
    zero = np.zeros_like(ang)
    pairs = lambda even, odd: np.stack([even, odd], axis=-1).reshape(seq, RET_DK)
    cos = pairs(np.cos(ang), np.cos(ang))
    sin_e = pairs(-np.sin(ang), zero)
    sin_o = pairs(zero, np.sin(ang))
    log_g = np.log1p(-np.exp2(f32(-5.0) - np.arange(RET_HEADS, dtype=f32)))
    idx = np.arange(c, dtype=f32)
    rel = idx[:, None] - idx[None, :]
    decay = np.where(rel >= 0, np.exp(log_g[:, None, None] * np.maximum(rel, f32(0.0))), f32(0.0)).astype(f32)
    lanes = lambda a: np.ascontiguousarray(np.broadcast_to(a[:, :, None], (RET_HEADS, c, RET_DK)), dtype=f32)
    w_k = lanes(np.exp(log_g[:, None] * (f32(c - 1) - idx)[None, :]))
    w_q = lanes(np.exp(log_g[:, None] * (idx + f32(1.0))[None, :]))
    g_chunk = np.broadcast_to(np.exp(log_g * f32(c))[:, None, None], (RET_HEADS, 1, RET_DV)).astype(f32)

    row = lambda b, n: b * nc + n
    return pl.pallas_call(
        _retention_kernel,
        grid=(batch, nc),
        in_specs=[
            pl.BlockSpec((c, hq), lambda b, n: (row(b, n), B_QR // hq)),
            pl.BlockSpec((c, hq), lambda b, n: (row(b, n), B_KR // hq)),
            pl.BlockSpec((c, hv), lambda b, n: (row(b, n), B_VR // hv)),
            pl.BlockSpec((c, hv), lambda b, n: (row(b, n), B_GR // hv)),
            pl.BlockSpec((c, RET_DK), lambda b, n: (n, 0)),
            pl.BlockSpec((c, RET_DK), lambda b, n: (n, 0)),
            pl.BlockSpec((c, RET_DK), lambda b, n: (n, 0)),
            pl.BlockSpec((RET_HEADS, c, c), lambda b, n: (0, 0, 0)),
            pl.BlockSpec((RET_HEADS, c, RET_DK), lambda b, n: (0, 0, 0)),
            pl.BlockSpec((RET_HEADS, c, RET_DK), lambda b, n: (0, 0, 0)),
            pl.BlockSpec((RET_HEADS, 1, RET_DV), lambda b, n: (0, 0, 0)),
            pl.BlockSpec((1, hv), lambda b, n: (0, 0)),
        ],
        out_specs=pl.BlockSpec((c, hv), lambda b, n: (row(b, n), 0)),
        out_shape=jax.ShapeDtypeStruct((batch * seq, hv), BF16),
        scratch_shapes=[pltpu.VMEM((RET_HEADS, RET_DK, RET_DV), F32)],
        compiler_params=_params(("parallel", "arbitrary"), 40),
        name="retention",
    )(zb, zb, zb, zb, cos, sin_e, sin_o, decay, w_q, w_k, g_chunk, gn_w.reshape(1, hv))


def _merge_kernel(on_ref, or_ref, ga_ref, gb_ref, x_ref, wa_ref, wb_ref, wo_ref, h_ref):
    a = _dot(on_ref[...], wa_ref[...])
    b = _dot(or_ref[...], wb_ref[...])
    merged = jax.nn.sigmoid(ga_ref[...]) * a + jax.nn.sigmoid(gb_ref[...]) * b
    h_ref[...] = x_ref[...] + _dot(merged.astype(BF16), wo_ref[...])


def _merge(o_nsa, o_ret, zb, x, w_a, w_b, w_out, tm=256):
    m, d = x.shape
    resident = lambda shape: pl.BlockSpec(shape, lambda i: (0, 0), pipeline_mode=pl.Buffered(1))
    return pl.pallas_call(
        _merge_kernel,
        grid=(m // tm,),
        in_specs=[
            pl.BlockSpec((tm, d), lambda i: (i, 0)),
            pl.BlockSpec((tm, d), lambda i: (i, 0)),
            pl.BlockSpec((tm, d), lambda i: (i, B_GA // d)),
            pl.BlockSpec((tm, d), lambda i: (i, B_GB // d)),
            pl.BlockSpec((tm, d), lambda i: (i, 0)),
            resident(w_a.shape), resident(w_b.shape), resident(w_out.shape),
        ],
        out_specs=pl.BlockSpec((tm, d), lambda i: (i, 0)),
        out_shape=jax.ShapeDtypeStruct((m, d), F32),
        compiler_params=_params(("parallel",), 56),
        name="merge_out_proj",
    )(o_nsa, o_ret, zb, zb, x, w_a, w_b, w_out)


def _mem_kv_kernel(m_ref, nw_ref, wk_ref, wv_ref, k_ref, v_ref):
    mn = _rms(m_ref[...], nw_ref[...]).astype(BF16)
    k_ref[...] = _dot(mn, wk_ref[...]).astype(k_ref.dtype)
    v_ref[...] = _dot(mn, wv_ref[...]).astype(v_ref.dtype)


def _mem_kv(mem2, nw, wk, wv, tm=256):
    m, d = mem2.shape
    n = wk.shape[1]
    out = jax.ShapeDtypeStruct((m, n), BF16)
    return pl.pallas_call(
        _mem_kv_kernel,
        grid=(m // tm,),
        in_specs=[
            pl.BlockSpec((tm, d), lambda i: (i, 0)),
            pl.BlockSpec((1, d), lambda i: (0, 0)),
            pl.BlockSpec((d, n), lambda i: (0, 0)),
            pl.BlockSpec((d, n), lambda i: (0, 0)),
        ],
        out_specs=[pl.BlockSpec((tm, n), lambda i: (i, 0))] * 2,
        out_shape=[out, out],
        compiler_params=_params(("parallel",), 40),
        name="mem_kv_proj",
    )(mem2, nw.reshape(1, d), wk, wv)


def _cross_kernel(h_ref, xw_ref, mw_ref, wq_ref, kx_ref, vx_ref, wo_ref, h2_ref, nm_ref):
    h = h_ref[...]
    nx = _rms(h, xw_ref[...]).astype(BF16)
    qx = _dot(nx, wq_ref[...]).astype(BF16)
    outs = []
    for hh in range(X_HEADS):
        cols = slice(hh * X_DH, (hh + 1) * X_DH)
        s = _dot_nt(qx[:, cols], kx_ref[:, cols]) * (X_DH ** -0.5)
        e = jnp.exp(s - jnp.max(s, axis=-1, keepdims=True))
        p = e / jnp.sum(e, axis=-1, keepdims=True)
        outs.append(_dot(p.astype(BF16), vx_ref[:, cols]))
    ox = jnp.concatenate(outs, axis=-1).astype(BF16)
    h2 = h + _dot(ox, wo_ref[...])
    h2_ref[...] = h2
    nm_ref[...] = _rms(h2, mw_ref[...]).astype(nm_ref.dtype)


def _cross_attention(h1, x_norm_w, mlp_norm_w, wq, kx, vx, wo, seq, tm=512):
    m, d = h1.shape
    n = wq.shape[1]
    per_batch = seq // tm
    vec = lambda: pl.BlockSpec((1, d), lambda i: (0, 0))
    return pl.pallas_call(
        _cross_kernel,
        grid=(m // tm,),
        in_specs=[
            pl.BlockSpec((tm, d), lambda i: (i, 0)),
            vec(), vec(),
            pl.BlockSpec((d, n), lambda i: (0, 0)),
            pl.BlockSpec((MEM_LEN, n), lambda i: (i // per_batch, 0)),
            pl.BlockSpec((MEM_LEN, n), lambda i: (i // per_batch, 0)),
            pl.BlockSpec((n, d), lambda i: (0, 0)),
        ],
        out_specs=[pl.BlockSpec((tm, d), lambda i: (i, 0))] * 2,
        out_shape=[jax.ShapeDtypeStruct((m, d), F32), jax.ShapeDtypeStruct((m, d), BF16)],
        compiler_params=_params(("parallel",), 40),
        name="cross_attention",
    )(h1, x_norm_w.reshape(1, d), mlp_norm_w.reshape(1, d), wq, kx, vx, wo)


def _mlp_kernel(nm_ref, wu_ref, wd_ref, h_ref, fw_ref, o_ref, acc_ref):
    j = pl.program_id(1)

    @pl.when(j == 0)
    def _():
        acc_ref[...] = jnp.zeros(acc_ref.shape, F32)

    u = jnp.maximum(_dot(nm_ref[...], wu_ref[...]), 0.0)
    acc_ref[...] += _dot((u * u).astype(BF16), wd_ref[...])

    @pl.when(j == pl.num_programs(1) - 1)
    def _():
        o_ref[...] = _rms(h_ref[...] + acc_ref[...], fw_ref[...])


def _mlp(nm, w_up, w_down, h2, final_w, tm=512, tf=1024):
    m, d = nm.shape
    f = w_up.shape[1]
    return pl.pallas_call(
        _mlp_kernel,
        grid=(m // tm, f // tf),
        in_specs=[
            pl.BlockSpec((tm, d), lambda i, j: (i, 0)),
            pl.BlockSpec((d, tf), lambda i, j: (0, j)),
            pl.BlockSpec((tf, d), lambda i, j: (j, 0)),
            pl.BlockSpec((tm, d), lambda i, j: (i, 0)),
            pl.BlockSpec((1, d), lambda i, j: (0, 0)),
        ],
        out_specs=pl.BlockSpec((tm, d), lambda i, j: (i, 0)),
        out_shape=jax.ShapeDtypeStruct((m, d), F32),
        scratch_shapes=[pltpu.VMEM((tm, d), F32)],
        compiler_params=_params(("parallel", "arbitrary"), 56),
        name="mlp_final_norm",
    )(nm, w_up, w_down, h2, final_w.reshape(1, d))


def _layer(h, mem, attn_norm_w, w_in, cmp_pe_k, cmp_w1_k, cmp_w2_k, cmp_pe_v, cmp_w1_v, cmp_w2_v,
           w_a, ret_gn_w, w_b, w_out, x_norm_w, mem_norm_w, wq_x, wk_x, wv_x, wo_x,
           mlp_norm_w, w_up, w_down, out_norm_w, batch, seq):
    bf = lambda a: a.astype(BF16)
    blk = W_IN_BLOCK
    kv_block = W_IN_KV // blk
    gate_shift = NSA_HEADS * 3

    n = _rmsnorm(h, attn_norm_w, BF16)
    wt = w_in.T
    skip_kv = lambda j: j + (j >= kv_block).astype(jnp.int32)
    za = _in_proj(n, wt, skip_kv, 4, BF16, name="in_proj_a")
    kv = _in_proj(n, wt, lambda j: kv_block, 1, BF16, regroup=CMP_STRIDE, name="in_proj_kv")
    zb = _in_proj(n, wt, lambda j: W_IN_NSA_GATE // blk + j, B_WIDTH // blk, F32, shift=gate_shift,
                  name="in_proj_b")
    zg = _in_proj(n, wt, lambda j: W_IN_NSA_GATE // LANES, 1, F32, tn=LANES, name="in_proj_gate")

    assert seq // CMP_STRIDE == LANES
    rows_kv = NSA_GROUPS * batch * LANES
    k2 = kv.reshape(2 * rows_kv, CMP_STRIDE * NSA_DK)
    kc = _compress(k2, 0, rows_kv, cmp_pe_k.reshape(1, -1), bf(cmp_w1_k), bf(cmp_w2_k))
    vc = _compress(k2, rows_kv, rows_kv, cmp_pe_v.reshape(1, -1), bf(cmp_w1_v), bf(cmp_w2_v))

    o_nsa = _nsa_attention(za, zg, kc, vc, batch, seq)
    o_ret = _retention(zb, ret_gn_w, batch, seq)
    h1 = _merge(o_nsa, o_ret, zb, h, bf(w_a), bf(w_b), bf(w_out))

    kx, vx = _mem_kv(mem.reshape(batch * MEM_LEN, D_MODEL), mem_norm_w, bf(wk_x), bf(wv_x))
    h2, nm = _cross_attention(h1, x_norm_w, mlp_norm_w, bf(wq_x), kx, vx, bf(wo_x), seq)
    return _mlp(nm, bf(w_up), bf(w_down), h2, out_norm_w)


def kernel(x, mem, attn_norm_w, w_in, cmp_pe_k, cmp_w1_k, cmp_w2_k, cmp_pe_v, cmp_w1_v, cmp_w2_v, w_a, ret_gn_w,
           w_b, w_out, x_norm_w, mem_norm_w, wq_x, wk_x, wv_x, wo_x, mlp_norm_w, w_up, w_down, final_norm_w):
    batch, seq, d = x.shape
    depth = w_in.shape[0]
    assert depth == 1
    h = x.reshape(batch * seq, d)
    out = _layer(h, mem, attn_norm_w[0], w_in[0], cmp_pe_k[0], cmp_w1_k[0], cmp_w2_k[0],
                 cmp_pe_v[0], cmp_w1_v[0], cmp_w2_v[0], w_a[0], ret_gn_w[0], w_b[0], w_out[0],
                 x_norm_w[0], mem_norm_w[0], wq_x[0], wk_x[0], wv_x[0], wo_x[0],
                 mlp_norm_w[0], w_up[0], w_down[0], final_norm_w, batch, seq)
    return out.reshape(batch, seq, d)
```

```python
import functools

import jax
import jax.numpy as jnp
import numpy as np
from jax import lax
from jax.experimental import pallas as pl
from jax.experimental.pallas import tpu as pltpu

F32 = jnp.float32
BF16 = jnp.bfloat16

D_MODEL = 2048
MEM_LEN = 256
NSA_HEADS = 16
NSA_GROUPS = 4
NSA_REP = NSA_HEADS // NSA_GROUPS
NSA_DK = 128
NSA_DV = 128
CMP_LEN = 32
CMP_STRIDE = 16
CMP_HIDDEN = 1024
SEL_LEN = 64
SEL_TOPK = 16
WIN = 512
RET_HEADS = 8
RET_DK = 128
RET_DV = 256
RET_CHUNK = 128
ROPE_BASE = 10000.0
X_HEADS = 4
X_DH = 128
D_FF = 4 * D_MODEL
EPS = 1e-6
NEG = -1e30
LOG2E = 1.4426950408889634

LANES = 128
MXU_ROWS = 256

W_IN_BLOCK = 1024
W_IN_KV = 2048
W_IN_NSA_GATE = 5120
A_QN = 0
A_KS = 2048
A_VS = 2560
A_KW = 3072
A_VW = 3584
B_QR = 0
B_KR = 1024
B_VR = 2048
B_GR = 4096
B_GA = 6144
B_GB = 8192
B_WIDTH = 10240

NT_DIMS = (((1,), (1,)), ((), ()))


def _params(sem, vmem_mb=None):
    kw = {"dimension_semantics": sem}
    if vmem_mb is not None:
        kw["vmem_limit_bytes"] = vmem_mb * 1024 * 1024
    return pltpu.CompilerParams(**kw)


def _rms(x, w):
    return x * lax.rsqrt(jnp.mean(x * x, axis=-1, keepdims=True) + EPS) * w


def _dot(a, b):
    return jnp.dot(a, b, preferred_element_type=F32)


def _dot_nt(a, b):
    return lax.dot_general(a, b, NT_DIMS, preferred_element_type=F32)


CAST_ROWS = 16


def _cast_rows(dst_ref, dst0, src_ref, src0, nrows):
    def body(r, carry):
        off = r * CAST_ROWS
        dst_ref[pl.ds(pl.multiple_of(dst0 + off, CAST_ROWS), CAST_ROWS), :] = (
            src_ref[pl.ds(pl.multiple_of(src0 + off, 8), CAST_ROWS), :].astype(BF16))
        return carry

    lax.fori_loop(0, nrows // CAST_ROWS, body, 0)


def _in_proj_kernel(*refs, shift, regroup, n_cast, norm):
    n_in = 4 if norm else 3
    n_ref, w_ref, wnext_ref = refs[:3]
    cast_in = refs[n_in:n_in + n_cast]
    o_ref = refs[n_in + n_cast]
    n_out = n_in + n_cast + 1 + int(norm)
    cast_out = refs[n_out:n_out + n_cast]
    wb_sc, *rest = refs[n_out + n_cast:]
    tn = wb_sc.shape[0]
    for src, dst in zip(cast_in, cast_out):
        dst[...] = src[...].astype(dst.dtype)

    @pl.when(pl.program_id(1) == 0)
    def _():
        _cast_rows(wb_sc, 0, w_ref, shift, tn - shift)
        if shift:
            _cast_rows(wb_sc, tn - shift, wnext_ref, 0, shift)

    if norm:
        xn = _rms(n_ref[...], refs[3][...]).astype(BF16)
        refs[n_in + n_cast + 1][...] = xn
    else:
        xn = n_ref[...]
    res = _dot_nt(xn, wb_sc[...])
    if regroup:
        (r_sc,) = rest
        tm = res.shape[0]
        for c in range(tn // LANES):
            r_sc[c] = res[:, c * LANES:(c + 1) * LANES]
        for c in range(tn // LANES):
            for j in range(regroup):
                o_ref[c, :, j * LANES:(j + 1) * LANES] = (
                    r_sc[c, pl.ds(j, tm // regroup, stride=regroup), :].astype(o_ref.dtype))
    else:
        o_ref[...] = res.astype(o_ref.dtype)


def _in_proj(n, wt, src_block, n_blocks, out_dtype, *, shift=0, regroup=0, casts=(), norm_w=None,
             tm=1024, tn=1024, name):
    m, k = n.shape
    per = tn // LANES
    m_tiles = m // tm
    norm = norm_w is not None
    assert shift % CAST_ROWS == 0 and shift <= LANES and not (norm and n_blocks > 1)
    scratch = [pltpu.VMEM((tn, k), BF16)]
    row_tile = pl.BlockSpec((tm, k), lambda j, i: (i, 0))
    once = {"pipeline_mode": pl.Buffered(1)} if n_blocks == 1 else {}
    if regroup:
        assert n_blocks == 1
        out_shape = jax.ShapeDtypeStruct((per, m // regroup, regroup * LANES), out_dtype)
        out_spec = pl.BlockSpec((per, tm // regroup, regroup * LANES), lambda j, i: (0, i, 0))
        scratch.append(pltpu.VMEM((per, tm, LANES), F32))
    else:
        out_shape = jax.ShapeDtypeStruct((m, n_blocks * tn), out_dtype)
        out_spec = pl.BlockSpec((tm, tn), lambda j, i: (i, j))

    cast_steps = 1 << ((n_blocks * m_tiles).bit_length() - 1)
    cast_specs = []
    for a in casts:
        rows = a.shape[0] // cast_steps
        assert a.shape[0] % cast_steps == 0 and rows % CAST_ROWS == 0
        cast_specs.append(pl.BlockSpec(
            (rows, a.shape[1]), lambda j, i: (jnp.minimum(j * m_tiles + i, cast_steps - 1), 0)))

    return pl.pallas_call(
        functools.partial(_in_proj_kernel, shift=shift, regroup=regroup, n_cast=len(casts), norm=norm),
        grid=(n_blocks, m_tiles),
        in_specs=[
            row_tile,
            pl.BlockSpec((tn, k), lambda j, i: (src_block(j), 0), **once),
            pl.BlockSpec((LANES, k), lambda j, i: ((src_block(j) + 1) * per, 0), **once),
            *([pl.BlockSpec((1, k), lambda j, i: (0, 0))] if norm else []),
            *cast_specs,
        ],
        out_specs=[out_spec, *([row_tile] if norm else []), *cast_specs],
        out_shape=[out_shape, *([jax.ShapeDtypeStruct((m, k), BF16)] if norm else []),
                   *[jax.ShapeDtypeStruct(a.shape, BF16) for a in casts]],
        scratch_shapes=scratch,
        compiler_params=_params(("arbitrary", "arbitrary"), 56),
        name=name,
    )(n, wt, wt, *([norm_w.reshape(1, k)] if norm else []), *casts)


def _compress_kernel(k2_ref, pe_ref, w1_ref, w2_ref, o_ref):
    rows, half = k2_ref.shape
    k2 = k2_ref[...].astype(F32)
    a_lo = (k2 + pe_ref[:, :half]).astype(BF16)
    a_hi = (k2 + pe_ref[:, half:]).astype(BF16)
    lo = _dot(a_lo, w1_ref[:half, :])
    hi = _dot(a_hi, w1_ref[half:, :])
    h = lo + pltpu.roll(hi, rows - 1, axis=0)
    act = h * jax.nn.sigmoid(h)
    o_ref[...] = _dot(act.astype(BF16), w2_ref[...]).astype(o_ref.dtype)


def _compress(k2, first_row, m, pe, w1, w2, rows=256):
    half = k2.shape[1]
    hid = w1.shape[1]
    dout = w2.shape[1]
    first = first_row // rows
    return pl.pallas_call(
        _compress_kernel,
        grid=(m // rows,),
        in_specs=[
            pl.BlockSpec((rows, half), lambda i: (first + i, 0)),
            pl.BlockSpec((1, 2 * half), lambda i: (0, 0)),
            pl.BlockSpec((2 * half, hid), lambda i: (0, 0)),
            pl.BlockSpec((hid, dout), lambda i: (0, 0)),
        ],
        out_specs=pl.BlockSpec((rows, dout), lambda i: (i, 0)),
        out_shape=jax.ShapeDtypeStruct((m, dout), BF16),
        compiler_params=_params(("parallel",), 40),
        name="nsa_compress",
    )(k2, pe, w1, w2)


def _nsa_kernel(*refs, seq, tq, kblk, gp, hm):
    q_ref = refs[0]
    kc_refs = refs[1:1 + gp]
    vc_refs = refs[1 + gp:1 + 2 * gp]
    ks_ref, vs_ref, kw_ref, vw_ref, g_ref, e_ref, o_ref, s_sc, mx_sc, acc_sc, po_sc = refs[1 + 2 * gp:]
    i = pl.program_id(2)
    t0 = i * tq
    rep = NSA_REP
    c1 = NSA_DK ** -0.5 * LOG2E
    n_cmp = (seq - CMP_LEN) // CMP_STRIDE + 1
    n_sel = seq // SEL_LEN
    topk = min(SEL_TOPK, n_sel)
    groups = range(gp)

    q = q_ref[...]
    units = range(rep // hm)
    rows = lambda gg, r: slice((gg * rep + r) * tq, (gg * rep + r + 1) * tq)
    urows = lambda gg, u: slice((gg * rep + u * hm) * tq, (gg * rep + (u + 1) * hm) * tq)
    part = lambda x, h: x[h * tq:(h + 1) * tq]
    head_q = lambda gg, r: q[:, (gg * rep + r) * NSA_DK:(gg * rep + r + 1) * NSA_DK]
    unit_q = {(gg, u): jnp.concatenate([head_q(gg, u * hm + h) for h in range(hm)], axis=0)
              for gg in groups for u in units}
    group_cols = lambda ref, gg: ref.at[:, gg * LANES:(gg + 1) * LANES]

    def masked(s, mask1):
        return jnp.concatenate([jnp.where(mask1, part(s, h), NEG) for h in range(hm)], axis=0)
    tcol = t0 + lax.broadcasted_iota(jnp.int32, (tq, 1), 0)
    per_group = rep * 3
    g_sig = jax.nn.sigmoid(g_ref[...])
    first_group = pl.program_id(1) * gp
    gs = [pltpu.roll(g_sig, lax.rem(LANES - (first_group + gg) * per_group, LANES), axis=1) for gg in groups]
    gate = lambda gg, r, branch: gs[gg][:, 3 * r + branch:3 * r + branch + 1]

    def exp_rows(sm):
        return jnp.exp2((sm - jnp.max(sm, axis=-1, keepdims=True)) * c1)

    def with_ones(v):
        return jnp.concatenate([v, jnp.ones(v.shape, v.dtype)], axis=1)

    def normalised(ev):
        return ev[:, :NSA_DV] / ev[:, NSA_DV:]

    c_idx = lax.broadcasted_iota(jnp.int32, (tq, LANES), 1)
    mask_c = ((c_idx * CMP_STRIDE + (CMP_LEN - 1)) <= tcol) & (c_idx < n_cmp)
    mask_cf = jnp.where(mask_c, 1.0, 0.0)
    o_cmp, psum = {}, []
    for gg in groups:
        kc = kc_refs[gg][...]
        vc = vc_refs[gg][...]
        tot = None
        for u in units:
            e = exp_rows(masked(_dot_nt(unit_q[gg, u], kc), mask_c))
            p = e / jnp.sum(e, axis=-1, keepdims=True)
            p = jnp.concatenate([part(p, h) * mask_cf for h in range(hm)], axis=0)
            o_cmp[gg, u] = _dot(p.astype(BF16), vc)
            for h in range(hm):
                tot = part(p, h) if tot is None else tot + part(p, h)
        psum.append(tot)

    wlen = WIN + tq
    start = pl.multiple_of(jnp.maximum(i - WIN // tq, 0) * tq, tq)
    dlt = tcol - (start + lax.broadcasted_iota(jnp.int32, (tq, wlen), 1))
    mask_w = (dlt >= 0) & (dlt < WIN)
    for gg in groups:
        kw = group_cols(kw_ref, gg)[pl.ds(start, wlen), :]
        vw1 = with_ones(group_cols(vw_ref, gg)[pl.ds(start, wlen), :])
        for u in units:
            e = exp_rows(masked(_dot_nt(unit_q[gg, u], kw), mask_w))
            o_win = normalised(_dot(e.astype(BF16), vw1))
            for h in range(hm):
                r = u * hm + h
                po_sc[rows(gg, r), :] = (gate(gg, r, 0) * part(o_cmp[gg, u], h)
                                         + gate(gg, r, 2) * part(o_win, h))

    jo = lax.broadcasted_iota(jnp.int32, (n_sel, LANES), 0)
    co = lax.broadcasted_iota(jnp.int32, (n_sel, LANES), 1)
    ov_t = jnp.where((co * CMP_STRIDE < jo * SEL_LEN + SEL_LEN) & (co * CMP_STRIDE + CMP_LEN > jo * SEL_LEN)
                     & (co < n_cmp), 1.0, 0.0).astype(BF16)
    j_idx = lax.broadcasted_iota(jnp.int32, (n_sel, tq), 0)
    cur = lax.shift_right_logical(t0 + lax.broadcasted_iota(jnp.int32, (n_sel, tq), 1), int(np.log2(SEL_LEN)))
    forced = (j_idx == 0) | (j_idx == cur) | (j_idx == cur - 1)
    future = j_idx > cur
    sel = []
    for gg in groups:
        p_hi = psum[gg].astype(BF16)
        p_lo = (psum[gg] - p_hi.astype(F32)).astype(BF16)
        imp = _dot_nt(ov_t, p_hi) + _dot_nt(ov_t, p_lo)
        impm = jnp.where(forced, jnp.inf, jnp.where(future, -jnp.inf, imp))
        rank = jnp.zeros((n_sel, tq), F32)
        for ii in range(n_sel):
            row = impm[ii:ii + 1, :]
            beats = (row > impm) | ((row == impm) & (j_idx > ii))
            rank = rank + jnp.where(beats, 1.0, 0.0)
        sel_t = jnp.where(rank < topk, 1.0, 0.0)
        sel.append(jnp.concatenate([sel_t, jnp.zeros((LANES - n_sel, tq), F32)], axis=0).T.astype(BF16))

    n_chunks = lax.div(t0 + tq + (kblk - 1), kblk)

    def lane_fold_max(x):
        out = x[:, :LANES]
        for c in range(1, kblk // LANES):
            out = jnp.maximum(out, x[:, c * LANES:(c + 1) * LANES])
        return out

    def chunk_start(kb):
        return kb * kblk if isinstance(kb, int) else pl.multiple_of(kb * kblk, kblk)

    def score_chunk(kb):
        k0 = chunk_start(kb)
        causal = (k0 + lax.broadcasted_iota(jnp.int32, (tq, kblk), 1)) <= tcol
        folds = []
        for gg in groups:
            k = group_cols(ks_ref, gg)[pl.ds(k0, kblk), :]
            mask1 = (_dot(sel[gg], e_ref[kb]) > 0.5) & causal
            for u in units:
                sm_ = masked(_dot_nt(unit_q[gg, u], k), mask1)
                s_sc[kb, urows(gg, u), :] = sm_
                folds.append(lane_fold_max(sm_))
        return jnp.concatenate(folds, axis=0)

    def score_pass(kb, carry):
        mx_sc[...] = jnp.maximum(mx_sc[...], score_chunk(kb))
        return carry

    mx_sc[...] = score_chunk(0)
    lax.fori_loop(1, n_chunks, score_pass, 0)
    m_sel = jnp.max(mx_sc[...], axis=-1, keepdims=True)

    def value_chunk(kb):
        k0 = chunk_start(kb)
        pvs = []
        for gg in groups:
            v1 = with_ones(group_cols(vs_ref, gg)[pl.ds(k0, kblk), :])
            for u in units:
                pk = jnp.exp2((s_sc[kb, urows(gg, u), :] - m_sel[urows(gg, u)]) * c1)
                pvs.append(_dot(pk.astype(BF16), v1))
        return jnp.concatenate(pvs, axis=0)

    def value_pass(kb, carry):
        acc_sc[...] += value_chunk(kb)
        return carry

    acc_sc[...] = value_chunk(0)
    lax.fori_loop(1, n_chunks, value_pass, 0)

    for gg in groups:
        for r in range(rep):
            o = po_sc[rows(gg, r), :] + gate(gg, r, 1) * normalised(acc_sc[rows(gg, r), :])
            col = (gg * rep + r) * NSA_DV
            o_ref[:, col:col + NSA_DV] = o.astype(o_ref.dtype)


def _nsa_attention(za, zg, kc, vc, batch, seq, tq=256, kblk=512, gp=2):
    assert seq % kblk == 0 and seq >= WIN + tq and WIN % tq == 0 and NSA_GROUPS % gp == 0
    nq = seq // tq
    gw = gp * NSA_REP * NSA_DK
    streams = gp * NSA_REP * tq
    hm = max(1, MXU_ROWS // tq)
    assert NSA_REP % hm == 0
    kern = functools.partial(_nsa_kernel, seq=seq, tq=tq, kblk=kblk, gp=gp, hm=hm)
    key = np.arange(seq).reshape(seq // kblk, 1, kblk)
    expand = jnp.asarray(key // SEL_LEN == np.arange(LANES).reshape(1, LANES, 1), BF16)

    def kv_spec(off):
        return pl.BlockSpec((seq, gp * LANES), lambda b, g, i: (b, off // (gp * LANES) + g))

    def cmp_spec(gg):
        return pl.BlockSpec((LANES, NSA_DK), lambda b, g, i: ((g * gp + gg) * batch + b, 0))

    return pl.pallas_call(
        kern,
        grid=(batch, NSA_GROUPS // gp, nq),
        in_specs=[
            pl.BlockSpec((tq, gw), lambda b, g, i: (b * nq + i, A_QN // gw + g)),
            *[cmp_spec(gg) for gg in range(gp)],
            *[cmp_spec(gg) for gg in range(gp)],
            kv_spec(A_KS), kv_spec(A_VS), kv_spec(A_KW), kv_spec(A_VW),
            pl.BlockSpec((tq, LANES), lambda b, g, i: (b * nq + i, 0)),
            pl.BlockSpec(expand.shape, lambda b, g, i: (0, 0, 0)),
        ],
        out_specs=pl.BlockSpec((tq, gw), lambda b, g, i: (b * nq + i, g)),
        out_shape=jax.ShapeDtypeStruct((batch * seq, NSA_HEADS * NSA_DV), BF16),
        scratch_shapes=[
            pltpu.VMEM((seq // kblk, streams, kblk), F32),
            pltpu.VMEM((streams, LANES), F32),
            pltpu.VMEM((streams, 2 * NSA_DV), F32),
            pltpu.VMEM((streams, NSA_DV), F32),
        ],
        compiler_params=_params(("parallel", "parallel", "arbitrary"), 56),
        name="nsa_attention",
    )(za, *([kc] * gp), *([vc] * gp), za, za, za, za, zg, expand)


def _retention_kernel(q_ref, k_ref, v_ref, g_ref, cos_ref, sin_e_ref, sin_o_ref, dec_ref, wq_ref, wk_ref,
                      gc_ref, gn_ref, o_ref, st_ref):
    @pl.when(pl.program_id(1) == 0)
    def _():
        st_ref[...] = jnp.zeros(st_ref.shape, F32)

    cos = cos_ref[...]
    sin_e = sin_e_ref[...]
    sin_o = sin_o_ref[...]

    def rotate(x):
        return x * cos + pltpu.roll(x, RET_DK - 1, axis=1) * sin_e + pltpu.roll(x, 1, axis=1) * sin_o

    for h in range(RET_HEADS):
        qf = rotate(q_ref[:, h * RET_DK:(h + 1) * RET_DK])
        kf = rotate(k_ref[:, h * RET_DK:(h + 1) * RET_DK]) * (RET_DK ** -0.5)
        qb = qf.astype(BF16)
        cols = slice(h * RET_DV, (h + 1) * RET_DV)
        v = v_ref[:, cols].astype(BF16)
        s = _dot_nt(qb, kf.astype(BF16)) * dec_ref[h]
        st = st_ref[h]
        wq = wq_ref[h]
        o = _dot(s.astype(BF16), v) + _dot(qb, st.astype(BF16)) * jnp.concatenate([wq, wq], axis=1)
        kv = _dot((kf * wk_ref[h]).T.astype(BF16), v)
        st_ref[h] = st * gc_ref[h] + kv
        mu = jnp.mean(o, axis=-1, keepdims=True)
        d = o - mu
        var = jnp.mean(d * d, axis=-1, keepdims=True)
        on = d * lax.rsqrt(var + EPS) * gn_ref[:, cols]
        gr = g_ref[:, cols]
        o_ref[:, cols] = (gr * jax.nn.sigmoid(gr) * on).astype(o_ref.dtype)


def _retention(zb, gn_w, batch, seq):
    c = RET_CHUNK
    nc = seq // c
    hq = RET_HEADS * RET_DK
    hv = RET_HEADS * RET_DV
    f32 = np.float32
    inv = f32(ROPE_BASE) ** (-np.arange(0, RET_DK, 2, dtype=f32) / f32(RET_DK))
    ang = np.arange(seq, dtype=f32)[:, None] * inv[None, :]
    zero = np.zeros_like(ang)
    pairs = lambda even, odd: np.stack([even, odd], axis=-1).reshape(seq, RET_DK)
    cos = pairs(np.cos(ang), np.cos(ang))
    sin_e = pairs(-np.sin(ang), zero)
    sin_o = pairs(zero, np.sin(ang))
    log_g = np.log1p(-np.exp2(f32(-5.0) - np.arange(RET_HEADS, dtype=f32)))
    idx = np.arange(c, dtype=f32)
    rel = idx[:, None] - idx[None, :]
    decay = np.where(rel >= 0, np.exp(log_g[:, None, None] * np.maximum(rel, f32(0.0))), f32(0.0)).astype(f32)
    lanes = lambda a: np.ascontiguousarray(np.broadcast_to(a[:, :, None], (RET_HEADS, c, RET_DK)), dtype=f32)
    w_k = lanes(np.exp(log_g[:, None] * (f32(c - 1) - idx)[None, :]))
    w_q = lanes(np.exp(log_g[:, None] * (idx + f32(1.0))[None, :]))
    g_chunk = np.broadcast_to(np.exp(log_g * f32(c))[:, None, None], (RET_HEADS, 1, RET_DV)).astype(f32)

    row = lambda b, n: b * nc + n
    return pl.pallas_call(
        _retention_kernel,
        grid=(batch, nc),
        in_specs=[
            pl.BlockSpec((c, hq), lambda b, n: (row(b, n), B_QR // hq)),
            pl.BlockSpec((c, hq), lambda b, n: (row(b, n), B_KR // hq)),
            pl.BlockSpec((c, hv), lambda b, n: (row(b, n), B_VR // hv)),
            pl.BlockSpec((c, hv), lambda b, n: (row(b, n), B_GR // hv)),
            pl.BlockSpec((c, RET_DK), lambda b, n: (n, 0)),
            pl.BlockSpec((c, RET_DK), lambda b, n: (n, 0)),
            pl.BlockSpec((c, RET_DK), lambda b, n: (n, 0)),
            pl.BlockSpec((RET_HEADS, c, c), lambda b, n: (0, 0, 0)),
            pl.BlockSpec((RET_HEADS, c, RET_DK), lambda b, n: (0, 0, 0)),
            pl.BlockSpec((RET_HEADS, c, RET_DK), lambda b, n: (0, 0, 0)),
            pl.BlockSpec((RET_HEADS, 1, RET_DV), lambda b, n: (0, 0, 0)),
            pl.BlockSpec((1, hv), lambda b, n: (0, 0)),
        ],
        out_specs=pl.BlockSpec((c, hv), lambda b, n: (row(b, n), 0)),
        out_shape=jax.ShapeDtypeStruct((batch * seq, hv), BF16),
        scratch_shapes=[pltpu.VMEM((RET_HEADS, RET_DK, RET_DV), F32)],
        compiler_params=_params(("parallel", "arbitrary"), 40),
        name="retention",
    )(zb, zb, zb, zb, cos, sin_e, sin_o, decay, w_q, w_k, g_chunk, gn_w.reshape(1, hv))


def _merge_kernel(on_ref, or_ref, ga_ref, gb_ref, x_ref, wa_ref, wb_ref, wo_ref, h_ref):
    a = _dot(on_ref[...], wa_ref[...])
    b = _dot(or_ref[...], wb_ref[...])
    merged = jax.nn.sigmoid(ga_ref[...]) * a + jax.nn.sigmoid(gb_ref[...]) * b
    h_ref[...] = x_ref[...] + _dot(merged.astype(BF16), wo_ref[...])


def _merge(o_nsa, o_ret, zb, x, w_a, w_b, w_out, tm=256):
    m, d = x.shape
    resident = lambda shape: pl.BlockSpec(shape, lambda i: (0, 0), pipeline_mode=pl.Buffered(1))
    return pl.pallas_call(
        _merge_kernel,
        grid=(m // tm,),
        in_specs=[
            pl.BlockSpec((tm, d), lambda i: (i, 0)),
            pl.BlockSpec((tm, d), lambda i: (i, 0)),
            pl.BlockSpec((tm, d), lambda i: (i, B_GA // d)),
            pl.BlockSpec((tm, d), lambda i: (i, B_GB // d)),
            pl.BlockSpec((tm, d), lambda i: (i, 0)),
            resident(w_a.shape), resident(w_b.shape), resident(w_out.shape),
        ],
        out_specs=pl.BlockSpec((tm, d), lambda i: (i, 0)),
        out_shape=jax.ShapeDtypeStruct((m, d), F32),
        compiler_params=_params(("parallel",), 56),
        name="merge_out_proj",
    )(o_nsa, o_ret, zb, zb, x, w_a, w_b, w_out)


def _mem_kv_kernel(m_ref, nw_ref, wk_ref, wv_ref, k_ref, v_ref):
    mn = _rms(m_ref[...], nw_ref[...]).astype(BF16)
    k_ref[...] = _dot(mn, wk_ref[...]).astype(k_ref.dtype)
    v_ref[...] = _dot(mn, wv_ref[...]).astype(v_ref.dtype)


def _mem_kv(mem2, nw, wk, wv, tm=256):
    m, d = mem2.shape
    n = wk.shape[1]
    out = jax.ShapeDtypeStruct((m, n), BF16)
    return pl.pallas_call(
        _mem_kv_kernel,
        grid=(m // tm,),
        in_specs=[
            pl.BlockSpec((tm, d), lambda i: (i, 0)),
            pl.BlockSpec((1, d), lambda i: (0, 0)),
            pl.BlockSpec((d, n), lambda i: (0, 0)),
            pl.BlockSpec((d, n), lambda i: (0, 0)),
        ],
        out_specs=[pl.BlockSpec((tm, n), lambda i: (i, 0))] * 2,
        out_shape=[out, out],
        compiler_params=_params(("parallel",), 40),
        name="mem_kv_proj",
    )(mem2, nw.reshape(1, d), wk, wv)


def _cross_kernel(h_ref, xw_ref, mw_ref, wq_ref, kx_ref, vx_ref, wo_ref, h2_ref, nm_ref):
    h = h_ref[...]
    nx = _rms(h, xw_ref[...]).astype(BF16)
    qx = _dot(nx, wq_ref[...]).astype(BF16)
    outs = []
    for hh in range(X_HEADS):
        cols = slice(hh * X_DH, (hh + 1) * X_DH)
        s = _dot_nt(qx[:, cols], kx_ref[:, cols]) * (X_DH ** -0.5)
        e = jnp.exp(s - jnp.max(s, axis=-1, keepdims=True))
        p = e / jnp.sum(e, axis=-1, keepdims=True)
        outs.append(_dot(p.astype(BF16), vx_ref[:, cols]))
    ox = jnp.concatenate(outs, axis=-1).astype(BF16)
    h2 = h + _dot(ox, wo_ref[...])
    h2_ref[...] = h2
    nm_ref[...] = _rms(h2, mw_ref[...]).astype(nm_ref.dtype)


def _cross_attention(h1, x_norm_w, mlp_norm_w, wq, kx, vx, wo, seq, tm=512):
    m, d = h1.shape
    n = wq.shape[1]
    per_batch = seq // tm
    vec = lambda: pl.BlockSpec((1, d), lambda i: (0, 0))
    return pl.pallas_call(
        _cross_kernel,
        grid=(m // tm,),
        in_specs=[
            pl.BlockSpec((tm, d), lambda i: (i, 0)),
            vec(), vec(),
            pl.BlockSpec((d, n), lambda i: (0, 0)),
            pl.BlockSpec((MEM_LEN, n), lambda i: (i // per_batch, 0)),
            pl.BlockSpec((MEM_LEN, n), lambda i: (i // per_batch, 0)),
            pl.BlockSpec((n, d), lambda i: (0, 0)),
        ],
        out_specs=[pl.BlockSpec((tm, d), lambda i: (i, 0))] * 2,
        out_shape=[jax.ShapeDtypeStruct((m, d), F32), jax.ShapeDtypeStruct((m, d), BF16)],
        compiler_params=_params(("parallel",), 40),
        name="cross_attention",
    )(h1, x_norm_w.reshape(1, d), mlp_norm_w.reshape(1, d), wq, kx, vx, wo)


def _mlp_kernel(nm_ref, wu_ref, wd_ref, h_ref, fw_ref, o_ref):
    j = pl.program_id(1)

    @pl.when(j == 0)
    def _():
        o_ref[...] = jnp.zeros(o_ref.shape, F32)

    u = jnp.maximum(_dot(nm_ref[...], wu_ref[...]), 0.0)
    o_ref[...] += _dot((u * u).astype(BF16), wd_ref[...])

    @pl.when(j == pl.num_programs(1) - 1)
    def _():
        o_ref[...] = _rms(h_ref[...] + o_ref[...], fw_ref[...])


def _mlp(nm, w_up, w_down, h2, final_w, tm=512, tf=1024):
    m, d = nm.shape
    f = w_up.shape[1]
    return pl.pallas_call(
        _mlp_kernel,
        grid=(m // tm, f // tf),
        in_specs=[
            pl.BlockSpec((tm, d), lambda i, j: (i, 0)),
            pl.BlockSpec((d, tf), lambda i, j: (0, j)),
            pl.BlockSpec((tf, d), lambda i, j: (j, 0)),
            pl.BlockSpec((tm, d), lambda i, j: (i, 0)),
            pl.BlockSpec((1, d), lambda i, j: (0, 0)),
        ],
        out_specs=pl.BlockSpec((tm, d), lambda i, j: (i, 0)),
        out_shape=jax.ShapeDtypeStruct((m, d), F32),
        compiler_params=_params(("parallel", "arbitrary"), 56),
        name="mlp_final_norm",
    )(nm, w_up, w_down, h2, final_w.reshape(1, d))


def _layer(h, mem, attn_norm_w, w_in, cmp_pe_k, cmp_w1_k, cmp_w2_k, cmp_pe_v, cmp_w1_v, cmp_w2_v,
           w_a, ret_gn_w, w_b, w_out, x_norm_w, mem_norm_w, wq_x, wk_x, wv_x, wo_x,
           mlp_norm_w, w_up, w_down, out_norm_w, batch, seq):
    blk = W_IN_BLOCK
    kv_block = W_IN_KV // blk
    gate_shift = NSA_HEADS * 3

    wt = w_in.T
    skip_kv = lambda j: j + (j >= kv_block).astype(jnp.int32)
    kv, n, wq_b, wk_b, wv_b, wo_b, c2k_b, c2v_b = _in_proj(
        h, wt, lambda j: kv_block, 1, BF16, regroup=CMP_STRIDE, norm_w=attn_norm_w,
        casts=(wq_x, wk_x, wv_x, wo_x, cmp_w2_k, cmp_w2_v), name="in_proj_kv")
    za, wa_b, wb_b, wout_b, c1k_b, c1v_b = _in_proj(
        n, wt, skip_kv, 4, BF16, casts=(w_a, w_b, w_out, cmp_w1_k, cmp_w1_v), name="in_proj_a")
    zb, wup_b, wdown_b = _in_proj(
        n, wt, lambda j: W_IN_NSA_GATE // blk + j, B_WIDTH // blk, F32, shift=gate_shift,
        casts=(w_up, w_down), name="in_proj_b")
    (zg,) = _in_proj(n, wt, lambda j: W_IN_NSA_GATE // LANES, 1, F32, tn=LANES, name="in_proj_gate")

    assert seq // CMP_STRIDE == LANES
    rows_kv = NSA_GROUPS * batch * LANES
    k2 = kv.reshape(2 * rows_kv, CMP_STRIDE * NSA_DK)
    kc = _compress(k2, 0, rows_kv, cmp_pe_k.reshape(1, -1), c1k_b, c2k_b)
    vc = _compress(k2, rows_kv, rows_kv, cmp_pe_v.reshape(1, -1), c1v_b, c2v_b)

    o_nsa = _nsa_attention(za, zg, kc, vc, batch, seq)
    o_ret = _retention(zb, ret_gn_w, batch, seq)
    h1 = _merge(o_nsa, o_ret, zb, h, wa_b, wb_b, wout_b)

    kx, vx = _mem_kv(mem.reshape(batch * MEM_LEN, D_MODEL), mem_norm_w, wk_b, wv_b)
    h2, nm = _cross_attention(h1, x_norm_w, mlp_norm_w, wq_b, kx, vx, wo_b, seq)
    return _mlp(nm, wup_b, wdown_b, h2, out_norm_w)


def kernel(x, mem, attn_norm_w, w_in, cmp_pe_k, cmp_w1_k, cmp_w2_k, cmp_pe_v, cmp_w1_v, cmp_w2_v, w_a, ret_gn_w,
           w_b, w_out, x_norm_w, mem_norm_w, wq_x, wk_x, wv_x, wo_x, mlp_norm_w, w_up, w_down, final_norm_w):
    batch, seq, d = x.shape
    depth = w_in.shape[0]
    assert depth == 1
    h = x.reshape(batch * seq, d)
    out = _layer(h, mem, attn_norm_w[0], w_in[0], cmp_pe_k[0], cmp_w1_k[0], cmp_w2_k[0],
                 cmp_pe_v[0], cmp_w1_v[0], cmp_w2_v[0], w_a[0], ret_gn_w[0], w_b[0], w_out[0],
                 x_norm_w[0], mem_norm_w[0], wq_x[0], wk_x[0], wv_x[0], wo_x[0],
                 mlp_norm_w[0], w_up[0], w_down[0], final_norm_w, batch, seq)
    return out.reshape(batch, seq, d)
```

```python
import functools

import jax
import jax.numpy as jnp
import numpy as np
from jax import lax
from jax.experimental import pallas as pl
from jax.experimental.pallas import tpu as pltpu

F32 = jnp.float32
BF16 = jnp.bfloat16

D_MODEL = 2048
MEM_LEN = 256
NSA_HEADS = 16
NSA_GROUPS = 4
NSA_REP = NSA_HEADS // NSA_GROUPS
NSA_DK = 128
NSA_DV = 128
CMP_LEN = 32
CMP_STRIDE = 16
CMP_HIDDEN = 1024
SEL_LEN = 64
SEL_TOPK = 16
WIN = 512
RET_HEADS = 8
RET_DK = 128
RET_DV = 256
RET_CHUNK = 128
ROPE_BASE = 10000.0
X_HEADS = 4
X_DH = 128
D_FF = 4 * D_MODEL
EPS = 1e-6
NEG = -1e30
LOG2E = 1.4426950408889634

LANES = 128
MXU_ROWS = 256

W_IN_BLOCK = 1024
W_IN_KV = 2048
W_IN_NSA_GATE = 5120
A_QN = 0
A_KS = 2048
A_VS = 2560
A_KW = 3072
A_VW = 3584
B_QR = 0
B_KR = 1024
B_VR = 2048
B_GR = 4096
B_GA = 6144
B_GB = 8192
B_WIDTH = 10240

NT_DIMS = (((1,), (1,)), ((), ()))


def _params(sem, vmem_mb=None):
    kw = {"dimension_semantics": sem}
    if vmem_mb is not None:
        kw["vmem_limit_bytes"] = vmem_mb * 1024 * 1024
    return pltpu.CompilerParams(**kw)


def _rms(x, w):
    return x * lax.rsqrt(jnp.mean(x * x, axis=-1, keepdims=True) + EPS) * w


def _dot(a, b):
    return jnp.dot(a, b, preferred_element_type=F32)


def _dot_nt(a, b):
    return lax.dot_general(a, b, NT_DIMS, preferred_element_type=F32)


CAST_ROWS = 16


def _cast_rows(dst_ref, dst0, src_ref, src0, nrows):
    def body(r, carry):
        off = r * CAST_ROWS
        dst_ref[pl.ds(pl.multiple_of(dst0 + off, CAST_ROWS), CAST_ROWS), :] = (
            src_ref[pl.ds(pl.multiple_of(src0 + off, 8), CAST_ROWS), :].astype(BF16))
        return carry

    lax.fori_loop(0, nrows // CAST_ROWS, body, 0)


def _in_proj_kernel(*refs, shift, regroup, n_cast, norm, side):
    n_in = 4 if norm else 3
    n_ref, w_ref, wnext_ref = refs[:3]
    cast_in = refs[n_in:n_in + n_cast]
    o_ref = refs[n_in + n_cast]
    n_out = n_in + n_cast + 1 + int(norm) + int(side)
    cast_out = refs[n_out:n_out + n_cast]
    wb_sc, *rest = refs[n_out + n_cast:]
    tn = wb_sc.shape[0]

    @pl.when(pl.program_id(1) == 0)
    def _():
        _cast_rows(wb_sc, 0, w_ref, shift, tn - shift)
        if shift:
            _cast_rows(wb_sc, tn - shift, wnext_ref, 0, shift)
        if side:
            _cast_rows(rest[-1], 0, wnext_ref, 0, LANES)

    if norm:
        xn = _rms(n_ref[...], refs[3][...]).astype(BF16)
        refs[n_in + n_cast + 1][...] = xn
    else:
        xn = n_ref[...]
    if side:
        refs[n_out - 1][...] = _dot_nt(xn, rest[-1][...])
    res = _dot_nt(xn, wb_sc[...])
    if regroup:
        r_sc = rest[0]
        tm = res.shape[0]
        for c in range(tn // LANES):
            r_sc[c] = res[:, c * LANES:(c + 1) * LANES]
        for c in range(tn // LANES):
            for j in range(regroup):
                o_ref[c, :, j * LANES:(j + 1) * LANES] = (
                    r_sc[c, pl.ds(j, tm // regroup, stride=regroup), :].astype(o_ref.dtype))
    else:
        o_ref[...] = res.astype(o_ref.dtype)
    for src, dst in zip(cast_in, cast_out):
        dst[...] = src[...].astype(dst.dtype)


def _in_proj(n, wt, src_block, n_blocks, out_dtype, *, shift=0, regroup=0, casts=(), norm_w=None,
             side_block=None, tm=1024, tn=1024, name):
    m, k = n.shape
    per = tn // LANES
    m_tiles = m // tm
    norm = norm_w is not None
    side = side_block is not None
    assert shift % CAST_ROWS == 0 and shift <= LANES and not (norm and n_blocks > 1)
    assert not side or (shift == 0 and n_blocks == 1)
    scratch = [pltpu.VMEM((tn, k), BF16)]
    row_tile = pl.BlockSpec((tm, k), lambda j, i: (i, 0))
    once = {"pipeline_mode": pl.Buffered(1)} if n_blocks == 1 else {}
    if regroup:
        assert n_blocks == 1
        out_shape = jax.ShapeDtypeStruct((per, m // regroup, regroup * LANES), out_dtype)
        out_spec = pl.BlockSpec((per, tm // regroup, regroup * LANES), lambda j, i: (0, i, 0))
        scratch.append(pltpu.VMEM((per, tm, LANES), F32))
    else:
        out_shape = jax.ShapeDtypeStruct((m, n_blocks * tn), out_dtype)
        out_spec = pl.BlockSpec((tm, tn), lambda j, i: (i, j))

    cast_steps = 1 << ((n_blocks * m_tiles).bit_length() - 1)
    cast_specs = []
    for a in casts:
        rows = a.shape[0] // cast_steps
        assert a.shape[0] % cast_steps == 0 and rows % CAST_ROWS == 0
        cast_specs.append(pl.BlockSpec(
            (rows, a.shape[1]), lambda j, i: (jnp.minimum(j * m_tiles + i, cast_steps - 1), 0)))

    if side:
        scratch.append(pltpu.VMEM((LANES, k), BF16))
    next_rows = (lambda j, i: (side_block, 0)) if side else (lambda j, i: ((src_block(j) + 1) * per, 0))
    return pl.pallas_call(
        functools.partial(_in_proj_kernel, shift=shift, regroup=regroup, n_cast=len(casts), norm=norm, side=side),
        grid=(n_blocks, m_tiles),
        in_specs=[
            row_tile,
            pl.BlockSpec((tn, k), lambda j, i: (src_block(j), 0), **once),
            pl.BlockSpec((LANES, k), next_rows, **once),
            *([pl.BlockSpec((1, k), lambda j, i: (0, 0))] if norm else []),
            *cast_specs,
        ],
        out_specs=[out_spec, *([row_tile] if norm else []),
                   *([pl.BlockSpec((tm, LANES), lambda j, i: (i, 0))] if side else []), *cast_specs],
        out_shape=[out_shape, *([jax.ShapeDtypeStruct((m, k), BF16)] if norm else []),
                   *([jax.ShapeDtypeStruct((m, LANES), F32)] if side else []),
                   *[jax.ShapeDtypeStruct(a.shape, BF16) for a in casts]],
        scratch_shapes=scratch,
        compiler_params=_params(("arbitrary", "arbitrary"), 60 if norm else 56),
        name=name,
    )(n, wt, wt, *([norm_w.reshape(1, k)] if norm else []), *casts)


def _compress_kernel(k2_ref, pe_ref, w1_ref, w2_ref, o_ref):
    rows, half = k2_ref.shape
    k2 = k2_ref[...].astype(F32)
    a_lo = (k2 + pe_ref[:, :half]).astype(BF16)
    a_hi = (k2 + pe_ref[:, half:]).astype(BF16)
    lo = _dot(a_lo, w1_ref[:half, :])
    hi = _dot(a_hi, w1_ref[half:, :])
    h = lo + pltpu.roll(hi, rows - 1, axis=0)
    act = h * jax.nn.sigmoid(h)
    o_ref[...] = _dot(act.astype(BF16), w2_ref[...]).astype(o_ref.dtype)


def _compress(k2, first_row, m, pe, w1, w2, rows=256):
    half = k2.shape[1]
    hid = w1.shape[1]
    dout = w2.shape[1]
    first = first_row // rows
    return pl.pallas_call(
        _compress_kernel,
        grid=(m // rows,),
        in_specs=[
            pl.BlockSpec((rows, half), lambda i: (first + i, 0)),
            pl.BlockSpec((1, 2 * half), lambda i: (0, 0)),
            pl.BlockSpec((2 * half, hid), lambda i: (0, 0)),
            pl.BlockSpec((hid, dout), lambda i: (0, 0)),
        ],
        out_specs=pl.BlockSpec((rows, dout), lambda i: (i, 0)),
        out_shape=jax.ShapeDtypeStruct((m, dout), BF16),
        compiler_params=_params(("parallel",), 40),
        name="nsa_compress",
    )(k2, pe, w1, w2)


def _nsa_kernel(*refs, seq, tq, kblk, gp, hm):
    q_ref = refs[0]
    kc_refs = refs[1:1 + gp]
    vc_refs = refs[1 + gp:1 + 2 * gp]
    ks_ref, vs_ref, kw_ref, vw_ref, g_ref, e_ref, o_ref, s_sc, mx_sc, acc_sc, po_sc = refs[1 + 2 * gp:]
    i = pl.program_id(2)
    t0 = i * tq
    rep = NSA_REP
    c1 = NSA_DK ** -0.5 * LOG2E
    n_cmp = (seq - CMP_LEN) // CMP_STRIDE + 1
    n_sel = seq // SEL_LEN
    topk = min(SEL_TOPK, n_sel)
    groups = range(gp)

    q = q_ref[...]
    units = range(rep // hm)
    rows = lambda gg, r: slice((gg * rep + r) * tq, (gg * rep + r + 1) * tq)
    urows = lambda gg, u: slice((gg * rep + u * hm) * tq, (gg * rep + (u + 1) * hm) * tq)
    part = lambda x, h: x[h * tq:(h + 1) * tq]
    head_q = lambda gg, r: q[:, (gg * rep + r) * NSA_DK:(gg * rep + r + 1) * NSA_DK]
    unit_q = {(gg, u): jnp.concatenate([head_q(gg, u * hm + h) for h in range(hm)], axis=0)
              for gg in groups for u in units}
    group_cols = lambda ref, gg: ref.at[:, gg * LANES:(gg + 1) * LANES]

    def masked(s, mask1):
        return jnp.concatenate([jnp.where(mask1, part(s, h), NEG) for h in range(hm)], axis=0)
    tcol = t0 + lax.broadcasted_iota(jnp.int32, (tq, 1), 0)
    per_group = rep * 3
    g_sig = jax.nn.sigmoid(g_ref[...])
    first_group = pl.program_id(1) * gp
    gs = [pltpu.roll(g_sig, lax.rem(LANES - (first_group + gg) * per_group, LANES), axis=1) for gg in groups]
    gate = lambda gg, r, branch: gs[gg][:, 3 * r + branch:3 * r + branch + 1]

    def exp_rows(sm):
        return jnp.exp2((sm - jnp.max(sm, axis=-1, keepdims=True)) * c1)

    def with_ones(v):
        return jnp.concatenate([v, jnp.ones(v.shape, v.dtype)], axis=1)

    def normalised(ev):
        return ev[:, :NSA_DV] / ev[:, NSA_DV:]

    c_idx = lax.broadcasted_iota(jnp.int32, (tq, LANES), 1)
    mask_c = ((c_idx * CMP_STRIDE + (CMP_LEN - 1)) <= tcol) & (c_idx < n_cmp)
    mask_cf = jnp.where(mask_c, 1.0, 0.0)
    o_cmp, psum = {}, []
    for gg in groups:
        kc = kc_refs[gg][...]
        vc = vc_refs[gg][...]
        tot = None
        for u in units:
            e = exp_rows(masked(_dot_nt(unit_q[gg, u], kc), mask_c))
            p = e / jnp.sum(e, axis=-1, keepdims=True)
            p = jnp.concatenate([part(p, h) * mask_cf for h in range(hm)], axis=0)
            o_cmp[gg, u] = _dot(p.astype(BF16), vc)
            for h in range(hm):
                tot = part(p, h) if tot is None else tot + part(p, h)
        psum.append(tot)

    wlen = WIN + tq
    start = pl.multiple_of(jnp.maximum(i - WIN // tq, 0) * tq, tq)
    dlt = tcol - (start + lax.broadcasted_iota(jnp.int32, (tq, wlen), 1))
    mask_w = (dlt >= 0) & (dlt < WIN)
    for gg in groups:
        kw = group_cols(kw_ref, gg)[pl.ds(start, wlen), :]
        vw1 = with_ones(group_cols(vw_ref, gg)[pl.ds(start, wlen), :])
        for u in units:
            e = exp_rows(masked(_dot_nt(unit_q[gg, u], kw), mask_w))
            o_win = normalised(_dot(e.astype(BF16), vw1))
            for h in range(hm):
                r = u * hm + h
                po_sc[rows(gg, r), :] = (gate(gg, r, 0) * part(o_cmp[gg, u], h)
                                         + gate(gg, r, 2) * part(o_win, h))

    jo = lax.broadcasted_iota(jnp.int32, (n_sel, LANES), 0)
    co = lax.broadcasted_iota(jnp.int32, (n_sel, LANES), 1)
    ov_t = jnp.where((co * CMP_STRIDE < jo * SEL_LEN + SEL_LEN) & (co * CMP_STRIDE + CMP_LEN > jo * SEL_LEN)
                     & (co < n_cmp), 1.0, 0.0).astype(BF16)
    j_idx = lax.broadcasted_iota(jnp.int32, (n_sel, tq), 0)
    cur = lax.shift_right_logical(t0 + lax.broadcasted_iota(jnp.int32, (n_sel, tq), 1), int(np.log2(SEL_LEN)))
    forced = (j_idx == 0) | (j_idx == cur) | (j_idx == cur - 1)
    future = j_idx > cur
    sel = []
    for gg in groups:
        p_hi = psum[gg].astype(BF16)
        p_lo = (psum[gg] - p_hi.astype(F32)).astype(BF16)
        imp = _dot_nt(ov_t, p_hi) + _dot_nt(ov_t, p_lo)
        impm = jnp.where(forced, jnp.inf, jnp.where(future, -jnp.inf, imp))
        rank = jnp.zeros((n_sel, tq), F32)
        for ii in range(n_sel):
            row = impm[ii:ii + 1, :]
            beats = (row > impm) | ((row == impm) & (j_idx > ii))
            rank = rank + jnp.where(beats, 1.0, 0.0)
        sel_t = jnp.where(rank < topk, 1.0, 0.0)
        sel.append(jnp.concatenate([sel_t, jnp.zeros((LANES - n_sel, tq), F32)], axis=0).T.astype(BF16))

    n_chunks = lax.div(t0 + tq + (kblk - 1), kblk)

    def lane_fold_max(x):
        out = x[:, :LANES]
        for c in range(1, kblk // LANES):
            out = jnp.maximum(out, x[:, c * LANES:(c + 1) * LANES])
        return out

    def chunk_start(kb):
        return kb * kblk if isinstance(kb, int) else pl.multiple_of(kb * kblk, kblk)

    def score_chunk(kb):
        k0 = chunk_start(kb)
        causal = (k0 + lax.broadcasted_iota(jnp.int32, (tq, kblk), 1)) <= tcol
        folds = []
        for gg in groups:
            k = group_cols(ks_ref, gg)[pl.ds(k0, kblk), :]
            mask1 = (_dot(sel[gg], e_ref[kb]) > 0.5) & causal
            for u in units:
                sm_ = masked(_dot_nt(unit_q[gg, u], k), mask1)
                s_sc[kb, urows(gg, u), :] = sm_
                folds.append(lane_fold_max(sm_))
        return jnp.concatenate(folds, axis=0)

    def score_pass(kb, carry):
        mx_sc[...] = jnp.maximum(mx_sc[...], score_chunk(kb))
        return carry

    mx_sc[...] = score_chunk(0)
    lax.fori_loop(1, n_chunks, score_pass, 0)
    m_sel = jnp.max(mx_sc[...], axis=-1, keepdims=True)

    def value_chunk(kb):
        k0 = chunk_start(kb)
        pvs = []
        for gg in groups:
            v1 = with_ones(group_cols(vs_ref, gg)[pl.ds(k0, kblk), :])
            for u in units:
                pk = jnp.exp2((s_sc[kb, urows(gg, u), :] - m_sel[urows(gg, u)]) * c1)
                pvs.append(_dot(pk.astype(BF16), v1))
        return jnp.concatenate(pvs, axis=0)

    def value_pass(kb, carry):
        acc_sc[...] += value_chunk(kb)
        return carry

    acc_sc[...] = value_chunk(0)
    lax.fori_loop(1, n_chunks, value_pass, 0)

    for gg in groups:
        for r in range(rep):
            o = po_sc[rows(gg, r), :] + gate(gg, r, 1) * normalised(acc_sc[rows(gg, r), :])
            col = (gg * rep + r) * NSA_DV
            o_ref[:, col:col + NSA_DV] = o.astype(o_ref.dtype)


def _nsa_attention(za, zg, kc, vc, batch, seq, tq=256, kblk=512, gp=2):
    assert seq % kblk == 0 and seq >= WIN + tq and WIN % tq == 0 and NSA_GROUPS % gp == 0
    nq = seq // tq
    gw = gp * NSA_REP * NSA_DK
    streams = gp * NSA_REP * tq
    hm = max(1, MXU_ROWS // tq)
    assert NSA_REP % hm == 0
    kern = functools.partial(_nsa_kernel, seq=seq, tq=tq, kblk=kblk, gp=gp, hm=hm)
    key = np.arange(seq).reshape(seq // kblk, 1, kblk)
    expand = jnp.asarray(key // SEL_LEN == np.arange(LANES).reshape(1, LANES, 1), BF16)

    def kv_spec(off):
        return pl.BlockSpec((seq, gp * LANES), lambda b, g, i: (b, off // (gp * LANES) + g))

    def cmp_spec(gg):
        return pl.BlockSpec((LANES, NSA_DK), lambda b, g, i: ((g * gp + gg) * batch + b, 0))

    return pl.pallas_call(
        kern,
        grid=(batch, NSA_GROUPS // gp, nq),
        in_specs=[
            pl.BlockSpec((tq, gw), lambda b, g, i: (b * nq + i, A_QN // gw + g)),
            *[cmp_spec(gg) for gg in range(gp)],
            *[cmp_spec(gg) for gg in range(gp)],
            kv_spec(A_KS), kv_spec(A_VS), kv_spec(A_KW), kv_spec(A_VW),
            pl.BlockSpec((tq, LANES), lambda b, g, i: (b * nq + i, 0)),
            pl.BlockSpec(expand.shape, lambda b, g, i: (0, 0, 0)),
        ],
        out_specs=pl.BlockSpec((tq, gw), lambda b, g, i: (b * nq + i, g)),
        out_shape=jax.ShapeDtypeStruct((batch * seq, NSA_HEADS * NSA_DV), BF16),
        scratch_shapes=[
            pltpu.VMEM((seq // kblk, streams, kblk), F32),
            pltpu.VMEM((streams, LANES), F32),
            pltpu.VMEM((streams, 2 * NSA_DV), F32),
            pltpu.VMEM((streams, NSA_DV), F32),
        ],
        compiler_params=_params(("parallel", "parallel", "arbitrary"), 56),
        name="nsa_attention",
    )(za, *([kc] * gp), *([vc] * gp), za, za, za, za, zg, expand)


def _retention_kernel(q_ref, k_ref, v_ref, g_ref, cos_ref, sin_e_ref, sin_o_ref, dec_ref, wq_ref, wk_ref,
                      gc_ref, gn_ref, o_ref, st_ref):
    @pl.when(pl.program_id(1) == 0)
    def _():
        st_ref[...] = jnp.zeros(st_ref.shape, F32)

    c = RET_CHUNK
    for h in range(RET_HEADS):
        st = st_ref[h]
        wq = wq_ref[h]
        wq2 = jnp.concatenate([wq, wq], axis=1)
        cols = slice(h * RET_DV, (h + 1) * RET_DV)
        for sub in range(q_ref.shape[0] // c):
            rows = slice(sub * c, (sub + 1) * c)
            cos = cos_ref[rows, :]
            sin_e = sin_e_ref[rows, :]
            sin_o = sin_o_ref[rows, :]

            def rotate(x):
                return x * cos + pltpu.roll(x, RET_DK - 1, axis=1) * sin_e + pltpu.roll(x, 1, axis=1) * sin_o

            qf = rotate(q_ref[rows, h * RET_DK:(h + 1) * RET_DK])
            kf = rotate(k_ref[rows, h * RET_DK:(h + 1) * RET_DK]) * (RET_DK ** -0.5)
            qb = qf.astype(BF16)
            v = v_ref[rows, cols].astype(BF16)
            s = _dot_nt(qb, kf.astype(BF16)) * dec_ref[h]
            o = _dot(s.astype(BF16), v) + _dot(qb, st.astype(BF16)) * wq2
            st = st * gc_ref[h] + _dot((kf * wk_ref[h]).T.astype(BF16), v)
            mu = jnp.mean(o, axis=-1, keepdims=True)
            d = o - mu
            var = jnp.mean(d * d, axis=-1, keepdims=True)
            on = d * lax.rsqrt(var + EPS) * gn_ref[:, cols]
            gr = g_ref[rows, cols]
            o_ref[rows, cols] = (gr * jax.nn.sigmoid(gr) * on).astype(o_ref.dtype)
        st_ref[h] = st


def _retention(zb, gn_w, batch, seq, chunks_per_step=2):
    c = RET_CHUNK
    rows = chunks_per_step * c
    nc = seq // rows
    hq = RET_HEADS * RET_DK
    hv = RET_HEADS * RET_DV
    f32 = np.float32
    inv = f32(ROPE_BASE) ** (-np.arange(0, RET_DK, 2, dtype=f32) / f32(RET_DK))
    ang = np.arange(seq, dtype=f32)[:, None] * inv[None, :]
    zero = np.zeros_like(ang)
    pairs = lambda even, odd: np.stack([even, odd], axis=-1).reshape(seq, RET_DK)
    cos = pairs(np.cos(ang), np.cos(ang))
    sin_e = pairs(-np.sin(ang), zero)
    sin_o = pairs(zero, np.sin(ang))
    log_g = np.log1p(-np.exp2(f32(-5.0) - np.arange(RET_HEADS, dtype=f32)))
    idx = np.arange(c, dtype=f32)
    rel = idx[:, None] - idx[None, :]
    decay = np.where(rel >= 0, np.exp(log_g[:, None, None] * np.maximum(rel, f32(0.0))), f32(0.0)).astype(f32)
    lanes = lambda a: np.ascontiguousarray(np.broadcast_to(a[:, :, None], (RET_HEADS, c, RET_DK)), dtype=f32)
    w_k = lanes(np.exp(log_g[:, None] * (f32(c - 1) - idx)[None, :]))
    w_q = lanes(np.exp(log_g[:, None] * (idx + f32(1.0))[None, :]))
    g_chunk = np.broadcast_to(np.exp(log_g * f32(c))[:, None, None], (RET_HEADS, 1, RET_DV)).astype(f32)

    row = lambda b, n: b * nc + n
    return pl.pallas_call(
        _retention_kernel,
        grid=(batch, nc),
        in_specs=[
            pl.BlockSpec((rows, hq), lambda b, n: (row(b, n), B_QR // hq)),
            pl.BlockSpec((rows, hq), lambda b, n: (row(b, n), B_KR // hq)),
            pl.BlockSpec((rows, hv), lambda b, n: (row(b, n), B_VR // hv)),
            pl.BlockSpec((rows, hv), lambda b, n: (row(b, n), B_GR // hv)),
            pl.BlockSpec((rows, RET_DK), lambda b, n: (n, 0)),
            pl.BlockSpec((rows, RET_DK), lambda b, n: (n, 0)),
            pl.BlockSpec((rows, RET_DK), lambda b, n: (n, 0)),
            pl.BlockSpec((RET_HEADS, c, c), lambda b, n: (0, 0, 0)),
            pl.BlockSpec((RET_HEADS, c, RET_DK), lambda b, n: (0, 0, 0)),
            pl.BlockSpec((RET_HEADS, c, RET_DK), lambda b, n: (0, 0, 0)),
            pl.BlockSpec((RET_HEADS, 1, RET_DV), lambda b, n: (0, 0, 0)),
            pl.BlockSpec((1, hv), lambda b, n: (0, 0)),
        ],
        out_specs=pl.BlockSpec((rows, hv), lambda b, n: (row(b, n), 0)),
        out_shape=jax.ShapeDtypeStruct((batch * seq, hv), BF16),
        scratch_shapes=[pltpu.VMEM((RET_HEADS, RET_DK, RET_DV), F32)],
        compiler_params=_params(("parallel", "arbitrary"), 40),
        name="retention",
    )(zb, zb, zb, zb, cos, sin_e, sin_o, decay, w_q, w_k, g_chunk, gn_w.reshape(1, hv))


def _merge_kernel(on_ref, or_ref, ga_ref, gb_ref, x_ref, wa_ref, wb_ref, wo_ref, h_ref):
    a = _dot(on_ref[...], wa_ref[...])
    b = _dot(or_ref[...], wb_ref[...])
    merged = jax.nn.sigmoid(ga_ref[...]) * a + jax.nn.sigmoid(gb_ref[...]) * b
    h_ref[...] = x_ref[...] + _dot(merged.astype(BF16), wo_ref[...])


def _merge(o_nsa, o_ret, zb, x, w_a, w_b, w_out, tm=256):
    m, d = x.shape
    resident = lambda shape: pl.BlockSpec(shape, lambda i: (0, 0), pipeline_mode=pl.Buffered(1))
    return pl.pallas_call(
        _merge_kernel,
        grid=(m // tm,),
        in_specs=[
            pl.BlockSpec((tm, d), lambda i: (i, 0)),
            pl.BlockSpec((tm, d), lambda i: (i, 0)),
            pl.BlockSpec((tm, d), lambda i: (i, B_GA // d)),
            pl.BlockSpec((tm, d), lambda i: (i, B_GB // d)),
            pl.BlockSpec((tm, d), lambda i: (i, 0)),
            resident(w_a.shape), resident(w_b.shape), resident(w_out.shape),
        ],
        out_specs=pl.BlockSpec((tm, d), lambda i: (i, 0)),
        out_shape=jax.ShapeDtypeStruct((m, d), F32),
        compiler_params=_params(("parallel",), 56),
        name="merge_out_proj",
    )(o_nsa, o_ret, zb, zb, x, w_a, w_b, w_out)


def _mem_kv_kernel(m_ref, nw_ref, wk_ref, wv_ref, k_ref, v_ref):
    mn = _rms(m_ref[...], nw_ref[...]).astype(BF16)
    k_ref[...] = _dot(mn, wk_ref[...]).astype(k_ref.dtype)
    v_ref[...] = _dot(mn, wv_ref[...]).astype(v_ref.dtype)


def _mem_kv(mem2, nw, wk, wv, tm=256):
    m, d = mem2.shape
    n = wk.shape[1]
    out = jax.ShapeDtypeStruct((m, n), BF16)
    return pl.pallas_call(
        _mem_kv_kernel,
        grid=(m // tm,),
        in_specs=[
            pl.BlockSpec((tm, d), lambda i: (i, 0)),
            pl.BlockSpec((1, d), lambda i: (0, 0)),
            pl.BlockSpec((d, n), lambda i: (0, 0)),
            pl.BlockSpec((d, n), lambda i: (0, 0)),
        ],
        out_specs=[pl.BlockSpec((tm, n), lambda i: (i, 0))] * 2,
        out_shape=[out, out],
        compiler_params=_params(("parallel",), 40),
        name="mem_kv_proj",
    )(mem2, nw.reshape(1, d), wk, wv)


def _cross_kernel(h_ref, xw_ref, mw_ref, wq_ref, kx_ref, vx_ref, wo_ref, h2_ref, nm_ref):
    h = h_ref[...]
    nx = _rms(h, xw_ref[...]).astype(BF16)
    qx = _dot(nx, wq_ref[...]).astype(BF16)
    outs = []
    for hh in range(X_HEADS):
        cols = slice(hh * X_DH, (hh + 1) * X_DH)
        s = _dot_nt(qx[:, cols], kx_ref[:, cols]) * (X_DH ** -0.5)
        e = jnp.exp(s - jnp.max(s, axis=-1, keepdims=True))
        p = e / jnp.sum(e, axis=-1, keepdims=True)
        outs.append(_dot(p.astype(BF16), vx_ref[:, cols]))
    ox = jnp.concatenate(outs, axis=-1).astype(BF16)
    h2 = h + _dot(ox, wo_ref[...])
    h2_ref[...] = h2
    nm_ref[...] = _rms(h2, mw_ref[...]).astype(nm_ref.dtype)


def _cross_attention(h1, x_norm_w, mlp_norm_w, wq, kx, vx, wo, seq, tm=512):
    m, d = h1.shape
    n = wq.shape[1]
    per_batch = seq // tm
    vec = lambda: pl.BlockSpec((1, d), lambda i: (0, 0))
    return pl.pallas_call(
        _cross_kernel,
        grid=(m // tm,),
        in_specs=[
            pl.BlockSpec((tm, d), lambda i: (i, 0)),
            vec(), vec(),
            pl.BlockSpec((d, n), lambda i: (0, 0)),
            pl.BlockSpec((MEM_LEN, n), lambda i: (i // per_batch, 0)),
            pl.BlockSpec((MEM_LEN, n), lambda i: (i // per_batch, 0)),
            pl.BlockSpec((n, d), lambda i: (0, 0)),
        ],
        out_specs=[pl.BlockSpec((tm, d), lambda i: (i, 0))] * 2,
        out_shape=[jax.ShapeDtypeStruct((m, d), F32), jax.ShapeDtypeStruct((m, d), BF16)],
        compiler_params=_params(("parallel",), 40),
        name="cross_attention",
    )(h1, x_norm_w.reshape(1, d), mlp_norm_w.reshape(1, d), wq, kx, vx, wo)


def _mlp_kernel(nm_ref, wu_ref, wd_ref, h_ref, fw_ref, o_ref):
    j = pl.program_id(1)

    @pl.when(j == 0)
    def _():
        o_ref[...] = jnp.zeros(o_ref.shape, F32)

    u = jnp.maximum(_dot(nm_ref[...], wu_ref[...]), 0.0)
    o_ref[...] += _dot((u * u).astype(BF16), wd_ref[...])

    @pl.when(j == pl.num_programs(1) - 1)
    def _():
        o_ref[...] = _rms(h_ref[...] + o_ref[...], fw_ref[...])


def _mlp(nm, w_up, w_down, h2, final_w, tm=512, tf=1024):
    m, d = nm.shape
    f = w_up.shape[1]
    return pl.pallas_call(
        _mlp_kernel,
        grid=(m // tm, f // tf),
        in_specs=[
            pl.BlockSpec((tm, d), lambda i, j: (i, 0)),
            pl.BlockSpec((d, tf), lambda i, j: (0, j)),
            pl.BlockSpec((tf, d), lambda i, j: (j, 0)),
            pl.BlockSpec((tm, d), lambda i, j: (i, 0)),
            pl.BlockSpec((1, d), lambda i, j: (0, 0)),
        ],
        out_specs=pl.BlockSpec((tm, d), lambda i, j: (i, 0)),
        out_shape=jax.ShapeDtypeStruct((m, d), F32),
        compiler_params=_params(("parallel", "arbitrary"), 56),
        name="mlp_final_norm",
    )(nm, w_up, w_down, h2, final_w.reshape(1, d))


def _layer(h, mem, attn_norm_w, w_in, cmp_pe_k, cmp_w1_k, cmp_w2_k, cmp_pe_v, cmp_w1_v, cmp_w2_v,
           w_a, ret_gn_w, w_b, w_out, x_norm_w, mem_norm_w, wq_x, wk_x, wv_x, wo_x,
           mlp_norm_w, w_up, w_down, out_norm_w, batch, seq):
    blk = W_IN_BLOCK
    kv_block = W_IN_KV // blk
    gate_shift = NSA_HEADS * 3

    wt = w_in.T
    skip_kv = lambda j: j + (j >= kv_block).astype(jnp.int32)
    kv, n, zg, wq_b, wk_b, wv_b, wo_b, c2k_b, c2v_b = _in_proj(
        h, wt, lambda j: kv_block, 1, BF16, regroup=CMP_STRIDE, norm_w=attn_norm_w,
        side_block=W_IN_NSA_GATE // LANES, casts=(wq_x, wk_x, wv_x, wo_x, cmp_w2_k, cmp_w2_v), name="in_proj_kv")
    za, wa_b, wb_b, wout_b, c1k_b, c1v_b = _in_proj(
        n, wt, skip_kv, 4, BF16, casts=(w_a, w_b, w_out, cmp_w1_k, cmp_w1_v), name="in_proj_a")
    zb, wup_b, wdown_b = _in_proj(
        n, wt, lambda j: W_IN_NSA_GATE // blk + j, B_WIDTH // blk, F32, shift=gate_shift,
        casts=(w_up, w_down), name="in_proj_b")

    assert seq // CMP_STRIDE == LANES
    rows_kv = NSA_GROUPS * batch * LANES
    k2 = kv.reshape(2 * rows_kv, CMP_STRIDE * NSA_DK)
    kc = _compress(k2, 0, rows_kv, cmp_pe_k.reshape(1, -1), c1k_b, c2k_b)
    vc = _compress(k2, rows_kv, rows_kv, cmp_pe_v.reshape(1, -1), c1v_b, c2v_b)

    o_nsa = _nsa_attention(za, zg, kc, vc, batch, seq)
    o_ret = _retention(zb, ret_gn_w, batch, seq)
    h1 = _merge(o_nsa, o_ret, zb, h, wa_b, wb_b, wout_b)

    kx, vx = _mem_kv(mem.reshape(batch * MEM_LEN, D_MODEL), mem_norm_w, wk_b, wv_b)
    h2, nm = _cross_attention(h1, x_norm_w, mlp_norm_w, wq_b, kx, vx, wo_b, seq)
    return _mlp(nm, wup_b, wdown_b, h2, out_norm_w)


def kernel(x, mem, attn_norm_w, w_in, cmp_pe_k, cmp_w1_k, cmp_w2_k, cmp_pe_v, cmp_w1_v, cmp_w2_v, w_a, ret_gn_w,
           w_b, w_out, x_norm_w, mem_norm_w, wq_x, wk_x, wv_x, wo_x, mlp_norm_w, w_up, w_down, final_norm_w):
    batch, seq, d = x.shape
    depth = w_in.shape[0]
    assert depth == 1
    h = x.reshape(batch * seq, d)
    out = _layer(h, mem, attn_norm_w[0], w_in[0], cmp_pe_k[0], cmp_w1_k[0], cmp_w2_k[0],
                 cmp_pe_v[0], cmp_w1_v[0], cmp_w2_v[0], w_a[0], ret_gn_w[0], w_b[0], w_out[0],
                 x_norm_w[0], mem_norm_w[0], wq_x[0], wk_x[0], wv_x[0], wo_x[0],
                 mlp_norm_w[0], w_up[0], w_down[0], final_norm_w, batch, seq)
    return out.reshape(batch, seq, d)
```

```python
import functools

import jax
import jax.numpy as jnp
import numpy as np
from jax import lax
from jax.experimental import pallas as pl
from jax.experimental.pallas import tpu as pltpu

F32 = jnp.float32
BF16 = jnp.bfloat16

D_MODEL = 2048
MEM_LEN = 256
NSA_HEADS = 16
NSA_GROUPS = 4
NSA_REP = NSA_HEADS // NSA_GROUPS
NSA_DK = 128
NSA_DV = 128
CMP_LEN = 32
CMP_STRIDE = 16
CMP_HIDDEN = 1024
SEL_LEN = 64
SEL_TOPK = 16
WIN = 512
RET_HEADS = 8
RET_DK = 128
RET_DV = 256
RET_CHUNK = 128
ROPE_BASE = 10000.0
X_HEADS = 4
X_DH = 128
D_FF = 4 * D_MODEL
EPS = 1e-6
NEG = -1e30
LOG2E = 1.4426950408889634
NSA_Q_SCALE = NSA_DK ** -0.5 * LOG2E

LANES = 128
MXU_ROWS = 256

W_IN_BLOCK = 1024
W_IN_KV = 2048
W_IN_NSA_GATE = 5120
A_QN = 0
A_KS = 2048
A_VS = 2560
A_KW = 3072
A_VW = 3584
B_QR = 0
B_KR = 1024
B_VR = 2048
B_GR = 4096
B_GA = 6144
B_GB = 8192
B_WIDTH = 10240

NT_DIMS = (((1,), (1,)), ((), ()))


def _params(sem, vmem_mb=None):
    kw = {"dimension_semantics": sem}
    if vmem_mb is not None:
        kw["vmem_limit_bytes"] = vmem_mb * 1024 * 1024
    return pltpu.CompilerParams(**kw)


def _rms(x, w):
    return x * lax.rsqrt(jnp.mean(x * x, axis=-1, keepdims=True) + EPS) * w


def _dot(a, b):
    return jnp.dot(a, b, preferred_element_type=F32)


def _dot_nt(a, b):
    return lax.dot_general(a, b, NT_DIMS, preferred_element_type=F32)


CAST_ROWS = 16


def _cast_rows(dst_ref, dst0, src_ref, src0, nrows):
    def body(r, carry):
        off = r * CAST_ROWS
        dst_ref[pl.ds(pl.multiple_of(dst0 + off, CAST_ROWS), CAST_ROWS), :] = (
            src_ref[pl.ds(pl.multiple_of(src0 + off, 8), CAST_ROWS), :].astype(BF16))
        return carry

    lax.fori_loop(0, nrows // CAST_ROWS, body, 0)


def _in_proj_kernel(*refs, shift, regroup, n_cast, norm, side, lead_scale):
    n_in = 4 if norm else 3
    n_ref, w_ref, wnext_ref = refs[:3]
    cast_in = refs[n_in:n_in + n_cast]
    o_ref = refs[n_in + n_cast]
    n_out = n_in + n_cast + 1 + int(norm) + int(side)
    cast_out = refs[n_out:n_out + n_cast]
    wb_sc, *rest = refs[n_out + n_cast:]
    tn = wb_sc.shape[0]
    for src, dst in zip(cast_in, cast_out):
        dst[...] = src[...].astype(dst.dtype)

    @pl.when(pl.program_id(1) == 0)
    def _():
        _cast_rows(wb_sc, 0, w_ref, shift, tn - shift)
        if shift:
            _cast_rows(wb_sc, tn - shift, wnext_ref, 0, shift)
        if side:
            _cast_rows(rest[-1], 0, wnext_ref, 0, LANES)

    if norm:
        xn = _rms(n_ref[...], refs[3][...]).astype(BF16)
        refs[n_in + n_cast + 1][...] = xn
    else:
        xn = n_ref[...]
    if side:
        refs[n_out - 1][...] = _dot_nt(xn, rest[-1][...])
    res = _dot_nt(xn, wb_sc[...])
    if regroup:
        r_sc = rest[0]
        tm = res.shape[0]
        for c in range(tn // LANES):
            r_sc[c] = res[:, c * LANES:(c + 1) * LANES]
        for c in range(tn // LANES):
            for j in range(regroup):
                o_ref[c, :, j * LANES:(j + 1) * LANES] = (
                    r_sc[c, pl.ds(j, tm // regroup, stride=regroup), :].astype(o_ref.dtype))
    else:
        if lead_scale is not None:
            factor, blocks = lead_scale
            res = res * jnp.where(pl.program_id(0) < blocks, factor, 1.0)
        o_ref[...] = res.astype(o_ref.dtype)


def _in_proj(n, wt, src_block, n_blocks, out_dtype, *, shift=0, regroup=0, casts=(), norm_w=None,
             side_block=None, lead_scale=None, tm=1024, tn=1024, name):
    m, k = n.shape
    per = tn // LANES
    m_tiles = m // tm
    norm = norm_w is not None
    side = side_block is not None
    assert shift % CAST_ROWS == 0 and shift <= LANES and not (norm and n_blocks > 1)
    assert not side or (shift == 0 and n_blocks == 1)
    scratch = [pltpu.VMEM((tn, k), BF16)]
    row_tile = pl.BlockSpec((tm, k), lambda j, i: (i, 0))
    once = {"pipeline_mode": pl.Buffered(1)} if n_blocks == 1 else {}
    if regroup:
        assert n_blocks == 1
        out_shape = jax.ShapeDtypeStruct((per, m // regroup, regroup * LANES), out_dtype)
        out_spec = pl.BlockSpec((per, tm // regroup, regroup * LANES), lambda j, i: (0, i, 0))
        scratch.append(pltpu.VMEM((per, tm, LANES), F32))
    else:
        out_shape = jax.ShapeDtypeStruct((m, n_blocks * tn), out_dtype)
        out_spec = pl.BlockSpec((tm, tn), lambda j, i: (i, j))

    cast_steps = 1 << ((n_blocks * m_tiles).bit_length() - 1)
    cast_specs = []
    for a in casts:
        rows = a.shape[0] // cast_steps
        assert a.shape[0] % cast_steps == 0 and rows % CAST_ROWS == 0
        cast_specs.append(pl.BlockSpec(
            (rows, a.shape[1]), lambda j, i: (jnp.minimum(j * m_tiles + i, cast_steps - 1), 0)))

    if side:
        scratch.append(pltpu.VMEM((LANES, k), BF16))
    next_rows = (lambda j, i: (side_block, 0)) if side else (lambda j, i: ((src_block(j) + 1) * per, 0))
    return pl.pallas_call(
        functools.partial(_in_proj_kernel, shift=shift, regroup=regroup, n_cast=len(casts), norm=norm, side=side,
                          lead_scale=lead_scale),
        grid=(n_blocks, m_tiles),
        in_specs=[
            row_tile,
            pl.BlockSpec((tn, k), lambda j, i: (src_block(j), 0), **once),
            pl.BlockSpec((LANES, k), next_rows, **once),
            *([pl.BlockSpec((1, k), lambda j, i: (0, 0))] if norm else []),
            *cast_specs,
        ],
        out_specs=[out_spec, *([row_tile] if norm else []),
                   *([pl.BlockSpec((tm, LANES), lambda j, i: (i, 0))] if side else []), *cast_specs],
        out_shape=[out_shape, *([jax.ShapeDtypeStruct((m, k), BF16)] if norm else []),
                   *([jax.ShapeDtypeStruct((m, LANES), F32)] if side else []),
                   *[jax.ShapeDtypeStruct(a.shape, BF16) for a in casts]],
        scratch_shapes=scratch,
        compiler_params=_params(("arbitrary", "arbitrary"), 60 if norm else 56),
        name=name,
    )(n, wt, wt, *([norm_w.reshape(1, k)] if norm else []), *casts)


def _compress_kernel(k2_ref, pe_ref, w1_ref, w2_ref, o_ref):
    rows, half = k2_ref.shape
    k2 = k2_ref[...].astype(F32)
    a_lo = (k2 + pe_ref[:, :half]).astype(BF16)
    a_hi = (k2 + pe_ref[:, half:]).astype(BF16)
    lo = _dot(a_lo, w1_ref[:half, :])
    hi = _dot(a_hi, w1_ref[half:, :])
    h = lo + pltpu.roll(hi, rows - 1, axis=0)
    act = h * jax.nn.sigmoid(h)
    o_ref[...] = _dot(act.astype(BF16), w2_ref[...]).astype(o_ref.dtype)


def _compress(k2, first_row, m, pe, w1, w2, rows=256):
    half = k2.shape[1]
    hid = w1.shape[1]
    dout = w2.shape[1]
    first = first_row // rows
    return pl.pallas_call(
        _compress_kernel,
        grid=(m // rows,),
        in_specs=[
            pl.BlockSpec((rows, half), lambda i: (first + i, 0)),
            pl.BlockSpec((1, 2 * half), lambda i: (0, 0)),
            pl.BlockSpec((2 * half, hid), lambda i: (0, 0)),
            pl.BlockSpec((hid, dout), lambda i: (0, 0)),
        ],
        out_specs=pl.BlockSpec((rows, dout), lambda i: (i, 0)),
        out_shape=jax.ShapeDtypeStruct((m, dout), BF16),
        compiler_params=_params(("parallel",), 40),
        name="nsa_compress",
    )(k2, pe, w1, w2)


def _nsa_kernel(*refs, seq, tq, kblk, gp, hm):
    q_ref = refs[0]
    kc_refs = refs[1:1 + gp]
    vc_refs = refs[1 + gp:1 + 2 * gp]
    ks_ref, vs_ref, kw_ref, vw_ref, g_ref, e_ref, o_ref, s_sc, mx_sc, acc_sc, po_sc = refs[1 + 2 * gp:]
    i = pl.program_id(2)
    t0 = i * tq
    rep = NSA_REP
    n_cmp = (seq - CMP_LEN) // CMP_STRIDE + 1
    n_sel = seq // SEL_LEN
    topk = min(SEL_TOPK, n_sel)
    groups = range(gp)

    q = q_ref[...]
    units = range(rep // hm)
    rows = lambda gg, r: slice((gg * rep + r) * tq, (gg * rep + r + 1) * tq)
    urows = lambda gg, u: slice((gg * rep + u * hm) * tq, (gg * rep + (u + 1) * hm) * tq)
    part = lambda x, h: x[h * tq:(h + 1) * tq]
    head_q = lambda gg, r: q[:, (gg * rep + r) * NSA_DK:(gg * rep + r + 1) * NSA_DK]
    unit_q = {(gg, u): jnp.concatenate([head_q(gg, u * hm + h) for h in range(hm)], axis=0)
              for gg in groups for u in units}
    group_cols = lambda ref, gg: ref.at[:, gg * LANES:(gg + 1) * LANES]

    def masked(s, mask1):
        return jnp.concatenate([jnp.where(mask1, part(s, h), NEG) for h in range(hm)], axis=0)
    tcol = t0 + lax.broadcasted_iota(jnp.int32, (tq, 1), 0)
    per_group = rep * 3
    g_sig = jax.nn.sigmoid(g_ref[...])
    first_group = pl.program_id(1) * gp
    gs = [pltpu.roll(g_sig, lax.rem(LANES - (first_group + gg) * per_group, LANES), axis=1) for gg in groups]
    gate = lambda gg, r, branch: gs[gg][:, 3 * r + branch:3 * r + branch + 1]

    def exp_rows(sm):
        return jnp.exp2(sm - jnp.max(sm, axis=-1, keepdims=True))

    def with_ones(v):
        return jnp.concatenate([v, jnp.ones(v.shape, v.dtype)], axis=1)

    def normalised(ev):
        return ev[:, :NSA_DV] / ev[:, NSA_DV:]

    c_idx = lax.broadcasted_iota(jnp.int32, (tq, LANES), 1)
    mask_c = ((c_idx * CMP_STRIDE + (CMP_LEN - 1)) <= tcol) & (c_idx < n_cmp)
    mask_cf = jnp.where(mask_c, 1.0, 0.0)
    o_cmp, psum = {}, []
    for gg in groups:
        kc = kc_refs[gg][...]
        vc = vc_refs[gg][...]
        tot = None
        for u in units:
            e = exp_rows(masked(_dot_nt(unit_q[gg, u], kc), mask_c))
            p = e / jnp.sum(e, axis=-1, keepdims=True)
            p = jnp.concatenate([part(p, h) * mask_cf for h in range(hm)], axis=0)
            o_cmp[gg, u] = _dot(p.astype(BF16), vc)
            for h in range(hm):
                tot = part(p, h) if tot is None else tot + part(p, h)
        psum.append(tot)

    wlen = WIN + tq
    start = pl.multiple_of(jnp.maximum(i - WIN // tq, 0) * tq, tq)
    dlt = tcol - (start + lax.broadcasted_iota(jnp.int32, (tq, wlen), 1))
    mask_w = (dlt >= 0) & (dlt < WIN)
    for gg in groups:
        kw = group_cols(kw_ref, gg)[pl.ds(start, wlen), :]
        vw1 = with_ones(group_cols(vw_ref, gg)[pl.ds(start, wlen), :])
        for u in units:
            e = exp_rows(masked(_dot_nt(unit_q[gg, u], kw), mask_w))
            o_win = normalised(_dot(e.astype(BF16), vw1))
            for h in range(hm):
                r = u * hm + h
                po_sc[rows(gg, r), :] = (gate(gg, r, 0) * part(o_cmp[gg, u], h)
                                         + gate(gg, r, 2) * part(o_win, h))

    jo = lax.broadcasted_iota(jnp.int32, (n_sel, LANES), 0)
    co = lax.broadcasted_iota(jnp.int32, (n_sel, LANES), 1)
    ov_t = jnp.where((co * CMP_STRIDE < jo * SEL_LEN + SEL_LEN) & (co * CMP_STRIDE + CMP_LEN > jo * SEL_LEN)
                     & (co < n_cmp), 1.0, 0.0).astype(BF16)
    j_idx = lax.broadcasted_iota(jnp.int32, (n_sel, tq), 0)
    cur = lax.shift_right_logical(t0 + lax.broadcasted_iota(jnp.int32, (n_sel, tq), 1), int(np.log2(SEL_LEN)))
    forced = (j_idx == 0) | (j_idx == cur) | (j_idx == cur - 1)
    future = j_idx > cur
    sel = []
    for gg in groups:
        p_hi = psum[gg].astype(BF16)
        p_lo = (psum[gg] - p_hi.astype(F32)).astype(BF16)
        imp = _dot_nt(ov_t, p_hi) + _dot_nt(ov_t, p_lo)
        impm = jnp.where(forced, jnp.inf, jnp.where(future, -jnp.inf, imp))
        rank = jnp.zeros((n_sel, tq), F32)
        for ii in range(n_sel):
            row = impm[ii:ii + 1, :]
            beats = (row > impm) | ((row == impm) & (j_idx > ii))
            rank = rank + jnp.where(beats, 1.0, 0.0)
        sel_t = jnp.where(rank < topk, 1.0, 0.0)
        sel.append(jnp.concatenate([sel_t, jnp.zeros((LANES - n_sel, tq), F32)], axis=0).T.astype(BF16))

    n_chunks = lax.div(t0 + tq + (kblk - 1), kblk)

    def lane_fold_max(x):
        out = x[:, :LANES]
        for c in range(1, kblk // LANES):
            out = jnp.maximum(out, x[:, c * LANES:(c + 1) * LANES])
        return out

    def chunk_start(kb):
        return kb * kblk if isinstance(kb, int) else pl.multiple_of(kb * kblk, kblk)

    def score_chunk(kb):
        k0 = chunk_start(kb)
        causal = (k0 + lax.broadcasted_iota(jnp.int32, (tq, kblk), 1)) <= tcol
        folds = []
        for gg in groups:
            k = group_cols(ks_ref, gg)[pl.ds(k0, kblk), :]
            mask1 = (_dot(sel[gg], e_ref[kb]) > 0.5) & causal
            for u in units:
                sm_ = masked(_dot_nt(unit_q[gg, u], k), mask1)
                s_sc[kb, urows(gg, u), :] = sm_
                folds.append(lane_fold_max(sm_))
        return jnp.concatenate(folds, axis=0)

    def score_pass(kb, carry):
        mx_sc[...] = jnp.maximum(mx_sc[...], score_chunk(kb))
        return carry

    mx_sc[...] = score_chunk(0)
    lax.fori_loop(1, n_chunks, score_pass, 0)
    m_sel = jnp.max(mx_sc[...], axis=-1, keepdims=True)

    def value_chunk(kb):
        k0 = chunk_start(kb)
        pvs = []
        for gg in groups:
            v1 = with_ones(group_cols(vs_ref, gg)[pl.ds(k0, kblk), :])
            for u in units:
                pk = jnp.exp2(s_sc[kb, urows(gg, u), :] - m_sel[urows(gg, u)])
                pvs.append(_dot(pk.astype(BF16), v1))
        return jnp.concatenate(pvs, axis=0)

    def value_pass(kb, carry):
        acc_sc[...] += value_chunk(kb)
        return carry

    acc_sc[...] = value_chunk(0)
    lax.fori_loop(1, n_chunks, value_pass, 0)

    for gg in groups:
        for r in range(rep):
            o = po_sc[rows(gg, r), :] + gate(gg, r, 1) * normalised(acc_sc[rows(gg, r), :])
            col = (gg * rep + r) * NSA_DV
            o_ref[:, col:col + NSA_DV] = o.astype(o_ref.dtype)


def _nsa_attention(za, zg, kc, vc, batch, seq, tq=256, kblk=512, gp=2):
    assert seq % kblk == 0 and seq >= WIN + tq and WIN % tq == 0 and NSA_GROUPS % gp == 0
    nq = seq // tq
    gw = gp * NSA_REP * NSA_DK
    streams = gp * NSA_REP * tq
    hm = max(1, MXU_ROWS // tq)
    assert NSA_REP % hm == 0
    kern = functools.partial(_nsa_kernel, seq=seq, tq=tq, kblk=kblk, gp=gp, hm=hm)
    key = np.arange(seq).reshape(seq // kblk, 1, kblk)
    expand = jnp.asarray(key // SEL_LEN == np.arange(LANES).reshape(1, LANES, 1), BF16)

    def kv_spec(off):
        return pl.BlockSpec((seq, gp * LANES), lambda b, g, i: (b, off // (gp * LANES) + g))

    def cmp_spec(gg):
        return pl.BlockSpec((LANES, NSA_DK), lambda b, g, i: ((g * gp + gg) * batch + b, 0))

    return pl.pallas_call(
        kern,
        grid=(batch, NSA_GROUPS // gp, nq),
        in_specs=[
            pl.BlockSpec((tq, gw), lambda b, g, i: (b * nq + i, A_QN // gw + g)),
            *[cmp_spec(gg) for gg in range(gp)],
            *[cmp_spec(gg) for gg in range(gp)],
            kv_spec(A_KS), kv_spec(A_VS), kv_spec(A_KW), kv_spec(A_VW),
            pl.BlockSpec((tq, LANES), lambda b, g, i: (b * nq + i, 0)),
            pl.BlockSpec(expand.shape, lambda b, g, i: (0, 0, 0)),
        ],
        out_specs=pl.BlockSpec((tq, gw), lambda b, g, i: (b * nq + i, g)),
        out_shape=jax.ShapeDtypeStruct((batch * seq, NSA_HEADS * NSA_DV), BF16),
        scratch_shapes=[
            pltpu.VMEM((seq // kblk, streams, kblk), F32),
            pltpu.VMEM((streams, LANES), F32),
            pltpu.VMEM((streams, 2 * NSA_DV), F32),
            pltpu.VMEM((streams, NSA_DV), F32),
        ],
        compiler_params=_params(("parallel", "parallel", "arbitrary"), 56),
        name="nsa_attention",
    )(za, *([kc] * gp), *([vc] * gp), za, za, za, za, zg, expand)


def _retention_kernel(q_ref, k_ref, v_ref, g_ref, cos_ref, sin_e_ref, sin_o_ref, dec_ref, wq_ref, wk_ref,
                      gc_ref, gn_ref, o_ref, st_ref):
    @pl.when(pl.program_id(1) == 0)
    def _():
        st_ref[...] = jnp.zeros(st_ref.shape, F32)

    c = RET_CHUNK
    for h in range(RET_HEADS):
        st = st_ref[h]
        wq = wq_ref[h]
        wq2 = jnp.concatenate([wq, wq], axis=1)
        cols = slice(h * RET_DV, (h + 1) * RET_DV)
        for sub in range(q_ref.shape[0] // c):
            rows = slice(sub * c, (sub + 1) * c)
            cos = cos_ref[rows, :]
            sin_e = sin_e_ref[rows, :]
            sin_o = sin_o_ref[rows, :]

            def rotate(x):
                return x * cos + pltpu.roll(x, RET_DK - 1, axis=1) * sin_e + pltpu.roll(x, 1, axis=1) * sin_o

            qf = rotate(q_ref[rows, h * RET_DK:(h + 1) * RET_DK])
            kf = rotate(k_ref[rows, h * RET_DK:(h + 1) * RET_DK]) * (RET_DK ** -0.5)
            qb = qf.astype(BF16)
            v = v_ref[rows, cols].astype(BF16)
            s = _dot_nt(qb, kf.astype(BF16)) * dec_ref[h]
            o = _dot(s.astype(BF16), v) + _dot(qb, st.astype(BF16)) * wq2
            st = st * gc_ref[h] + _dot((kf * wk_ref[h]).T.astype(BF16), v)
            mu = jnp.mean(o, axis=-1, keepdims=True)
            d = o - mu
            var = jnp.mean(d * d, axis=-1, keepdims=True)
            on = d * lax.rsqrt(var + EPS) * gn_ref[:, cols]
            gr = g_ref[rows, cols]
            o_ref[rows, cols] = (gr * jax.nn.sigmoid(gr) * on).astype(o_ref.dtype)
        st_ref[h] = st


def _retention(zb, gn_w, batch, seq, chunks_per_step=2):
    c = RET_CHUNK
    rows = chunks_per_step * c
    nc = seq // rows
    hq = RET_HEADS * RET_DK
    hv = RET_HEADS * RET_DV
    f32 = np.float32
    inv = f32(ROPE_BASE) ** (-np.arange(0, RET_DK, 2, dtype=f32) / f32(RET_DK))
    ang = np.arange(seq, dtype=f32)[:, None] * inv[None, :]
    zero = np.zeros_like(ang)
    pairs = lambda even, odd: np.stack([even, odd], axis=-1).reshape(seq, RET_DK)
    cos = pairs(np.cos(ang), np.cos(ang))
    sin_e = pairs(-np.sin(ang), zero)
    sin_o = pairs(zero, np.sin(ang))
    log_g = np.log1p(-np.exp2(f32(-5.0) - np.arange(RET_HEADS, dtype=f32)))
    idx = np.arange(c, dtype=f32)
    rel = idx[:, None] - idx[None, :]
    decay = np.where(rel >= 0, np.exp(log_g[:, None, None] * np.maximum(rel, f32(0.0))), f32(0.0)).astype(f32)
    lanes = lambda a: np.ascontiguousarray(np.broadcast_to(a[:, :, None], (RET_HEADS, c, RET_DK)), dtype=f32)
    w_k = lanes(np.exp(log_g[:, None] * (f32(c - 1) - idx)[None, :]))
    w_q = lanes(np.exp(log_g[:, None] * (idx + f32(1.0))[None, :]))
    g_chunk = np.broadcast_to(np.exp(log_g * f32(c))[:, None, None], (RET_HEADS, 1, RET_DV)).astype(f32)

    row = lambda b, n: b * nc + n
    return pl.pallas_call(
        _retention_kernel,
        grid=(batch, nc),
        in_specs=[
            pl.BlockSpec((rows, hq), lambda b, n: (row(b, n), B_QR // hq)),
            pl.BlockSpec((rows, hq), lambda b, n: (row(b, n), B_KR // hq)),
            pl.BlockSpec((rows, hv), lambda b, n: (row(b, n), B_VR // hv)),
            pl.BlockSpec((rows, hv), lambda b, n: (row(b, n), B_GR // hv)),
            pl.BlockSpec((rows, RET_DK), lambda b, n: (n, 0)),
            pl.BlockSpec((rows, RET_DK), lambda b, n: (n, 0)),
            pl.BlockSpec((rows, RET_DK), lambda b, n: (n, 0)),
            pl.BlockSpec((RET_HEADS, c, c), lambda b, n: (0, 0, 0)),
            pl.BlockSpec((RET_HEADS, c, RET_DK), lambda b, n: (0, 0, 0)),
            pl.BlockSpec((RET_HEADS, c, RET_DK), lambda b, n: (0, 0, 0)),
            pl.BlockSpec((RET_HEADS, 1, RET_DV), lambda b, n: (0, 0, 0)),
            pl.BlockSpec((1, hv), lambda b, n: (0, 0)),
        ],
        out_specs=pl.BlockSpec((rows, hv), lambda b, n: (row(b, n), 0)),
        out_shape=jax.ShapeDtypeStruct((batch * seq, hv), BF16),
        scratch_shapes=[pltpu.VMEM((RET_HEADS, RET_DK, RET_DV), F32)],
        compiler_params=_params(("parallel", "arbitrary"), 40),
        name="retention",
    )(zb, zb, zb, zb, cos, sin_e, sin_o, decay, w_q, w_k, g_chunk, gn_w.reshape(1, hv))


def _merge_kernel(on_ref, or_ref, ga_ref, gb_ref, x_ref, wa_ref, wb_ref, wo_ref, h_ref):
    a = _dot(on_ref[...], wa_ref[...])
    b = _dot(or_ref[...], wb_ref[...])
    merged = jax.nn.sigmoid(ga_ref[...]) * a + jax.nn.sigmoid(gb_ref[...]) * b
    h_ref[...] = x_ref[...] + _dot(merged.astype(BF16), wo_ref[...])


def _merge(o_nsa, o_ret, zb, x, w_a, w_b, w_out, tm=256):
    m, d = x.shape
    resident = lambda shape: pl.BlockSpec(shape, lambda i: (0, 0), pipeline_mode=pl.Buffered(1))
    return pl.pallas_call(
        _merge_kernel,
        grid=(m // tm,),
        in_specs=[
            pl.BlockSpec((tm, d), lambda i: (i, 0)),
            pl.BlockSpec((tm, d), lambda i: (i, 0)),
            pl.BlockSpec((tm, d), lambda i: (i, B_GA // d)),
            pl.BlockSpec((tm, d), lambda i: (i, B_GB // d)),
            pl.BlockSpec((tm, d), lambda i: (i, 0)),
            resident(w_a.shape), resident(w_b.shape), resident(w_out.shape),
        ],
        out_specs=pl.BlockSpec((tm, d), lambda i: (i, 0)),
        out_shape=jax.ShapeDtypeStruct((m, d), F32),
        compiler_params=_params(("parallel",), 56),
        name="merge_out_proj",
    )(o_nsa, o_ret, zb, zb, x, w_a, w_b, w_out)


def _mem_kv_kernel(m_ref, nw_ref, wk_ref, wv_ref, k_ref, v_ref):
    mn = _rms(m_ref[...], nw_ref[...]).astype(BF16)
    k_ref[...] = _dot(mn, wk_ref[...]).astype(k_ref.dtype)
    v_ref[...] = _dot(mn, wv_ref[...]).astype(v_ref.dtype)


def _mem_kv(mem2, nw, wk, wv, tm=256):
    m, d = mem2.shape
    n = wk.shape[1]
    out = jax.ShapeDtypeStruct((m, n), BF16)
    return pl.pallas_call(
        _mem_kv_kernel,
        grid=(m // tm,),
        in_specs=[
            pl.BlockSpec((tm, d), lambda i: (i, 0)),
            pl.BlockSpec((1, d), lambda i: (0, 0)),
            pl.BlockSpec((d, n), lambda i: (0, 0)),
            pl.BlockSpec((d, n), lambda i: (0, 0)),
        ],
        out_specs=[pl.BlockSpec((tm, n), lambda i: (i, 0))] * 2,
        out_shape=[out, out],
        compiler_params=_params(("parallel",), 40),
        name="mem_kv_proj",
    )(mem2, nw.reshape(1, d), wk, wv)


def _cross_kernel(h_ref, xw_ref, mw_ref, wq_ref, kx_ref, vx_ref, wo_ref, h2_ref, nm_ref):
    h = h_ref[...]
    nx = _rms(h, xw_ref[...]).astype(BF16)
    qx = _dot(nx, wq_ref[...]).astype(BF16)
    outs = []
    for hh in range(X_HEADS):
        cols = slice(hh * X_DH, (hh + 1) * X_DH)
        s = _dot_nt(qx[:, cols], kx_ref[:, cols]) * (X_DH ** -0.5)
        e = jnp.exp(s - jnp.max(s, axis=-1, keepdims=True))
        p = e / jnp.sum(e, axis=-1, keepdims=True)
        outs.append(_dot(p.astype(BF16), vx_ref[:, cols]))
    ox = jnp.concatenate(outs, axis=-1).astype(BF16)
    h2 = h + _dot(ox, wo_ref[...])
    h2_ref[...] = h2
    nm_ref[...] = _rms(h2, mw_ref[...]).astype(nm_ref.dtype)


def _cross_attention(h1, x_norm_w, mlp_norm_w, wq, kx, vx, wo, seq, tm=512):
    m, d = h1.shape
    n = wq.shape[1]
    per_batch = seq // tm
    vec = lambda: pl.BlockSpec((1, d), lambda i: (0, 0))
    return pl.pallas_call(
        _cross_kernel,
        grid=(m // tm,),
        in_specs=[
            pl.BlockSpec((tm, d), lambda i: (i, 0)),
            vec(), vec(),
            pl.BlockSpec((d, n), lambda i: (0, 0)),
            pl.BlockSpec((MEM_LEN, n), lambda i: (i // per_batch, 0)),
            pl.BlockSpec((MEM_LEN, n), lambda i: (i // per_batch, 0)),
            pl.BlockSpec((n, d), lambda i: (0, 0)),
        ],
        out_specs=[pl.BlockSpec((tm, d), lambda i: (i, 0))] * 2,
        out_shape=[jax.ShapeDtypeStruct((m, d), F32), jax.ShapeDtypeStruct((m, d), BF16)],
        compiler_params=_params(("parallel",), 40),
        name="cross_attention",
    )(h1, x_norm_w.reshape(1, d), mlp_norm_w.reshape(1, d), wq, kx, vx, wo)


def _mlp_kernel(nm_ref, wu_ref, wd_ref, h_ref, fw_ref, o_ref):
    j = pl.program_id(1)

    @pl.when(j == 0)
    def _():
        o_ref[...] = jnp.zeros(o_ref.shape, F32)

    u = jnp.maximum(_dot(nm_ref[...], wu_ref[...]), 0.0)
    o_ref[...] += _dot((u * u).astype(BF16), wd_ref[...])

    @pl.when(j == pl.num_programs(1) - 1)
    def _():
        o_ref[...] = _rms(h_ref[...] + o_ref[...], fw_ref[...])


def _mlp(nm, w_up, w_down, h2, final_w, tm=512, tf=1024):
    m, d = nm.shape
    f = w_up.shape[1]
    return pl.pallas_call(
        _mlp_kernel,
        grid=(m // tm, f // tf),
        in_specs=[
            pl.BlockSpec((tm, d), lambda i, j: (i, 0)),
            pl.BlockSpec((d, tf), lambda i, j: (0, j)),
            pl.BlockSpec((tf, d), lambda i, j: (j, 0)),
            pl.BlockSpec((tm, d), lambda i, j: (i, 0)),
            pl.BlockSpec((1, d), lambda i, j: (0, 0)),
        ],
        out_specs=pl.BlockSpec((tm, d), lambda i, j: (i, 0)),
        out_shape=jax.ShapeDtypeStruct((m, d), F32),
        compiler_params=_params(("parallel", "arbitrary"), 56),
        name="mlp_final_norm",
    )(nm, w_up, w_down, h2, final_w.reshape(1, d))


def _layer(h, mem, attn_norm_w, w_in, cmp_pe_k, cmp_w1_k, cmp_w2_k, cmp_pe_v, cmp_w1_v, cmp_w2_v,
           w_a, ret_gn_w, w_b, w_out, x_norm_w, mem_norm_w, wq_x, wk_x, wv_x, wo_x,
           mlp_norm_w, w_up, w_down, out_norm_w, batch, seq):
    blk = W_IN_BLOCK
    kv_block = W_IN_KV // blk
    gate_shift = NSA_HEADS * 3

    wt = w_in.T
    skip_kv = lambda j: j + (j >= kv_block).astype(jnp.int32)
    kv, n, zg, wq_b, wk_b, wv_b, wo_b, c2k_b, c2v_b = _in_proj(
        h, wt, lambda j: kv_block, 1, BF16, regroup=CMP_STRIDE, norm_w=attn_norm_w,
        side_block=W_IN_NSA_GATE // LANES, casts=(wq_x, wk_x, wv_x, wo_x, cmp_w2_k, cmp_w2_v), name="in_proj_kv")
    za, wa_b, wb_b, wout_b, c1k_b, c1v_b = _in_proj(
        n, wt, skip_kv, 4, BF16, casts=(w_a, w_b, w_out, cmp_w1_k, cmp_w1_v),
        lead_scale=(NSA_Q_SCALE, NSA_HEADS * NSA_DK // blk), name="in_proj_a")
    zb, wup_b, wdown_b = _in_proj(
        n, wt, lambda j: W_IN_NSA_GATE // blk + j, B_WIDTH // blk, F32, shift=gate_shift,
        casts=(w_up, w_down), name="in_proj_b")

    assert seq // CMP_STRIDE == LANES
    rows_kv = NSA_GROUPS * batch * LANES
    k2 = kv.reshape(2 * rows_kv, CMP_STRIDE * NSA_DK)
    kc = _compress(k2, 0, rows_kv, cmp_pe_k.reshape(1, -1), c1k_b, c2k_b)
    vc = _compress(k2, rows_kv, rows_kv, cmp_pe_v.reshape(1, -1), c1v_b, c2v_b)

    o_nsa = _nsa_attention(za, zg, kc, vc, batch, seq)
    o_ret = _retention(zb, ret_gn_w, batch, seq)
    h1 = _merge(o_nsa, o_ret, zb, h, wa_b, wb_b, wout_b)

    kx, vx = _mem_kv(mem.reshape(batch * MEM_LEN, D_MODEL), mem_norm_w, wk_b, wv_b)
    h2, nm = _cross_attention(h1, x_norm_w, mlp_norm_w, wq_b, kx, vx, wo_b, seq)
    return _mlp(nm, wup_b, wdown_b, h2, out_norm_w)


def kernel(x, mem, attn_norm_w, w_in, cmp_pe_k, cmp_w1_k, cmp_w2_k, cmp_pe_v, cmp_w1_v, cmp_w2_v, w_a, ret_gn_w,
           w_b, w_out, x_norm_w, mem_norm_w, wq_x, wk_x, wv_x, wo_x, mlp_norm_w, w_up, w_down, final_norm_w):
    batch, seq, d = x.shape
    depth = w_in.shape[0]
    assert depth == 1
    h = x.reshape(batch * seq, d)
    out = _layer(h, mem, attn_norm_w[0], w_in[0], cmp_pe_k[0], cmp_w1_k[0], cmp_w2_k[0],
                 cmp_pe_v[0], cmp_w1_v[0], cmp_w2_v[0], w_a[0], ret_gn_w[0], w_b[0], w_out[0],
                 x_norm_w[0], mem_norm_w[0], wq_x[0], wk_x[0], wv_x[0], wo_x[0],
                 mlp_norm_w[0], w_up[0], w_down[0], final_norm_w, batch, seq)
    return out.reshape(batch, seq, d)
```

```python
import functools

import jax
import jax.numpy as jnp
import numpy as np
from jax import lax
from jax.experimental import pallas as pl
from jax.experimental.pallas import tpu as pltpu

F32 = jnp.float32
BF16 = jnp.bfloat16

D_MODEL = 2048
MEM_LEN = 256
NSA_HEADS = 16
NSA_GROUPS = 4
NSA_REP = NSA_HEADS // NSA_GROUPS
NSA_DK = 128
NSA_DV = 128
CMP_LEN = 32
CMP_STRIDE = 16
CMP_HIDDEN = 1024
SEL_LEN = 64
SEL_TOPK = 16
WIN = 512
RET_HEADS = 8
RET_DK = 128
RET_DV = 256
RET_CHUNK = 128
ROPE_BASE = 10000.0
X_HEADS = 4
X_DH = 128
D_FF = 4 * D_MODEL
EPS = 1e-6
NEG = -1e30
LOG2E = 1.4426950408889634
NSA_Q_SCALE = NSA_DK ** -0.5 * LOG2E

LANES = 128
F32_SUBLANES = 8
MXU_ROWS = 256
V7X_VMEM_MIB = 64

W_IN_BLOCK = 1024
W_IN_KV = 2048
W_IN_NSA_GATE = 5120
A_QN = 0
A_KS = 2048
A_VS = 2560
A_KW = 3072
A_VW = 3584
B_QR = 0
B_KR = 1024
B_VR = 2048
B_GR = 4096
B_GA = 6144
B_GB = 8192
B_WIDTH = 10240

NT_DIMS = (((1,), (1,)), ((), ()))
TN_DIMS = (((0,), (0,)), ((), ()))


def _params(sem, vmem_mib):
    assert vmem_mib < V7X_VMEM_MIB
    return pltpu.CompilerParams(dimension_semantics=sem, vmem_limit_bytes=vmem_mib * 1024 * 1024)


def _rms(x, w):
    return x * lax.rsqrt(jnp.mean(x * x, axis=-1, keepdims=True) + EPS) * w


def _dot(a, b):
    return jnp.dot(a, b, preferred_element_type=F32)


def _dot_nt(a, b):
    return lax.dot_general(a, b, NT_DIMS, preferred_element_type=F32)


CAST_ROWS = 16


def _cast_rows(dst_ref, dst0, src_ref, src0, nrows):
    def body(r, carry):
        off = r * CAST_ROWS
        dst_ref[pl.ds(pl.multiple_of(dst0 + off, CAST_ROWS), CAST_ROWS), :] = (
            src_ref[pl.ds(pl.multiple_of(src0 + off, F32_SUBLANES), CAST_ROWS), :].astype(BF16))
        return carry

    lax.fori_loop(0, nrows // CAST_ROWS, body, 0)


def _in_proj_kernel(*refs, shift, regroup, n_cast, norm, side, lead_scale):
    n_in = 4 if norm else 3
    n_ref, w_ref, wnext_ref = refs[:3]
    cast_in = refs[n_in:n_in + n_cast]
    o_ref = refs[n_in + n_cast]
    n_out = n_in + n_cast + 1 + int(norm) + int(side)
    cast_out = refs[n_out:n_out + n_cast]
    wb_sc, *rest = refs[n_out + n_cast:]
    tn = wb_sc.shape[0]
    for src, dst in zip(cast_in, cast_out):
        dst[...] = src[...].astype(dst.dtype)

    @pl.when(pl.program_id(1) == 0)
    def _():
        _cast_rows(wb_sc, 0, w_ref, shift, tn - shift)
        if shift:
            _cast_rows(wb_sc, tn - shift, wnext_ref, 0, shift)
        if side:
            _cast_rows(rest[-1], 0, wnext_ref, 0, LANES)

    if norm:
        xn = _rms(n_ref[...], refs[3][...]).astype(BF16)
        refs[n_in + n_cast + 1][...] = xn
    else:
        xn = n_ref[...]
    if side:
        refs[n_out - 1][...] = _dot_nt(xn, rest[-1][...])
    res = _dot_nt(xn, wb_sc[...])
    if regroup:
        r_sc = rest[0]
        tm = res.shape[0]
        for c in range(tn // LANES):
            r_sc[c] = res[:, c * LANES:(c + 1) * LANES]
        for c in range(tn // LANES):
            for j in range(regroup):
                o_ref[c, :, j * LANES:(j + 1) * LANES] = (
                    r_sc[c, pl.ds(j, tm // regroup, stride=regroup), :].astype(o_ref.dtype))
    else:
        if lead_scale is not None:
            factor, blocks = lead_scale
            res = res * jnp.where(pl.program_id(0) < blocks, factor, 1.0)
        o_ref[...] = res.astype(o_ref.dtype)


def _in_proj(n, wt, src_block, n_blocks, out_dtype, *, shift=0, regroup=0, casts=(), norm_w=None,
             side_block=None, lead_scale=None, tm=1024, tn=1024, name):
    m, k = n.shape
    per = tn // LANES
    m_tiles = m // tm
    norm = norm_w is not None
    side = side_block is not None
    assert shift % CAST_ROWS == 0 and shift <= LANES and not (norm and n_blocks > 1)
    assert not side or (shift == 0 and n_blocks == 1)
    scratch = [pltpu.VMEM((tn, k), BF16)]
    row_tile = pl.BlockSpec((tm, k), lambda j, i: (i, 0))
    once = {"pipeline_mode": pl.Buffered(1)} if n_blocks == 1 else {}
    if regroup:
        assert n_blocks == 1
        out_shape = jax.ShapeDtypeStruct((per, m // regroup, regroup * LANES), out_dtype)
        out_spec = pl.BlockSpec((per, tm // regroup, regroup * LANES), lambda j, i: (0, i, 0))
        scratch.append(pltpu.VMEM((per, tm, LANES), F32))
    else:
        out_shape = jax.ShapeDtypeStruct((m, n_blocks * tn), out_dtype)
        out_spec = pl.BlockSpec((tm, tn), lambda j, i: (i, j))

    cast_steps = 1 << ((n_blocks * m_tiles).bit_length() - 1)
    cast_specs = []
    for a in casts:
        rows = a.shape[0] // cast_steps
        assert a.shape[0] % cast_steps == 0 and rows % CAST_ROWS == 0
        cast_specs.append(pl.BlockSpec(
            (rows, a.shape[1]), lambda j, i: (jnp.minimum(j * m_tiles + i, cast_steps - 1), 0)))

    if side:
        scratch.append(pltpu.VMEM((LANES, k), BF16))
    next_rows = (lambda j, i: (side_block, 0)) if side else (lambda j, i: ((src_block(j) + 1) * per, 0))
    return pl.pallas_call(
        functools.partial(_in_proj_kernel, shift=shift, regroup=regroup, n_cast=len(casts), norm=norm, side=side,
                          lead_scale=lead_scale),
        grid=(n_blocks, m_tiles),
        in_specs=[
            row_tile,
            pl.BlockSpec((tn, k), lambda j, i: (src_block(j), 0), **once),
            pl.BlockSpec((LANES, k), next_rows, **once),
            *([pl.BlockSpec((1, k), lambda j, i: (0, 0))] if norm else []),
            *cast_specs,
        ],
        out_specs=[out_spec, *([row_tile] if norm else []),
                   *([pl.BlockSpec((tm, LANES), lambda j, i: (i, 0))] if side else []), *cast_specs],
        out_shape=[out_shape, *([jax.ShapeDtypeStruct((m, k), BF16)] if norm else []),
                   *([jax.ShapeDtypeStruct((m, LANES), F32)] if side else []),
                   *[jax.ShapeDtypeStruct(a.shape, BF16) for a in casts]],
        scratch_shapes=scratch,
        compiler_params=_params(("arbitrary", "arbitrary"), 60 if norm else 56),
        name=name,
    )(n, wt, wt, *([norm_w.reshape(1, k)] if norm else []), *casts)


def _compress_kernel(k2_ref, pe_ref, w1_ref, w2_ref, o_ref):
    rows, half = k2_ref.shape
    k2 = k2_ref[...].astype(F32)
    a_lo = (k2 + pe_ref[:, :half]).astype(BF16)
    a_hi = (k2 + pe_ref[:, half:]).astype(BF16)
    lo = _dot(a_lo, w1_ref[:half, :])
    hi = _dot(a_hi, w1_ref[half:, :])
    h = lo + pltpu.roll(hi, rows - 1, axis=0)
    act = h * jax.nn.sigmoid(h)
    o_ref[...] = _dot(act.astype(BF16), w2_ref[...]).astype(o_ref.dtype)


def _compress(k2, first_row, m, pe, w1, w2, rows=256):
    half = k2.shape[1]
    hid = w1.shape[1]
    dout = w2.shape[1]
    first = first_row // rows
    return pl.pallas_call(
        _compress_kernel,
        grid=(m // rows,),
        in_specs=[
            pl.BlockSpec((rows, half), lambda i: (first + i, 0)),
            pl.BlockSpec((1, 2 * half), lambda i: (0, 0)),
            pl.BlockSpec((2 * half, hid), lambda i: (0, 0)),
            pl.BlockSpec((hid, dout), lambda i: (0, 0)),
        ],
        out_specs=pl.BlockSpec((rows, dout), lambda i: (i, 0)),
        out_shape=jax.ShapeDtypeStruct((m, dout), BF16),
        compiler_params=_params(("parallel",), 40),
        name="nsa_compress",
    )(k2, pe, w1, w2)


def _nsa_kernel(*refs, seq, tq, kblk, gp, hm):
    q_ref = refs[0]
    kc_refs = refs[1:1 + gp]
    vc_refs = refs[1 + gp:1 + 2 * gp]
    ks_ref, vs_ref, kw_ref, vw_ref, g_ref, e_ref, o_ref, s_sc, mx_sc, acc_sc, po_sc = refs[1 + 2 * gp:]
    i = pl.program_id(2)
    t0 = i * tq
    rep = NSA_REP
    n_cmp = (seq - CMP_LEN) // CMP_STRIDE + 1
    n_sel = seq // SEL_LEN
    topk = min(SEL_TOPK, n_sel)
    groups = range(gp)

    q = q_ref[...]
    units = range(rep // hm)
    rows = lambda gg, r: slice((gg * rep + r) * tq, (gg * rep + r + 1) * tq)
    urows = lambda gg, u: slice((gg * rep + u * hm) * tq, (gg * rep + (u + 1) * hm) * tq)
    part = lambda x, h: x[h * tq:(h + 1) * tq]
    head_q = lambda gg, r: q[:, (gg * rep + r) * NSA_DK:(gg * rep + r + 1) * NSA_DK]
    unit_q = {(gg, u): jnp.concatenate([head_q(gg, u * hm + h) for h in range(hm)], axis=0)
              for gg in groups for u in units}
    group_cols = lambda ref, gg: ref.at[:, gg * LANES:(gg + 1) * LANES]

    def masked(s, mask1):
        return jnp.concatenate([jnp.where(mask1, part(s, h), NEG) for h in range(hm)], axis=0)
    tcol = t0 + lax.broadcasted_iota(jnp.int32, (tq, 1), 0)
    per_group = rep * 3
    g_sig = jax.nn.sigmoid(g_ref[...])
    first_group = pl.program_id(1) * gp
    gs = [pltpu.roll(g_sig, lax.rem(LANES - (first_group + gg) * per_group, LANES), axis=1) for gg in groups]
    gate = lambda gg, r, branch: gs[gg][:, 3 * r + branch:3 * r + branch + 1]

    def exp_rows(sm):
        return jnp.exp2(sm - jnp.max(sm, axis=-1, keepdims=True))

    def with_ones(v):
        return jnp.concatenate([v, jnp.ones(v.shape, v.dtype)], axis=1)

    def normalised(ev):
        return ev[:, :NSA_DV] / ev[:, NSA_DV:]

    c_idx = lax.broadcasted_iota(jnp.int32, (tq, LANES), 1)
    mask_c = ((c_idx * CMP_STRIDE + (CMP_LEN - 1)) <= tcol) & (c_idx < n_cmp)
    mask_cf = jnp.where(mask_c, 1.0, 0.0)
    o_cmp, psum = {}, []
    for gg in groups:
        kc = kc_refs[gg][...]
        vc = vc_refs[gg][...]
        tot = None
        for u in units:
            e = exp_rows(masked(_dot_nt(unit_q[gg, u], kc), mask_c))
            p = e / jnp.sum(e, axis=-1, keepdims=True)
            p = jnp.concatenate([part(p, h) * mask_cf for h in range(hm)], axis=0)
            o_cmp[gg, u] = _dot(p.astype(BF16), vc)
            for h in range(hm):
                tot = part(p, h) if tot is None else tot + part(p, h)
        psum.append(tot)

    wlen = WIN + tq
    start = pl.multiple_of(jnp.maximum(i - WIN // tq, 0) * tq, tq)
    dlt = tcol - (start + lax.broadcasted_iota(jnp.int32, (tq, wlen), 1))
    mask_w = (dlt >= 0) & (dlt < WIN)
    for gg in groups:
        kw = group_cols(kw_ref, gg)[pl.ds(start, wlen), :]
        vw1 = with_ones(group_cols(vw_ref, gg)[pl.ds(start, wlen), :])
        for u in units:
            e = exp_rows(masked(_dot_nt(unit_q[gg, u], kw), mask_w))
            o_win = normalised(_dot(e.astype(BF16), vw1))
            for h in range(hm):
                r = u * hm + h
                po_sc[rows(gg, r), :] = (gate(gg, r, 0) * part(o_cmp[gg, u], h)
                                         + gate(gg, r, 2) * part(o_win, h))

    jo = lax.broadcasted_iota(jnp.int32, (n_sel, LANES), 0)
    co = lax.broadcasted_iota(jnp.int32, (n_sel, LANES), 1)
    ov_t = jnp.where((co * CMP_STRIDE < jo * SEL_LEN + SEL_LEN) & (co * CMP_STRIDE + CMP_LEN > jo * SEL_LEN)
                     & (co < n_cmp), 1.0, 0.0).astype(BF16)
    j_idx = lax.broadcasted_iota(jnp.int32, (n_sel, tq), 0)
    cur = lax.shift_right_logical(t0 + lax.broadcasted_iota(jnp.int32, (n_sel, tq), 1), int(np.log2(SEL_LEN)))
    forced = (j_idx == 0) | (j_idx == cur) | (j_idx == cur - 1)
    future = j_idx > cur
    sel = []
    for gg in groups:
        p_hi = psum[gg].astype(BF16)
        p_lo = (psum[gg] - p_hi.astype(F32)).astype(BF16)
        imp = _dot_nt(ov_t, p_hi) + _dot_nt(ov_t, p_lo)
        impm = jnp.where(forced, jnp.inf, jnp.where(future, -jnp.inf, imp))
        rank = jnp.zeros((n_sel, tq), F32)
        for ii in range(n_sel):
            row = impm[ii:ii + 1, :]
            beats = (row > impm) | ((row == impm) & (j_idx > ii))
            rank = rank + jnp.where(beats, 1.0, 0.0)
        sel_t = jnp.where(rank < topk, 1.0, 0.0)
        sel.append(jnp.concatenate([sel_t, jnp.zeros((LANES - n_sel, tq), F32)], axis=0).T.astype(BF16))

    n_chunks = lax.div(t0 + tq + (kblk - 1), kblk)

    def lane_fold_max(x):
        out = x[:, :LANES]
        for c in range(1, kblk // LANES):
            out = jnp.maximum(out, x[:, c * LANES:(c + 1) * LANES])
        return out

    def chunk_start(kb):
        return kb * kblk if isinstance(kb, int) else pl.multiple_of(kb * kblk, kblk)

    def score_chunk(kb):
        k0 = chunk_start(kb)
        causal = (k0 + lax.broadcasted_iota(jnp.int32, (tq, kblk), 1)) <= tcol
        folds = []
        for gg in groups:
            k = group_cols(ks_ref, gg)[pl.ds(k0, kblk), :]
            mask1 = (_dot(sel[gg], e_ref[kb]) > 0.5) & causal
            for u in units:
                sm_ = masked(_dot_nt(unit_q[gg, u], k), mask1)
                s_sc[kb, urows(gg, u), :] = sm_
                folds.append(lane_fold_max(sm_))
        return jnp.concatenate(folds, axis=0)

    def score_pass(kb, carry):
        mx_sc[...] = jnp.maximum(mx_sc[...], score_chunk(kb))
        return carry

    mx_sc[...] = score_chunk(0)
    lax.fori_loop(1, n_chunks, score_pass, 0)
    m_sel = jnp.max(mx_sc[...], axis=-1, keepdims=True)

    def value_chunk(kb):
        k0 = chunk_start(kb)
        pvs = []
        for gg in groups:
            v1 = with_ones(group_cols(vs_ref, gg)[pl.ds(k0, kblk), :])
            for u in units:
                pk = jnp.exp2(s_sc[kb, urows(gg, u), :] - m_sel[urows(gg, u)])
                pvs.append(_dot(pk.astype(BF16), v1))
        return jnp.concatenate(pvs, axis=0)

    def value_pass(kb, carry):
        acc_sc[...] += value_chunk(kb)
        return carry

    acc_sc[...] = value_chunk(0)
    lax.fori_loop(1, n_chunks, value_pass, 0)

    for gg in groups:
        for r in range(rep):
            o = po_sc[rows(gg, r), :] + gate(gg, r, 1) * normalised(acc_sc[rows(gg, r), :])
            col = (gg * rep + r) * NSA_DV
            o_ref[:, col:col + NSA_DV] = o.astype(o_ref.dtype)


def _nsa_attention(za, zg, kc, vc, batch, seq, tq=256, kblk=512, gp=2):
    assert seq % kblk == 0 and seq >= WIN + tq and WIN % tq == 0 and NSA_GROUPS % gp == 0
    nq = seq // tq
    gw = gp * NSA_REP * NSA_DK
    streams = gp * NSA_REP * tq
    hm = max(1, MXU_ROWS // tq)
    assert NSA_REP % hm == 0
    kern = functools.partial(_nsa_kernel, seq=seq, tq=tq, kblk=kblk, gp=gp, hm=hm)
    key = np.arange(seq).reshape(seq // kblk, 1, kblk)
    expand = jnp.asarray(key // SEL_LEN == np.arange(LANES).reshape(1, LANES, 1), BF16)

    def kv_spec(off):
        return pl.BlockSpec((seq, gp * LANES), lambda b, g, i: (b, off // (gp * LANES) + g))

    def cmp_spec(gg):
        return pl.BlockSpec((LANES, NSA_DK), lambda b, g, i: ((g * gp + gg) * batch + b, 0))

    return pl.pallas_call(
        kern,
        grid=(batch, NSA_GROUPS // gp, nq),
        in_specs=[
            pl.BlockSpec((tq, gw), lambda b, g, i: (b * nq + i, A_QN // gw + g)),
            *[cmp_spec(gg) for gg in range(gp)],
            *[cmp_spec(gg) for gg in range(gp)],
            kv_spec(A_KS), kv_spec(A_VS), kv_spec(A_KW), kv_spec(A_VW),
            pl.BlockSpec((tq, LANES), lambda b, g, i: (b * nq + i, 0)),
            pl.BlockSpec(expand.shape, lambda b, g, i: (0, 0, 0)),
        ],
        out_specs=pl.BlockSpec((tq, gw), lambda b, g, i: (b * nq + i, g)),
        out_shape=jax.ShapeDtypeStruct((batch * seq, NSA_HEADS * NSA_DV), BF16),
        scratch_shapes=[
            pltpu.VMEM((seq // kblk, streams, kblk), F32),
            pltpu.VMEM((streams, LANES), F32),
            pltpu.VMEM((streams, 2 * NSA_DV), F32),
            pltpu.VMEM((streams, NSA_DV), F32),
        ],
        compiler_params=_params(("parallel", "parallel", "arbitrary"), 56),
        name="nsa_attention",
    )(za, *([kc] * gp), *([vc] * gp), za, za, za, za, zg, expand)


def _retention_kernel(q_ref, k_ref, v_ref, g_ref, cos_ref, sin_e_ref, sin_o_ref, dec_ref, wq_ref, wk_ref,
                      gc_ref, gn_ref, o_ref, st_ref):
    @pl.when(pl.program_id(1) == 0)
    def _():
        st_ref[...] = jnp.zeros(st_ref.shape, F32)

    c = RET_CHUNK
    for h in range(RET_HEADS):
        st = st_ref[h]
        wq = wq_ref[h]
        wq2 = jnp.concatenate([wq, wq], axis=1)
        cols = slice(h * RET_DV, (h + 1) * RET_DV)
        for sub in range(q_ref.shape[0] // c):
            rows = slice(sub * c, (sub + 1) * c)
            cos = cos_ref[rows, :]
            sin_e = sin_e_ref[rows, :]
            sin_o = sin_o_ref[rows, :]

            def rotate(x):
                return x * cos + pltpu.roll(x, RET_DK - 1, axis=1) * sin_e + pltpu.roll(x, 1, axis=1) * sin_o

            qf = rotate(q_ref[rows, h * RET_DK:(h + 1) * RET_DK])
            kf = rotate(k_ref[rows, h * RET_DK:(h + 1) * RET_DK]) * (RET_DK ** -0.5)
            qb = qf.astype(BF16)
            v = v_ref[rows, cols].astype(BF16)
            s = _dot_nt(qb, kf.astype(BF16)) * dec_ref[h]
            o = _dot(s.astype(BF16), v) + _dot(qb, st.astype(BF16)) * wq2
            st = st * gc_ref[h] + lax.dot_general((kf * wk_ref[h]).astype(BF16), v, TN_DIMS,
                                                  preferred_element_type=F32)
            mu = jnp.mean(o, axis=-1, keepdims=True)
            d = o - mu
            var = jnp.mean(d * d, axis=-1, keepdims=True)
            on = d * lax.rsqrt(var + EPS) * gn_ref[:, cols]
            gr = g_ref[rows, cols]
            o_ref[rows, cols] = (gr * jax.nn.sigmoid(gr) * on).astype(o_ref.dtype)
        st_ref[h] = st


def _retention(zb, gn_w, batch, seq, chunks_per_step=2):
    c = RET_CHUNK
    rows = chunks_per_step * c
    nc = seq // rows
    hq = RET_HEADS * RET_DK
    hv = RET_HEADS * RET_DV
    f32 = np.float32
    inv = f32(ROPE_BASE) ** (-np.arange(0, RET_DK, 2, dtype=f32) / f32(RET_DK))
    ang = np.arange(seq, dtype=f32)[:, None] * inv[None, :]
    zero = np.zeros_like(ang)
    pairs = lambda even, odd: np.stack([even, odd], axis=-1).reshape(seq, RET_DK)
    cos = pairs(np.cos(ang), np.cos(ang))
    sin_e = pairs(-np.sin(ang), zero)
    sin_o = pairs(zero, np.sin(ang))
    log_g = np.log1p(-np.exp2(f32(-5.0) - np.arange(RET_HEADS, dtype=f32)))
    idx = np.arange(c, dtype=f32)
    rel = idx[:, None] - idx[None, :]
    decay = np.where(rel >= 0, np.exp(log_g[:, None, None] * np.maximum(rel, f32(0.0))), f32(0.0)).astype(f32)
    lanes = lambda a: np.ascontiguousarray(np.broadcast_to(a[:, :, None], (RET_HEADS, c, RET_DK)), dtype=f32)
    w_k = lanes(np.exp(log_g[:, None] * (f32(c - 1) - idx)[None, :]))
    w_q = lanes(np.exp(log_g[:, None] * (idx + f32(1.0))[None, :]))
    g_chunk = np.broadcast_to(np.exp(log_g * f32(c))[:, None, None], (RET_HEADS, 1, RET_DV)).astype(f32)

    row = lambda b, n: b * nc + n
    return pl.pallas_call(
        _retention_kernel,
        grid=(batch, nc),
        in_specs=[
            pl.BlockSpec((rows, hq), lambda b, n: (row(b, n), B_QR // hq)),
            pl.BlockSpec((rows, hq), lambda b, n: (row(b, n), B_KR // hq)),
            pl.BlockSpec((rows, hv), lambda b, n: (row(b, n), B_VR // hv)),
            pl.BlockSpec((rows, hv), lambda b, n: (row(b, n), B_GR // hv)),
            pl.BlockSpec((rows, RET_DK), lambda b, n: (n, 0)),
            pl.BlockSpec((rows, RET_DK), lambda b, n: (n, 0)),
            pl.BlockSpec((rows, RET_DK), lambda b, n: (n, 0)),
            pl.BlockSpec((RET_HEADS, c, c), lambda b, n: (0, 0, 0)),
            pl.BlockSpec((RET_HEADS, c, RET_DK), lambda b, n: (0, 0, 0)),
            pl.BlockSpec((RET_HEADS, c, RET_DK), lambda b, n: (0, 0, 0)),
            pl.BlockSpec((RET_HEADS, 1, RET_DV), lambda b, n: (0, 0, 0)),
            pl.BlockSpec((1, hv), lambda b, n: (0, 0)),
        ],
        out_specs=pl.BlockSpec((rows, hv), lambda b, n: (row(b, n), 0)),
        out_shape=jax.ShapeDtypeStruct((batch * seq, hv), BF16),
        scratch_shapes=[pltpu.VMEM((RET_HEADS, RET_DK, RET_DV), F32)],
        compiler_params=_params(("parallel", "arbitrary"), 40),
        name="retention",
    )(zb, zb, zb, zb, cos, sin_e, sin_o, decay, w_q, w_k, g_chunk, gn_w.reshape(1, hv))


def _merge_kernel(on_ref, or_ref, ga_ref, gb_ref, x_ref, wa_ref, wb_ref, wo_ref, h_ref):
    a = _dot(on_ref[...], wa_ref[...])
    b = _dot(or_ref[...], wb_ref[...])
    merged = jax.nn.sigmoid(ga_ref[...]) * a + jax.nn.sigmoid(gb_ref[...]) * b
    h_ref[...] = x_ref[...] + _dot(merged.astype(BF16), wo_ref[...])


def _merge(o_nsa, o_ret, zb, x, w_a, w_b, w_out, tm=256):
    m, d = x.shape
    resident = lambda shape: pl.BlockSpec(shape, lambda i: (0, 0), pipeline_mode=pl.Buffered(1))
    return pl.pallas_call(
        _merge_kernel,
        grid=(m // tm,),
        in_specs=[
            pl.BlockSpec((tm, d), lambda i: (i, 0)),
            pl.BlockSpec((tm, d), lambda i: (i, 0)),
            pl.BlockSpec((tm, d), lambda i: (i, B_GA // d)),
            pl.BlockSpec((tm, d), lambda i: (i, B_GB // d)),
            pl.BlockSpec((tm, d), lambda i: (i, 0)),
            resident(w_a.shape), resident(w_b.shape), resident(w_out.shape),
        ],
        out_specs=pl.BlockSpec((tm, d), lambda i: (i, 0)),
        out_shape=jax.ShapeDtypeStruct((m, d), F32),
        compiler_params=_params(("parallel",), 56),
        name="merge_out_proj",
    )(o_nsa, o_ret, zb, zb, x, w_a, w_b, w_out)


def _mem_kv_kernel(m_ref, nw_ref, wk_ref, wv_ref, k_ref, v_ref):
    mn = _rms(m_ref[...], nw_ref[...]).astype(BF16)
    k_ref[...] = _dot(mn, wk_ref[...]).astype(k_ref.dtype)
    v_ref[...] = _dot(mn, wv_ref[...]).astype(v_ref.dtype)


def _mem_kv(mem2, nw, wk, wv, tm=256):
    m, d = mem2.shape
    n = wk.shape[1]
    out = jax.ShapeDtypeStruct((m, n), BF16)
    return pl.pallas_call(
        _mem_kv_kernel,
        grid=(m // tm,),
        in_specs=[
            pl.BlockSpec((tm, d), lambda i: (i, 0)),
            pl.BlockSpec((1, d), lambda i: (0, 0)),
            pl.BlockSpec((d, n), lambda i: (0, 0)),
            pl.BlockSpec((d, n), lambda i: (0, 0)),
        ],
        out_specs=[pl.BlockSpec((tm, n), lambda i: (i, 0))] * 2,
        out_shape=[out, out],
        compiler_params=_params(("parallel",), 40),
        name="mem_kv_proj",
    )(mem2, nw.reshape(1, d), wk, wv)


def _cross_kernel(h_ref, xw_ref, mw_ref, wq_ref, kx_ref, vx_ref, wo_ref, h2_ref, nm_ref):
    h = h_ref[...]
    nx = _rms(h, xw_ref[...]).astype(BF16)
    qx = _dot(nx, wq_ref[...]).astype(BF16)
    outs = []
    for hh in range(X_HEADS):
        cols = slice(hh * X_DH, (hh + 1) * X_DH)
        s = _dot_nt(qx[:, cols], kx_ref[:, cols]) * (X_DH ** -0.5)
        e = jnp.exp(s - jnp.max(s, axis=-1, keepdims=True))
        p = e / jnp.sum(e, axis=-1, keepdims=True)
        outs.append(_dot(p.astype(BF16), vx_ref[:, cols]))
    ox = jnp.concatenate(outs, axis=-1).astype(BF16)
    h2 = h + _dot(ox, wo_ref[...])
    h2_ref[...] = h2
    nm_ref[...] = _rms(h2, mw_ref[...]).astype(nm_ref.dtype)


def _cross_attention(h1, x_norm_w, mlp_norm_w, wq, kx, vx, wo, seq, tm=512):
    m, d = h1.shape
    n = wq.shape[1]
    per_batch = seq // tm
    vec = lambda: pl.BlockSpec((1, d), lambda i: (0, 0))
    return pl.pallas_call(
        _cross_kernel,
        grid=(m // tm,),
        in_specs=[
            pl.BlockSpec((tm, d), lambda i: (i, 0)),
            vec(), vec(),
            pl.BlockSpec((d, n), lambda i: (0, 0)),
            pl.BlockSpec((MEM_LEN, n), lambda i: (i // per_batch, 0)),
            pl.BlockSpec((MEM_LEN, n), lambda i: (i // per_batch, 0)),
            pl.BlockSpec((n, d), lambda i: (0, 0)),
        ],
        out_specs=[pl.BlockSpec((tm, d), lambda i: (i, 0))] * 2,
        out_shape=[jax.ShapeDtypeStruct((m, d), F32), jax.ShapeDtypeStruct((m, d), BF16)],
        compiler_params=_params(("parallel",), 40),
        name="cross_attention",
    )(h1, x_norm_w.reshape(1, d), mlp_norm_w.reshape(1, d), wq, kx, vx, wo)


def _mlp_kernel(nm_ref, wu_ref, wd_ref, h_ref, fw_ref, o_ref):
    j = pl.program_id(1)

    @pl.when(j == 0)
    def _():
        o_ref[...] = jnp.zeros(o_ref.shape, F32)

    u = jnp.maximum(_dot(nm_ref[...], wu_ref[...]), 0.0)
    o_ref[...] += _dot((u * u).astype(BF16), wd_ref[...])

    @pl.when(j == pl.num_programs(1) - 1)
    def _():
        o_ref[...] = _rms(h_ref[...] + o_ref[...], fw_ref[...])


def _mlp(nm, w_up, w_down, h2, final_w, tm=512, tf=1024):
    m, d = nm.shape
    f = w_up.shape[1]
    return pl.pallas_call(
        _mlp_kernel,
        grid=(m // tm, f // tf),
        in_specs=[
            pl.BlockSpec((tm, d), lambda i, j: (i, 0)),
            pl.BlockSpec((d, tf), lambda i, j: (0, j)),
            pl.BlockSpec((tf, d), lambda i, j: (j, 0)),
            pl.BlockSpec((tm, d), lambda i, j: (i, 0)),
            pl.BlockSpec((1, d), lambda i, j: (0, 0)),
        ],
        out_specs=pl.BlockSpec((tm, d), lambda i, j: (i, 0)),
        out_shape=jax.ShapeDtypeStruct((m, d), F32),
        compiler_params=_params(("parallel", "arbitrary"), 56),
        name="mlp_final_norm",
    )(nm, w_up, w_down, h2, final_w.reshape(1, d))


def _layer(h, mem, attn_norm_w, w_in, cmp_pe_k, cmp_w1_k, cmp_w2_k, cmp_pe_v, cmp_w1_v, cmp_w2_v,
           w_a, ret_gn_w, w_b, w_out, x_norm_w, mem_norm_w, wq_x, wk_x, wv_x, wo_x,
           mlp_norm_w, w_up, w_down, out_norm_w, batch, seq):
    blk = W_IN_BLOCK
    kv_block = W_IN_KV // blk
    gate_shift = NSA_HEADS * 3

    wt = w_in.T
    skip_kv = lambda j: j + (j >= kv_block).astype(jnp.int32)
    kv, n, zg, wq_b, wk_b, wv_b, wo_b, c2k_b, c2v_b = _in_proj(
        h, wt, lambda j: kv_block, 1, BF16, regroup=CMP_STRIDE, norm_w=attn_norm_w,
        side_block=W_IN_NSA_GATE // LANES, casts=(wq_x, wk_x, wv_x, wo_x, cmp_w2_k, cmp_w2_v), name="in_proj_kv")
    za, wa_b, wb_b, wout_b, c1k_b, c1v_b = _in_proj(
        n, wt, skip_kv, 4, BF16, casts=(w_a, w_b, w_out, cmp_w1_k, cmp_w1_v),
        lead_scale=(NSA_Q_SCALE, NSA_HEADS * NSA_DK // blk), name="in_proj_a")
    zb, wup_b, wdown_b = _in_proj(
        n, wt, lambda j: W_IN_NSA_GATE // blk + j, B_WIDTH // blk, F32, shift=gate_shift,
        casts=(w_up, w_down), name="in_proj_b")

    assert seq // CMP_STRIDE == LANES
    rows_kv = NSA_GROUPS * batch * LANES
    k2 = kv.reshape(2 * rows_kv, CMP_STRIDE * NSA_DK)
    kc = _compress(k2, 0, rows_kv, cmp_pe_k.reshape(1, -1), c1k_b, c2k_b)
    vc = _compress(k2, rows_kv, rows_kv, cmp_pe_v.reshape(1, -1), c1v_b, c2v_b)

    o_nsa = _nsa_attention(za, zg, kc, vc, batch, seq)
    o_ret = _retention(zb, ret_gn_w, batch, seq)
    h1 = _merge(o_nsa, o_ret, zb, h, wa_b, wb_b, wout_b)

    kx, vx = _mem_kv(mem.reshape(batch * MEM_LEN, D_MODEL), mem_norm_w, wk_b, wv_b)
    h2, nm = _cross_attention(h1, x_norm_w, mlp_norm_w, wq_b, kx, vx, wo_b, seq)
    return _mlp(nm, wup_b, wdown_b, h2, out_norm_w)


def kernel(x, mem, attn_norm_w, w_in, cmp_pe_k, cmp_w1_k, cmp_w2_k, cmp_pe_v, cmp_w1_v, cmp_w2_v, w_a, ret_gn_w,
           w_b, w_out, x_norm_w, mem_norm_w, wq_x, wk_x, wv_x, wo_x, mlp_norm_w, w_up, w_down, final_norm_w):
    batch, seq, d = x.shape
    depth = w_in.shape[0]
    assert depth == 1
    h = x.reshape(batch * seq, d)
    out = _layer(h, mem, attn_norm_w[0], w_in[0], cmp_pe_k[0], cmp_w1_k[0], cmp_w2_k[0],
                 cmp_pe_v[0], cmp_w1_v[0], cmp_w2_v[0], w_a[0], ret_gn_w[0], w_b[0], w_out[0],
                 x_norm_w[0], mem_norm_w[0], wq_x[0], wk_x[0], wv_x[0], wo_x[0],
                 mlp_norm_w[0], w_up[0], w_down[0], final_norm_w, batch, seq)
    return out.reshape(batch, seq, d)
```

```python
import functools

import jax
import jax.numpy as jnp
import numpy as np
from jax import lax
from jax.experimental import pallas as pl
from jax.experimental.pallas import tpu as pltpu

F32 = jnp.float32
BF16 = jnp.bfloat16

D_MODEL = 2048
MEM_LEN = 256
NSA_HEADS = 16
NSA_GROUPS = 4
NSA_REP = NSA_HEADS // NSA_GROUPS
NSA_DK = 128
NSA_DV = 128
CMP_LEN = 32
CMP_STRIDE = 16
CMP_HIDDEN = 1024
SEL_LEN = 64
SEL_TOPK = 16
WIN = 512
RET_HEADS = 8
RET_DK = 128
RET_DV = 256
RET_CHUNK = 128
ROPE_BASE = 10000.0
X_HEADS = 4
X_DH = 128
D_FF = 4 * D_MODEL
EPS = 1e-6
NEG = -1e30
LOG2E = 1.4426950408889634
NSA_Q_SCALE = NSA_DK ** -0.5 * LOG2E

LANES = 128
F32_SUBLANES = 8
MXU_ROWS = 256
V7X_VMEM_MIB = 64

W_IN_BLOCK = 1024
W_IN_KV = 2048
W_IN_NSA_GATE = 5120
A_QN = 0
A_KS = 2048
A_VS = 2560
A_KW = 3072
A_VW = 3584
B_QR = 0
B_KR = 1024
B_VR = 2048
B_GR = 4096
B_GA = 6144
B_GB = 8192
B_WIDTH = 10240

NT_DIMS = (((1,), (1,)), ((), ()))
TN_DIMS = (((0,), (0,)), ((), ()))


def _params(sem, vmem_mib):
    assert vmem_mib < V7X_VMEM_MIB
    return pltpu.CompilerParams(dimension_semantics=sem, vmem_limit_bytes=vmem_mib * 1024 * 1024)


def _rms(x, w):
    return x * lax.rsqrt(jnp.mean(x * x, axis=-1, keepdims=True) + EPS) * w


def _dot(a, b):
    return jnp.dot(a, b, preferred_element_type=F32)


def _dot_nt(a, b):
    return lax.dot_general(a, b, NT_DIMS, preferred_element_type=F32)


CAST_ROWS = 16


def _cast_rows(dst_ref, dst0, src_ref, src0, nrows):
    def body(r, carry):
        off = r * CAST_ROWS
        dst_ref[pl.ds(pl.multiple_of(dst0 + off, CAST_ROWS), CAST_ROWS), :] = (
            src_ref[pl.ds(pl.multiple_of(src0 + off, F32_SUBLANES), CAST_ROWS), :].astype(BF16))
        return carry

    lax.fori_loop(0, nrows // CAST_ROWS, body, 0)


def _in_proj_kernel(*refs, shift, regroup, n_cast, norm, side, lead_scale):
    n_in = 4 if norm else 3
    n_ref, w_ref, wnext_ref = refs[:3]
    cast_in = refs[n_in:n_in + n_cast]
    o_ref = refs[n_in + n_cast]
    n_out = n_in + n_cast + 1 + int(norm) + int(side)
    cast_out = refs[n_out:n_out + n_cast]
    wb_sc, *rest = refs[n_out + n_cast:]
    tn = wb_sc.shape[0]
    for src, dst in zip(cast_in, cast_out):
        dst[...] = src[...].astype(dst.dtype)

    @pl.when(pl.program_id(1) == 0)
    def _():
        _cast_rows(wb_sc, 0, w_ref, shift, tn - shift)
        if shift:
            _cast_rows(wb_sc, tn - shift, wnext_ref, 0, shift)
        if side:
            _cast_rows(rest[-1], 0, wnext_ref, 0, LANES)

    if norm:
        xn = _rms(n_ref[...], refs[3][...]).astype(BF16)
        refs[n_in + n_cast + 1][...] = xn
    else:
        xn = n_ref[...]
    if side:
        refs[n_out - 1][...] = _dot_nt(xn, rest[-1][...])
    res = _dot_nt(xn, wb_sc[...])
    if regroup:
        r_sc = rest[0]
        tm = res.shape[0]
        for c in range(tn // LANES):
            r_sc[c] = res[:, c * LANES:(c + 1) * LANES]
        for c in range(tn // LANES):
            for j in range(regroup):
                o_ref[c, :, j * LANES:(j + 1) * LANES] = (
                    r_sc[c, pl.ds(j, tm // regroup, stride=regroup), :].astype(o_ref.dtype))
    else:
        if lead_scale is not None:
            factor, blocks = lead_scale
            res = res * jnp.where(pl.program_id(0) < blocks, factor, 1.0)
        o_ref[...] = res.astype(o_ref.dtype)


def _in_proj(n, wt, src_block, n_blocks, out_dtype, *, shift=0, regroup=0, casts=(), norm_w=None,
             side_block=None, lead_scale=None, tm=1024, tn=1024, name):
    m, k = n.shape
    per = tn // LANES
    m_tiles = m // tm
    norm = norm_w is not None
    side = side_block is not None
    assert shift % CAST_ROWS == 0 and shift <= LANES and not (norm and n_blocks > 1)
    assert not side or (shift == 0 and n_blocks == 1)
    scratch = [pltpu.VMEM((tn, k), BF16)]
    row_tile = pl.BlockSpec((tm, k), lambda j, i: (i, 0))
    once = {"pipeline_mode": pl.Buffered(1)} if n_blocks == 1 else {}
    if regroup:
        assert n_blocks == 1
        out_shape = jax.ShapeDtypeStruct((per, m // regroup, regroup * LANES), out_dtype)
        out_spec = pl.BlockSpec((per, tm // regroup, regroup * LANES), lambda j, i: (0, i, 0))
        scratch.append(pltpu.VMEM((per, tm, LANES), F32))
    else:
        out_shape = jax.ShapeDtypeStruct((m, n_blocks * tn), out_dtype)
        out_spec = pl.BlockSpec((tm, tn), lambda j, i: (i, j))

    cast_steps = 1 << ((n_blocks * m_tiles).bit_length() - 1)
    cast_specs = []
    for a in casts:
        rows = a.shape[0] // cast_steps
        assert a.shape[0] % cast_steps == 0 and rows % CAST_ROWS == 0
        cast_specs.append(pl.BlockSpec(
            (rows, a.shape[1]), lambda j, i: (jnp.minimum(j * m_tiles + i, cast_steps - 1), 0)))

    if side:
        scratch.append(pltpu.VMEM((LANES, k), BF16))
    next_rows = (lambda j, i: (side_block, 0)) if side else (lambda j, i: ((src_block(j) + 1) * per, 0))
    return pl.pallas_call(
        functools.partial(_in_proj_kernel, shift=shift, regroup=regroup, n_cast=len(casts), norm=norm, side=side,
                          lead_scale=lead_scale),
        grid=(n_blocks, m_tiles),
        in_specs=[
            row_tile,
            pl.BlockSpec((tn, k), lambda j, i: (src_block(j), 0), **once),
            pl.BlockSpec((LANES, k), next_rows, **once),
            *([pl.BlockSpec((1, k), lambda j, i: (0, 0))] if norm else []),
            *cast_specs,
        ],
        out_specs=[out_spec, *([row_tile] if norm else []),
                   *([pl.BlockSpec((tm, LANES), lambda j, i: (i, 0))] if side else []), *cast_specs],
        out_shape=[out_shape, *([jax.ShapeDtypeStruct((m, k), BF16)] if norm else []),
                   *([jax.ShapeDtypeStruct((m, LANES), F32)] if side else []),
                   *[jax.ShapeDtypeStruct(a.shape, BF16) for a in casts]],
        scratch_shapes=scratch,
        compiler_params=_params(("arbitrary", "arbitrary"), 60 if norm else 56),
        name=name,
    )(n, wt, wt, *([norm_w.reshape(1, k)] if norm else []), *casts)


def _compress_kernel(k2_ref, pe_ref, w1_ref, w2_ref, o_ref):
    rows, half = k2_ref.shape
    k2 = k2_ref[...].astype(F32)
    a_lo = (k2 + pe_ref[:, :half]).astype(BF16)
    a_hi = (k2 + pe_ref[:, half:]).astype(BF16)
    lo = _dot(a_lo, w1_ref[:half, :])
    hi = _dot(a_hi, w1_ref[half:, :])
    h = lo + pltpu.roll(hi, rows - 1, axis=0)
    act = h * jax.nn.sigmoid(h)
    o_ref[...] = _dot(act.astype(BF16), w2_ref[...]).astype(o_ref.dtype)


def _compress(k2, first_row, m, pe, w1, w2, rows=256):
    half = k2.shape[1]
    hid = w1.shape[1]
    dout = w2.shape[1]
    first = first_row // rows
    return pl.pallas_call(
        _compress_kernel,
        grid=(m // rows,),
        in_specs=[
            pl.BlockSpec((rows, half), lambda i: (first + i, 0)),
            pl.BlockSpec((1, 2 * half), lambda i: (0, 0)),
            pl.BlockSpec((2 * half, hid), lambda i: (0, 0)),
            pl.BlockSpec((hid, dout), lambda i: (0, 0)),
        ],
        out_specs=pl.BlockSpec((rows, dout), lambda i: (i, 0)),
        out_shape=jax.ShapeDtypeStruct((m, dout), BF16),
        compiler_params=_params(("parallel",), 40),
        name="nsa_compress",
    )(k2, pe, w1, w2)


def _nsa_kernel(*refs, seq, tq, kblk, gp, hm):
    q_ref = refs[0]
    kc_refs = refs[1:1 + gp]
    vc_refs = refs[1 + gp:1 + 2 * gp]
    ks_ref, vs_ref, kw_ref, vw_ref, g_ref, e_ref, o_ref, s_sc, mx_sc, acc_sc, po_sc = refs[1 + 2 * gp:]
    i = pl.program_id(2)
    t0 = i * tq
    rep = NSA_REP
    n_cmp = (seq - CMP_LEN) // CMP_STRIDE + 1
    n_sel = seq // SEL_LEN
    topk = min(SEL_TOPK, n_sel)
    groups = range(gp)

    q = q_ref[...]
    units = range(rep // hm)
    rows = lambda gg, r: slice((gg * rep + r) * tq, (gg * rep + r + 1) * tq)
    urows = lambda gg, u: slice((gg * rep + u * hm) * tq, (gg * rep + (u + 1) * hm) * tq)
    part = lambda x, h: x[h * tq:(h + 1) * tq]
    head_q = lambda gg, r: q[:, (gg * rep + r) * NSA_DK:(gg * rep + r + 1) * NSA_DK]
    unit_q = {(gg, u): jnp.concatenate([head_q(gg, u * hm + h) for h in range(hm)], axis=0)
              for gg in groups for u in units}
    group_cols = lambda ref, gg: ref.at[:, gg * LANES:(gg + 1) * LANES]

    def masked(s, mask1):
        return jnp.concatenate([jnp.where(mask1, part(s, h), NEG) for h in range(hm)], axis=0)
    tcol = t0 + lax.broadcasted_iota(jnp.int32, (tq, 1), 0)
    per_group = rep * 3
    g_sig = jax.nn.sigmoid(g_ref[...])
    first_group = pl.program_id(1) * gp
    gs = [pltpu.roll(g_sig, lax.rem(LANES - (first_group + gg) * per_group, LANES), axis=1) for gg in groups]
    gate = lambda gg, r, branch: gs[gg][:, 3 * r + branch:3 * r + branch + 1]

    def exp_rows(sm):
        return jnp.exp2(sm - jnp.max(sm, axis=-1, keepdims=True))

    def with_ones(v):
        return jnp.concatenate([v, jnp.ones(v.shape, v.dtype)], axis=1)

    def normalised(ev):
        return ev[:, :NSA_DV] / ev[:, NSA_DV:]

    c_idx = lax.broadcasted_iota(jnp.int32, (tq, LANES), 1)
    mask_c = ((c_idx * CMP_STRIDE + (CMP_LEN - 1)) <= tcol) & (c_idx < n_cmp)
    mask_cf = jnp.where(mask_c, 1.0, 0.0)
    o_cmp, psum = {}, []
    for gg in groups:
        kc = kc_refs[gg][...]
        vc = vc_refs[gg][...]
        tot = None
        for u in units:
            e = exp_rows(masked(_dot_nt(unit_q[gg, u], kc), mask_c))
            p = e / jnp.sum(e, axis=-1, keepdims=True)
            p = jnp.concatenate([part(p, h) * mask_cf for h in range(hm)], axis=0)
            o_cmp[gg, u] = _dot(p.astype(BF16), vc)
            for h in range(hm):
                tot = part(p, h) if tot is None else tot + part(p, h)
        psum.append(tot)

    wlen = WIN + tq
    start = pl.multiple_of(jnp.maximum(i - WIN // tq, 0) * tq, tq)
    dlt = tcol - (start + lax.broadcasted_iota(jnp.int32, (tq, wlen), 1))
    mask_w = (dlt >= 0) & (dlt < WIN)
    for gg in groups:
        kw = group_cols(kw_ref, gg)[pl.ds(start, wlen), :]
        vw1 = with_ones(group_cols(vw_ref, gg)[pl.ds(start, wlen), :])
        for u in units:
            e = exp_rows(masked(_dot_nt(unit_q[gg, u], kw), mask_w))
            o_win = normalised(_dot(e.astype(BF16), vw1))
            for h in range(hm):
                r = u * hm + h
                po_sc[rows(gg, r), :] = (gate(gg, r, 0) * part(o_cmp[gg, u], h)
                                         + gate(gg, r, 2) * part(o_win, h))

    jo = lax.broadcasted_iota(jnp.int32, (n_sel, LANES), 0)
    co = lax.broadcasted_iota(jnp.int32, (n_sel, LANES), 1)
    ov_t = jnp.where((co * CMP_STRIDE < jo * SEL_LEN + SEL_LEN) & (co * CMP_STRIDE + CMP_LEN > jo * SEL_LEN)
                     & (co < n_cmp), 1.0, 0.0).astype(BF16)
    j_idx = lax.broadcasted_iota(jnp.int32, (n_sel, tq), 0)
    cur = lax.shift_right_logical(t0 + lax.broadcasted_iota(jnp.int32, (n_sel, tq), 1), int(np.log2(SEL_LEN)))
    forced = (j_idx == 0) | (j_idx == cur) | (j_idx == cur - 1)
    future = j_idx > cur
    sel = []
    for gg in groups:
        p_hi = psum[gg].astype(BF16)
        p_lo = (psum[gg] - p_hi.astype(F32)).astype(BF16)
        imp = _dot_nt(ov_t, p_hi) + _dot_nt(ov_t, p_lo)
        impm = jnp.where(forced, jnp.inf, jnp.where(future, -jnp.inf, imp))
        rank = jnp.zeros((n_sel, tq), F32)
        for ii in range(n_sel):
            row = impm[ii:ii + 1, :]
            beats = (row > impm) | ((row == impm) & (j_idx > ii))
            rank = rank + jnp.where(beats, 1.0, 0.0)
        sel_t = jnp.where(rank < topk, 1.0, 0.0)
        sel.append(jnp.concatenate([sel_t, jnp.zeros((LANES - n_sel, tq), F32)], axis=0).T.astype(BF16))

    n_chunks = lax.div(t0 + tq + (kblk - 1), kblk)

    def lane_fold_max(x):
        out = x[:, :LANES]
        for c in range(1, kblk // LANES):
            out = jnp.maximum(out, x[:, c * LANES:(c + 1) * LANES])
        return out

    def chunk_start(kb):
        return kb * kblk if isinstance(kb, int) else pl.multiple_of(kb * kblk, kblk)

    def score_chunk(kb):
        k0 = chunk_start(kb)
        causal = (k0 + lax.broadcasted_iota(jnp.int32, (tq, kblk), 1)) <= tcol
        folds = []
        for gg in groups:
            k = group_cols(ks_ref, gg)[pl.ds(k0, kblk), :]
            mask1 = (_dot(sel[gg], e_ref[kb]) > 0.5) & causal
            for u in units:
                sm_ = masked(_dot_nt(unit_q[gg, u], k), mask1)
                s_sc[kb, urows(gg, u), :] = sm_
                folds.append(lane_fold_max(sm_))
        return jnp.concatenate(folds, axis=0)

    def score_pass(kb, carry):
        mx_sc[...] = jnp.maximum(mx_sc[...], score_chunk(kb))
        return carry

    mx_sc[...] = score_chunk(0)
    lax.fori_loop(1, n_chunks, score_pass, 0)
    m_sel = jnp.max(mx_sc[...], axis=-1, keepdims=True)

    def value_chunk(kb):
        k0 = chunk_start(kb)
        pvs = []
        for gg in groups:
            v1 = with_ones(group_cols(vs_ref, gg)[pl.ds(k0, kblk), :])
            for u in units:
                pk = jnp.exp2(s_sc[kb, urows(gg, u), :] - m_sel[urows(gg, u)])
                pvs.append(_dot(pk.astype(BF16), v1))
        return jnp.concatenate(pvs, axis=0)

    def value_pass(kb, carry):
        acc_sc[...] += value_chunk(kb)
        return carry

    acc_sc[...] = value_chunk(0)
    lax.fori_loop(1, n_chunks, value_pass, 0)

    for gg in groups:
        for r in range(rep):
            o = po_sc[rows(gg, r), :] + gate(gg, r, 1) * normalised(acc_sc[rows(gg, r), :])
            col = (gg * rep + r) * NSA_DV
            o_ref[:, col:col + NSA_DV] = o.astype(o_ref.dtype)


def _nsa_attention(za, zg, kc, vc, batch, seq, tq=256, kblk=512, gp=2):
    assert seq % kblk == 0 and seq >= WIN + tq and WIN % tq == 0 and NSA_GROUPS % gp == 0
    nq = seq // tq
    gw = gp * NSA_REP * NSA_DK
    streams = gp * NSA_REP * tq
    hm = max(1, MXU_ROWS // tq)
    assert NSA_REP % hm == 0
    kern = functools.partial(_nsa_kernel, seq=seq, tq=tq, kblk=kblk, gp=gp, hm=hm)
    key = np.arange(seq).reshape(seq // kblk, 1, kblk)
    expand = jnp.asarray(key // SEL_LEN == np.arange(LANES).reshape(1, LANES, 1), BF16)

    def kv_spec(off):
        return pl.BlockSpec((seq, gp * LANES), lambda b, g, i: (b, off // (gp * LANES) + g))

    def cmp_spec(gg):
        return pl.BlockSpec((LANES, NSA_DK), lambda b, g, i: ((g * gp + gg) * batch + b, 0))

    return pl.pallas_call(
        kern,
        grid=(batch, NSA_GROUPS // gp, nq),
        in_specs=[
            pl.BlockSpec((tq, gw), lambda b, g, i: (b * nq + i, A_QN // gw + g)),
            *[cmp_spec(gg) for gg in range(gp)],
            *[cmp_spec(gg) for gg in range(gp)],
            kv_spec(A_KS), kv_spec(A_VS), kv_spec(A_KW), kv_spec(A_VW),
            pl.BlockSpec((tq, LANES), lambda b, g, i: (b * nq + i, 0)),
            pl.BlockSpec(expand.shape, lambda b, g, i: (0, 0, 0)),
        ],
        out_specs=pl.BlockSpec((tq, gw), lambda b, g, i: (b * nq + i, g)),
        out_shape=jax.ShapeDtypeStruct((batch * seq, NSA_HEADS * NSA_DV), BF16),
        scratch_shapes=[
            pltpu.VMEM((seq // kblk, streams, kblk), F32),
            pltpu.VMEM((streams, LANES), F32),
            pltpu.VMEM((streams, 2 * NSA_DV), F32),
            pltpu.VMEM((streams, NSA_DV), F32),
        ],
        compiler_params=_params(("parallel", "parallel", "arbitrary"), 56),
        name="nsa_attention",
    )(za, *([kc] * gp), *([vc] * gp), za, za, za, za, zg, expand)


def _retention_kernel(q_ref, k_ref, v_ref, g_ref, cos_ref, sin_e_ref, sin_o_ref, dec_ref, wq_ref, wk_ref,
                      gc_ref, gn_ref, o_ref, st_ref):
    @pl.when(pl.program_id(1) == 0)
    def _():
        st_ref[...] = jnp.zeros(st_ref.shape, F32)

    c = RET_CHUNK
    for h in range(RET_HEADS):
        st = st_ref[h]
        wq = wq_ref[h]
        wq2 = jnp.concatenate([wq, wq], axis=1)
        cols = slice(h * RET_DV, (h + 1) * RET_DV)
        for sub in range(q_ref.shape[0] // c):
            rows = slice(sub * c, (sub + 1) * c)
            cos = cos_ref[rows, :]
            sin_e = sin_e_ref[rows, :]
            sin_o = sin_o_ref[rows, :]

            def rotate(x):
                return x * cos + pltpu.roll(x, RET_DK - 1, axis=1) * sin_e + pltpu.roll(x, 1, axis=1) * sin_o

            qf = rotate(q_ref[rows, h * RET_DK:(h + 1) * RET_DK])
            kf = rotate(k_ref[rows, h * RET_DK:(h + 1) * RET_DK]) * (RET_DK ** -0.5)
            qb = qf.astype(BF16)
            v = v_ref[rows, cols].astype(BF16)
            s = _dot_nt(qb, kf.astype(BF16)) * dec_ref[h]
            o = _dot(s.astype(BF16), v) + _dot(qb, st.astype(BF16)) * wq2
            st = st * gc_ref[h] + lax.dot_general((kf * wk_ref[h]).astype(BF16), v, TN_DIMS,
                                                  preferred_element_type=F32)
            mu = jnp.mean(o, axis=-1, keepdims=True)
            d = o - mu
            var = jnp.mean(d * d, axis=-1, keepdims=True)
            on = d * lax.rsqrt(var + EPS) * gn_ref[:, cols]
            gr = g_ref[rows, cols]
            o_ref[rows, cols] = (gr * jax.nn.sigmoid(gr) * on).astype(o_ref.dtype)
        st_ref[h] = st


def _retention(zb, gn_w, batch, seq, chunks_per_step=2):
    c = RET_CHUNK
    rows = chunks_per_step * c
    nc = seq // rows
    hq = RET_HEADS * RET_DK
    hv = RET_HEADS * RET_DV
    f32 = np.float32
    inv = f32(ROPE_BASE) ** (-np.arange(0, RET_DK, 2, dtype=f32) / f32(RET_DK))
    ang = np.arange(seq, dtype=f32)[:, None] * inv[None, :]
    zero = np.zeros_like(ang)
    pairs = lambda even, odd: np.stack([even, odd], axis=-1).reshape(seq, RET_DK)
    cos = pairs(np.cos(ang), np.cos(ang))
    sin_e = pairs(-np.sin(ang), zero)
    sin_o = pairs(zero, np.sin(ang))
    log_g = np.log1p(-np.exp2(f32(-5.0) - np.arange(RET_HEADS, dtype=f32)))
    idx = np.arange(c, dtype=f32)
    rel = idx[:, None] - idx[None, :]
    decay = np.where(rel >= 0, np.exp(log_g[:, None, None] * np.maximum(rel, f32(0.0))), f32(0.0)).astype(f32)
    lanes = lambda a: np.ascontiguousarray(np.broadcast_to(a[:, :, None], (RET_HEADS, c, RET_DK)), dtype=f32)
    w_k = lanes(np.exp(log_g[:, None] * (f32(c - 1) - idx)[None, :]))
    w_q = lanes(np.exp(log_g[:, None] * (idx + f32(1.0))[None, :]))
    g_chunk = np.broadcast_to(np.exp(log_g * f32(c))[:, None, None], (RET_HEADS, 1, RET_DV)).astype(f32)

    row = lambda b, n: b * nc + n
    return pl.pallas_call(
        _retention_kernel,
        grid=(batch, nc),
        in_specs=[
            pl.BlockSpec((rows, hq), lambda b, n: (row(b, n), B_QR // hq)),
            pl.BlockSpec((rows, hq), lambda b, n: (row(b, n), B_KR // hq)),
            pl.BlockSpec((rows, hv), lambda b, n: (row(b, n), B_VR // hv)),
            pl.BlockSpec((rows, hv), lambda b, n: (row(b, n), B_GR // hv)),
            pl.BlockSpec((rows, RET_DK), lambda b, n: (n, 0)),
            pl.BlockSpec((rows, RET_DK), lambda b, n: (n, 0)),
            pl.BlockSpec((rows, RET_DK), lambda b, n: (n, 0)),
            pl.BlockSpec((RET_HEADS, c, c), lambda b, n: (0, 0, 0)),
            pl.BlockSpec((RET_HEADS, c, RET_DK), lambda b, n: (0, 0, 0)),
            pl.BlockSpec((RET_HEADS, c, RET_DK), lambda b, n: (0, 0, 0)),
            pl.BlockSpec((RET_HEADS, 1, RET_DV), lambda b, n: (0, 0, 0)),
            pl.BlockSpec((1, hv), lambda b, n: (0, 0)),
        ],
        out_specs=pl.BlockSpec((rows, hv), lambda b, n: (row(b, n), 0)),
        out_shape=jax.ShapeDtypeStruct((batch * seq, hv), BF16),
        scratch_shapes=[pltpu.VMEM((RET_HEADS, RET_DK, RET_DV), F32)],
        compiler_params=_params(("parallel", "arbitrary"), 40),
        name="retention",
    )(zb, zb, zb, zb, cos, sin_e, sin_o, decay, w_q, w_k, g_chunk, gn_w.reshape(1, hv))


def _merge_kernel(on_ref, or_ref, ga_ref, gb_ref, x_ref, wa_ref, wb_ref, wo_ref, h_ref):
    a = _dot(on_ref[...], wa_ref[...])
    b = _dot(or_ref[...], wb_ref[...])
    merged = jax.nn.sigmoid(ga_ref[...]) * a + jax.nn.sigmoid(gb_ref[...]) * b
    h_ref[...] = x_ref[...] + _dot(merged.astype(BF16), wo_ref[...])


def _merge(o_nsa, o_ret, zb, x, w_a, w_b, w_out, tm=256):
    m, d = x.shape
    resident = lambda shape: pl.BlockSpec(shape, lambda i: (0, 0), pipeline_mode=pl.Buffered(1))
    return pl.pallas_call(
        _merge_kernel,
        grid=(m // tm,),
        in_specs=[
            pl.BlockSpec((tm, d), lambda i: (i, 0)),
            pl.BlockSpec((tm, d), lambda i: (i, 0)),
            pl.BlockSpec((tm, d), lambda i: (i, B_GA // d)),
            pl.BlockSpec((tm, d), lambda i: (i, B_GB // d)),
            pl.BlockSpec((tm, d), lambda i: (i, 0)),
            resident(w_a.shape), resident(w_b.shape), resident(w_out.shape),
        ],
        out_specs=pl.BlockSpec((tm, d), lambda i: (i, 0)),
        out_shape=jax.ShapeDtypeStruct((m, d), F32),
        compiler_params=_params(("parallel",), 56),
        name="merge_out_proj",
    )(o_nsa, o_ret, zb, zb, x, w_a, w_b, w_out)


def _mem_kv_kernel(m_ref, nw_ref, wk_ref, wv_ref, k_ref, v_ref):
    mn = _rms(m_ref[...], nw_ref[...]).astype(BF16)
    k_ref[...] = _dot(mn, wk_ref[...]).astype(k_ref.dtype)
    v_ref[...] = _dot(mn, wv_ref[...]).astype(v_ref.dtype)


def _mem_kv(mem2, nw, wk, wv, tm=256):
    m, d = mem2.shape
    n = wk.shape[1]
    out = jax.ShapeDtypeStruct((m, n), BF16)
    return pl.pallas_call(
        _mem_kv_kernel,
        grid=(m // tm,),
        in_specs=[
            pl.BlockSpec((tm, d), lambda i: (i, 0)),
            pl.BlockSpec((1, d), lambda i: (0, 0)),
            pl.BlockSpec((d, n), lambda i: (0, 0)),
            pl.BlockSpec((d, n), lambda i: (0, 0)),
        ],
        out_specs=[pl.BlockSpec((tm, n), lambda i: (i, 0))] * 2,
        out_shape=[out, out],
        compiler_params=_params(("parallel",), 40),
        name="mem_kv_proj",
    )(mem2, nw.reshape(1, d), wk, wv)


def _cross_kernel(h_ref, xw_ref, mw_ref, wq_ref, kx_ref, vx_ref, wo_ref, h2_ref, nm_ref):
    h = h_ref[...]
    nx = _rms(h, xw_ref[...]).astype(BF16)
    qx = _dot(nx, wq_ref[...]).astype(BF16)
    outs = []
    for hh in range(X_HEADS):
        cols = slice(hh * X_DH, (hh + 1) * X_DH)
        s = _dot_nt(qx[:, cols], kx_ref[:, cols]) * (X_DH ** -0.5)
        e = jnp.exp(s - jnp.max(s, axis=-1, keepdims=True))
        p = e / jnp.sum(e, axis=-1, keepdims=True)
        outs.append(_dot(p.astype(BF16), vx_ref[:, cols]))
    ox = jnp.concatenate(outs, axis=-1).astype(BF16)
    h2 = h + _dot(ox, wo_ref[...])
    h2_ref[...] = h2
    nm_ref[...] = _rms(h2, mw_ref[...]).astype(nm_ref.dtype)


def _cross_attention(h1, x_norm_w, mlp_norm_w, wq, kx, vx, wo, seq, tm=512):
    m, d = h1.shape
    n = wq.shape[1]
    per_batch = seq // tm
    vec = lambda: pl.BlockSpec((1, d), lambda i: (0, 0))
    return pl.pallas_call(
        _cross_kernel,
        grid=(m // tm,),
        in_specs=[
            pl.BlockSpec((tm, d), lambda i: (i, 0)),
            vec(), vec(),
            pl.BlockSpec((d, n), lambda i: (0, 0)),
            pl.BlockSpec((MEM_LEN, n), lambda i: (i // per_batch, 0)),
            pl.BlockSpec((MEM_LEN, n), lambda i: (i // per_batch, 0)),
            pl.BlockSpec((n, d), lambda i: (0, 0)),
        ],
        out_specs=[pl.BlockSpec((tm, d), lambda i: (i, 0))] * 2,
        out_shape=[jax.ShapeDtypeStruct((m, d), F32), jax.ShapeDtypeStruct((m, d), BF16)],
        compiler_params=_params(("parallel",), 40),
        name="cross_attention",
    )(h1, x_norm_w.reshape(1, d), mlp_norm_w.reshape(1, d), wq, kx, vx, wo)


def _mlp_kernel(nm_ref, wu_ref, wd_ref, h_ref, fw_ref, o_ref):
    j = pl.program_id(1)

    @pl.when(j == 0)
    def _():
        o_ref[...] = jnp.zeros(o_ref.shape, F32)

    u = jnp.maximum(_dot(nm_ref[...], wu_ref[...]), 0.0)
    o_ref[...] += _dot((u * u).astype(BF16), wd_ref[...])

    @pl.when(j == pl.num_programs(1) - 1)
    def _():
        o_ref[...] = _rms(h_ref[...] + o_ref[...], fw_ref[...])


def _mlp(nm, w_up, w_down, h2, final_w, tm=512, tf=2048):
    m, d = nm.shape
    f = w_up.shape[1]
    return pl.pallas_call(
        _mlp_kernel,
        grid=(m // tm, f // tf),
        in_specs=[
            pl.BlockSpec((tm, d), lambda i, j: (i, 0)),
            pl.BlockSpec((d, tf), lambda i, j: (0, j)),
            pl.BlockSpec((tf, d), lambda i, j: (j, 0)),
            pl.BlockSpec((tm, d), lambda i, j: (i, 0)),
            pl.BlockSpec((1, d), lambda i, j: (0, 0)),
        ],
        out_specs=pl.BlockSpec((tm, d), lambda i, j: (i, 0)),
        out_shape=jax.ShapeDtypeStruct((m, d), F32),
        compiler_params=_params(("parallel", "arbitrary"), 60),
        name="mlp_final_norm",
    )(nm, w_up, w_down, h2, final_w.reshape(1, d))


def _layer(h, mem, attn_norm_w, w_in, cmp_pe_k, cmp_w1_k, cmp_w2_k, cmp_pe_v, cmp_w1_v, cmp_w2_v,
           w_a, ret_gn_w, w_b, w_out, x_norm_w, mem_norm_w, wq_x, wk_x, wv_x, wo_x,
           mlp_norm_w, w_up, w_down, out_norm_w, batch, seq):
    blk = W_IN_BLOCK
    kv_block = W_IN_KV // blk
    gate_shift = NSA_HEADS * 3

    wt = w_in.T
    skip_kv = lambda j: j + (j >= kv_block).astype(jnp.int32)
    kv, n, zg, wq_b, wk_b, wv_b, wo_b, c2k_b, c2v_b = _in_proj(
        h, wt, lambda j: kv_block, 1, BF16, regroup=CMP_STRIDE, norm_w=attn_norm_w,
        side_block=W_IN_NSA_GATE // LANES, casts=(wq_x, wk_x, wv_x, wo_x, cmp_w2_k, cmp_w2_v), name="in_proj_kv")
    za, wa_b, wb_b, wout_b, c1k_b, c1v_b = _in_proj(
        n, wt, skip_kv, 4, BF16, casts=(w_a, w_b, w_out, cmp_w1_k, cmp_w1_v),
        lead_scale=(NSA_Q_SCALE, NSA_HEADS * NSA_DK // blk), name="in_proj_a")
    zb, wup_b, wdown_b = _in_proj(
        n, wt, lambda j: W_IN_NSA_GATE // blk + j, B_WIDTH // blk, F32, shift=gate_shift,
        casts=(w_up, w_down), name="in_proj_b")

    assert seq // CMP_STRIDE == LANES
    rows_kv = NSA_GROUPS * batch * LANES
    k2 = kv.reshape(2 * rows_kv, CMP_STRIDE * NSA_DK)
    kc = _compress(k2, 0, rows_kv, cmp_pe_k.reshape(1, -1), c1k_b, c2k_b)
    vc = _compress(k2, rows_kv, rows_kv, cmp_pe_v.reshape(1, -1), c1v_b, c2v_b)

    o_nsa = _nsa_attention(za, zg, kc, vc, batch, seq)
    o_ret = _retention(zb, ret_gn_w, batch, seq)
    h1 = _merge(o_nsa, o_ret, zb, h, wa_b, wb_b, wout_b)

    kx, vx = _mem_kv(mem.reshape(batch * MEM_LEN, D_MODEL), mem_norm_w, wk_b, wv_b)
    h2, nm = _cross_attention(h1, x_norm_w, mlp_norm_w, wq_b, kx, vx, wo_b, seq)
    return _mlp(nm, wup_b, wdown_b, h2, out_norm_w)


def kernel(x, mem, attn_norm_w, w_in, cmp_pe_k, cmp_w1_k, cmp_w2_k, cmp_pe_v, cmp_w1_v, cmp_w2_v, w_a, ret_gn_w,
           w_b, w_out, x_norm_w, mem_norm_w, wq_x, wk_x, wv_x, wo_x, mlp_norm_w, w_up, w_down, final_norm_w):
    batch, seq, d = x.shape
    depth = w_in.shape[0]
    assert depth == 1
    h = x.reshape(batch * seq, d)
    out = _layer(h, mem, attn_norm_w[0], w_in[0], cmp_pe_k[0], cmp_w1_k[0], cmp_w2_k[0],
                 cmp_pe_v[0], cmp_w1_v[0], cmp_w2_v[0], w_a[0], ret_gn_w[0], w_b[0], w_out[0],
                 x_norm_w[0], mem_norm_w[0], wq_x[0], wk_x[0], wv_x[0], wo_x[0],
                 mlp_norm_w[0], w_up[0], w_down[0], final_norm_w, batch, seq)
    return out.reshape(batch, seq, d)
```

```python
import functools

import jax
import jax.numpy as jnp
import numpy as np
from jax import lax
from jax.experimental import pallas as pl
from jax.experimental.pallas import tpu as pltpu

F32 = jnp.float32
BF16 = jnp.bfloat16

D_MODEL = 2048
MEM_LEN = 256
NSA_HEADS = 16
NSA_GROUPS = 4
NSA_REP = NSA_HEADS // NSA_GROUPS
NSA_DK = 128
NSA_DV = 128
CMP_LEN = 32
CMP_STRIDE = 16
CMP_HIDDEN = 1024
SEL_LEN = 64
SEL_TOPK = 16
WIN = 512
RET_HEADS = 8
RET_DK = 128
RET_DV = 256
RET_CHUNK = 128
ROPE_BASE = 10000.0
X_HEADS = 4
X_DH = 128
D_FF = 4 * D_MODEL
EPS = 1e-6
NEG = -1e30
LOG2E = 1.4426950408889634
NSA_Q_SCALE = NSA_DK ** -0.5 * LOG2E

LANES = 128
F32_SUBLANES = 8
MXU_ROWS = 256
V7X_VMEM_MIB = 64

W_IN_BLOCK = 1024
W_IN_KV = 2048
W_IN_NSA_GATE = 5120
A_QN = 0
A_KS = 2048
A_VS = 2560
A_KW = 3072
A_VW = 3584
B_QR = 0
B_KR = 1024
B_VR = 2048
B_GR = 4096
B_GA = 6144
B_GB = 8192
B_WIDTH = 10240

NT_DIMS = (((1,), (1,)), ((), ()))
TN_DIMS = (((0,), (0,)), ((), ()))


def _params(sem, vmem_mib):
    assert vmem_mib < V7X_VMEM_MIB
    return pltpu.CompilerParams(dimension_semantics=sem, vmem_limit_bytes=vmem_mib * 1024 * 1024)


def _rms(x, w):
    return x * lax.rsqrt(jnp.mean(x * x, axis=-1, keepdims=True) + EPS) * w


def _dot(a, b):
    return jnp.dot(a, b, preferred_element_type=F32)


def _dot_nt(a, b):
    return lax.dot_general(a, b, NT_DIMS, preferred_element_type=F32)


CAST_ROWS = 16


def _cast_rows(dst_ref, dst0, src_ref, src0, nrows):
    def body(r, carry):
        off = r * CAST_ROWS
        dst_ref[pl.ds(pl.multiple_of(dst0 + off, CAST_ROWS), CAST_ROWS), :] = (
            src_ref[pl.ds(pl.multiple_of(src0 + off, F32_SUBLANES), CAST_ROWS), :].astype(BF16))
        return carry

    lax.fori_loop(0, nrows // CAST_ROWS, body, 0)


def _in_proj_kernel(*refs, shift, regroup, n_cast, norm, side, lead_scale):
    n_in = 4 if norm else 3
    n_ref, w_ref, wnext_ref = refs[:3]
    cast_in = refs[n_in:n_in + n_cast]
    o_ref = refs[n_in + n_cast]
    n_out = n_in + n_cast + 1 + int(norm) + int(side)
    cast_out = refs[n_out:n_out + n_cast]
    wb_sc, *rest = refs[n_out + n_cast:]
    tn = wb_sc.shape[0]
    for src, dst in zip(cast_in, cast_out):
        dst[...] = src[...].astype(dst.dtype)

    @pl.when(pl.program_id(1) == 0)
    def _():
        _cast_rows(wb_sc, 0, w_ref, shift, tn - shift)
        if shift:
            _cast_rows(wb_sc, tn - shift, wnext_ref, 0, shift)
        if side:
            _cast_rows(rest[-1], 0, wnext_ref, 0, LANES)

    if norm:
        xn = _rms(n_ref[...], refs[3][...]).astype(BF16)
        refs[n_in + n_cast + 1][...] = xn
    else:
        xn = n_ref[...]
    if side:
        refs[n_out - 1][...] = _dot_nt(xn, rest[-1][...])
    res = _dot_nt(xn, wb_sc[...])
    if regroup:
        r_sc = rest[0]
        tm = res.shape[0]
        for c in range(tn // LANES):
            r_sc[c] = res[:, c * LANES:(c + 1) * LANES]
        for c in range(tn // LANES):
            for j in range(regroup):
                o_ref[c, :, j * LANES:(j + 1) * LANES] = (
                    r_sc[c, pl.ds(j, tm // regroup, stride=regroup), :].astype(o_ref.dtype))
    else:
        if lead_scale is not None:
            factor, blocks = lead_scale
            res = res * jnp.where(pl.program_id(0) < blocks, factor, 1.0)
        o_ref[...] = res.astype(o_ref.dtype)


def _in_proj(n, wt, src_block, n_blocks, out_dtype, *, shift=0, regroup=0, casts=(), norm_w=None,
             side_block=None, lead_scale=None, tm=1024, tn=1024, name):
    m, k = n.shape
    per = tn // LANES
    m_tiles = m // tm
    norm = norm_w is not None
    side = side_block is not None
    assert shift % CAST_ROWS == 0 and shift <= LANES and not (norm and n_blocks > 1)
    assert not side or (shift == 0 and n_blocks == 1)
    scratch = [pltpu.VMEM((tn, k), BF16)]
    row_tile = pl.BlockSpec((tm, k), lambda j, i: (i, 0))
    once = {"pipeline_mode": pl.Buffered(1)} if n_blocks == 1 else {}
    if regroup:
        assert n_blocks == 1
        out_shape = jax.ShapeDtypeStruct((per, m // regroup, regroup * LANES), out_dtype)
        out_spec = pl.BlockSpec((per, tm // regroup, regroup * LANES), lambda j, i: (0, i, 0))
        scratch.append(pltpu.VMEM((per, tm, LANES), F32))
    else:
        out_shape = jax.ShapeDtypeStruct((m, n_blocks * tn), out_dtype)
        out_spec = pl.BlockSpec((tm, tn), lambda j, i: (i, j))

    cast_steps = 1 << ((n_blocks * m_tiles).bit_length() - 1)
    cast_specs = []
    for a in casts:
        rows = a.shape[0] // cast_steps
        assert a.shape[0] % cast_steps == 0 and rows % CAST_ROWS == 0
        cast_specs.append(pl.BlockSpec(
            (rows, a.shape[1]), lambda j, i: (jnp.minimum(j * m_tiles + i, cast_steps - 1), 0)))

    if side:
        scratch.append(pltpu.VMEM((LANES, k), BF16))
    next_rows = (lambda j, i: (side_block, 0)) if side else (lambda j, i: ((src_block(j) + 1) * per, 0))
    return pl.pallas_call(
        functools.partial(_in_proj_kernel, shift=shift, regroup=regroup, n_cast=len(casts), norm=norm, side=side,
                          lead_scale=lead_scale),
        grid=(n_blocks, m_tiles),
        in_specs=[
            row_tile,
            pl.BlockSpec((tn, k), lambda j, i: (src_block(j), 0), **once),
            pl.BlockSpec((LANES, k), next_rows, **once),
            *([pl.BlockSpec((1, k), lambda j, i: (0, 0))] if norm else []),
            *cast_specs,
        ],
        out_specs=[out_spec, *([row_tile] if norm else []),
                   *([pl.BlockSpec((tm, LANES), lambda j, i: (i, 0))] if side else []), *cast_specs],
        out_shape=[out_shape, *([jax.ShapeDtypeStruct((m, k), BF16)] if norm else []),
                   *([jax.ShapeDtypeStruct((m, LANES), F32)] if side else []),
                   *[jax.ShapeDtypeStruct(a.shape, BF16) for a in casts]],
        scratch_shapes=scratch,
        compiler_params=_params(("arbitrary", "arbitrary"), 60 if norm else 56),
        name=name,
    )(n, wt, wt, *([norm_w.reshape(1, k)] if norm else []), *casts)


def _compress_kernel(k2_ref, pe_ref, w1_ref, w2_ref, o_ref):
    rows, half = k2_ref.shape
    k2 = k2_ref[...].astype(F32)
    a_lo = (k2 + pe_ref[:, :half]).astype(BF16)
    a_hi = (k2 + pe_ref[:, half:]).astype(BF16)
    lo = _dot(a_lo, w1_ref[:half, :])
    hi = _dot(a_hi, w1_ref[half:, :])
    h = lo + pltpu.roll(hi, rows - 1, axis=0)
    act = h * jax.nn.sigmoid(h)
    o_ref[...] = _dot(act.astype(BF16), w2_ref[...]).astype(o_ref.dtype)


def _compress(k2, first_row, m, pe, w1, w2, rows=512):
    half = k2.shape[1]
    hid = w1.shape[1]
    dout = w2.shape[1]
    first = first_row // rows
    return pl.pallas_call(
        _compress_kernel,
        grid=(m // rows,),
        in_specs=[
            pl.BlockSpec((rows, half), lambda i: (first + i, 0)),
            pl.BlockSpec((1, 2 * half), lambda i: (0, 0)),
            pl.BlockSpec((2 * half, hid), lambda i: (0, 0)),
            pl.BlockSpec((hid, dout), lambda i: (0, 0)),
        ],
        out_specs=pl.BlockSpec((rows, dout), lambda i: (i, 0)),
        out_shape=jax.ShapeDtypeStruct((m, dout), BF16),
        compiler_params=_params(("parallel",), 40),
        name="nsa_compress",
    )(k2, pe, w1, w2)


def _nsa_kernel(*refs, seq, tq, kblk, gp, hm):
    q_ref = refs[0]
    kc_refs = refs[1:1 + gp]
    vc_refs = refs[1 + gp:1 + 2 * gp]
    ks_ref, vs_ref, kw_ref, vw_ref, g_ref, e_ref, o_ref, s_sc, mx_sc, acc_sc, po_sc = refs[1 + 2 * gp:]
    i = pl.program_id(2)
    t0 = i * tq
    rep = NSA_REP
    n_cmp = (seq - CMP_LEN) // CMP_STRIDE + 1
    n_sel = seq // SEL_LEN
    topk = min(SEL_TOPK, n_sel)
    groups = range(gp)

    q = q_ref[...]
    units = range(rep // hm)
    rows = lambda gg, r: slice((gg * rep + r) * tq, (gg * rep + r + 1) * tq)
    urows = lambda gg, u: slice((gg * rep + u * hm) * tq, (gg * rep + (u + 1) * hm) * tq)
    part = lambda x, h: x[h * tq:(h + 1) * tq]
    head_q = lambda gg, r: q[:, (gg * rep + r) * NSA_DK:(gg * rep + r + 1) * NSA_DK]
    unit_q = {(gg, u): jnp.concatenate([head_q(gg, u * hm + h) for h in range(hm)], axis=0)
              for gg in groups for u in units}
    group_cols = lambda ref, gg: ref.at[:, gg * LANES:(gg + 1) * LANES]

    def masked(s, mask1):
        return jnp.concatenate([jnp.where(mask1, part(s, h), NEG) for h in range(hm)], axis=0)
    tcol = t0 + lax.broadcasted_iota(jnp.int32, (tq, 1), 0)
    per_group = rep * 3
    g_sig = jax.nn.sigmoid(g_ref[...])
    first_group = pl.program_id(1) * gp
    gs = [pltpu.roll(g_sig, lax.rem(LANES - (first_group + gg) * per_group, LANES), axis=1) for gg in groups]
    gate = lambda gg, r, branch: gs[gg][:, 3 * r + branch:3 * r + branch + 1]

    def exp_rows(sm):
        return jnp.exp2(sm - jnp.max(sm, axis=-1, keepdims=True))

    def with_ones(v):
        return jnp.concatenate([v, jnp.ones(v.shape, v.dtype)], axis=1)

    def normalised(ev):
        return ev[:, :NSA_DV] / ev[:, NSA_DV:]

    c_idx = lax.broadcasted_iota(jnp.int32, (tq, LANES), 1)
    mask_c = ((c_idx * CMP_STRIDE + (CMP_LEN - 1)) <= tcol) & (c_idx < n_cmp)
    mask_cf = jnp.where(mask_c, 1.0, 0.0)
    o_cmp, psum = {}, []
    for gg in groups:
        kc = kc_refs[gg][...]
        vc = vc_refs[gg][...]
        tot = None
        for u in units:
            e = exp_rows(masked(_dot_nt(unit_q[gg, u], kc), mask_c))
            p = e / jnp.sum(e, axis=-1, keepdims=True)
            p = jnp.concatenate([part(p, h) * mask_cf for h in range(hm)], axis=0)
            o_cmp[gg, u] = _dot(p.astype(BF16), vc)
            for h in range(hm):
                tot = part(p, h) if tot is None else tot + part(p, h)
        psum.append(tot)

    wlen = WIN + tq
    start = pl.multiple_of(jnp.maximum(i - WIN // tq, 0) * tq, tq)
    dlt = tcol - (start + lax.broadcasted_iota(jnp.int32, (tq, wlen), 1))
    mask_w = (dlt >= 0) & (dlt < WIN)
    for gg in groups:
        kw = group_cols(kw_ref, gg)[pl.ds(start, wlen), :]
        vw1 = with_ones(group_cols(vw_ref, gg)[pl.ds(start, wlen), :])
        for u in units:
            e = exp_rows(masked(_dot_nt(unit_q[gg, u], kw), mask_w))
            o_win = normalised(_dot(e.astype(BF16), vw1))
            for h in range(hm):
                r = u * hm + h
                po_sc[rows(gg, r), :] = (gate(gg, r, 0) * part(o_cmp[gg, u], h)
                                         + gate(gg, r, 2) * part(o_win, h))

    jo = lax.broadcasted_iota(jnp.int32, (n_sel, LANES), 0)
    co = lax.broadcasted_iota(jnp.int32, (n_sel, LANES), 1)
    ov_t = jnp.where((co * CMP_STRIDE < jo * SEL_LEN + SEL_LEN) & (co * CMP_STRIDE + CMP_LEN > jo * SEL_LEN)
                     & (co < n_cmp), 1.0, 0.0).astype(BF16)
    j_idx = lax.broadcasted_iota(jnp.int32, (n_sel, tq), 0)
    cur = lax.shift_right_logical(t0 + lax.broadcasted_iota(jnp.int32, (n_sel, tq), 1), int(np.log2(SEL_LEN)))
    forced = (j_idx == 0) | (j_idx == cur) | (j_idx == cur - 1)
    future = j_idx > cur
    sel = []
    for gg in groups:
        p_hi = psum[gg].astype(BF16)
        p_lo = (psum[gg] - p_hi.astype(F32)).astype(BF16)
        imp = _dot_nt(ov_t, p_hi) + _dot_nt(ov_t, p_lo)
        impm = jnp.where(forced, jnp.inf, jnp.where(future, -jnp.inf, imp))
        rank = jnp.zeros((n_sel, tq), F32)
        for ii in range(n_sel):
            row = impm[ii:ii + 1, :]
            beats = (row > impm) | ((row == impm) & (j_idx > ii))
            rank = rank + jnp.where(beats, 1.0, 0.0)
        sel_t = jnp.where(rank < topk, 1.0, 0.0)
        sel.append(jnp.concatenate([sel_t, jnp.zeros((LANES - n_sel, tq), F32)], axis=0).T.astype(BF16))

    n_chunks = lax.div(t0 + tq + (kblk - 1), kblk)

    def lane_fold_max(x):
        out = x[:, :LANES]
        for c in range(1, kblk // LANES):
            out = jnp.maximum(out, x[:, c * LANES:(c + 1) * LANES])
        return out

    def chunk_start(kb):
        return kb * kblk if isinstance(kb, int) else pl.multiple_of(kb * kblk, kblk)

    def score_chunk(kb):
        k0 = chunk_start(kb)
        causal = (k0 + lax.broadcasted_iota(jnp.int32, (tq, kblk), 1)) <= tcol
        folds = []
        for gg in groups:
            k = group_cols(ks_ref, gg)[pl.ds(k0, kblk), :]
            mask1 = (_dot(sel[gg], e_ref[kb]) > 0.5) & causal
            for u in units:
                sm_ = masked(_dot_nt(unit_q[gg, u], k), mask1)
                s_sc[kb, urows(gg, u), :] = sm_
                folds.append(lane_fold_max(sm_))
        return jnp.concatenate(folds, axis=0)

    def score_pass(kb, carry):
        mx_sc[...] = jnp.maximum(mx_sc[...], score_chunk(kb))
        return carry

    mx_sc[...] = score_chunk(0)
    lax.fori_loop(1, n_chunks, score_pass, 0)
    m_sel = jnp.max(mx_sc[...], axis=-1, keepdims=True)

    def value_chunk(kb):
        k0 = chunk_start(kb)
        pvs = []
        for gg in groups:
            v1 = with_ones(group_cols(vs_ref, gg)[pl.ds(k0, kblk), :])
            for u in units:
                pk = jnp.exp2(s_sc[kb, urows(gg, u), :] - m_sel[urows(gg, u)])
                pvs.append(_dot(pk.astype(BF16), v1))
        return jnp.concatenate(pvs, axis=0)

    def value_pass(kb, carry):
        acc_sc[...] += value_chunk(kb)
        return carry

    acc_sc[...] = value_chunk(0)
    lax.fori_loop(1, n_chunks, value_pass, 0)

    for gg in groups:
        for r in range(rep):
            o = po_sc[rows(gg, r), :] + gate(gg, r, 1) * normalised(acc_sc[rows(gg, r), :])
            col = (gg * rep + r) * NSA_DV
            o_ref[:, col:col + NSA_DV] = o.astype(o_ref.dtype)


def _nsa_attention(za, zg, kc, vc, batch, seq, tq=128, kblk=512, gp=4):
    assert seq % kblk == 0 and seq >= WIN + tq and WIN % tq == 0 and NSA_GROUPS % gp == 0
    nq = seq // tq
    gw = gp * NSA_REP * NSA_DK
    streams = gp * NSA_REP * tq
    hm = max(1, MXU_ROWS // tq)
    assert NSA_REP % hm == 0
    kern = functools.partial(_nsa_kernel, seq=seq, tq=tq, kblk=kblk, gp=gp, hm=hm)
    key = np.arange(seq).reshape(seq // kblk, 1, kblk)
    expand = jnp.asarray(key // SEL_LEN == np.arange(LANES).reshape(1, LANES, 1), BF16)

    def kv_spec(off):
        return pl.BlockSpec((seq, gp * LANES), lambda b, g, i: (b, off // (gp * LANES) + g))

    def cmp_spec(gg):
        return pl.BlockSpec((LANES, NSA_DK), lambda b, g, i: ((g * gp + gg) * batch + b, 0))

    return pl.pallas_call(
        kern,
        grid=(batch, NSA_GROUPS // gp, nq),
        in_specs=[
            pl.BlockSpec((tq, gw), lambda b, g, i: (b * nq + i, A_QN // gw + g)),
            *[cmp_spec(gg) for gg in range(gp)],
            *[cmp_spec(gg) for gg in range(gp)],
            kv_spec(A_KS), kv_spec(A_VS), kv_spec(A_KW), kv_spec(A_VW),
            pl.BlockSpec((tq, LANES), lambda b, g, i: (b * nq + i, 0)),
            pl.BlockSpec(expand.shape, lambda b, g, i: (0, 0, 0)),
        ],
        out_specs=pl.BlockSpec((tq, gw), lambda b, g, i: (b * nq + i, g)),
        out_shape=jax.ShapeDtypeStruct((batch * seq, NSA_HEADS * NSA_DV), BF16),
        scratch_shapes=[
            pltpu.VMEM((seq // kblk, streams, kblk), F32),
            pltpu.VMEM((streams, LANES), F32),
            pltpu.VMEM((streams, 2 * NSA_DV), F32),
            pltpu.VMEM((streams, NSA_DV), F32),
        ],
        compiler_params=_params(("parallel", "parallel", "arbitrary"), 56),
        name="nsa_attention",
    )(za, *([kc] * gp), *([vc] * gp), za, za, za, za, zg, expand)


def _retention_kernel(q_ref, k_ref, v_ref, g_ref, cos_ref, sin_e_ref, sin_o_ref, dec_ref, wq_ref, wk_ref,
                      gc_ref, gn_ref, o_ref, st_ref):
    @pl.when(pl.program_id(1) == 0)
    def _():
        st_ref[...] = jnp.zeros(st_ref.shape, F32)

    c = RET_CHUNK
    for h in range(RET_HEADS):
        st = st_ref[h]
        wq = wq_ref[h]
        wq2 = jnp.concatenate([wq, wq], axis=1)
        cols = slice(h * RET_DV, (h + 1) * RET_DV)
        for sub in range(q_ref.shape[0] // c):
            rows = slice(sub * c, (sub + 1) * c)
            cos = cos_ref[rows, :]
            sin_e = sin_e_ref[rows, :]
            sin_o = sin_o_ref[rows, :]

            def rotate(x):
                return x * cos + pltpu.roll(x, RET_DK - 1, axis=1) * sin_e + pltpu.roll(x, 1, axis=1) * sin_o

            qf = rotate(q_ref[rows, h * RET_DK:(h + 1) * RET_DK])
            kf = rotate(k_ref[rows, h * RET_DK:(h + 1) * RET_DK]) * (RET_DK ** -0.5)
            qb = qf.astype(BF16)
            v = v_ref[rows, cols].astype(BF16)
            s = _dot_nt(qb, kf.astype(BF16)) * dec_ref[h]
            o = _dot(s.astype(BF16), v) + _dot(qb, st.astype(BF16)) * wq2
            st = st * gc_ref[h] + lax.dot_general((kf * wk_ref[h]).astype(BF16), v, TN_DIMS,
                                                  preferred_element_type=F32)
            mu = jnp.mean(o, axis=-1, keepdims=True)
            d = o - mu
            var = jnp.mean(d * d, axis=-1, keepdims=True)
            on = d * lax.rsqrt(var + EPS) * gn_ref[:, cols]
            gr = g_ref[rows, cols]
            o_ref[rows, cols] = (gr * jax.nn.sigmoid(gr) * on).astype(o_ref.dtype)
        st_ref[h] = st


def _retention(zb, gn_w, batch, seq, chunks_per_step=2):
    c = RET_CHUNK
    rows = chunks_per_step * c
    nc = seq // rows
    hq = RET_HEADS * RET_DK
    hv = RET_HEADS * RET_DV
    f32 = np.float32
    inv = f32(ROPE_BASE) ** (-np.arange(0, RET_DK, 2, dtype=f32) / f32(RET_DK))
    ang = np.arange(seq, dtype=f32)[:, None] * inv[None, :]
    zero = np.zeros_like(ang)
    pairs = lambda even, odd: np.stack([even, odd], axis=-1).reshape(seq, RET_DK)
    cos = pairs(np.cos(ang), np.cos(ang))
    sin_e = pairs(-np.sin(ang), zero)
    sin_o = pairs(zero, np.sin(ang))
    log_g = np.log1p(-np.exp2(f32(-5.0) - np.arange(RET_HEADS, dtype=f32)))
    idx = np.arange(c, dtype=f32)
    rel = idx[:, None] - idx[None, :]
    decay = np.where(rel >= 0, np.exp(log_g[:, None, None] * np.maximum(rel, f32(0.0))), f32(0.0)).astype(f32)
    lanes = lambda a: np.ascontiguousarray(np.broadcast_to(a[:, :, None], (RET_HEADS, c, RET_DK)), dtype=f32)
    w_k = lanes(np.exp(log_g[:, None] * (f32(c - 1) - idx)[None, :]))
    w_q = lanes(np.exp(log_g[:, None] * (idx + f32(1.0))[None, :]))
    g_chunk = np.broadcast_to(np.exp(log_g * f32(c))[:, None, None], (RET_HEADS, 1, RET_DV)).astype(f32)

    row = lambda b, n: b * nc + n
    return pl.pallas_call(
        _retention_kernel,
        grid=(batch, nc),
        in_specs=[
            pl.BlockSpec((rows, hq), lambda b, n: (row(b, n), B_QR // hq)),
            pl.BlockSpec((rows, hq), lambda b, n: (row(b, n), B_KR // hq)),
            pl.BlockSpec((rows, hv), lambda b, n: (row(b, n), B_VR // hv)),
            pl.BlockSpec((rows, hv), lambda b, n: (row(b, n), B_GR // hv)),
            pl.BlockSpec((rows, RET_DK), lambda b, n: (n, 0)),
            pl.BlockSpec((rows, RET_DK), lambda b, n: (n, 0)),
            pl.BlockSpec((rows, RET_DK), lambda b, n: (n, 0)),
            pl.BlockSpec((RET_HEADS, c, c), lambda b, n: (0, 0, 0)),
            pl.BlockSpec((RET_HEADS, c, RET_DK), lambda b, n: (0, 0, 0)),
            pl.BlockSpec((RET_HEADS, c, RET_DK), lambda b, n: (0, 0, 0)),
            pl.BlockSpec((RET_HEADS, 1, RET_DV), lambda b, n: (0, 0, 0)),
            pl.BlockSpec((1, hv), lambda b, n: (0, 0)),
        ],
        out_specs=pl.BlockSpec((rows, hv), lambda b, n: (row(b, n), 0)),
        out_shape=jax.ShapeDtypeStruct((batch * seq, hv), BF16),
        scratch_shapes=[pltpu.VMEM((RET_HEADS, RET_DK, RET_DV), F32)],
        compiler_params=_params(("parallel", "arbitrary"), 40),
        name="retention",
    )(zb, zb, zb, zb, cos, sin_e, sin_o, decay, w_q, w_k, g_chunk, gn_w.reshape(1, hv))


def _merge_kernel(on_ref, or_ref, ga_ref, gb_ref, x_ref, wa_ref, wb_ref, wo_ref, h_ref):
    a = _dot(on_ref[...], wa_ref[...])
    b = _dot(or_ref[...], wb_ref[...])
    merged = jax.nn.sigmoid(ga_ref[...]) * a + jax.nn.sigmoid(gb_ref[...]) * b
    h_ref[...] = x_ref[...] + _dot(merged.astype(BF16), wo_ref[...])


def _merge(o_nsa, o_ret, zb, x, w_a, w_b, w_out, tm=256):
    m, d = x.shape
    resident = lambda shape: pl.BlockSpec(shape, lambda i: (0, 0), pipeline_mode=pl.Buffered(1))
    return pl.pallas_call(
        _merge_kernel,
        grid=(m // tm,),
        in_specs=[
            pl.BlockSpec((tm, d), lambda i: (i, 0)),
            pl.BlockSpec((tm, d), lambda i: (i, 0)),
            pl.BlockSpec((tm, d), lambda i: (i, B_GA // d)),
            pl.BlockSpec((tm, d), lambda i: (i, B_GB // d)),
            pl.BlockSpec((tm, d), lambda i: (i, 0)),
            resident(w_a.shape), resident(w_b.shape), resident(w_out.shape),
        ],
        out_specs=pl.BlockSpec((tm, d), lambda i: (i, 0)),
        out_shape=jax.ShapeDtypeStruct((m, d), F32),
        compiler_params=_params(("parallel",), 56),
        name="merge_out_proj",
    )(o_nsa, o_ret, zb, zb, x, w_a, w_b, w_out)


def _mem_kv_kernel(m_ref, nw_ref, wk_ref, wv_ref, k_ref, v_ref):
    mn = _rms(m_ref[...], nw_ref[...]).astype(BF16)
    k_ref[...] = _dot(mn, wk_ref[...]).astype(k_ref.dtype)
    v_ref[...] = _dot(mn, wv_ref[...]).astype(v_ref.dtype)


def _mem_kv(mem2, nw, wk, wv, tm=256):
    m, d = mem2.shape
    n = wk.shape[1]
    out = jax.ShapeDtypeStruct((m, n), BF16)
    return pl.pallas_call(
        _mem_kv_kernel,
        grid=(m // tm,),
        in_specs=[
            pl.BlockSpec((tm, d), lambda i: (i, 0)),
            pl.BlockSpec((1, d), lambda i: (0, 0)),
            pl.BlockSpec((d, n), lambda i: (0, 0)),
            pl.BlockSpec((d, n), lambda i: (0, 0)),
        ],
        out_specs=[pl.BlockSpec((tm, n), lambda i: (i, 0))] * 2,
        out_shape=[out, out],
        compiler_params=_params(("parallel",), 40),
        name="mem_kv_proj",
    )(mem2, nw.reshape(1, d), wk, wv)


def _cross_kernel(h_ref, xw_ref, mw_ref, wq_ref, kx_ref, vx_ref, wo_ref, h2_ref, nm_ref):
    h = h_ref[...]
    nx = _rms(h, xw_ref[...]).astype(BF16)
    qx = _dot(nx, wq_ref[...]).astype(BF16)
    outs = []
    for hh in range(X_HEADS):
        cols = slice(hh * X_DH, (hh + 1) * X_DH)
        s = _dot_nt(qx[:, cols], kx_ref[:, cols]) * (X_DH ** -0.5)
        e = jnp.exp(s - jnp.max(s, axis=-1, keepdims=True))
        p = e / jnp.sum(e, axis=-1, keepdims=True)
        outs.append(_dot(p.astype(BF16), vx_ref[:, cols]))
    ox = jnp.concatenate(outs, axis=-1).astype(BF16)
    h2 = h + _dot(ox, wo_ref[...])
    h2_ref[...] = h2
    nm_ref[...] = _rms(h2, mw_ref[...]).astype(nm_ref.dtype)


def _cross_attention(h1, x_norm_w, mlp_norm_w, wq, kx, vx, wo, seq, tm=512):
    m, d = h1.shape
    n = wq.shape[1]
    per_batch = seq // tm
    vec = lambda: pl.BlockSpec((1, d), lambda i: (0, 0))
    return pl.pallas_call(
        _cross_kernel,
        grid=(m // tm,),
        in_specs=[
            pl.BlockSpec((tm, d), lambda i: (i, 0)),
            vec(), vec(),
            pl.BlockSpec((d, n), lambda i: (0, 0)),
            pl.BlockSpec((MEM_LEN, n), lambda i: (i // per_batch, 0)),
            pl.BlockSpec((MEM_LEN, n), lambda i: (i // per_batch, 0)),
            pl.BlockSpec((n, d), lambda i: (0, 0)),
        ],
        out_specs=[pl.BlockSpec((tm, d), lambda i: (i, 0))] * 2,
        out_shape=[jax.ShapeDtypeStruct((m, d), F32), jax.ShapeDtypeStruct((m, d), BF16)],
        compiler_params=_params(("parallel",), 40),
        name="cross_attention",
    )(h1, x_norm_w.reshape(1, d), mlp_norm_w.reshape(1, d), wq, kx, vx, wo)


def _mlp_kernel(nm_ref, wu_ref, wd_ref, h_ref, fw_ref, o_ref):
    j = pl.program_id(1)

    @pl.when(j == 0)
    def _():
        o_ref[...] = jnp.zeros(o_ref.shape, F32)

    u = jnp.maximum(_dot(nm_ref[...], wu_ref[...]), 0.0)
    o_ref[...] += _dot((u * u).astype(BF16), wd_ref[...])

    @pl.when(j == pl.num_programs(1) - 1)
    def _():
        o_ref[...] = _rms(h_ref[...] + o_ref[...], fw_ref[...])


def _mlp(nm, w_up, w_down, h2, final_w, tm=512, tf=2048):
    m, d = nm.shape
    f = w_up.shape[1]
    return pl.pallas_call(
        _mlp_kernel,
        grid=(m // tm, f // tf),
        in_specs=[
            pl.BlockSpec((tm, d), lambda i, j: (i, 0)),
            pl.BlockSpec((d, tf), lambda i, j: (0, j)),
            pl.BlockSpec((tf, d), lambda i, j: (j, 0)),
            pl.BlockSpec((tm, d), lambda i, j: (i, 0)),
            pl.BlockSpec((1, d), lambda i, j: (0, 0)),
        ],
        out_specs=pl.BlockSpec((tm, d), lambda i, j: (i, 0)),
        out_shape=jax.ShapeDtypeStruct((m, d), F32),
        compiler_params=_params(("parallel", "arbitrary"), 60),
        name="mlp_final_norm",
    )(nm, w_up, w_down, h2, final_w.reshape(1, d))


def _layer(h, mem, attn_norm_w, w_in, cmp_pe_k, cmp_w1_k, cmp_w2_k, cmp_pe_v, cmp_w1_v, cmp_w2_v,
           w_a, ret_gn_w, w_b, w_out, x_norm_w, mem_norm_w, wq_x, wk_x, wv_x, wo_x,
           mlp_norm_w, w_up, w_down, out_norm_w, batch, seq):
    blk = W_IN_BLOCK
    kv_block = W_IN_KV // blk
    gate_shift = NSA_HEADS * 3

    wt = w_in.T
    skip_kv = lambda j: j + (j >= kv_block).astype(jnp.int32)
    kv, n, zg, wq_b, wk_b, wv_b, wo_b, c2k_b, c2v_b = _in_proj(
        h, wt, lambda j: kv_block, 1, BF16, regroup=CMP_STRIDE, norm_w=attn_norm_w,
        side_block=W_IN_NSA_GATE // LANES, casts=(wq_x, wk_x, wv_x, wo_x, cmp_w2_k, cmp_w2_v), name="in_proj_kv")
    za, wa_b, wb_b, wout_b, c1k_b, c1v_b = _in_proj(
        n, wt, skip_kv, 4, BF16, casts=(w_a, w_b, w_out, cmp_w1_k, cmp_w1_v),
        lead_scale=(NSA_Q_SCALE, NSA_HEADS * NSA_DK // blk), name="in_proj_a")
    zb, wup_b, wdown_b = _in_proj(
        n, wt, lambda j: W_IN_NSA_GATE // blk + j, B_WIDTH // blk, F32, shift=gate_shift,
        casts=(w_up, w_down), name="in_proj_b")

    assert seq // CMP_STRIDE == LANES
    rows_kv = NSA_GROUPS * batch * LANES
    k2 = kv.reshape(2 * rows_kv, CMP_STRIDE * NSA_DK)
    kc = _compress(k2, 0, rows_kv, cmp_pe_k.reshape(1, -1), c1k_b, c2k_b)
    vc = _compress(k2, rows_kv, rows_kv, cmp_pe_v.reshape(1, -1), c1v_b, c2v_b)

    o_nsa = _nsa_attention(za, zg, kc, vc, batch, seq)
    o_ret = _retention(zb, ret_gn_w, batch, seq)
    h1 = _merge(o_nsa, o_ret, zb, h, wa_b, wb_b, wout_b)

    kx, vx = _mem_kv(mem.reshape(batch * MEM_LEN, D_MODEL), mem_norm_w, wk_b, wv_b)
    h2, nm = _cross_attention(h1, x_norm_w, mlp_norm_w, wq_b, kx, vx, wo_b, seq)
    return _mlp(nm, wup_b, wdown_b, h2, out_norm_w)


def kernel(x, mem, attn_norm_w, w_in, cmp_pe_k, cmp_w1_k, cmp_w2_k, cmp_pe_v, cmp_w1_v, cmp_w2_v, w_a, ret_gn_w,
           w_b, w_out, x_norm_w, mem_norm_w, wq_x, wk_x, wv_x, wo_x, mlp_norm_w, w_up, w_down, final_norm_w):
    batch, seq, d = x.shape
    depth = w_in.shape[0]
    assert depth == 1
    h = x.reshape(batch * seq, d)
    out = _layer(h, mem, attn_norm_w[0], w_in[0], cmp_pe_k[0], cmp_w1_k[0], cmp_w2_k[0],
                 cmp_pe_v[0], cmp_w1_v[0], cmp_w2_v[0], w_a[0], ret_gn_w[0], w_b[0], w_out[0],
                 x_norm_w[0], mem_norm_w[0], wq_x[0], wk_x[0], wv_x[0], wo_x[0],
                 mlp_norm_w[0], w_up[0], w_down[0], final_norm_w, batch, seq)
    return out.reshape(batch, seq, d)
```

```python
import functools

import jax
import jax.numpy as jnp
import numpy as np
from jax import lax
from jax.experimental import pallas as pl
from jax.experimental.pallas import tpu as pltpu

F32 = jnp.float32
BF16 = jnp.bfloat16

D_MODEL = 2048
MEM_LEN = 256
NSA_HEADS = 16
NSA_GROUPS = 4
NSA_REP = NSA_HEADS // NSA_GROUPS
NSA_DK = 128
NSA_DV = 128
CMP_LEN = 32
CMP_STRIDE = 16
CMP_HIDDEN = 1024
SEL_LEN = 64
SEL_TOPK = 16
WIN = 512
RET_HEADS = 8
RET_DK = 128
RET_DV = 256
RET_CHUNK = 128
ROPE_BASE = 10000.0
X_HEADS = 4
X_DH = 128
D_FF = 4 * D_MODEL
EPS = 1e-6
NEG = -1e30
LOG2E = 1.4426950408889634
NSA_Q_SCALE = NSA_DK ** -0.5 * LOG2E

LANES = 128
F32_SUBLANES = 8
MXU_ROWS = 256
V7X_VMEM_MIB = 64

W_IN_BLOCK = 1024
W_IN_KV = 2048
W_IN_NSA_GATE = 5120
A_QN = 0
A_KS = 2048
A_VS = 2560
A_KW = 3072
A_VW = 3584
B_QR = 0
B_KR = 1024
B_VR = 2048
B_GR = 4096
B_GA = 6144
B_GB = 8192
B_WIDTH = 10240

NT_DIMS = (((1,), (1,)), ((), ()))
TN_DIMS = (((0,), (0,)), ((), ()))


def _params(sem, vmem_mib):
    assert vmem_mib < V7X_VMEM_MIB
    return pltpu.CompilerParams(dimension_semantics=sem, vmem_limit_bytes=vmem_mib * 1024 * 1024)


def _rms(x, w):
    return x * lax.rsqrt(jnp.mean(x * x, axis=-1, keepdims=True) + EPS) * w


def _dot(a, b):
    return jnp.dot(a, b, preferred_element_type=F32)


def _dot_nt(a, b):
    return lax.dot_general(a, b, NT_DIMS, preferred_element_type=F32)


CAST_ROWS = 16


def _cast_rows(dst_ref, dst0, src_ref, src0, nrows):
    def body(r, carry):
        off = r * CAST_ROWS
        dst_ref[pl.ds(pl.multiple_of(dst0 + off, CAST_ROWS), CAST_ROWS), :] = (
            src_ref[pl.ds(pl.multiple_of(src0 + off, F32_SUBLANES), CAST_ROWS), :].astype(BF16))
        return carry

    lax.fori_loop(0, nrows // CAST_ROWS, body, 0)


def _in_proj_kernel(*refs, shift, regroup, n_cast, norm, side, lead_scale):
    n_in = 4 if norm else 3
    n_ref, w_ref, wnext_ref = refs[:3]
    cast_in = refs[n_in:n_in + n_cast]
    o_ref = refs[n_in + n_cast]
    n_out = n_in + n_cast + 1 + int(norm) + int(side)
    cast_out = refs[n_out:n_out + n_cast]
    wb_sc, *rest = refs[n_out + n_cast:]
    tn = wb_sc.shape[0]
    for src, dst in zip(cast_in, cast_out):
        dst[...] = src[...].astype(dst.dtype)

    @pl.when(pl.program_id(1) == 0)
    def _():
        _cast_rows(wb_sc, 0, w_ref, shift, tn - shift)
        if shift:
            _cast_rows(wb_sc, tn - shift, wnext_ref, 0, shift)
        if side:
            _cast_rows(rest[-1], 0, wnext_ref, 0, LANES)

    if norm:
        xn = _rms(n_ref[...], refs[3][...]).astype(BF16)
        refs[n_in + n_cast + 1][...] = xn
    else:
        xn = n_ref[...]
    if side:
        refs[n_out - 1][...] = _dot_nt(xn, rest[-1][...])
    res = _dot_nt(xn, wb_sc[...])
    if regroup:
        r_sc = rest[0]
        tm = res.shape[0]
        for c in range(tn // LANES):
            r_sc[c] = res[:, c * LANES:(c + 1) * LANES]
        for c in range(tn // LANES):
            for j in range(regroup):
                o_ref[c, :, j * LANES:(j + 1) * LANES] = (
                    r_sc[c, pl.ds(j, tm // regroup, stride=regroup), :].astype(o_ref.dtype))
    else:
        if lead_scale is not None:
            factor, blocks = lead_scale
            res = res * jnp.where(pl.program_id(0) < blocks, factor, 1.0)
        o_ref[...] = res.astype(o_ref.dtype)


def _in_proj(n, wt, src_block, n_blocks, out_dtype, *, shift=0, regroup=0, casts=(), norm_w=None,
             side_block=None, lead_scale=None, tm=1024, tn=1024, name):
    m, k = n.shape
    per = tn // LANES
    m_tiles = m // tm
    norm = norm_w is not None
    side = side_block is not None
    assert shift % CAST_ROWS == 0 and shift <= LANES and not (norm and n_blocks > 1)
    assert not side or (shift == 0 and n_blocks == 1)
    scratch = [pltpu.VMEM((tn, k), BF16)]
    row_tile = pl.BlockSpec((tm, k), lambda j, i: (i, 0))
    once = {"pipeline_mode": pl.Buffered(1)} if n_blocks == 1 else {}
    if regroup:
        assert n_blocks == 1
        out_shape = jax.ShapeDtypeStruct((per, m // regroup, regroup * LANES), out_dtype)
        out_spec = pl.BlockSpec((per, tm // regroup, regroup * LANES), lambda j, i: (0, i, 0))
        scratch.append(pltpu.VMEM((per, tm, LANES), F32))
    else:
        out_shape = jax.ShapeDtypeStruct((m, n_blocks * tn), out_dtype)
        out_spec = pl.BlockSpec((tm, tn), lambda j, i: (i, j))

    cast_steps = 1 << ((n_blocks * m_tiles).bit_length() - 1)
    cast_specs = []
    for a in casts:
        rows = a.shape[0] // cast_steps
        assert a.shape[0] % cast_steps == 0 and rows % CAST_ROWS == 0
        cast_specs.append(pl.BlockSpec(
            (rows, a.shape[1]), lambda j, i: (jnp.minimum(j * m_tiles + i, cast_steps - 1), 0)))

    if side:
        scratch.append(pltpu.VMEM((LANES, k), BF16))
    next_rows = (lambda j, i: (side_block, 0)) if side else (lambda j, i: ((src_block(j) + 1) * per, 0))
    return pl.pallas_call(
        functools.partial(_in_proj_kernel, shift=shift, regroup=regroup, n_cast=len(casts), norm=norm, side=side,
                          lead_scale=lead_scale),
        grid=(n_blocks, m_tiles),
        in_specs=[
            row_tile,
            pl.BlockSpec((tn, k), lambda j, i: (src_block(j), 0), **once),
            pl.BlockSpec((LANES, k), next_rows, **once),
            *([pl.BlockSpec((1, k), lambda j, i: (0, 0))] if norm else []),
            *cast_specs,
        ],
        out_specs=[out_spec, *([row_tile] if norm else []),
                   *([pl.BlockSpec((tm, LANES), lambda j, i: (i, 0))] if side else []), *cast_specs],
        out_shape=[out_shape, *([jax.ShapeDtypeStruct((m, k), BF16)] if norm else []),
                   *([jax.ShapeDtypeStruct((m, LANES), F32)] if side else []),
                   *[jax.ShapeDtypeStruct(a.shape, BF16) for a in casts]],
        scratch_shapes=scratch,
        compiler_params=_params(("arbitrary", "arbitrary"), 60 if norm else 56),
        name=name,
    )(n, wt, wt, *([norm_w.reshape(1, k)] if norm else []), *casts)


def _compress_kernel(k2_ref, pe_ref, w1_ref, w2_ref, o_ref):
    rows, half = k2_ref.shape
    k2 = k2_ref[...].astype(F32)
    a_lo = (k2 + pe_ref[:, :half]).astype(BF16)
    a_hi = (k2 + pe_ref[:, half:]).astype(BF16)
    lo = _dot(a_lo, w1_ref[:half, :])
    hi = _dot(a_hi, w1_ref[half:, :])
    h = lo + pltpu.roll(hi, rows - 1, axis=0)
    act = h * jax.nn.sigmoid(h)
    o_ref[...] = _dot(act.astype(BF16), w2_ref[...]).astype(o_ref.dtype)


def _compress(k2, first_row, m, pe, w1, w2, rows=512):
    half = k2.shape[1]
    hid = w1.shape[1]
    dout = w2.shape[1]
    first = first_row // rows
    return pl.pallas_call(
        _compress_kernel,
        grid=(m // rows,),
        in_specs=[
            pl.BlockSpec((rows, half), lambda i: (first + i, 0)),
            pl.BlockSpec((1, 2 * half), lambda i: (0, 0)),
            pl.BlockSpec((2 * half, hid), lambda i: (0, 0)),
            pl.BlockSpec((hid, dout), lambda i: (0, 0)),
        ],
        out_specs=pl.BlockSpec((rows, dout), lambda i: (i, 0)),
        out_shape=jax.ShapeDtypeStruct((m, dout), BF16),
        compiler_params=_params(("parallel",), 40),
        name="nsa_compress",
    )(k2, pe, w1, w2)


def _nsa_kernel(*refs, seq, tq, kblk, gp, hm):
    q_ref = refs[0]
    kc_refs = refs[1:1 + gp]
    vc_refs = refs[1 + gp:1 + 2 * gp]
    ks_ref, vs_ref, kw_ref, vw_ref, g_ref, e_ref, o_ref, s_sc, mx_sc, acc_sc, po_sc = refs[1 + 2 * gp:]
    i = pl.program_id(2)
    t0 = i * tq
    rep = NSA_REP
    n_cmp = (seq - CMP_LEN) // CMP_STRIDE + 1
    n_sel = seq // SEL_LEN
    topk = min(SEL_TOPK, n_sel)
    groups = range(gp)

    q = q_ref[...]
    units = range(rep // hm)
    rows = lambda gg, r: slice((gg * rep + r) * tq, (gg * rep + r + 1) * tq)
    urows = lambda gg, u: slice((gg * rep + u * hm) * tq, (gg * rep + (u + 1) * hm) * tq)
    part = lambda x, h: x[h * tq:(h + 1) * tq]
    head_q = lambda gg, r: q[:, (gg * rep + r) * NSA_DK:(gg * rep + r + 1) * NSA_DK]
    unit_q = {(gg, u): jnp.concatenate([head_q(gg, u * hm + h) for h in range(hm)], axis=0)
              for gg in groups for u in units}
    group_cols = lambda ref, gg: ref.at[:, gg * LANES:(gg + 1) * LANES]

    def masked(s, mask1):
        return jnp.concatenate([jnp.where(mask1, part(s, h), NEG) for h in range(hm)], axis=0)
    tcol = t0 + lax.broadcasted_iota(jnp.int32, (tq, 1), 0)
    per_group = rep * 3
    g_sig = jax.nn.sigmoid(g_ref[...])
    first_group = pl.program_id(1) * gp
    gs = [pltpu.roll(g_sig, lax.rem(LANES - (first_group + gg) * per_group, LANES), axis=1) for gg in groups]
    gate = lambda gg, r, branch: gs[gg][:, 3 * r + branch:3 * r + branch + 1]

    def exp_rows(sm):
        return jnp.exp2(sm - jnp.max(sm, axis=-1, keepdims=True))

    def with_ones(v):
        return jnp.concatenate([v, jnp.ones(v.shape, v.dtype)], axis=1)

    def normalised(ev):
        return ev[:, :NSA_DV] / ev[:, NSA_DV:]

    c_idx = lax.broadcasted_iota(jnp.int32, (tq, LANES), 1)
    mask_c = ((c_idx * CMP_STRIDE + (CMP_LEN - 1)) <= tcol) & (c_idx < n_cmp)
    mask_cf = jnp.where(mask_c, 1.0, 0.0)
    o_cmp, psum = {}, []
    for gg in groups:
        kc = kc_refs[gg][...]
        vc = vc_refs[gg][...]
        tot = None
        for u in units:
            e = exp_rows(masked(_dot_nt(unit_q[gg, u], kc), mask_c))
            p = e / jnp.sum(e, axis=-1, keepdims=True)
            p = jnp.concatenate([part(p, h) * mask_cf for h in range(hm)], axis=0)
            o_cmp[gg, u] = _dot(p.astype(BF16), vc)
            for h in range(hm):
                tot = part(p, h) if tot is None else tot + part(p, h)
        psum.append(tot)

    wlen = WIN + tq
    start = pl.multiple_of(jnp.maximum(i - WIN // tq, 0) * tq, tq)
    dlt = tcol - (start + lax.broadcasted_iota(jnp.int32, (tq, wlen), 1))
    mask_w = (dlt >= 0) & (dlt < WIN)
    for gg in groups:
        kw = group_cols(kw_ref, gg)[pl.ds(start, wlen), :]
        vw1 = with_ones(group_cols(vw_ref, gg)[pl.ds(start, wlen), :])
        for u in units:
            e = exp_rows(masked(_dot_nt(unit_q[gg, u], kw), mask_w))
            o_win = normalised(_dot(e.astype(BF16), vw1))
            for h in range(hm):
                r = u * hm + h
                po_sc[rows(gg, r), :] = (gate(gg, r, 0) * part(o_cmp[gg, u], h)
                                         + gate(gg, r, 2) * part(o_win, h))

    jo = lax.broadcasted_iota(jnp.int32, (n_sel, LANES), 0)
    co = lax.broadcasted_iota(jnp.int32, (n_sel, LANES), 1)
    ov_t = jnp.where((co * CMP_STRIDE < jo * SEL_LEN + SEL_LEN) & (co * CMP_STRIDE + CMP_LEN > jo * SEL_LEN)
                     & (co < n_cmp), 1.0, 0.0).astype(BF16)
    j_idx = lax.broadcasted_iota(jnp.int32, (n_sel, tq), 0)
    cur = lax.shift_right_logical(t0 + lax.broadcasted_iota(jnp.int32, (n_sel, tq), 1), int(np.log2(SEL_LEN)))
    forced = (j_idx == 0) | (j_idx == cur) | (j_idx == cur - 1)
    future = j_idx > cur
    q_bias = {}
    for gg in groups:
        p_hi = psum[gg].astype(BF16)
        p_lo = (psum[gg] - p_hi.astype(F32)).astype(BF16)
        imp = _dot_nt(ov_t, p_hi) + _dot_nt(ov_t, p_lo)
        impm = jnp.where(forced, jnp.inf, jnp.where(future, -jnp.inf, imp))
        rank = jnp.zeros((n_sel, tq), F32)
        for ii in range(n_sel):
            row = impm[ii:ii + 1, :]
            beats = (row > impm) | ((row == impm) & (j_idx > ii))
            rank = rank + jnp.where(beats, 1.0, 0.0)
        bias_t = jnp.where(rank < topk, 0.0, NEG)
        bias = jnp.concatenate([bias_t, jnp.zeros((LANES - n_sel, tq), F32)], axis=0).T.astype(BF16)
        for u in units:
            q_bias[gg, u] = jnp.concatenate([unit_q[gg, u], jnp.concatenate([bias] * hm, axis=0)], axis=1)

    n_chunks = lax.div(t0 + tq + (kblk - 1), kblk)

    def lane_fold_max(x):
        out = x[:, :LANES]
        for c in range(1, kblk // LANES):
            out = jnp.maximum(out, x[:, c * LANES:(c + 1) * LANES])
        return out

    def chunk_start(kb):
        return kb * kblk if isinstance(kb, int) else pl.multiple_of(kb * kblk, kblk)

    def score_chunk(kb):
        k0 = chunk_start(kb)
        causal = (k0 + lax.broadcasted_iota(jnp.int32, (tq, kblk), 1)) <= tcol
        folds = []
        for gg in groups:
            k = jnp.concatenate([group_cols(ks_ref, gg)[pl.ds(k0, kblk), :], e_ref[kb]], axis=1)
            for u in units:
                sm_ = masked(_dot_nt(q_bias[gg, u], k), causal)
                s_sc[kb, urows(gg, u), :] = sm_
                folds.append(lane_fold_max(sm_))
        return jnp.concatenate(folds, axis=0)

    def score_pass(kb, carry):
        mx_sc[...] = jnp.maximum(mx_sc[...], score_chunk(kb))
        return carry

    mx_sc[...] = score_chunk(0)
    lax.fori_loop(1, n_chunks, score_pass, 0)
    m_sel = jnp.max(mx_sc[...], axis=-1, keepdims=True)

    def value_chunk(kb):
        k0 = chunk_start(kb)
        pvs = []
        for gg in groups:
            v1 = with_ones(group_cols(vs_ref, gg)[pl.ds(k0, kblk), :])
            for u in units:
                pk = jnp.exp2(s_sc[kb, urows(gg, u), :] - m_sel[urows(gg, u)])
                pvs.append(_dot(pk.astype(BF16), v1))
        return jnp.concatenate(pvs, axis=0)

    def value_pass(kb, carry):
        acc_sc[...] += value_chunk(kb)
        return carry

    acc_sc[...] = value_chunk(0)
    lax.fori_loop(1, n_chunks, value_pass, 0)

    for gg in groups:
        for r in range(rep):
            o = po_sc[rows(gg, r), :] + gate(gg, r, 1) * normalised(acc_sc[rows(gg, r), :])
            col = (gg * rep + r) * NSA_DV
            o_ref[:, col:col + NSA_DV] = o.astype(o_ref.dtype)


def _nsa_attention(za, zg, kc, vc, batch, seq, tq=256, kblk=512, gp=2):
    assert seq % kblk == 0 and seq >= WIN + tq and WIN % tq == 0 and NSA_GROUPS % gp == 0
    nq = seq // tq
    gw = gp * NSA_REP * NSA_DK
    streams = gp * NSA_REP * tq
    hm = max(1, MXU_ROWS // tq)
    assert NSA_REP % hm == 0
    kern = functools.partial(_nsa_kernel, seq=seq, tq=tq, kblk=kblk, gp=gp, hm=hm)
    key = np.arange(seq).reshape(seq // kblk, kblk, 1)
    expand = jnp.asarray(key // SEL_LEN == np.arange(LANES).reshape(1, 1, LANES), BF16)

    def kv_spec(off):
        return pl.BlockSpec((seq, gp * LANES), lambda b, g, i: (b, off // (gp * LANES) + g))

    def cmp_spec(gg):
        return pl.BlockSpec((LANES, NSA_DK), lambda b, g, i: ((g * gp + gg) * batch + b, 0))

    return pl.pallas_call(
        kern,
        grid=(batch, NSA_GROUPS // gp, nq),
        in_specs=[
            pl.BlockSpec((tq, gw), lambda b, g, i: (b * nq + i, A_QN // gw + g)),
            *[cmp_spec(gg) for gg in range(gp)],
            *[cmp_spec(gg) for gg in range(gp)],
            kv_spec(A_KS), kv_spec(A_VS), kv_spec(A_KW), kv_spec(A_VW),
            pl.BlockSpec((tq, LANES), lambda b, g, i: (b * nq + i, 0)),
            pl.BlockSpec(expand.shape, lambda b, g, i: (0, 0, 0)),
        ],
        out_specs=pl.BlockSpec((tq, gw), lambda b, g, i: (b * nq + i, g)),
        out_shape=jax.ShapeDtypeStruct((batch * seq, NSA_HEADS * NSA_DV), BF16),
        scratch_shapes=[
            pltpu.VMEM((seq // kblk, streams, kblk), F32),
            pltpu.VMEM((streams, LANES), F32),
            pltpu.VMEM((streams, 2 * NSA_DV), F32),
            pltpu.VMEM((streams, NSA_DV), F32),
        ],
        compiler_params=_params(("parallel", "parallel", "arbitrary"), 56),
        name="nsa_attention",
    )(za, *([kc] * gp), *([vc] * gp), za, za, za, za, zg, expand)


def _retention_kernel(q_ref, k_ref, v_ref, g_ref, cos_ref, sin_e_ref, sin_o_ref, dec_ref, wq_ref, wk_ref,
                      gc_ref, gn_ref, o_ref, st_ref):
    @pl.when(pl.program_id(1) == 0)
    def _():
        st_ref[...] = jnp.zeros(st_ref.shape, F32)

    c = RET_CHUNK
    for h in range(RET_HEADS):
        st = st_ref[h]
        wq = wq_ref[h]
        wq2 = jnp.concatenate([wq, wq], axis=1)
        cols = slice(h * RET_DV, (h + 1) * RET_DV)
        for sub in range(q_ref.shape[0] // c):
            rows = slice(sub * c, (sub + 1) * c)
            cos = cos_ref[rows, :]
            sin_e = sin_e_ref[rows, :]
            sin_o = sin_o_ref[rows, :]

            def rotate(x):
                return x * cos + pltpu.roll(x, RET_DK - 1, axis=1) * sin_e + pltpu.roll(x, 1, axis=1) * sin_o

            qf = rotate(q_ref[rows, h * RET_DK:(h + 1) * RET_DK])
            kf = rotate(k_ref[rows, h * RET_DK:(h + 1) * RET_DK]) * (RET_DK ** -0.5)
            qb = qf.astype(BF16)
            v = v_ref[rows, cols].astype(BF16)
            s = _dot_nt(qb, kf.astype(BF16)) * dec_ref[h]
            o = _dot(s.astype(BF16), v) + _dot(qb, st.astype(BF16)) * wq2
            st = st * gc_ref[h] + lax.dot_general((kf * wk_ref[h]).astype(BF16), v, TN_DIMS,
                                                  preferred_element_type=F32)
            mu = jnp.mean(o, axis=-1, keepdims=True)
            d = o - mu
            var = jnp.mean(d * d, axis=-1, keepdims=True)
            on = d * lax.rsqrt(var + EPS) * gn_ref[:, cols]
            gr = g_ref[rows, cols]
            o_ref[rows, cols] = (gr * jax.nn.sigmoid(gr) * on).astype(o_ref.dtype)
        st_ref[h] = st


def _retention(zb, gn_w, batch, seq, chunks_per_step=2):
    c = RET_CHUNK
    rows = chunks_per_step * c
    nc = seq // rows
    hq = RET_HEADS * RET_DK
    hv = RET_HEADS * RET_DV
    f32 = np.float32
    inv = f32(ROPE_BASE) ** (-np.arange(0, RET_DK, 2, dtype=f32) / f32(RET_DK))
    ang = np.arange(seq, dtype=f32)[:, None] * inv[None, :]
    zero = np.zeros_like(ang)
    pairs = lambda even, odd: np.stack([even, odd], axis=-1).reshape(seq, RET_DK)
    cos = pairs(np.cos(ang), np.cos(ang))
    sin_e = pairs(-np.sin(ang), zero)
    sin_o = pairs(zero, np.sin(ang))
    log_g = np.log1p(-np.exp2(f32(-5.0) - np.arange(RET_HEADS, dtype=f32)))
    idx = np.arange(c, dtype=f32)
    rel = idx[:, None] - idx[None, :]
    decay = np.where(rel >= 0, np.exp(log_g[:, None, None] * np.maximum(rel, f32(0.0))), f32(0.0)).astype(f32)
    lanes = lambda a: np.ascontiguousarray(np.broadcast_to(a[:, :, None], (RET_HEADS, c, RET_DK)), dtype=f32)
    w_k = lanes(np.exp(log_g[:, None] * (f32(c - 1) - idx)[None, :]))
    w_q = lanes(np.exp(log_g[:, None] * (idx + f32(1.0))[None, :]))
    g_chunk = np.broadcast_to(np.exp(log_g * f32(c))[:, None, None], (RET_HEADS, 1, RET_DV)).astype(f32)

    row = lambda b, n: b * nc + n
    return pl.pallas_call(
        _retention_kernel,
        grid=(batch, nc),
        in_specs=[
            pl.BlockSpec((rows, hq), lambda b, n: (row(b, n), B_QR // hq)),
            pl.BlockSpec((rows, hq), lambda b, n: (row(b, n), B_KR // hq)),
            pl.BlockSpec((rows, hv), lambda b, n: (row(b, n), B_VR // hv)),
            pl.BlockSpec((rows, hv), lambda b, n: (row(b, n), B_GR // hv)),
            pl.BlockSpec((rows, RET_DK), lambda b, n: (n, 0)),
            pl.BlockSpec((rows, RET_DK), lambda b, n: (n, 0)),
            pl.BlockSpec((rows, RET_DK), lambda b, n: (n, 0)),
            pl.BlockSpec((RET_HEADS, c, c), lambda b, n: (0, 0, 0)),
            pl.BlockSpec((RET_HEADS, c, RET_DK), lambda b, n: (0, 0, 0)),
            pl.BlockSpec((RET_HEADS, c, RET_DK), lambda b, n: (0, 0, 0)),
            pl.BlockSpec((RET_HEADS, 1, RET_DV), lambda b, n: (0, 0, 0)),
            pl.BlockSpec((1, hv), lambda b, n: (0, 0)),
        ],
        out_specs=pl.BlockSpec((rows, hv), lambda b, n: (row(b, n), 0)),
        out_shape=jax.ShapeDtypeStruct((batch * seq, hv), BF16),
        scratch_shapes=[pltpu.VMEM((RET_HEADS, RET_DK, RET_DV), F32)],
        compiler_params=_params(("parallel", "arbitrary"), 40),
        name="retention",
    )(zb, zb, zb, zb, cos, sin_e, sin_o, decay, w_q, w_k, g_chunk, gn_w.reshape(1, hv))


def _merge_kernel(on_ref, or_ref, ga_ref, gb_ref, x_ref, wa_ref, wb_ref, wo_ref, h_ref):
    a = _dot(on_ref[...], wa_ref[...])
    b = _dot(or_ref[...], wb_ref[...])
    merged = jax.nn.sigmoid(ga_ref[...]) * a + jax.nn.sigmoid(gb_ref[...]) * b
    h_ref[...] = x_ref[...] + _dot(merged.astype(BF16), wo_ref[...])


def _merge(o_nsa, o_ret, zb, x, w_a, w_b, w_out, tm=256):
    m, d = x.shape
    resident = lambda shape: pl.BlockSpec(shape, lambda i: (0, 0), pipeline_mode=pl.Buffered(1))
    return pl.pallas_call(
        _merge_kernel,
        grid=(m // tm,),
        in_specs=[
            pl.BlockSpec((tm, d), lambda i: (i, 0)),
            pl.BlockSpec((tm, d), lambda i: (i, 0)),
            pl.BlockSpec((tm, d), lambda i: (i, B_GA // d)),
            pl.BlockSpec((tm, d), lambda i: (i, B_GB // d)),
            pl.BlockSpec((tm, d), lambda i: (i, 0)),
            resident(w_a.shape), resident(w_b.shape), resident(w_out.shape),
        ],
        out_specs=pl.BlockSpec((tm, d), lambda i: (i, 0)),
        out_shape=jax.ShapeDtypeStruct((m, d), F32),
        compiler_params=_params(("parallel",), 56),
        name="merge_out_proj",
    )(o_nsa, o_ret, zb, zb, x, w_a, w_b, w_out)


def _mem_kv_kernel(m_ref, nw_ref, wk_ref, wv_ref, k_ref, v_ref):
    mn = _rms(m_ref[...], nw_ref[...]).astype(BF16)
    k_ref[...] = _dot(mn, wk_ref[...]).astype(k_ref.dtype)
    v_ref[...] = _dot(mn, wv_ref[...]).astype(v_ref.dtype)


def _mem_kv(mem2, nw, wk, wv, tm=256):
    m, d = mem2.shape
    n = wk.shape[1]
    out = jax.ShapeDtypeStruct((m, n), BF16)
    return pl.pallas_call(
        _mem_kv_kernel,
        grid=(m // tm,),
        in_specs=[
            pl.BlockSpec((tm, d), lambda i: (i, 0)),
            pl.BlockSpec((1, d), lambda i: (0, 0)),
            pl.BlockSpec((d, n), lambda i: (0, 0)),
            pl.BlockSpec((d, n), lambda i: (0, 0)),
        ],
        out_specs=[pl.BlockSpec((tm, n), lambda i: (i, 0))] * 2,
        out_shape=[out, out],
        compiler_params=_params(("parallel",), 40),
        name="mem_kv_proj",
    )(mem2, nw.reshape(1, d), wk, wv)


def _cross_kernel(h_ref, xw_ref, mw_ref, wq_ref, kx_ref, vx_ref, wo_ref, h2_ref, nm_ref):
    h = h_ref[...]
    nx = _rms(h, xw_ref[...]).astype(BF16)
    qx = _dot(nx, wq_ref[...]).astype(BF16)
    outs = []
    for hh in range(X_HEADS):
        cols = slice(hh * X_DH, (hh + 1) * X_DH)
        s = _dot_nt(qx[:, cols], kx_ref[:, cols]) * (X_DH ** -0.5)
        e = jnp.exp(s - jnp.max(s, axis=-1, keepdims=True))
        p = e / jnp.sum(e, axis=-1, keepdims=True)
        outs.append(_dot(p.astype(BF16), vx_ref[:, cols]))
    ox = jnp.concatenate(outs, axis=-1).astype(BF16)
    h2 = h + _dot(ox, wo_ref[...])
    h2_ref[...] = h2
    nm_ref[...] = _rms(h2, mw_ref[...]).astype(nm_ref.dtype)


def _cross_attention(h1, x_norm_w, mlp_norm_w, wq, kx, vx, wo, seq, tm=512):
    m, d = h1.shape
    n = wq.shape[1]
    per_batch = seq // tm
    vec = lambda: pl.BlockSpec((1, d), lambda i: (0, 0))
    return pl.pallas_call(
        _cross_kernel,
        grid=(m // tm,),
        in_specs=[
            pl.BlockSpec((tm, d), lambda i: (i, 0)),
            vec(), vec(),
            pl.BlockSpec((d, n), lambda i: (0, 0)),
            pl.BlockSpec((MEM_LEN, n), lambda i: (i // per_batch, 0)),
            pl.BlockSpec((MEM_LEN, n), lambda i: (i // per_batch, 0)),
            pl.BlockSpec((n, d), lambda i: (0, 0)),
        ],
        out_specs=[pl.BlockSpec((tm, d), lambda i: (i, 0))] * 2,
        out_shape=[jax.ShapeDtypeStruct((m, d), F32), jax.ShapeDtypeStruct((m, d), BF16)],
        compiler_params=_params(("parallel",), 40),
        name="cross_attention",
    )(h1, x_norm_w.reshape(1, d), mlp_norm_w.reshape(1, d), wq, kx, vx, wo)


def _mlp_kernel(nm_ref, wu_ref, wd_ref, h_ref, fw_ref, o_ref):
    j = pl.program_id(1)

    @pl.when(j == 0)
    def _():
        o_ref[...] = jnp.zeros(o_ref.shape, F32)

    u = jnp.maximum(_dot(nm_ref[...], wu_ref[...]), 0.0)
    o_ref[...] += _dot((u * u).astype(BF16), wd_ref[...])

    @pl.when(j == pl.num_programs(1) - 1)
    def _():
        o_ref[...] = _rms(h_ref[...] + o_ref[...], fw_ref[...])


def _mlp(nm, w_up, w_down, h2, final_w, tm=512, tf=2048):
    m, d = nm.shape
    f = w_up.shape[1]
    return pl.pallas_call(
        _mlp_kernel,
        grid=(m // tm, f // tf),
        in_specs=[
            pl.BlockSpec((tm, d), lambda i, j: (i, 0)),
            pl.BlockSpec((d, tf), lambda i, j: (0, j)),
            pl.BlockSpec((tf, d), lambda i, j: (j, 0)),
            pl.BlockSpec((tm, d), lambda i, j: (i, 0)),
            pl.BlockSpec((1, d), lambda i, j: (0, 0)),
        ],
        out_specs=pl.BlockSpec((tm, d), lambda i, j: (i, 0)),
        out_shape=jax.ShapeDtypeStruct((m, d), F32),
        compiler_params=_params(("parallel", "arbitrary"), 60),
        name="mlp_final_norm",
    )(nm, w_up, w_down, h2, final_w.reshape(1, d))


def _layer(h, mem, attn_norm_w, w_in, cmp_pe_k, cmp_w1_k, cmp_w2_k, cmp_pe_v, cmp_w1_v, cmp_w2_v,
           w_a, ret_gn_w, w_b, w_out, x_norm_w, mem_norm_w, wq_x, wk_x, wv_x, wo_x,
           mlp_norm_w, w_up, w_down, out_norm_w, batch, seq):
    blk = W_IN_BLOCK
    kv_block = W_IN_KV // blk
    gate_shift = NSA_HEADS * 3

    wt = w_in.T
    skip_kv = lambda j: j + (j >= kv_block).astype(jnp.int32)
    kv, n, zg, wq_b, wk_b, wv_b, wo_b, c2k_b, c2v_b = _in_proj(
        h, wt, lambda j: kv_block, 1, BF16, regroup=CMP_STRIDE, norm_w=attn_norm_w,
        side_block=W_IN_NSA_GATE // LANES, casts=(wq_x, wk_x, wv_x, wo_x, cmp_w2_k, cmp_w2_v), name="in_proj_kv")
    za, wa_b, wb_b, wout_b, c1k_b, c1v_b = _in_proj(
        n, wt, skip_kv, 4, BF16, casts=(w_a, w_b, w_out, cmp_w1_k, cmp_w1_v),
        lead_scale=(NSA_Q_SCALE, NSA_HEADS * NSA_DK // blk), name="in_proj_a")
    zb, wup_b, wdown_b = _in_proj(
        n, wt, lambda j: W_IN_NSA_GATE // blk + j, B_WIDTH // blk, F32, shift=gate_shift,
        casts=(w_up, w_down), name="in_proj_b")

    assert seq // CMP_STRIDE == LANES
    rows_kv = NSA_GROUPS * batch * LANES
    k2 = kv.reshape(2 * rows_kv, CMP_STRIDE * NSA_DK)
    kc = _compress(k2, 0, rows_kv, cmp_pe_k.reshape(1, -1), c1k_b, c2k_b)
    vc = _compress(k2, rows_kv, rows_kv, cmp_pe_v.reshape(1, -1), c1v_b, c2v_b)

    o_nsa = _nsa_attention(za, zg, kc, vc, batch, seq)
    o_ret = _retention(zb, ret_gn_w, batch, seq)
    h1 = _merge(o_nsa, o_ret, zb, h, wa_b, wb_b, wout_b)

    kx, vx = _mem_kv(mem.reshape(batch * MEM_LEN, D_MODEL), mem_norm_w, wk_b, wv_b)
    h2, nm = _cross_attention(h1, x_norm_w, mlp_norm_w, wq_b, kx, vx, wo_b, seq)
    return _mlp(nm, wup_b, wdown_b, h2, out_norm_w)


def kernel(x, mem, attn_norm_w, w_in, cmp_pe_k, cmp_w1_k, cmp_w2_k, cmp_pe_v, cmp_w1_v, cmp_w2_v, w_a, ret_gn_w,
           w_b, w_out, x_norm_w, mem_norm_w, wq_x, wk_x, wv_x, wo_x, mlp_norm_w, w_up, w_down, final_norm_w):
    batch, seq, d = x.shape
    depth = w_in.shape[0]
    assert depth == 1
    h = x.reshape(batch * seq, d)
    out = _layer(h, mem, attn_norm_w[0], w_in[0], cmp_pe_k[0], cmp_w1_k[0], cmp_w2_k[0],
                 cmp_pe_v[0], cmp_w1_v[0], cmp_w2_v[0], w_a[0], ret_gn_w[0], w_b[0], w_out[0],
                 x_norm_w[0], mem_norm_w[0], wq_x[0], wk_x[0], wv_x[0], wo_x[0],
                 mlp_norm_w[0], w_up[0], w_down[0], final_norm_w, batch, seq)
    return out.reshape(batch, seq, d)
```

```python
import functools

import jax
import jax.numpy as jnp
import numpy as np
from jax import lax
from jax.experimental import pallas as pl
from jax.experimental.pallas import tpu as pltpu

F32 = jnp.float32
BF16 = jnp.bfloat16

D_MODEL = 2048
MEM_LEN = 256
NSA_HEADS = 16
NSA_GROUPS = 4
NSA_REP = NSA_HEADS // NSA_GROUPS
NSA_DK = 128
NSA_DV = 128
CMP_LEN = 32
CMP_STRIDE = 16
CMP_HIDDEN = 1024
SEL_LEN = 64
SEL_TOPK = 16
WIN = 512
RET_HEADS = 8
RET_DK = 128
RET_DV = 256
RET_CHUNK = 128
ROPE_BASE = 10000.0
X_HEADS = 4
X_DH = 128
D_FF = 4 * D_MODEL
EPS = 1e-6
NEG = -1e30
LOG2E = 1.4426950408889634
NSA_Q_SCALE = NSA_DK ** -0.5 * LOG2E

LANES = 128
F32_SUBLANES = 8
MXU_ROWS = 256
V7X_VMEM_MIB = 64

W_IN_BLOCK = 1024
W_IN_KV = 2048
W_IN_NSA_GATE = 5120
A_QN = 0
A_KS = 2048
A_VS = 2560
A_KW = 3072
A_VW = 3584
B_QR = 0
B_KR = 1024
B_VR = 2048
B_GR = 4096
B_GA = 6144
B_GB = 8192
B_WIDTH = 10240

NT_DIMS = (((1,), (1,)), ((), ()))
TN_DIMS = (((0,), (0,)), ((), ()))


def _params(sem, vmem_mib):
    assert vmem_mib < V7X_VMEM_MIB
    return pltpu.CompilerParams(dimension_semantics=sem, vmem_limit_bytes=vmem_mib * 1024 * 1024)


def _rms(x, w):
    return x * lax.rsqrt(jnp.mean(x * x, axis=-1, keepdims=True) + EPS) * w


def _dot(a, b):
    return jnp.dot(a, b, preferred_element_type=F32)


def _dot_nt(a, b):
    return lax.dot_general(a, b, NT_DIMS, preferred_element_type=F32)


CAST_ROWS = 16


def _cast_rows(dst_ref, dst0, src_ref, src0, nrows):
    def body(r, carry):
        off = r * CAST_ROWS
        dst_ref[pl.ds(pl.multiple_of(dst0 + off, CAST_ROWS), CAST_ROWS), :] = (
            src_ref[pl.ds(pl.multiple_of(src0 + off, F32_SUBLANES), CAST_ROWS), :].astype(BF16))
        return carry

    lax.fori_loop(0, nrows // CAST_ROWS, body, 0)


def _in_proj_kernel(*refs, shift, regroup, n_cast, norm, side, lead_scale):
    n_in = 4 if norm else 3
    n_ref, w_ref, wnext_ref = refs[:3]
    cast_in = refs[n_in:n_in + n_cast]
    o_ref = refs[n_in + n_cast]
    n_out = n_in + n_cast + 1 + int(norm) + int(side)
    cast_out = refs[n_out:n_out + n_cast]
    wb_sc, *rest = refs[n_out + n_cast:]
    tn = wb_sc.shape[0]
    for src, dst in zip(cast_in, cast_out):
        dst[...] = src[...].astype(dst.dtype)

    @pl.when(pl.program_id(1) == 0)
    def _():
        _cast_rows(wb_sc, 0, w_ref, shift, tn - shift)
        if shift:
            _cast_rows(wb_sc, tn - shift, wnext_ref, 0, shift)
        if side:
            _cast_rows(rest[-1], 0, wnext_ref, 0, LANES)

    if norm:
        xn = _rms(n_ref[...], refs[3][...]).astype(BF16)
        refs[n_in + n_cast + 1][...] = xn
    else:
        xn = n_ref[...]
    if side:
        refs[n_out - 1][...] = _dot_nt(xn, rest[-1][...])
    res = _dot_nt(xn, wb_sc[...])
    if regroup:
        r_sc = rest[0]
        tm = res.shape[0]
        for c in range(tn // LANES):
            r_sc[c] = res[:, c * LANES:(c + 1) * LANES]
        for c in range(tn // LANES):
            for j in range(regroup):
                o_ref[c, :, j * LANES:(j + 1) * LANES] = (
                    r_sc[c, pl.ds(j, tm // regroup, stride=regroup), :].astype(o_ref.dtype))
    else:
        if lead_scale is not None:
            factor, blocks = lead_scale
            res = res * jnp.where(pl.program_id(0) < blocks, factor, 1.0)
        o_ref[...] = res.astype(o_ref.dtype)


def _in_proj(n, wt, src_block, n_blocks, out_dtype, *, shift=0, regroup=0, casts=(), norm_w=None,
             side_block=None, lead_scale=None, tm=1024, tn=1024, name):
    m, k = n.shape
    per = tn // LANES
    m_tiles = m // tm
    norm = norm_w is not None
    side = side_block is not None
    assert shift % CAST_ROWS == 0 and shift <= LANES and not (norm and n_blocks > 1)
    assert not side or (shift == 0 and n_blocks == 1)
    scratch = [pltpu.VMEM((tn, k), BF16)]
    row_tile = pl.BlockSpec((tm, k), lambda j, i: (i, 0))
    once = {"pipeline_mode": pl.Buffered(1)} if n_blocks == 1 else {}
    if regroup:
        assert n_blocks == 1
        out_shape = jax.ShapeDtypeStruct((per, m // regroup, regroup * LANES), out_dtype)
        out_spec = pl.BlockSpec((per, tm // regroup, regroup * LANES), lambda j, i: (0, i, 0))
        scratch.append(pltpu.VMEM((per, tm, LANES), F32))
    else:
        out_shape = jax.ShapeDtypeStruct((m, n_blocks * tn), out_dtype)
        out_spec = pl.BlockSpec((tm, tn), lambda j, i: (i, j))

    cast_steps = 1 << ((n_blocks * m_tiles).bit_length() - 1)
    cast_specs = []
    for a in casts:
        rows = a.shape[0] // cast_steps
        assert a.shape[0] % cast_steps == 0 and rows % CAST_ROWS == 0
        cast_specs.append(pl.BlockSpec(
            (rows, a.shape[1]), lambda j, i: (jnp.minimum(j * m_tiles + i, cast_steps - 1), 0)))

    if side:
        scratch.append(pltpu.VMEM((LANES, k), BF16))
    next_rows = (lambda j, i: (side_block, 0)) if side else (lambda j, i: ((src_block(j) + 1) * per, 0))
    return pl.pallas_call(
        functools.partial(_in_proj_kernel, shift=shift, regroup=regroup, n_cast=len(casts), norm=norm, side=side,
                          lead_scale=lead_scale),
        grid=(n_blocks, m_tiles),
        in_specs=[
            row_tile,
            pl.BlockSpec((tn, k), lambda j, i: (src_block(j), 0), **once),
            pl.BlockSpec((LANES, k), next_rows, **once),
            *([pl.BlockSpec((1, k), lambda j, i: (0, 0))] if norm else []),
            *cast_specs,
        ],
        out_specs=[out_spec, *([row_tile] if norm else []),
                   *([pl.BlockSpec((tm, LANES), lambda j, i: (i, 0))] if side else []), *cast_specs],
        out_shape=[out_shape, *([jax.ShapeDtypeStruct((m, k), BF16)] if norm else []),
                   *([jax.ShapeDtypeStruct((m, LANES), F32)] if side else []),
                   *[jax.ShapeDtypeStruct(a.shape, BF16) for a in casts]],
        scratch_shapes=scratch,
        compiler_params=_params(("arbitrary", "arbitrary"), 60 if norm else 56),
        name=name,
    )(n, wt, wt, *([norm_w.reshape(1, k)] if norm else []), *casts)


def _compress_kernel(k2_ref, pe_ref, w1_ref, w2_ref, o_ref):
    rows, half = k2_ref.shape
    k2 = k2_ref[...].astype(F32)
    a_lo = (k2 + pe_ref[:, :half]).astype(BF16)
    a_hi = (k2 + pe_ref[:, half:]).astype(BF16)
    lo = _dot(a_lo, w1_ref[:half, :])
    hi = _dot(a_hi, w1_ref[half:, :])
    h = lo + pltpu.roll(hi, rows - 1, axis=0)
    act = h * jax.nn.sigmoid(h)
    o_ref[...] = _dot(act.astype(BF16), w2_ref[...]).astype(o_ref.dtype)


def _compress(k2, first_row, m, pe, w1, w2, rows=512):
    half = k2.shape[1]
    hid = w1.shape[1]
    dout = w2.shape[1]
    first = first_row // rows
    return pl.pallas_call(
        _compress_kernel,
        grid=(m // rows,),
        in_specs=[
            pl.BlockSpec((rows, half), lambda i: (first + i, 0)),
            pl.BlockSpec((1, 2 * half), lambda i: (0, 0)),
            pl.BlockSpec((2 * half, hid), lambda i: (0, 0)),
            pl.BlockSpec((hid, dout), lambda i: (0, 0)),
        ],
        out_specs=pl.BlockSpec((rows, dout), lambda i: (i, 0)),
        out_shape=jax.ShapeDtypeStruct((m, dout), BF16),
        compiler_params=_params(("parallel",), 40),
        name="nsa_compress",
    )(k2, pe, w1, w2)


def _nsa_kernel(*refs, seq, tq, kblk, gp, hm):
    q_ref = refs[0]
    kc_refs = refs[1:1 + gp]
    vc_refs = refs[1 + gp:1 + 2 * gp]
    ks_ref, vs_ref, kw_ref, vw_ref, g_ref, e_ref, o_ref, s_sc, mx_sc, acc_sc, po_sc = refs[1 + 2 * gp:]
    i = pl.program_id(2)
    t0 = i * tq
    rep = NSA_REP
    n_cmp = (seq - CMP_LEN) // CMP_STRIDE + 1
    n_sel = seq // SEL_LEN
    topk = min(SEL_TOPK, n_sel)
    groups = range(gp)

    q = q_ref[...]
    units = range(rep // hm)
    rows = lambda gg, r: slice((gg * rep + r) * tq, (gg * rep + r + 1) * tq)
    urows = lambda gg, u: slice((gg * rep + u * hm) * tq, (gg * rep + (u + 1) * hm) * tq)
    part = lambda x, h: x[h * tq:(h + 1) * tq]
    head_q = lambda gg, r: q[:, (gg * rep + r) * NSA_DK:(gg * rep + r + 1) * NSA_DK]
    unit_q = {(gg, u): jnp.concatenate([head_q(gg, u * hm + h) for h in range(hm)], axis=0)
              for gg in groups for u in units}
    group_cols = lambda ref, gg: ref.at[:, gg * LANES:(gg + 1) * LANES]

    def masked(s, mask1):
        return jnp.concatenate([jnp.where(mask1, part(s, h), NEG) for h in range(hm)], axis=0)
    tcol = t0 + lax.broadcasted_iota(jnp.int32, (tq, 1), 0)
    per_group = rep * 3
    g_sig = jax.nn.sigmoid(g_ref[...])
    first_group = pl.program_id(1) * gp
    gs = [pltpu.roll(g_sig, lax.rem(LANES - (first_group + gg) * per_group, LANES), axis=1) for gg in groups]
    gate = lambda gg, r, branch: gs[gg][:, 3 * r + branch:3 * r + branch + 1]

    def exp_rows(sm):
        return jnp.exp2(sm - jnp.max(sm, axis=-1, keepdims=True))

    def with_ones(v):
        return jnp.concatenate([v, jnp.ones(v.shape, v.dtype)], axis=1)

    def normalised(ev):
        return ev[:, :NSA_DV] / ev[:, NSA_DV:]

    c_idx = lax.broadcasted_iota(jnp.int32, (tq, LANES), 1)
    mask_c = ((c_idx * CMP_STRIDE + (CMP_LEN - 1)) <= tcol) & (c_idx < n_cmp)
    mask_cf = jnp.where(mask_c, 1.0, 0.0)
    o_cmp, psum = {}, []
    for gg in groups:
        kc = kc_refs[gg][...]
        vc = vc_refs[gg][...]
        tot = None
        for u in units:
            e = exp_rows(masked(_dot_nt(unit_q[gg, u], kc), mask_c))
            p = e / jnp.sum(e, axis=-1, keepdims=True)
            p = jnp.concatenate([part(p, h) * mask_cf for h in range(hm)], axis=0)
            o_cmp[gg, u] = _dot(p.astype(BF16), vc)
            for h in range(hm):
                tot = part(p, h) if tot is None else tot + part(p, h)
        psum.append(tot)

    wlen = WIN + tq
    start = pl.multiple_of(jnp.maximum(i - WIN // tq, 0) * tq, tq)
    dlt = tcol - (start + lax.broadcasted_iota(jnp.int32, (tq, wlen), 1))
    mask_w = (dlt >= 0) & (dlt < WIN)
    for gg in groups:
        kw = group_cols(kw_ref, gg)[pl.ds(start, wlen), :]
        vw1 = with_ones(group_cols(vw_ref, gg)[pl.ds(start, wlen), :])
        for u in units:
            e = exp_rows(masked(_dot_nt(unit_q[gg, u], kw), mask_w))
            o_win = normalised(_dot(e.astype(BF16), vw1))
            for h in range(hm):
                r = u * hm + h
                po_sc[rows(gg, r), :] = (gate(gg, r, 0) * part(o_cmp[gg, u], h)
                                         + gate(gg, r, 2) * part(o_win, h))

    jo = lax.broadcasted_iota(jnp.int32, (n_sel, LANES), 0)
    co = lax.broadcasted_iota(jnp.int32, (n_sel, LANES), 1)
    ov_t = jnp.where((co * CMP_STRIDE < jo * SEL_LEN + SEL_LEN) & (co * CMP_STRIDE + CMP_LEN > jo * SEL_LEN)
                     & (co < n_cmp), 1.0, 0.0).astype(BF16)
    j_idx = lax.broadcasted_iota(jnp.int32, (n_sel, tq), 0)
    cur = lax.shift_right_logical(t0 + lax.broadcasted_iota(jnp.int32, (n_sel, tq), 1), int(np.log2(SEL_LEN)))
    forced = (j_idx == 0) | (j_idx == cur) | (j_idx == cur - 1)
    future = j_idx > cur
    q_bias = {}
    for gg in groups:
        p_hi = psum[gg].astype(BF16)
        p_lo = (psum[gg] - p_hi.astype(F32)).astype(BF16)
        imp = _dot_nt(ov_t, p_hi) + _dot_nt(ov_t, p_lo)
        impm = jnp.where(forced, jnp.inf, jnp.where(future, -jnp.inf, imp))
        rank = jnp.zeros((n_sel, tq), F32)
        for ii in range(n_sel):
            row = impm[ii:ii + 1, :]
            beats = (row > impm) | ((row == impm) & (j_idx > ii))
            rank = rank + jnp.where(beats, 1.0, 0.0)
        bias_t = jnp.where(rank < topk, 0.0, NEG)
        bias = jnp.concatenate([bias_t, jnp.zeros((LANES - n_sel, tq), F32)], axis=0).T.astype(BF16)
        for u in units:
            q_bias[gg, u] = jnp.concatenate([unit_q[gg, u], jnp.concatenate([bias] * hm, axis=0)], axis=1)

    n_chunks = lax.div(t0 + tq + (kblk - 1), kblk)

    def lane_fold_max(x):
        out = x[:, :LANES]
        for c in range(1, kblk // LANES):
            out = jnp.maximum(out, x[:, c * LANES:(c + 1) * LANES])
        return out

    def chunk_start(kb):
        return kb * kblk if isinstance(kb, int) else pl.multiple_of(kb * kblk, kblk)

    def score_chunk(kb):
        k0 = chunk_start(kb)
        causal = (k0 + lax.broadcasted_iota(jnp.int32, (tq, kblk), 1)) <= tcol
        folds = []
        for gg in groups:
            k = jnp.concatenate([group_cols(ks_ref, gg)[pl.ds(k0, kblk), :], e_ref[kb]], axis=1)
            for u in units:
                sm_ = masked(_dot_nt(q_bias[gg, u], k), causal)
                s_sc[kb, urows(gg, u), :] = sm_
                folds.append(lane_fold_max(sm_))
        return jnp.concatenate(folds, axis=0)

    def score_pass(kb, carry):
        mx_sc[...] = jnp.maximum(mx_sc[...], score_chunk(kb))
        return carry

    mx_sc[...] = score_chunk(0)
    lax.fori_loop(1, n_chunks, score_pass, 0)
    m_sel = jnp.max(mx_sc[...], axis=-1, keepdims=True)

    def value_chunk(kb):
        k0 = chunk_start(kb)
        pvs = []
        for gg in groups:
            v1 = with_ones(group_cols(vs_ref, gg)[pl.ds(k0, kblk), :])
            for u in units:
                pk = jnp.exp2(s_sc[kb, urows(gg, u), :] - m_sel[urows(gg, u)])
                pvs.append(_dot(pk.astype(BF16), v1))
        return jnp.concatenate(pvs, axis=0)

    def value_pass(kb, carry):
        acc_sc[...] += value_chunk(kb)
        return carry

    acc_sc[...] = value_chunk(0)
    lax.fori_loop(1, n_chunks, value_pass, 0)

    for gg in groups:
        for r in range(rep):
            o = po_sc[rows(gg, r), :] + gate(gg, r, 1) * normalised(acc_sc[rows(gg, r), :])
            col = (gg * rep + r) * NSA_DV
            o_ref[:, col:col + NSA_DV] = o.astype(o_ref.dtype)


def _nsa_attention(za, zg, kc, vc, batch, seq, tq=256, kblk=512, gp=2):
    assert seq % kblk == 0 and seq >= WIN + tq and WIN % tq == 0 and NSA_GROUPS % gp == 0
    nq = seq // tq
    gw = gp * NSA_REP * NSA_DK
    streams = gp * NSA_REP * tq
    hm = max(1, MXU_ROWS // tq)
    assert NSA_REP % hm == 0
    kern = functools.partial(_nsa_kernel, seq=seq, tq=tq, kblk=kblk, gp=gp, hm=hm)
    key = np.arange(seq).reshape(seq // kblk, kblk, 1)
    expand = jnp.asarray(key // SEL_LEN == np.arange(LANES).reshape(1, 1, LANES), BF16)

    def kv_spec(off):
        return pl.BlockSpec((seq, gp * LANES), lambda b, g, i: (b, off // (gp * LANES) + g))

    def cmp_spec(gg):
        return pl.BlockSpec((LANES, NSA_DK), lambda b, g, i: ((g * gp + gg) * batch + b, 0))

    return pl.pallas_call(
        kern,
        grid=(batch, NSA_GROUPS // gp, nq),
        in_specs=[
            pl.BlockSpec((tq, gw), lambda b, g, i: (b * nq + i, A_QN // gw + g)),
            *[cmp_spec(gg) for gg in range(gp)],
            *[cmp_spec(gg) for gg in range(gp)],
            kv_spec(A_KS), kv_spec(A_VS), kv_spec(A_KW), kv_spec(A_VW),
            pl.BlockSpec((tq, LANES), lambda b, g, i: (b * nq + i, 0)),
            pl.BlockSpec(expand.shape, lambda b, g, i: (0, 0, 0)),
        ],
        out_specs=pl.BlockSpec((tq, gw), lambda b, g, i: (b * nq + i, g)),
        out_shape=jax.ShapeDtypeStruct((batch * seq, NSA_HEADS * NSA_DV), BF16),
        scratch_shapes=[
            pltpu.VMEM((seq // kblk, streams, kblk), F32),
            pltpu.VMEM((streams, LANES), F32),
            pltpu.VMEM((streams, 2 * NSA_DV), F32),
            pltpu.VMEM((streams, NSA_DV), F32),
        ],
        compiler_params=_params(("parallel", "parallel", "arbitrary"), 56),
        name="nsa_attention",
    )(za, *([kc] * gp), *([vc] * gp), za, za, za, za, zg, expand)


def _retention_kernel(q_ref, k_ref, v_ref, g_ref, cos_ref, sin_e_ref, sin_o_ref, dec_ref, wq_ref, wk_ref,
                      gc_ref, gn_ref, o_ref, st_ref):
    @pl.when(pl.program_id(1) == 0)
    def _():
        st_ref[...] = jnp.zeros(st_ref.shape, F32)

    c = RET_CHUNK
    for h in range(RET_HEADS):
        st = st_ref[h]
        wq = wq_ref[h]
        wq2 = jnp.concatenate([wq, wq], axis=1)
        cols = slice(h * RET_DV, (h + 1) * RET_DV)
        for sub in range(q_ref.shape[0] // c):
            rows = slice(sub * c, (sub + 1) * c)
            cos = cos_ref[rows, :]
            sin_e = sin_e_ref[rows, :]
            sin_o = sin_o_ref[rows, :]

            def rotate(x):
                return x * cos + pltpu.roll(x, RET_DK - 1, axis=1) * sin_e + pltpu.roll(x, 1, axis=1) * sin_o

            qf = rotate(q_ref[rows, h * RET_DK:(h + 1) * RET_DK])
            kf = rotate(k_ref[rows, h * RET_DK:(h + 1) * RET_DK]) * (RET_DK ** -0.5)
            qb = qf.astype(BF16)
            v = v_ref[rows, cols].astype(BF16)
            s = _dot_nt(qb, kf.astype(BF16)) * dec_ref[h]
            o = _dot(s.astype(BF16), v) + _dot(qb, st.astype(BF16)) * wq2
            st = st * gc_ref[h] + lax.dot_general((kf * wk_ref[h]).astype(BF16), v, TN_DIMS,
                                                  preferred_element_type=F32)
            mu = jnp.mean(o, axis=-1, keepdims=True)
            d = o - mu
            var = jnp.mean(d * d, axis=-1, keepdims=True)
            on = d * lax.rsqrt(var + EPS) * gn_ref[:, cols]
            gr = g_ref[rows, cols]
            o_ref[rows, cols] = (gr * jax.nn.sigmoid(gr) * on).astype(o_ref.dtype)
        st_ref[h] = st


def _retention(zb, gn_w, batch, seq, chunks_per_step=2):
    c = RET_CHUNK
    rows = chunks_per_step * c
    nc = seq // rows
    hq = RET_HEADS * RET_DK
    hv = RET_HEADS * RET_DV
    f32 = np.float32
    inv = f32(ROPE_BASE) ** (-np.arange(0, RET_DK, 2, dtype=f32) / f32(RET_DK))
    ang = np.arange(seq, dtype=f32)[:, None] * inv[None, :]
    zero = np.zeros_like(ang)
    pairs = lambda even, odd: np.stack([even, odd], axis=-1).reshape(seq, RET_DK)
    cos = pairs(np.cos(ang), np.cos(ang))
    sin_e = pairs(-np.sin(ang), zero)
    sin_o = pairs(zero, np.sin(ang))
    log_g = np.log1p(-np.exp2(f32(-5.0) - np.arange(RET_HEADS, dtype=f32)))
    idx = np.arange(c, dtype=f32)
    rel = idx[:, None] - idx[None, :]
    decay = np.where(rel >= 0, np.exp(log_g[:, None, None] * np.maximum(rel, f32(0.0))), f32(0.0)).astype(f32)
    lanes = lambda a: np.ascontiguousarray(np.broadcast_to(a[:, :, None], (RET_HEADS, c, RET_DK)), dtype=f32)
    w_k = lanes(np.exp(log_g[:, None] * (f32(c - 1) - idx)[None, :]))
    w_q = lanes(np.exp(log_g[:, None] * (idx + f32(1.0))[None, :]))
    g_chunk = np.broadcast_to(np.exp(log_g * f32(c))[:, None, None], (RET_HEADS, 1, RET_DV)).astype(f32)

    row = lambda b, n: b * nc + n
    return pl.pallas_call(
        _retention_kernel,
        grid=(batch, nc),
        in_specs=[
            pl.BlockSpec((rows, hq), lambda b, n: (row(b, n), B_QR // hq)),
            pl.BlockSpec((rows, hq), lambda b, n: (row(b, n), B_KR // hq)),
            pl.BlockSpec((rows, hv), lambda b, n: (row(b, n), B_VR // hv)),
            pl.BlockSpec((rows, hv), lambda b, n: (row(b, n), B_GR // hv)),
            pl.BlockSpec((rows, RET_DK), lambda b, n: (n, 0)),
            pl.BlockSpec((rows, RET_DK), lambda b, n: (n, 0)),
            pl.BlockSpec((rows, RET_DK), lambda b, n: (n, 0)),
            pl.BlockSpec((RET_HEADS, c, c), lambda b, n: (0, 0, 0)),
            pl.BlockSpec((RET_HEADS, c, RET_DK), lambda b, n: (0, 0, 0)),
            pl.BlockSpec((RET_HEADS, c, RET_DK), lambda b, n: (0, 0, 0)),
            pl.BlockSpec((RET_HEADS, 1, RET_DV), lambda b, n: (0, 0, 0)),
            pl.BlockSpec((1, hv), lambda b, n: (0, 0)),
        ],
        out_specs=pl.BlockSpec((rows, hv), lambda b, n: (row(b, n), 0)),
        out_shape=jax.ShapeDtypeStruct((batch * seq, hv), BF16),
        scratch_shapes=[pltpu.VMEM((RET_HEADS, RET_DK, RET_DV), F32)],
        compiler_params=_params(("parallel", "arbitrary"), 40),
        name="retention",
    )(zb, zb, zb, zb, cos, sin_e, sin_o, decay, w_q, w_k, g_chunk, gn_w.reshape(1, hv))


def _merge_kernel(on_ref, or_ref, ga_ref, gb_ref, x_ref, wa_ref, wb_ref, wo_ref, h_ref):
    a = _dot(on_ref[...], wa_ref[...])
    b = _dot(or_ref[...], wb_ref[...])
    merged = jax.nn.sigmoid(ga_ref[...]) * a + jax.nn.sigmoid(gb_ref[...]) * b
    h_ref[...] = x_ref[...] + _dot(merged.astype(BF16), wo_ref[...])


def _merge(o_nsa, o_ret, zb, x, w_a, w_b, w_out, tm=256):
    m, d = x.shape
    resident = lambda shape: pl.BlockSpec(shape, lambda i: (0, 0), pipeline_mode=pl.Buffered(1))
    return pl.pallas_call(
        _merge_kernel,
        grid=(m // tm,),
        in_specs=[
            pl.BlockSpec((tm, d), lambda i: (i, 0)),
            pl.BlockSpec((tm, d), lambda i: (i, 0)),
            pl.BlockSpec((tm, d), lambda i: (i, B_GA // d)),
            pl.BlockSpec((tm, d), lambda i: (i, B_GB // d)),
            pl.BlockSpec((tm, d), lambda i: (i, 0)),
            resident(w_a.shape), resident(w_b.shape), resident(w_out.shape),
        ],
        out_specs=pl.BlockSpec((tm, d), lambda i: (i, 0)),
        out_shape=jax.ShapeDtypeStruct((m, d), F32),
        compiler_params=_params(("parallel",), 56),
        name="merge_out_proj",
    )(o_nsa, o_ret, zb, zb, x, w_a, w_b, w_out)


def _mem_kv_kernel(m_ref, nw_ref, wk_ref, wv_ref, k_ref, v_ref):
    mn = _rms(m_ref[...], nw_ref[...]).astype(BF16)
    k_ref[...] = _dot(mn, wk_ref[...]).astype(k_ref.dtype)
    v_ref[...] = _dot(mn, wv_ref[...]).astype(v_ref.dtype)


def _mem_kv(mem2, nw, wk, wv, tm=256):
    m, d = mem2.shape
    n = wk.shape[1]
    out = jax.ShapeDtypeStruct((m, n), BF16)
    return pl.pallas_call(
        _mem_kv_kernel,
        grid=(m // tm,),
        in_specs=[
            pl.BlockSpec((tm, d), lambda i: (i, 0)),
            pl.BlockSpec((1, d), lambda i: (0, 0)),
            pl.BlockSpec((d, n), lambda i: (0, 0)),
            pl.BlockSpec((d, n), lambda i: (0, 0)),
        ],
        out_specs=[pl.BlockSpec((tm, n), lambda i: (i, 0))] * 2,
        out_shape=[out, out],
        compiler_params=_params(("parallel",), 40),
        name="mem_kv_proj",
    )(mem2, nw.reshape(1, d), wk, wv)


def _cross_kernel(h_ref, xw_ref, mw_ref, wq_ref, kx_ref, vx_ref, wo_ref, h2_ref, nm_ref):
    h = h_ref[...]
    nx = _rms(h, xw_ref[...]).astype(BF16)
    qx = _dot(nx, wq_ref[...]).astype(BF16)
    outs = []
    for hh in range(X_HEADS):
        cols = slice(hh * X_DH, (hh + 1) * X_DH)
        s = _dot_nt(qx[:, cols], kx_ref[:, cols]) * (X_DH ** -0.5)
        e = jnp.exp(s - jnp.max(s, axis=-1, keepdims=True))
        p = e / jnp.sum(e, axis=-1, keepdims=True)
        outs.append(_dot(p.astype(BF16), vx_ref[:, cols]))
    ox = jnp.concatenate(outs, axis=-1).astype(BF16)
    h2 = h + _dot(ox, wo_ref[...])
    h2_ref[...] = h2
    nm_ref[...] = _rms(h2, mw_ref[...]).astype(nm_ref.dtype)


def _cross_attention(h1, x_norm_w, mlp_norm_w, wq, kx, vx, wo, seq, tm=1024):
    m, d = h1.shape
    n = wq.shape[1]
    per_batch = seq // tm
    vec = lambda: pl.BlockSpec((1, d), lambda i: (0, 0))
    return pl.pallas_call(
        _cross_kernel,
        grid=(m // tm,),
        in_specs=[
            pl.BlockSpec((tm, d), lambda i: (i, 0)),
            vec(), vec(),
            pl.BlockSpec((d, n), lambda i: (0, 0)),
            pl.BlockSpec((MEM_LEN, n), lambda i: (i // per_batch, 0)),
            pl.BlockSpec((MEM_LEN, n), lambda i: (i // per_batch, 0)),
            pl.BlockSpec((n, d), lambda i: (0, 0)),
        ],
        out_specs=[pl.BlockSpec((tm, d), lambda i: (i, 0))] * 2,
        out_shape=[jax.ShapeDtypeStruct((m, d), F32), jax.ShapeDtypeStruct((m, d), BF16)],
        compiler_params=_params(("parallel",), 56),
        name="cross_attention",
    )(h1, x_norm_w.reshape(1, d), mlp_norm_w.reshape(1, d), wq, kx, vx, wo)


def _mlp_kernel(nm_ref, wu_ref, wd_ref, h_ref, fw_ref, o_ref):
    j = pl.program_id(1)

    @pl.when(j == 0)
    def _():
        o_ref[...] = jnp.zeros(o_ref.shape, F32)

    u = jnp.maximum(_dot(nm_ref[...], wu_ref[...]), 0.0)
    o_ref[...] += _dot((u * u).astype(BF16), wd_ref[...])

    @pl.when(j == pl.num_programs(1) - 1)
    def _():
        o_ref[...] = _rms(h_ref[...] + o_ref[...], fw_ref[...])


def _mlp(nm, w_up, w_down, h2, final_w, tm=512, tf=2048):
    m, d = nm.shape
    f = w_up.shape[1]
    return pl.pallas_call(
        _mlp_kernel,
        grid=(m // tm, f // tf),
        in_specs=[
            pl.BlockSpec((tm, d), lambda i, j: (i, 0)),
            pl.BlockSpec((d, tf), lambda i, j: (0, j)),
            pl.BlockSpec((tf, d), lambda i, j: (j, 0)),
            pl.BlockSpec((tm, d), lambda i, j: (i, 0)),
            pl.BlockSpec((1, d), lambda i, j: (0, 0)),
        ],
        out_specs=pl.BlockSpec((tm, d), lambda i, j: (i, 0)),
        out_shape=jax.ShapeDtypeStruct((m, d), F32),
        compiler_params=_params(("parallel", "arbitrary"), 60),
        name="mlp_final_norm",
    )(nm, w_up, w_down, h2, final_w.reshape(1, d))


def _layer(h, mem, attn_norm_w, w_in, cmp_pe_k, cmp_w1_k, cmp_w2_k, cmp_pe_v, cmp_w1_v, cmp_w2_v,
           w_a, ret_gn_w, w_b, w_out, x_norm_w, mem_norm_w, wq_x, wk_x, wv_x, wo_x,
           mlp_norm_w, w_up, w_down, out_norm_w, batch, seq):
    blk = W_IN_BLOCK
    kv_block = W_IN_KV // blk
    gate_shift = NSA_HEADS * 3

    wt = w_in.T
    skip_kv = lambda j: j + (j >= kv_block).astype(jnp.int32)
    kv, n, zg, wq_b, wk_b, wv_b, wo_b, c2k_b, c2v_b = _in_proj(
        h, wt, lambda j: kv_block, 1, BF16, regroup=CMP_STRIDE, norm_w=attn_norm_w,
        side_block=W_IN_NSA_GATE // LANES, casts=(wq_x, wk_x, wv_x, wo_x, cmp_w2_k, cmp_w2_v), name="in_proj_kv")
    za, wa_b, wb_b, wout_b, c1k_b, c1v_b = _in_proj(
        n, wt, skip_kv, 4, BF16, casts=(w_a, w_b, w_out, cmp_w1_k, cmp_w1_v),
        lead_scale=(NSA_Q_SCALE, NSA_HEADS * NSA_DK // blk), name="in_proj_a")
    zb, wup_b, wdown_b = _in_proj(
        n, wt, lambda j: W_IN_NSA_GATE // blk + j, B_WIDTH // blk, F32, shift=gate_shift,
        casts=(w_up, w_down), name="in_proj_b")

    assert seq // CMP_STRIDE == LANES
    rows_kv = NSA_GROUPS * batch * LANES
    k2 = kv.reshape(2 * rows_kv, CMP_STRIDE * NSA_DK)
    kc = _compress(k2, 0, rows_kv, cmp_pe_k.reshape(1, -1), c1k_b, c2k_b)
    vc = _compress(k2, rows_kv, rows_kv, cmp_pe_v.reshape(1, -1), c1v_b, c2v_b)

    o_nsa = _nsa_attention(za, zg, kc, vc, batch, seq)
    o_ret = _retention(zb, ret_gn_w, batch, seq)
    h1 = _merge(o_nsa, o_ret, zb, h, wa_b, wb_b, wout_b)

    kx, vx = _mem_kv(mem.reshape(batch * MEM_LEN, D_MODEL), mem_norm_w, wk_b, wv_b)
    h2, nm = _cross_attention(h1, x_norm_w, mlp_norm_w, wq_b, kx, vx, wo_b, seq)
    return _mlp(nm, wup_b, wdown_b, h2, out_norm_w)


def kernel(x, mem, attn_norm_w, w_in, cmp_pe_k, cmp_w1_k, cmp_w2_k, cmp_pe_v, cmp_w1_v, cmp_w2_v, w_a, ret_gn_w,
           w_b, w_out, x_norm_w, mem_norm_w, wq_x, wk_x, wv_x, wo_x, mlp_norm_w, w_up, w_down, final_norm_w):
    batch, seq, d = x.shape
    depth = w_in.shape[0]
    assert depth == 1
    h = x.reshape(batch * seq, d)
    out = _layer(h, mem, attn_norm_w[0], w_in[0], cmp_pe_k[0], cmp_w1_k[0], cmp_w2_k[0],
                 cmp_pe_v[0], cmp_w1_v[0], cmp_w2_v[0], w_a[0], ret_gn_w[0], w_b[0], w_out[0],
                 x_norm_w[0], mem_norm_w[0], wq_x[0], wk_x[0], wv_x[0], wo_x[0],
                 mlp_norm_w[0], w_up[0], w_down[0], final_norm_w, batch, seq)
    return out.reshape(batch, seq, d)
```

```python
import functools

import jax
import jax.numpy as jnp
import numpy as np
from jax import lax
from jax.experimental import pallas as pl
from jax.experimental.pallas import tpu as pltpu

F32 = jnp.float32
BF16 = jnp.bfloat16

D_MODEL = 2048
MEM_LEN = 256
NSA_HEADS = 16
NSA_GROUPS = 4
NSA_REP = NSA_HEADS // NSA_GROUPS
NSA_DK = 128
NSA_DV = 128
CMP_LEN = 32
CMP_STRIDE = 16
CMP_HIDDEN = 1024
SEL_LEN = 64
SEL_TOPK = 16
WIN = 512
RET_HEADS = 8
RET_DK = 128
RET_DV = 256
RET_CHUNK = 128
ROPE_BASE = 10000.0
X_HEADS = 4
X_DH = 128
D_FF = 4 * D_MODEL
EPS = 1e-6
NEG = -1e30
LOG2E = 1.4426950408889634
NSA_Q_SCALE = NSA_DK ** -0.5 * LOG2E

LANES = 128
F32_SUBLANES = 8
MXU_ROWS = 256
V7X_VMEM_MIB = 64

W_IN_BLOCK = 1024
W_IN_KV = 2048
W_IN_NSA_GATE = 5120
A_QN = 0
A_KS = 2048
A_VS = 2560
A_KW = 3072
A_VW = 3584
B_QR = 0
B_KR = 1024
B_VR = 2048
B_GR = 4096
B_GA = 6144
B_GB = 8192
B_WIDTH = 10240

NT_DIMS = (((1,), (1,)), ((), ()))
TN_DIMS = (((0,), (0,)), ((), ()))


def _params(sem, vmem_mib):
    assert vmem_mib < V7X_VMEM_MIB
    return pltpu.CompilerParams(dimension_semantics=sem, vmem_limit_bytes=vmem_mib * 1024 * 1024)


def _rms(x, w):
    return x * lax.rsqrt(jnp.mean(x * x, axis=-1, keepdims=True) + EPS) * w


def _dot(a, b):
    return jnp.dot(a, b, preferred_element_type=F32)


def _dot_nt(a, b):
    return lax.dot_general(a, b, NT_DIMS, preferred_element_type=F32)


CAST_ROWS = 16
SLAB_TAIL_ROWS = 64


def _cast_rows(dst_ref, dst0, src_ref, src0, nrows):
    def body(r, carry):
        off = r * CAST_ROWS
        dst_ref[pl.ds(pl.multiple_of(dst0 + off, CAST_ROWS), CAST_ROWS), :] = (
            src_ref[pl.ds(pl.multiple_of(src0 + off, F32_SUBLANES), CAST_ROWS), :].astype(BF16))
        return carry

    lax.fori_loop(0, nrows // CAST_ROWS, body, 0)


def _in_proj_kernel(*refs, shift, regroup, n_cast, norm, side, lead_scale, slab_off=None):
    n_in = 4 if norm else 3
    slab = slab_off is not None
    n_ref, w_ref, wnext_ref = refs[:3]
    cast_in = refs[n_in:n_in + n_cast]
    o_idx = n_in + n_cast + 2 * int(slab)
    o_ref = refs[o_idx]
    n_out = o_idx + 1 + int(norm) + int(side)
    cast_out = refs[n_out:n_out + n_cast]
    wb_sc, *rest = refs[n_out + n_cast + int(slab):]
    tn = wb_sc.shape[0]
    for src, dst in zip(cast_in, cast_out):
        dst[...] = src[...].astype(dst.dtype)
    if slab:
        main_ref, tail_ref, slab_out = refs[n_in + n_cast], refs[n_in + n_cast + 1], refs[n_out + n_cast]
        keep = main_ref.shape[0] - slab_off
        slab_out[:keep, :] = main_ref[slab_off:, :].astype(BF16)
        slab_out[keep:, :] = tail_ref[:slab_off, :].astype(BF16)

    @pl.when(pl.program_id(1) == 0)
    def _():
        _cast_rows(wb_sc, 0, w_ref, shift, tn - shift)
        if shift:
            _cast_rows(wb_sc, tn - shift, wnext_ref, 0, shift)
        if side:
            _cast_rows(rest[-1], 0, wnext_ref, 0, LANES)

    if norm:
        xn = _rms(n_ref[...], refs[3][...]).astype(BF16)
        refs[o_idx + 1][...] = xn
    else:
        xn = n_ref[...]
    if side:
        refs[n_out - 1][...] = _dot_nt(xn, rest[-1][...])
    res = _dot_nt(xn, wb_sc[...])
    if regroup:
        r_sc = rest[0]
        tm = res.shape[0]
        for c in range(tn // LANES):
            r_sc[c] = res[:, c * LANES:(c + 1) * LANES]
        for c in range(tn // LANES):
            for j in range(regroup):
                o_ref[c, :, j * LANES:(j + 1) * LANES] = (
                    r_sc[c, pl.ds(j, tm // regroup, stride=regroup), :].astype(o_ref.dtype))
    else:
        if lead_scale is not None:
            factor, blocks = lead_scale
            res = res * jnp.where(pl.program_id(0) < blocks, factor, 1.0)
        o_ref[...] = res.astype(o_ref.dtype)


def _in_proj(n, wt, src_block, n_blocks, out_dtype, *, shift=0, regroup=0, casts=(), norm_w=None,
             side_block=None, lead_scale=None, slab=None, tm=1024, tn=1024, name):
    m, k = n.shape
    per = tn // LANES
    m_tiles = m // tm
    norm = norm_w is not None
    side = side_block is not None
    assert shift % CAST_ROWS == 0 and shift <= LANES and not (norm and n_blocks > 1)
    assert not side or (shift == 0 and n_blocks == 1)
    scratch = [pltpu.VMEM((tn, k), BF16)]
    row_tile = pl.BlockSpec((tm, k), lambda j, i: (i, 0))
    once = {"pipeline_mode": pl.Buffered(1)} if n_blocks == 1 else {}
    if regroup:
        assert n_blocks == 1
        out_shape = jax.ShapeDtypeStruct((per, m // regroup, regroup * LANES), out_dtype)
        out_spec = pl.BlockSpec((per, tm // regroup, regroup * LANES), lambda j, i: (0, i, 0))
        scratch.append(pltpu.VMEM((per, tm, LANES), F32))
    else:
        out_shape = jax.ShapeDtypeStruct((m, n_blocks * tn), out_dtype)
        out_spec = pl.BlockSpec((tm, tn), lambda j, i: (i, j))

    cast_steps = 1 << ((n_blocks * m_tiles).bit_length() - 1)
    cast_specs = []
    for a in casts:
        rows = a.shape[0] // cast_steps
        assert a.shape[0] % cast_steps == 0 and rows % CAST_ROWS == 0
        cast_specs.append(pl.BlockSpec(
            (rows, a.shape[1]), lambda j, i: (jnp.minimum(j * m_tiles + i, cast_steps - 1), 0)))

    if side:
        scratch.append(pltpu.VMEM((LANES, k), BF16))
    next_rows = (lambda j, i: (side_block, 0)) if side else (lambda j, i: ((src_block(j) + 1) * per, 0))

    slab_in, slab_out_spec, slab_out_shape, slab_args, slab_off = [], [], [], [], None
    if slab is not None:
        arr, first_row, n_rows = slab
        steps = n_blocks * m_tiles
        rps = n_rows // steps
        slab_off = first_row % rps
        base = first_row - slab_off
        assert n_rows % steps == 0 and base % rps == 0 and rps % SLAB_TAIL_ROWS == 0 and slab_off <= SLAB_TAIL_ROWS
        assert slab_off % CAST_ROWS == 0 and (rps - slab_off) % CAST_ROWS == 0
        step = lambda j, i: j * m_tiles + i
        slab_in = [
            pl.BlockSpec((rps, arr.shape[1]), lambda j, i: (base // rps + step(j, i), 0)),
            pl.BlockSpec((SLAB_TAIL_ROWS, arr.shape[1]),
                         lambda j, i: (base // SLAB_TAIL_ROWS + (rps // SLAB_TAIL_ROWS) * (step(j, i) + 1), 0)),
        ]
        slab_out_spec = [pl.BlockSpec((rps, arr.shape[1]), lambda j, i: (step(j, i), 0))]
        slab_out_shape = [jax.ShapeDtypeStruct((n_rows, arr.shape[1]), BF16)]
        slab_args = [arr, arr]

    return pl.pallas_call(
        functools.partial(_in_proj_kernel, shift=shift, regroup=regroup, n_cast=len(casts), norm=norm, side=side,
                          lead_scale=lead_scale, slab_off=slab_off),
        grid=(n_blocks, m_tiles),
        in_specs=[
            row_tile,
            pl.BlockSpec((tn, k), lambda j, i: (src_block(j), 0), **once),
            pl.BlockSpec((LANES, k), next_rows, **once),
            *([pl.BlockSpec((1, k), lambda j, i: (0, 0))] if norm else []),
            *cast_specs,
            *slab_in,
        ],
        out_specs=[out_spec, *([row_tile] if norm else []),
                   *([pl.BlockSpec((tm, LANES), lambda j, i: (i, 0))] if side else []), *cast_specs,
                   *slab_out_spec],
        out_shape=[out_shape, *([jax.ShapeDtypeStruct((m, k), BF16)] if norm else []),
                   *([jax.ShapeDtypeStruct((m, LANES), F32)] if side else []),
                   *[jax.ShapeDtypeStruct(a.shape, BF16) for a in casts], *slab_out_shape],
        scratch_shapes=scratch,
        compiler_params=_params(("arbitrary", "arbitrary"), 60 if (norm or slab is not None) else 56),
        name=name,
    )(n, wt, wt, *([norm_w.reshape(1, k)] if norm else []), *casts, *slab_args)


def _proj_kernel(n_ref, w_ref, *refs):
    n_cast = (len(refs) - 1) // 2
    for src, dst in zip(refs[:n_cast], refs[n_cast + 1:]):
        dst[...] = src[...].astype(dst.dtype)
    refs[n_cast][...] = _dot_nt(n_ref[...], w_ref[...]).astype(refs[n_cast].dtype)


def _proj(n, wt_bf16, out_dtype, *, casts=(), tm=1024, tn=2048, name):
    m, k = n.shape
    n_cols = wt_bf16.shape[0]
    n_blocks, m_tiles = n_cols // tn, m // tm
    cast_steps = 1 << ((n_blocks * m_tiles).bit_length() - 1)
    cast_specs = []
    for a in casts:
        rows = a.shape[0] // cast_steps
        assert a.shape[0] % cast_steps == 0 and rows % CAST_ROWS == 0
        cast_specs.append(pl.BlockSpec(
            (rows, a.shape[1]), lambda j, i: (jnp.minimum(j * m_tiles + i, cast_steps - 1), 0)))
    return pl.pallas_call(
        _proj_kernel,
        grid=(n_blocks, m_tiles),
        in_specs=[pl.BlockSpec((tm, k), lambda j, i: (i, 0)), pl.BlockSpec((tn, k), lambda j, i: (j, 0)),
                  *cast_specs],
        out_specs=[pl.BlockSpec((tm, tn), lambda j, i: (i, j)), *cast_specs],
        out_shape=[jax.ShapeDtypeStruct((m, n_cols), out_dtype),
                   *[jax.ShapeDtypeStruct(a.shape, BF16) for a in casts]],
        compiler_params=_params(("arbitrary", "arbitrary"), 58),
        name=name,
    )(n, wt_bf16, *casts)


def _compress_kernel(k2_ref, pe_ref, w1_ref, w2_ref, o_ref):
    rows, half = k2_ref.shape
    k2 = k2_ref[...].astype(F32)
    a_lo = (k2 + pe_ref[:, :half]).astype(BF16)
    a_hi = (k2 + pe_ref[:, half:]).astype(BF16)
    lo = _dot(a_lo, w1_ref[:half, :])
    hi = _dot(a_hi, w1_ref[half:, :])
    h = lo + pltpu.roll(hi, rows - 1, axis=0)
    act = h * jax.nn.sigmoid(h)
    o_ref[...] = _dot(act.astype(BF16), w2_ref[...]).astype(o_ref.dtype)


def _compress(k2, first_row, m, pe, w1, w2, rows=512):
    half = k2.shape[1]
    hid = w1.shape[1]
    dout = w2.shape[1]
    first = first_row // rows
    return pl.pallas_call(
        _compress_kernel,
        grid=(m // rows,),
        in_specs=[
            pl.BlockSpec((rows, half), lambda i: (first + i, 0)),
            pl.BlockSpec((1, 2 * half), lambda i: (0, 0)),
            pl.BlockSpec((2 * half, hid), lambda i: (0, 0)),
            pl.BlockSpec((hid, dout), lambda i: (0, 0)),
        ],
        out_specs=pl.BlockSpec((rows, dout), lambda i: (i, 0)),
        out_shape=jax.ShapeDtypeStruct((m, dout), BF16),
        compiler_params=_params(("parallel",), 40),
        name="nsa_compress",
    )(k2, pe, w1, w2)


def _nsa_kernel(*refs, seq, tq, kblk, gp, hm):
    q_ref = refs[0]
    kc_refs = refs[1:1 + gp]
    vc_refs = refs[1 + gp:1 + 2 * gp]
    ks_ref, vs_ref, kw_ref, vw_ref, g_ref, e_ref, o_ref, s_sc, mx_sc, acc_sc, po_sc = refs[1 + 2 * gp:]
    i = pl.program_id(2)
    t0 = i * tq
    rep = NSA_REP
    n_cmp = (seq - CMP_LEN) // CMP_STRIDE + 1
    n_sel = seq // SEL_LEN
    topk = min(SEL_TOPK, n_sel)
    groups = range(gp)

    q = q_ref[...]
    units = range(rep // hm)
    rows = lambda gg, r: slice((gg * rep + r) * tq, (gg * rep + r + 1) * tq)
    urows = lambda gg, u: slice((gg * rep + u * hm) * tq, (gg * rep + (u + 1) * hm) * tq)
    part = lambda x, h: x[h * tq:(h + 1) * tq]
    head_q = lambda gg, r: q[:, (gg * rep + r) * NSA_DK:(gg * rep + r + 1) * NSA_DK]
    unit_q = {(gg, u): jnp.concatenate([head_q(gg, u * hm + h) for h in range(hm)], axis=0)
              for gg in groups for u in units}
    group_cols = lambda ref, gg: ref.at[:, gg * LANES:(gg + 1) * LANES]

    def masked(s, mask1):
        return jnp.concatenate([jnp.where(mask1, part(s, h), NEG) for h in range(hm)], axis=0)
    tcol = t0 + lax.broadcasted_iota(jnp.int32, (tq, 1), 0)
    per_group = rep * 3
    g_sig = jax.nn.sigmoid(g_ref[...])
    first_group = pl.program_id(1) * gp
    gs = [pltpu.roll(g_sig, lax.rem(LANES - (first_group + gg) * per_group, LANES), axis=1) for gg in groups]
    gate = lambda gg, r, branch: gs[gg][:, 3 * r + branch:3 * r + branch + 1]

    def exp_rows(sm):
        return jnp.exp2(sm - jnp.max(sm, axis=-1, keepdims=True))

    def with_ones(v):
        return jnp.concatenate([v, jnp.ones(v.shape, v.dtype)], axis=1)

    def normalised(ev):
        return ev[:, :NSA_DV] / ev[:, NSA_DV:]

    c_idx = lax.broadcasted_iota(jnp.int32, (tq, LANES), 1)
    mask_c = ((c_idx * CMP_STRIDE + (CMP_LEN - 1)) <= tcol) & (c_idx < n_cmp)
    mask_cf = jnp.where(mask_c, 1.0, 0.0)
    o_cmp, psum = {}, []
    for gg in groups:
        kc = kc_refs[gg][...]
        vc = vc_refs[gg][...]
        tot = None
        for u in units:
            e = exp_rows(masked(_dot_nt(unit_q[gg, u], kc), mask_c))
            p = e / jnp.sum(e, axis=-1, keepdims=True)
            p = jnp.concatenate([part(p, h) * mask_cf for h in range(hm)], axis=0)
            o_cmp[gg, u] = _dot(p.astype(BF16), vc)
            for h in range(hm):
                tot = part(p, h) if tot is None else tot + part(p, h)
        psum.append(tot)

    wlen = WIN + tq
    start = pl.multiple_of(jnp.maximum(i - WIN // tq, 0) * tq, tq)
    dlt = tcol - (start + lax.broadcasted_iota(jnp.int32, (tq, wlen), 1))
    mask_w = (dlt >= 0) & (dlt < WIN)
    for gg in groups:
        kw = group_cols(kw_ref, gg)[pl.ds(start, wlen), :]
        vw1 = with_ones(group_cols(vw_ref, gg)[pl.ds(start, wlen), :])
        for u in units:
            e = exp_rows(masked(_dot_nt(unit_q[gg, u], kw), mask_w))
            o_win = normalised(_dot(e.astype(BF16), vw1))
            for h in range(hm):
                r = u * hm + h
                po_sc[rows(gg, r), :] = (gate(gg, r, 0) * part(o_cmp[gg, u], h)
                                         + gate(gg, r, 2) * part(o_win, h))

    jo = lax.broadcasted_iota(jnp.int32, (n_sel, LANES), 0)
    co = lax.broadcasted_iota(jnp.int32, (n_sel, LANES), 1)
    ov_t = jnp.where((co * CMP_STRIDE < jo * SEL_LEN + SEL_LEN) & (co * CMP_STRIDE + CMP_LEN > jo * SEL_LEN)
                     & (co < n_cmp), 1.0, 0.0).astype(BF16)
    j_idx = lax.broadcasted_iota(jnp.int32, (n_sel, tq), 0)
    cur = lax.shift_right_logical(t0 + lax.broadcasted_iota(jnp.int32, (n_sel, tq), 1), int(np.log2(SEL_LEN)))
    forced = (j_idx == 0) | (j_idx == cur) | (j_idx == cur - 1)
    future = j_idx > cur
    q_bias = {}
    for gg in groups:
        p_hi = psum[gg].astype(BF16)
        p_lo = (psum[gg] - p_hi.astype(F32)).astype(BF16)
        imp = _dot_nt(ov_t, p_hi) + _dot_nt(ov_t, p_lo)
        impm = jnp.where(forced, jnp.inf, jnp.where(future, -jnp.inf, imp))
        rank = jnp.zeros((n_sel, tq), F32)
        for ii in range(n_sel):
            row = impm[ii:ii + 1, :]
            beats = (row > impm) | ((row == impm) & (j_idx > ii))
            rank = rank + jnp.where(beats, 1.0, 0.0)
        bias_t = jnp.where(rank < topk, 0.0, NEG)
        bias = jnp.concatenate([bias_t, jnp.zeros((LANES - n_sel, tq), F32)], axis=0).T.astype(BF16)
        for u in units:
            q_bias[gg, u] = jnp.concatenate([unit_q[gg, u], jnp.concatenate([bias] * hm, axis=0)], axis=1)

    n_chunks = lax.div(t0 + tq + (kblk - 1), kblk)

    def lane_fold_max(x):
        out = x[:, :LANES]
        for c in range(1, kblk // LANES):
            out = jnp.maximum(out, x[:, c * LANES:(c + 1) * LANES])
        return out

    def chunk_start(kb):
        return kb * kblk if isinstance(kb, int) else pl.multiple_of(kb * kblk, kblk)

    def score_chunk(kb):
        k0 = chunk_start(kb)
        causal = (k0 + lax.broadcasted_iota(jnp.int32, (tq, kblk), 1)) <= tcol
        folds = []
        for gg in groups:
            k = jnp.concatenate([group_cols(ks_ref, gg)[pl.ds(k0, kblk), :], e_ref[kb]], axis=1)
            for u in units:
                sm_ = masked(_dot_nt(q_bias[gg, u], k), causal)
                s_sc[kb, urows(gg, u), :] = sm_
                folds.append(lane_fold_max(sm_))
        return jnp.concatenate(folds, axis=0)

    def score_pass(kb, carry):
        mx_sc[...] = jnp.maximum(mx_sc[...], score_chunk(kb))
        return carry

    mx_sc[...] = score_chunk(0)
    lax.fori_loop(1, n_chunks, score_pass, 0)
    m_sel = jnp.max(mx_sc[...], axis=-1, keepdims=True)

    def value_chunk(kb):
        k0 = chunk_start(kb)
        pvs = []
        for gg in groups:
            v1 = with_ones(group_cols(vs_ref, gg)[pl.ds(k0, kblk), :])
            for u in units:
                pk = jnp.exp2(s_sc[kb, urows(gg, u), :] - m_sel[urows(gg, u)])
                pvs.append(_dot(pk.astype(BF16), v1))
        return jnp.concatenate(pvs, axis=0)

    def value_pass(kb, carry):
        acc_sc[...] += value_chunk(kb)
        return carry

    acc_sc[...] = value_chunk(0)
    lax.fori_loop(1, n_chunks, value_pass, 0)

    for gg in groups:
        for r in range(rep):
            o = po_sc[rows(gg, r), :] + gate(gg, r, 1) * normalised(acc_sc[rows(gg, r), :])
            col = (gg * rep + r) * NSA_DV
            o_ref[:, col:col + NSA_DV] = o.astype(o_ref.dtype)


def _nsa_attention(za, zg, kc, vc, batch, seq, tq=256, kblk=512, gp=2):
    assert seq % kblk == 0 and seq >= WIN + tq and WIN % tq == 0 and NSA_GROUPS % gp == 0
    nq = seq // tq
    gw = gp * NSA_REP * NSA_DK
    streams = gp * NSA_REP * tq
    hm = max(1, MXU_ROWS // tq)
    assert NSA_REP % hm == 0
    kern = functools.partial(_nsa_kernel, seq=seq, tq=tq, kblk=kblk, gp=gp, hm=hm)
    key = np.arange(seq).reshape(seq // kblk, kblk, 1)
    expand = jnp.asarray(key // SEL_LEN == np.arange(LANES).reshape(1, 1, LANES), BF16)

    def kv_spec(off):
        return pl.BlockSpec((seq, gp * LANES), lambda b, g, i: (b, off // (gp * LANES) + g))

    def cmp_spec(gg):
        return pl.BlockSpec((LANES, NSA_DK), lambda b, g, i: ((g * gp + gg) * batch + b, 0))

    return pl.pallas_call(
        kern,
        grid=(batch, NSA_GROUPS // gp, nq),
        in_specs=[
            pl.BlockSpec((tq, gw), lambda b, g, i: (b * nq + i, A_QN // gw + g)),
            *[cmp_spec(gg) for gg in range(gp)],
            *[cmp_spec(gg) for gg in range(gp)],
            kv_spec(A_KS), kv_spec(A_VS), kv_spec(A_KW), kv_spec(A_VW),
            pl.BlockSpec((tq, LANES), lambda b, g, i: (b * nq + i, 0)),
            pl.BlockSpec(expand.shape, lambda b, g, i: (0, 0, 0)),
        ],
        out_specs=pl.BlockSpec((tq, gw), lambda b, g, i: (b * nq + i, g)),
        out_shape=jax.ShapeDtypeStruct((batch * seq, NSA_HEADS * NSA_DV), BF16),
        scratch_shapes=[
            pltpu.VMEM((seq // kblk, streams, kblk), F32),
            pltpu.VMEM((streams, LANES), F32),
            pltpu.VMEM((streams, 2 * NSA_DV), F32),
            pltpu.VMEM((streams, NSA_DV), F32),
        ],
        compiler_params=_params(("parallel", "parallel", "arbitrary"), 56),
        name="nsa_attention",
    )(za, *([kc] * gp), *([vc] * gp), za, za, za, za, zg, expand)


def _retention_kernel(q_ref, k_ref, v_ref, g_ref, cos_ref, sin_e_ref, sin_o_ref, dec_ref, wq_ref, wk_ref,
                      gc_ref, gn_ref, o_ref, st_ref):
    @pl.when(pl.program_id(1) == 0)
    def _():
        st_ref[...] = jnp.zeros(st_ref.shape, F32)

    c = RET_CHUNK
    for h in range(RET_HEADS):
        st = st_ref[h]
        wq = wq_ref[h]
        wq2 = jnp.concatenate([wq, wq], axis=1)
        cols = slice(h * RET_DV, (h + 1) * RET_DV)
        for sub in range(q_ref.shape[0] // c):
            rows = slice(sub * c, (sub + 1) * c)
            cos = cos_ref[rows, :]
            sin_e = sin_e_ref[rows, :]
            sin_o = sin_o_ref[rows, :]

            def rotate(x):
                return x * cos + pltpu.roll(x, RET_DK - 1, axis=1) * sin_e + pltpu.roll(x, 1, axis=1) * sin_o

            qf = rotate(q_ref[rows, h * RET_DK:(h + 1) * RET_DK])
            kf = rotate(k_ref[rows, h * RET_DK:(h + 1) * RET_DK]) * (RET_DK ** -0.5)
            qb = qf.astype(BF16)
            v = v_ref[rows, cols].astype(BF16)
            s = _dot_nt(qb, kf.astype(BF16)) * dec_ref[h]
            o = _dot(s.astype(BF16), v) + _dot(qb, st.astype(BF16)) * wq2
            st = st * gc_ref[h] + lax.dot_general((kf * wk_ref[h]).astype(BF16), v, TN_DIMS,
                                                  preferred_element_type=F32)
            mu = jnp.mean(o, axis=-1, keepdims=True)
            d = o - mu
            var = jnp.mean(d * d, axis=-1, keepdims=True)
            on = d * lax.rsqrt(var + EPS) * gn_ref[:, cols]
            gr = g_ref[rows, cols]
            o_ref[rows, cols] = (gr * jax.nn.sigmoid(gr) * on).astype(o_ref.dtype)
        st_ref[h] = st


def _retention(zb, gn_w, batch, seq, chunks_per_step=2):
    c = RET_CHUNK
    rows = chunks_per_step * c
    nc = seq // rows
    hq = RET_HEADS * RET_DK
    hv = RET_HEADS * RET_DV
    f32 = np.float32
    inv = f32(ROPE_BASE) ** (-np.arange(0, RET_DK, 2, dtype=f32) / f32(RET_DK))
    ang = np.arange(seq, dtype=f32)[:, None] * inv[None, :]
    zero = np.zeros_like(ang)
    pairs = lambda even, odd: np.stack([even, odd], axis=-1).reshape(seq, RET_DK)
    cos = pairs(np.cos(ang), np.cos(ang))
    sin_e = pairs(-np.sin(ang), zero)
    sin_o = pairs(zero, np.sin(ang))
    log_g = np.log1p(-np.exp2(f32(-5.0) - np.arange(RET_HEADS, dtype=f32)))
    idx = np.arange(c, dtype=f32)
    rel = idx[:, None] - idx[None, :]
    decay = np.where(rel >= 0, np.exp(log_g[:, None, None] * np.maximum(rel, f32(0.0))), f32(0.0)).astype(f32)
    lanes = lambda a: np.ascontiguousarray(np.broadcast_to(a[:, :, None], (RET_HEADS, c, RET_DK)), dtype=f32)
    w_k = lanes(np.exp(log_g[:, None] * (f32(c - 1) - idx)[None, :]))
    w_q = lanes(np.exp(log_g[:, None] * (idx + f32(1.0))[None, :]))
    g_chunk = np.broadcast_to(np.exp(log_g * f32(c))[:, None, None], (RET_HEADS, 1, RET_DV)).astype(f32)

    row = lambda b, n: b * nc + n
    return pl.pallas_call(
        _retention_kernel,
        grid=(batch, nc),
        in_specs=[
            pl.BlockSpec((rows, hq), lambda b, n: (row(b, n), B_QR // hq)),
            pl.BlockSpec((rows, hq), lambda b, n: (row(b, n), B_KR // hq)),
            pl.BlockSpec((rows, hv), lambda b, n: (row(b, n), B_VR // hv)),
            pl.BlockSpec((rows, hv), lambda b, n: (row(b, n), B_GR // hv)),
            pl.BlockSpec((rows, RET_DK), lambda b, n: (n, 0)),
            pl.BlockSpec((rows, RET_DK), lambda b, n: (n, 0)),
            pl.BlockSpec((rows, RET_DK), lambda b, n: (n, 0)),
            pl.BlockSpec((RET_HEADS, c, c), lambda b, n: (0, 0, 0)),
            pl.BlockSpec((RET_HEADS, c, RET_DK), lambda b, n: (0, 0, 0)),
            pl.BlockSpec((RET_HEADS, c, RET_DK), lambda b, n: (0, 0, 0)),
            pl.BlockSpec((RET_HEADS, 1, RET_DV), lambda b, n: (0, 0, 0)),
            pl.BlockSpec((1, hv), lambda b, n: (0, 0)),
        ],
        out_specs=pl.BlockSpec((rows, hv), lambda b, n: (row(b, n), 0)),
        out_shape=jax.ShapeDtypeStruct((batch * seq, hv), BF16),
        scratch_shapes=[pltpu.VMEM((RET_HEADS, RET_DK, RET_DV), F32)],
        compiler_params=_params(("parallel", "arbitrary"), 40),
        name="retention",
    )(zb, zb, zb, zb, cos, sin_e, sin_o, decay, w_q, w_k, g_chunk, gn_w.reshape(1, hv))


def _merge_kernel(on_ref, or_ref, ga_ref, gb_ref, x_ref, wa_ref, wb_ref, wo_ref, h_ref):
    a = _dot(on_ref[...], wa_ref[...])
    b = _dot(or_ref[...], wb_ref[...])
    merged = jax.nn.sigmoid(ga_ref[...]) * a + jax.nn.sigmoid(gb_ref[...]) * b
    h_ref[...] = x_ref[...] + _dot(merged.astype(BF16), wo_ref[...])


def _merge(o_nsa, o_ret, zb, x, w_a, w_b, w_out, tm=256):
    m, d = x.shape
    resident = lambda shape: pl.BlockSpec(shape, lambda i: (0, 0), pipeline_mode=pl.Buffered(1))
    return pl.pallas_call(
        _merge_kernel,
        grid=(m // tm,),
        in_specs=[
            pl.BlockSpec((tm, d), lambda i: (i, 0)),
            pl.BlockSpec((tm, d), lambda i: (i, 0)),
            pl.BlockSpec((tm, d), lambda i: (i, B_GA // d)),
            pl.BlockSpec((tm, d), lambda i: (i, B_GB // d)),
            pl.BlockSpec((tm, d), lambda i: (i, 0)),
            resident(w_a.shape), resident(w_b.shape), resident(w_out.shape),
        ],
        out_specs=pl.BlockSpec((tm, d), lambda i: (i, 0)),
        out_shape=jax.ShapeDtypeStruct((m, d), F32),
        compiler_params=_params(("parallel",), 56),
        name="merge_out_proj",
    )(o_nsa, o_ret, zb, zb, x, w_a, w_b, w_out)


def _mem_kv_kernel(m_ref, nw_ref, wk_ref, wv_ref, k_ref, v_ref):
    mn = _rms(m_ref[...], nw_ref[...]).astype(BF16)
    k_ref[...] = _dot(mn, wk_ref[...]).astype(k_ref.dtype)
    v_ref[...] = _dot(mn, wv_ref[...]).astype(v_ref.dtype)


def _mem_kv(mem2, nw, wk, wv, tm=256):
    m, d = mem2.shape
    n = wk.shape[1]
    out = jax.ShapeDtypeStruct((m, n), BF16)
    return pl.pallas_call(
        _mem_kv_kernel,
        grid=(m // tm,),
        in_specs=[
            pl.BlockSpec((tm, d), lambda i: (i, 0)),
            pl.BlockSpec((1, d), lambda i: (0, 0)),
            pl.BlockSpec((d, n), lambda i: (0, 0)),
            pl.BlockSpec((d, n), lambda i: (0, 0)),
        ],
        out_specs=[pl.BlockSpec((tm, n), lambda i: (i, 0))] * 2,
        out_shape=[out, out],
        compiler_params=_params(("parallel",), 40),
        name="mem_kv_proj",
    )(mem2, nw.reshape(1, d), wk, wv)


def _cross_kernel(h_ref, xw_ref, mw_ref, wq_ref, kx_ref, vx_ref, wo_ref, h2_ref, nm_ref):
    h = h_ref[...]
    nx = _rms(h, xw_ref[...]).astype(BF16)
    qx = _dot(nx, wq_ref[...]).astype(BF16)
    outs = []
    for hh in range(X_HEADS):
        cols = slice(hh * X_DH, (hh + 1) * X_DH)
        s = _dot_nt(qx[:, cols], kx_ref[:, cols]) * (X_DH ** -0.5)
        e = jnp.exp(s - jnp.max(s, axis=-1, keepdims=True))
        p = e / jnp.sum(e, axis=-1, keepdims=True)
        outs.append(_dot(p.astype(BF16), vx_ref[:, cols]))
    ox = jnp.concatenate(outs, axis=-1).astype(BF16)
    h2 = h + _dot(ox, wo_ref[...])
    h2_ref[...] = h2
    nm_ref[...] = _rms(h2, mw_ref[...]).astype(nm_ref.dtype)


def _cross_attention(h1, x_norm_w, mlp_norm_w, wq, kx, vx, wo, seq, tm=512):
    m, d = h1.shape
    n = wq.shape[1]
    per_batch = seq // tm
    vec = lambda: pl.BlockSpec((1, d), lambda i: (0, 0))
    return pl.pallas_call(
        _cross_kernel,
        grid=(m // tm,),
        in_specs=[
            pl.BlockSpec((tm, d), lambda i: (i, 0)),
            vec(), vec(),
            pl.BlockSpec((d, n), lambda i: (0, 0)),
            pl.BlockSpec((MEM_LEN, n), lambda i: (i // per_batch, 0)),
            pl.BlockSpec((MEM_LEN, n), lambda i: (i // per_batch, 0)),
            pl.BlockSpec((n, d), lambda i: (0, 0)),
        ],
        out_specs=[pl.BlockSpec((tm, d), lambda i: (i, 0))] * 2,
        out_shape=[jax.ShapeDtypeStruct((m, d), F32), jax.ShapeDtypeStruct((m, d), BF16)],
        compiler_params=_params(("parallel",), 40),
        name="cross_attention",
    )(h1, x_norm_w.reshape(1, d), mlp_norm_w.reshape(1, d), wq, kx, vx, wo)


def _mlp_kernel(nm_ref, wu_ref, wd_ref, h_ref, fw_ref, o_ref):
    j = pl.program_id(1)

    @pl.when(j == 0)
    def _():
        o_ref[...] = jnp.zeros(o_ref.shape, F32)

    u = jnp.maximum(_dot(nm_ref[...], wu_ref[...]), 0.0)
    o_ref[...] += _dot((u * u).astype(BF16), wd_ref[...])

    @pl.when(j == pl.num_programs(1) - 1)
    def _():
        o_ref[...] = _rms(h_ref[...] + o_ref[...], fw_ref[...])


def _mlp(nm, w_up, w_down, h2, final_w, tm=512, tf=2048):
    m, d = nm.shape
    f = w_up.shape[1]
    return pl.pallas_call(
        _mlp_kernel,
        grid=(m // tm, f // tf),
        in_specs=[
            pl.BlockSpec((tm, d), lambda i, j: (i, 0)),
            pl.BlockSpec((d, tf), lambda i, j: (0, j)),
            pl.BlockSpec((tf, d), lambda i, j: (j, 0)),
            pl.BlockSpec((tm, d), lambda i, j: (i, 0)),
            pl.BlockSpec((1, d), lambda i, j: (0, 0)),
        ],
        out_specs=pl.BlockSpec((tm, d), lambda i, j: (i, 0)),
        out_shape=jax.ShapeDtypeStruct((m, d), F32),
        compiler_params=_params(("parallel", "arbitrary"), 60),
        name="mlp_final_norm",
    )(nm, w_up, w_down, h2, final_w.reshape(1, d))


def _layer(h, mem, attn_norm_w, w_in, cmp_pe_k, cmp_w1_k, cmp_w2_k, cmp_pe_v, cmp_w1_v, cmp_w2_v,
           w_a, ret_gn_w, w_b, w_out, x_norm_w, mem_norm_w, wq_x, wk_x, wv_x, wo_x,
           mlp_norm_w, w_up, w_down, out_norm_w, batch, seq):
    blk = W_IN_BLOCK
    kv_block = W_IN_KV // blk
    gate_shift = NSA_HEADS * 3

    wt = w_in.T
    skip_kv = lambda j: j + (j >= kv_block).astype(jnp.int32)
    kv, n, zg, wq_b, wk_b, wv_b, wo_b, c2k_b, c2v_b = _in_proj(
        h, wt, lambda j: kv_block, 1, BF16, regroup=CMP_STRIDE, norm_w=attn_norm_w,
        side_block=W_IN_NSA_GATE // LANES, casts=(wq_x, wk_x, wv_x, wo_x, cmp_w2_k, cmp_w2_v), name="in_proj_kv")
    za, wa_b, wb_b, wout_b, c1k_b, c1v_b, wt_b = _in_proj(
        n, wt, skip_kv, 4, BF16, casts=(w_a, w_b, w_out, cmp_w1_k, cmp_w1_v),
        lead_scale=(NSA_Q_SCALE, NSA_HEADS * NSA_DK // blk),
        slab=(wt, W_IN_NSA_GATE + gate_shift, B_WIDTH), name="in_proj_a")
    zb, wup_b, wdown_b = _proj(n, wt_b, F32, casts=(w_up, w_down), name="in_proj_b")

    assert seq // CMP_STRIDE == LANES
    rows_kv = NSA_GROUPS * batch * LANES
    k2 = kv.reshape(2 * rows_kv, CMP_STRIDE * NSA_DK)
    kc = _compress(k2, 0, rows_kv, cmp_pe_k.reshape(1, -1), c1k_b, c2k_b)
    vc = _compress(k2, rows_kv, rows_kv, cmp_pe_v.reshape(1, -1), c1v_b, c2v_b)

    o_nsa = _nsa_attention(za, zg, kc, vc, batch, seq)
    o_ret = _retention(zb, ret_gn_w, batch, seq)
    h1 = _merge(o_nsa, o_ret, zb, h, wa_b, wb_b, wout_b)

    kx, vx = _mem_kv(mem.reshape(batch * MEM_LEN, D_MODEL), mem_norm_w, wk_b, wv_b)
    h2, nm = _cross_attention(h1, x_norm_w, mlp_norm_w, wq_b, kx, vx, wo_b, seq)
    return _mlp(nm, wup_b, wdown_b, h2, out_norm_w)


def kernel(x, mem, attn_norm_w, w_in, cmp_pe_k, cmp_w1_k, cmp_w2_k, cmp_pe_v, cmp_w1_v, cmp_w2_v, w_a, ret_gn_w,
           w_b, w_out, x_norm_w, mem_norm_w, wq_x, wk_x, wv_x, wo_x, mlp_norm_w, w_up, w_down, final_norm_w):
    batch, seq, d = x.shape
    depth = w_in.shape[0]
    assert depth == 1
    h = x.reshape(batch * seq, d)
    out = _layer(h, mem, attn_norm_w[0], w_in[0], cmp_pe_k[0], cmp_w1_k[0], cmp_w2_k[0],
                 cmp_pe_v[0], cmp_w1_v[0], cmp_w2_v[0], w_a[0], ret_gn_w[0], w_b[0], w_out[0],
                 x_norm_w[0], mem_norm_w[0], wq_x[0], wk_x[0], wv_x[0], wo_x[0],
                 mlp_norm_w[0], w_up[0], w_down[0], final_norm_w, batch, seq)
    return out.reshape(batch, seq, d)
```

```python
import functools

import jax
import jax.numpy as jnp
import numpy as np
from jax import lax
from jax.experimental import pallas as pl
from jax.experimental.pallas import tpu as pltpu

F32 = jnp.float32
BF16 = jnp.bfloat16

D_MODEL = 2048
MEM_LEN = 256
NSA_HEADS = 16
NSA_GROUPS = 4
NSA_REP = NSA_HEADS // NSA_GROUPS
NSA_DK = 128
NSA_DV = 128
CMP_LEN = 32
CMP_STRIDE = 16
CMP_HIDDEN = 1024
SEL_LEN = 64
SEL_TOPK = 16
WIN = 512
RET_HEADS = 8
RET_DK = 128
RET_DV = 256
RET_CHUNK = 128
ROPE_BASE = 10000.0
X_HEADS = 4
X_DH = 128
D_FF = 4 * D_MODEL
EPS = 1e-6
NEG = -1e30
LOG2E = 1.4426950408889634
NSA_Q_SCALE = NSA_DK ** -0.5 * LOG2E

LANES = 128
F32_SUBLANES = 8
MXU_ROWS = 256
V7X_VMEM_MIB = 64

W_IN_BLOCK = 1024
W_IN_KV = 2048
W_IN_NSA_GATE = 5120
A_QN = 0
A_KS = 2048
A_VS = 2560
A_KW = 3072
A_VW = 3584
B_QR = 0
B_KR = 1024
B_VR = 2048
B_GR = 4096
B_GA = 6144
B_GB = 8192
B_WIDTH = 10240

NT_DIMS = (((1,), (1,)), ((), ()))
TN_DIMS = (((0,), (0,)), ((), ()))


def _params(sem, vmem_mib):
    assert vmem_mib < V7X_VMEM_MIB
    return pltpu.CompilerParams(dimension_semantics=sem, vmem_limit_bytes=vmem_mib * 1024 * 1024)


def _rms(x, w):
    return x * lax.rsqrt(jnp.mean(x * x, axis=-1, keepdims=True) + EPS) * w


def _dot(a, b):
    return jnp.dot(a, b, preferred_element_type=F32)


def _dot_nt(a, b):
    return lax.dot_general(a, b, NT_DIMS, preferred_element_type=F32)


CAST_ROWS = 16
SLAB_TAIL_ROWS = 64


def _cast_rows(dst_ref, dst0, src_ref, src0, nrows):
    def body(r, carry):
        off = r * CAST_ROWS
        dst_ref[pl.ds(pl.multiple_of(dst0 + off, CAST_ROWS), CAST_ROWS), :] = (
            src_ref[pl.ds(pl.multiple_of(src0 + off, F32_SUBLANES), CAST_ROWS), :].astype(BF16))
        return carry

    lax.fori_loop(0, nrows // CAST_ROWS, body, 0)


def _slab_inputs(off):
    return 0 if off is None else (2 if off else 1)


def _convert_slab(in_refs, out_ref, off):
    main_ref = in_refs[0]
    keep = main_ref.shape[0] - off
    out_ref[:keep, :] = main_ref[off:, :].astype(BF16)
    if off:
        out_ref[keep:, :] = in_refs[1][:off, :].astype(BF16)


def _slab_specs(arr, n_rows, n_steps, step_of, block_of, off):
    rps = n_rows // n_steps
    cols = arr.shape[1]
    assert n_rows % n_steps == 0 and off % CAST_ROWS == 0 and (rps - off) % CAST_ROWS == 0
    ins = [pl.BlockSpec((rps, cols), lambda *g: (block_of(step_of(*g)), 0))]
    if off:
        assert rps % SLAB_TAIL_ROWS == 0 and off <= SLAB_TAIL_ROWS
        per = rps // SLAB_TAIL_ROWS
        ins.append(pl.BlockSpec((SLAB_TAIL_ROWS, cols), lambda *g: ((block_of(step_of(*g)) + 1) * per, 0)))
    out = pl.BlockSpec((rps, cols), lambda *g: (step_of(*g), 0))
    return ins, out, jax.ShapeDtypeStruct((n_rows, cols), BF16), [arr] * len(ins)


def _in_proj_kernel(*refs, shift, regroup, n_cast, norm, side, lead_scale, slab_off=None):
    n_in = 4 if norm else 3
    slab = slab_off is not None
    n_slab_in = _slab_inputs(slab_off)
    n_ref, w_ref, wnext_ref = refs[:3]
    cast_in = refs[n_in:n_in + n_cast]
    o_idx = n_in + n_cast + n_slab_in
    o_ref = refs[o_idx]
    n_out = o_idx + 1 + int(norm) + int(side)
    cast_out = refs[n_out:n_out + n_cast]
    wb_sc, *rest = refs[n_out + n_cast + int(slab):]
    tn = wb_sc.shape[0]
    for src, dst in zip(cast_in, cast_out):
        dst[...] = src[...].astype(dst.dtype)
    if slab:
        _convert_slab(refs[n_in + n_cast:o_idx], refs[n_out + n_cast], slab_off)

    @pl.when(pl.program_id(1) == 0)
    def _():
        _cast_rows(wb_sc, 0, w_ref, shift, tn - shift)
        if shift:
            _cast_rows(wb_sc, tn - shift, wnext_ref, 0, shift)
        if side:
            _cast_rows(rest[-1], 0, wnext_ref, 0, LANES)

    if norm:
        xn = _rms(n_ref[...], refs[3][...]).astype(BF16)
        refs[o_idx + 1][...] = xn
    else:
        xn = n_ref[...]
    if side:
        refs[n_out - 1][...] = _dot_nt(xn, rest[-1][...])
    if regroup:
        r_sc = rest[0]
        tm = xn.shape[0]
        pair = 2 * LANES
        for cc in range(tn // pair):
            res = _dot_nt(xn, wb_sc[cc * pair:(cc + 1) * pair, :])
            for c in (2 * cc, 2 * cc + 1):
                r_sc[c] = res[:, (c % 2) * LANES:(c % 2 + 1) * LANES]
                for j in range(regroup):
                    o_ref[c, :, j * LANES:(j + 1) * LANES] = (
                        r_sc[c, pl.ds(j, tm // regroup, stride=regroup), :].astype(o_ref.dtype))
    else:
        res = _dot_nt(xn, wb_sc[...])
        if lead_scale is not None:
            factor, blocks = lead_scale
            res = res * jnp.where(pl.program_id(0) < blocks, factor, 1.0)
        o_ref[...] = res.astype(o_ref.dtype)


def _in_proj(n, wt, src_block, n_blocks, out_dtype, *, shift=0, regroup=0, casts=(), norm_w=None,
             side_block=None, lead_scale=None, slab=None, tm=1024, tn=1024, name):
    m, k = n.shape
    per = tn // LANES
    m_tiles = m // tm
    norm = norm_w is not None
    side = side_block is not None
    assert shift % CAST_ROWS == 0 and shift <= LANES and not (norm and n_blocks > 1)
    assert not side or (shift == 0 and n_blocks == 1)
    scratch = [pltpu.VMEM((tn, k), BF16)]
    row_tile = pl.BlockSpec((tm, k), lambda j, i: (i, 0))
    once = {"pipeline_mode": pl.Buffered(1)} if n_blocks == 1 else {}
    if regroup:
        assert n_blocks == 1
        out_shape = jax.ShapeDtypeStruct((per, m // regroup, regroup * LANES), out_dtype)
        out_spec = pl.BlockSpec((per, tm // regroup, regroup * LANES), lambda j, i: (0, i, 0))
        scratch.append(pltpu.VMEM((per, tm, LANES), F32))
    else:
        out_shape = jax.ShapeDtypeStruct((m, n_blocks * tn), out_dtype)
        out_spec = pl.BlockSpec((tm, tn), lambda j, i: (i, j))

    cast_steps = 1 << ((n_blocks * m_tiles).bit_length() - 1)
    cast_specs = []
    for a in casts:
        rows = a.shape[0] // cast_steps
        assert a.shape[0] % cast_steps == 0 and rows % CAST_ROWS == 0
        cast_specs.append(pl.BlockSpec(
            (rows, a.shape[1]), lambda j, i: (jnp.minimum(j * m_tiles + i, cast_steps - 1), 0)))

    if side:
        scratch.append(pltpu.VMEM((LANES, k), BF16))
    next_rows = (lambda j, i: (side_block, 0)) if side else (lambda j, i: ((src_block(j) + 1) * per, 0))

    slab_in, slab_out_spec, slab_out_shape, slab_args, slab_off = [], [], [], [], None
    if slab is not None:
        arr, n_rows, block_of, slab_off = slab
        ins, out, shape, slab_args = _slab_specs(arr, n_rows, n_blocks * m_tiles, lambda j, i: j * m_tiles + i,
                                                 block_of, slab_off)
        slab_in, slab_out_spec, slab_out_shape = ins, [out], [shape]

    return pl.pallas_call(
        functools.partial(_in_proj_kernel, shift=shift, regroup=regroup, n_cast=len(casts), norm=norm, side=side,
                          lead_scale=lead_scale, slab_off=slab_off),
        grid=(n_blocks, m_tiles),
        in_specs=[
            row_tile,
            pl.BlockSpec((tn, k), lambda j, i: (src_block(j), 0), **once),
            pl.BlockSpec((LANES, k), next_rows, **once),
            *([pl.BlockSpec((1, k), lambda j, i: (0, 0))] if norm else []),
            *cast_specs,
            *slab_in,
        ],
        out_specs=[out_spec, *([row_tile] if norm else []),
                   *([pl.BlockSpec((tm, LANES), lambda j, i: (i, 0))] if side else []), *cast_specs,
                   *slab_out_spec],
        out_shape=[out_shape, *([jax.ShapeDtypeStruct((m, k), BF16)] if norm else []),
                   *([jax.ShapeDtypeStruct((m, LANES), F32)] if side else []),
                   *[jax.ShapeDtypeStruct(a.shape, BF16) for a in casts], *slab_out_shape],
        scratch_shapes=scratch,
        compiler_params=_params(("arbitrary", "arbitrary"), 60 if (norm or slab is not None) else 56),
        name=name,
    )(n, wt, wt, *([norm_w.reshape(1, k)] if norm else []), *casts, *slab_args)


def _proj_kernel(n_ref, w_ref, *refs, n_cast, slab_off, lead_scale):
    n_slab_in = _slab_inputs(slab_off)
    o_ref = refs[n_cast + n_slab_in]
    for src, dst in zip(refs[:n_cast], refs[n_cast + n_slab_in + 1:]):
        dst[...] = src[...].astype(dst.dtype)
    if slab_off is not None:
        _convert_slab(refs[n_cast:n_cast + n_slab_in], refs[-1], slab_off)
    res = _dot_nt(n_ref[...], w_ref[...])
    if lead_scale is not None:
        factor, blocks = lead_scale
        res = res * jnp.where(pl.program_id(0) < blocks, factor, 1.0)
    o_ref[...] = res.astype(o_ref.dtype)


def _proj(n, wt_bf16, out_dtype, *, casts=(), slab=None, lead_scale=None, tm=1024, tn=2048, name):
    m, k = n.shape
    n_cols = wt_bf16.shape[0]
    n_blocks, m_tiles = n_cols // tn, m // tm
    cast_steps = 1 << ((n_blocks * m_tiles).bit_length() - 1)
    cast_specs = []
    for a in casts:
        rows = a.shape[0] // cast_steps
        assert a.shape[0] % cast_steps == 0 and rows % CAST_ROWS == 0
        cast_specs.append(pl.BlockSpec(
            (rows, a.shape[1]), lambda j, i: (jnp.minimum(j * m_tiles + i, cast_steps - 1), 0)))
    slab_in, slab_out_spec, slab_out_shape, slab_args, slab_off = [], [], [], [], None
    if slab is not None:
        arr, n_rows, block_of, slab_off = slab
        ins, out, shape, slab_args = _slab_specs(arr, n_rows, n_blocks * m_tiles, lambda j, i: j * m_tiles + i,
                                                 block_of, slab_off)
        slab_in, slab_out_spec, slab_out_shape = ins, [out], [shape]
    return pl.pallas_call(
        functools.partial(_proj_kernel, n_cast=len(casts), slab_off=slab_off, lead_scale=lead_scale),
        grid=(n_blocks, m_tiles),
        in_specs=[pl.BlockSpec((tm, k), lambda j, i: (i, 0)), pl.BlockSpec((tn, k), lambda j, i: (j, 0)),
                  *cast_specs, *slab_in],
        out_specs=[pl.BlockSpec((tm, tn), lambda j, i: (i, j)), *cast_specs, *slab_out_spec],
        out_shape=[jax.ShapeDtypeStruct((m, n_cols), out_dtype),
                   *[jax.ShapeDtypeStruct(a.shape, BF16) for a in casts], *slab_out_shape],
        compiler_params=_params(("arbitrary", "arbitrary"), 58),
        name=name,
    )(n, wt_bf16, *casts, *slab_args)


def _compress_kernel(k2_ref, pe_ref, w1_ref, w2_ref, o_ref):
    rows, half = k2_ref.shape
    k2 = k2_ref[...].astype(F32)
    a_lo = (k2 + pe_ref[:, :half]).astype(BF16)
    a_hi = (k2 + pe_ref[:, half:]).astype(BF16)
    lo = _dot(a_lo, w1_ref[:half, :])
    hi = _dot(a_hi, w1_ref[half:, :])
    h = lo + pltpu.roll(hi, rows - 1, axis=0)
    act = h * jax.nn.sigmoid(h)
    o_ref[...] = _dot(act.astype(BF16), w2_ref[...]).astype(o_ref.dtype)


def _compress(k2, first_row, m, pe, w1, w2, rows=512):
    half = k2.shape[1]
    hid = w1.shape[1]
    dout = w2.shape[1]
    first = first_row // rows
    return pl.pallas_call(
        _compress_kernel,
        grid=(m // rows,),
        in_specs=[
            pl.BlockSpec((rows, half), lambda i: (first + i, 0)),
            pl.BlockSpec((1, 2 * half), lambda i: (0, 0)),
            pl.BlockSpec((2 * half, hid), lambda i: (0, 0)),
            pl.BlockSpec((hid, dout), lambda i: (0, 0)),
        ],
        out_specs=pl.BlockSpec((rows, dout), lambda i: (i, 0)),
        out_shape=jax.ShapeDtypeStruct((m, dout), BF16),
        compiler_params=_params(("parallel",), 40),
        name="nsa_compress",
    )(k2, pe, w1, w2)


def _nsa_kernel(*refs, seq, tq, kblk, gp, hm, n_cast):
    q_ref = refs[0]
    kc_refs = refs[1:1 + gp]
    vc_refs = refs[1 + gp:1 + 2 * gp]
    ks_ref, vs_ref, kw_ref, vw_ref, g_ref, e_ref = refs[1 + 2 * gp:7 + 2 * gp]
    cast_in = refs[7 + 2 * gp:7 + 2 * gp + n_cast]
    o_ref = refs[7 + 2 * gp + n_cast]
    cast_out = refs[8 + 2 * gp + n_cast:8 + 2 * gp + 2 * n_cast]
    s_sc, mx_sc, acc_sc, po_sc = refs[8 + 2 * gp + 2 * n_cast:]
    i = pl.program_id(2)
    t0 = i * tq
    rep = NSA_REP
    n_cmp = (seq - CMP_LEN) // CMP_STRIDE + 1
    n_sel = seq // SEL_LEN
    topk = min(SEL_TOPK, n_sel)
    groups = range(gp)

    q = q_ref[...]
    units = range(rep // hm)
    rows = lambda gg, r: slice((gg * rep + r) * tq, (gg * rep + r + 1) * tq)
    urows = lambda gg, u: slice((gg * rep + u * hm) * tq, (gg * rep + (u + 1) * hm) * tq)
    part = lambda x, h: x[h * tq:(h + 1) * tq]
    head_q = lambda gg, r: q[:, (gg * rep + r) * NSA_DK:(gg * rep + r + 1) * NSA_DK]
    unit_q = {(gg, u): jnp.concatenate([head_q(gg, u * hm + h) for h in range(hm)], axis=0)
              for gg in groups for u in units}
    group_cols = lambda ref, gg: ref.at[:, gg * LANES:(gg + 1) * LANES]

    def masked(s, mask1):
        return jnp.concatenate([jnp.where(mask1, part(s, h), NEG) for h in range(hm)], axis=0)
    tcol = t0 + lax.broadcasted_iota(jnp.int32, (tq, 1), 0)
    per_group = rep * 3
    g_sig = jax.nn.sigmoid(g_ref[...])
    first_group = pl.program_id(1) * gp
    gs = [pltpu.roll(g_sig, lax.rem(LANES - (first_group + gg) * per_group, LANES), axis=1) for gg in groups]
    gate = lambda gg, r, branch: gs[gg][:, 3 * r + branch:3 * r + branch + 1]

    def exp_rows(sm):
        return jnp.exp2(sm - jnp.max(sm, axis=-1, keepdims=True))

    def with_ones(v):
        return jnp.concatenate([v, jnp.ones(v.shape, v.dtype)], axis=1)

    def normalised(ev):
        return ev[:, :NSA_DV] / ev[:, NSA_DV:]

    c_idx = lax.broadcasted_iota(jnp.int32, (tq, LANES), 1)
    mask_c = ((c_idx * CMP_STRIDE + (CMP_LEN - 1)) <= tcol) & (c_idx < n_cmp)
    mask_cf = jnp.where(mask_c, 1.0, 0.0)
    o_cmp, psum = {}, []
    for gg in groups:
        kc = kc_refs[gg][...]
        vc = vc_refs[gg][...]
        tot = None
        for u in units:
            e = exp_rows(masked(_dot_nt(unit_q[gg, u], kc), mask_c))
            p = e / jnp.sum(e, axis=-1, keepdims=True)
            p = jnp.concatenate([part(p, h) * mask_cf for h in range(hm)], axis=0)
            o_cmp[gg, u] = _dot(p.astype(BF16), vc)
            for h in range(hm):
                tot = part(p, h) if tot is None else tot + part(p, h)
        psum.append(tot)

    wlen = WIN + tq
    start = pl.multiple_of(jnp.maximum(i - WIN // tq, 0) * tq, tq)
    dlt = tcol - (start + lax.broadcasted_iota(jnp.int32, (tq, wlen), 1))
    mask_w = (dlt >= 0) & (dlt < WIN)
    for gg in groups:
        kw = group_cols(kw_ref, gg)[pl.ds(start, wlen), :]
        vw1 = with_ones(group_cols(vw_ref, gg)[pl.ds(start, wlen), :])
        for u in units:
            e = exp_rows(masked(_dot_nt(unit_q[gg, u], kw), mask_w))
            o_win = normalised(_dot(e.astype(BF16), vw1))
            for h in range(hm):
                r = u * hm + h
                po_sc[rows(gg, r), :] = (gate(gg, r, 0) * part(o_cmp[gg, u], h)
                                         + gate(gg, r, 2) * part(o_win, h))

    jo = lax.broadcasted_iota(jnp.int32, (n_sel, LANES), 0)
    co = lax.broadcasted_iota(jnp.int32, (n_sel, LANES), 1)
    ov_t = jnp.where((co * CMP_STRIDE < jo * SEL_LEN + SEL_LEN) & (co * CMP_STRIDE + CMP_LEN > jo * SEL_LEN)
                     & (co < n_cmp), 1.0, 0.0).astype(BF16)
    j_idx = lax.broadcasted_iota(jnp.int32, (n_sel, tq), 0)
    cur = lax.shift_right_logical(t0 + lax.broadcasted_iota(jnp.int32, (n_sel, tq), 1), int(np.log2(SEL_LEN)))
    forced = (j_idx == 0) | (j_idx == cur) | (j_idx == cur - 1)
    future = j_idx > cur
    q_bias = {}
    for gg in groups:
        p_hi = psum[gg].astype(BF16)
        p_lo = (psum[gg] - p_hi.astype(F32)).astype(BF16)
        imp = _dot_nt(ov_t, p_hi) + _dot_nt(ov_t, p_lo)
        impm = jnp.where(forced, jnp.inf, jnp.where(future, -jnp.inf, imp))
        rank = jnp.zeros((n_sel, tq), F32)
        for ii in range(n_sel):
            row = impm[ii:ii + 1, :]
            beats = (row > impm) | ((row == impm) & (j_idx > ii))
            rank = rank + jnp.where(beats, 1.0, 0.0)
        bias_t = jnp.where(rank < topk, 0.0, NEG)
        bias = jnp.concatenate([bias_t, jnp.zeros((LANES - n_sel, tq), F32)], axis=0).T.astype(BF16)
        for u in units:
            q_bias[gg, u] = jnp.concatenate([unit_q[gg, u], jnp.concatenate([bias] * hm, axis=0)], axis=1)

    n_chunks = lax.div(t0 + tq + (kblk - 1), kblk)

    def lane_fold_max(x):
        out = x[:, :LANES]
        for c in range(1, kblk // LANES):
            out = jnp.maximum(out, x[:, c * LANES:(c + 1) * LANES])
        return out

    def chunk_start(kb):
        return kb * kblk if isinstance(kb, int) else pl.multiple_of(kb * kblk, kblk)

    def score_chunk(kb):
        k0 = chunk_start(kb)
        causal = (k0 + lax.broadcasted_iota(jnp.int32, (tq, kblk), 1)) <= tcol
        folds = []
        for gg in groups:
            k = jnp.concatenate([group_cols(ks_ref, gg)[pl.ds(k0, kblk), :], e_ref[kb]], axis=1)
            for u in units:
                sm_ = masked(_dot_nt(q_bias[gg, u], k), causal)
                s_sc[kb, urows(gg, u), :] = sm_
                folds.append(lane_fold_max(sm_))
        return jnp.concatenate(folds, axis=0)

    def score_pass(kb, carry):
        mx_sc[...] = jnp.maximum(mx_sc[...], score_chunk(kb))
        return carry

    mx_sc[...] = score_chunk(0)
    lax.fori_loop(1, n_chunks, score_pass, 0)
    m_sel = jnp.max(mx_sc[...], axis=-1, keepdims=True)
    for src, dst in zip(cast_in, cast_out):
        dst[...] = src[...].astype(dst.dtype)

    def value_chunk(kb):
        k0 = chunk_start(kb)
        pvs = []
        for gg in groups:
            v1 = with_ones(group_cols(vs_ref, gg)[pl.ds(k0, kblk), :])
            for u in units:
                pk = jnp.exp2(s_sc[kb, urows(gg, u), :] - m_sel[urows(gg, u)])
                pvs.append(_dot(pk.astype(BF16), v1))
        return jnp.concatenate(pvs, axis=0)

    def value_pass(kb, carry):
        acc_sc[...] += value_chunk(kb)
        return carry

    acc_sc[...] = value_chunk(0)
    lax.fori_loop(1, n_chunks, value_pass, 0)

    for gg in groups:
        for r in range(rep):
            o = po_sc[rows(gg, r), :] + gate(gg, r, 1) * normalised(acc_sc[rows(gg, r), :])
            col = (gg * rep + r) * NSA_DV
            o_ref[:, col:col + NSA_DV] = o.astype(o_ref.dtype)


def _nsa_attention(za, zg, kc, vc, batch, seq, casts=(), tq=256, kblk=512, gp=2):
    assert seq % kblk == 0 and seq >= WIN + tq and WIN % tq == 0 and NSA_GROUPS % gp == 0
    nq = seq // tq
    gw = gp * NSA_REP * NSA_DK
    streams = gp * NSA_REP * tq
    hm = max(1, MXU_ROWS // tq)
    assert NSA_REP % hm == 0
    kern = functools.partial(_nsa_kernel, seq=seq, tq=tq, kblk=kblk, gp=gp, hm=hm, n_cast=len(casts))
    n_g = NSA_GROUPS // gp
    steps = batch * n_g * nq
    cast_specs = []
    for a in casts:
        assert a.shape[0] % steps == 0 and (a.shape[0] // steps) % CAST_ROWS == 0
        cast_specs.append(pl.BlockSpec((a.shape[0] // steps, a.shape[1]),
                                       lambda b, g, i: ((b * n_g + g) * nq + i, 0)))
    key = np.arange(seq).reshape(seq // kblk, kblk, 1)
    expand = jnp.asarray(key // SEL_LEN == np.arange(LANES).reshape(1, 1, LANES), BF16)

    def kv_spec(off):
        return pl.BlockSpec((seq, gp * LANES), lambda b, g, i: (b, off // (gp * LANES) + g))

    def cmp_spec(gg):
        return pl.BlockSpec((LANES, NSA_DK), lambda b, g, i: ((g * gp + gg) * batch + b, 0))

    return pl.pallas_call(
        kern,
        grid=(batch, NSA_GROUPS // gp, nq),
        in_specs=[
            pl.BlockSpec((tq, gw), lambda b, g, i: (b * nq + i, A_QN // gw + g)),
            *[cmp_spec(gg) for gg in range(gp)],
            *[cmp_spec(gg) for gg in range(gp)],
            kv_spec(A_KS), kv_spec(A_VS), kv_spec(A_KW), kv_spec(A_VW),
            pl.BlockSpec((tq, LANES), lambda b, g, i: (b * nq + i, 0)),
            pl.BlockSpec(expand.shape, lambda b, g, i: (0, 0, 0)),
            *cast_specs,
        ],
        out_specs=[pl.BlockSpec((tq, gw), lambda b, g, i: (b * nq + i, g)), *cast_specs],
        out_shape=[jax.ShapeDtypeStruct((batch * seq, NSA_HEADS * NSA_DV), BF16),
                   *[jax.ShapeDtypeStruct(a.shape, BF16) for a in casts]],
        scratch_shapes=[
            pltpu.VMEM((seq // kblk, streams, kblk), F32),
            pltpu.VMEM((streams, LANES), F32),
            pltpu.VMEM((streams, 2 * NSA_DV), F32),
            pltpu.VMEM((streams, NSA_DV), F32),
        ],
        compiler_params=_params(("parallel", "parallel", "arbitrary"), 56),
        name="nsa_attention",
    )(za, *([kc] * gp), *([vc] * gp), za, za, za, za, zg, expand, *casts)


def _retention_kernel(q_ref, k_ref, v_ref, g_ref, cos_ref, sin_e_ref, sin_o_ref, dec_ref, wq_ref, wk_ref,
                      gc_ref, gn_ref, o_ref, st_ref):
    @pl.when(pl.program_id(1) == 0)
    def _():
        st_ref[...] = jnp.zeros(st_ref.shape, F32)

    c = RET_CHUNK
    for h in range(RET_HEADS):
        st = st_ref[h]
        wq = wq_ref[h]
        wq2 = jnp.concatenate([wq, wq], axis=1)
        cols = slice(h * RET_DV, (h + 1) * RET_DV)
        for sub in range(q_ref.shape[0] // c):
            rows = slice(sub * c, (sub + 1) * c)
            cos = cos_ref[rows, :]
            sin_e = sin_e_ref[rows, :]
            sin_o = sin_o_ref[rows, :]

            def rotate(x):
                return x * cos + pltpu.roll(x, RET_DK - 1, axis=1) * sin_e + pltpu.roll(x, 1, axis=1) * sin_o

            qf = rotate(q_ref[rows, h * RET_DK:(h + 1) * RET_DK])
            kf = rotate(k_ref[rows, h * RET_DK:(h + 1) * RET_DK]) * (RET_DK ** -0.5)
            qb = qf.astype(BF16)
            v = v_ref[rows, cols].astype(BF16)
            s = _dot_nt(qb, kf.astype(BF16)) * dec_ref[h]
            o = _dot(s.astype(BF16), v) + _dot(qb, st.astype(BF16)) * wq2
            st = st * gc_ref[h] + lax.dot_general((kf * wk_ref[h]).astype(BF16), v, TN_DIMS,
                                                  preferred_element_type=F32)
            mu = jnp.mean(o, axis=-1, keepdims=True)
            d = o - mu
            var = jnp.mean(d * d, axis=-1, keepdims=True)
            on = d * lax.rsqrt(var + EPS) * gn_ref[:, cols]
            gr = g_ref[rows, cols]
            o_ref[rows, cols] = (gr * jax.nn.sigmoid(gr) * on).astype(o_ref.dtype)
        st_ref[h] = st


def _retention(zb, gn_w, batch, seq, chunks_per_step=2):
    c = RET_CHUNK
    rows = chunks_per_step * c
    nc = seq // rows
    hq = RET_HEADS * RET_DK
    hv = RET_HEADS * RET_DV
    f32 = np.float32
    inv = f32(ROPE_BASE) ** (-np.arange(0, RET_DK, 2, dtype=f32) / f32(RET_DK))
    ang = np.arange(seq, dtype=f32)[:, None] * inv[None, :]
    zero = np.zeros_like(ang)
    pairs = lambda even, odd: np.stack([even, odd], axis=-1).reshape(seq, RET_DK)
    cos = pairs(np.cos(ang), np.cos(ang))
    sin_e = pairs(-np.sin(ang), zero)
    sin_o = pairs(zero, np.sin(ang))
    log_g = np.log1p(-np.exp2(f32(-5.0) - np.arange(RET_HEADS, dtype=f32)))
    idx = np.arange(c, dtype=f32)
    rel = idx[:, None] - idx[None, :]
    decay = np.where(rel >= 0, np.exp(log_g[:, None, None] * np.maximum(rel, f32(0.0))), f32(0.0)).astype(f32)
    lanes = lambda a: np.ascontiguousarray(np.broadcast_to(a[:, :, None], (RET_HEADS, c, RET_DK)), dtype=f32)
    w_k = lanes(np.exp(log_g[:, None] * (f32(c - 1) - idx)[None, :]))
    w_q = lanes(np.exp(log_g[:, None] * (idx + f32(1.0))[None, :]))
    g_chunk = np.broadcast_to(np.exp(log_g * f32(c))[:, None, None], (RET_HEADS, 1, RET_DV)).astype(f32)

    row = lambda b, n: b * nc + n
    return pl.pallas_call(
        _retention_kernel,
        grid=(batch, nc),
        in_specs=[
            pl.BlockSpec((rows, hq), lambda b, n: (row(b, n), B_QR // hq)),
            pl.BlockSpec((rows, hq), lambda b, n: (row(b, n), B_KR // hq)),
            pl.BlockSpec((rows, hv), lambda b, n: (row(b, n), B_VR // hv)),
            pl.BlockSpec((rows, hv), lambda b, n: (row(b, n), B_GR // hv)),
            pl.BlockSpec((rows, RET_DK), lambda b, n: (n, 0)),
            pl.BlockSpec((rows, RET_DK), lambda b, n: (n, 0)),
            pl.BlockSpec((rows, RET_DK), lambda b, n: (n, 0)),
            pl.BlockSpec((RET_HEADS, c, c), lambda b, n: (0, 0, 0)),
            pl.BlockSpec((RET_HEADS, c, RET_DK), lambda b, n: (0, 0, 0)),
            pl.BlockSpec((RET_HEADS, c, RET_DK), lambda b, n: (0, 0, 0)),
            pl.BlockSpec((RET_HEADS, 1, RET_DV), lambda b, n: (0, 0, 0)),
            pl.BlockSpec((1, hv), lambda b, n: (0, 0)),
        ],
        out_specs=pl.BlockSpec((rows, hv), lambda b, n: (row(b, n), 0)),
        out_shape=jax.ShapeDtypeStruct((batch * seq, hv), BF16),
        scratch_shapes=[pltpu.VMEM((RET_HEADS, RET_DK, RET_DV), F32)],
        compiler_params=_params(("parallel", "arbitrary"), 40),
        name="retention",
    )(zb, zb, zb, zb, cos, sin_e, sin_o, decay, w_q, w_k, g_chunk, gn_w.reshape(1, hv))


def _merge_kernel(on_ref, or_ref, ga_ref, gb_ref, x_ref, wa_ref, wb_ref, wo_ref, h_ref):
    a = _dot(on_ref[...], wa_ref[...])
    b = _dot(or_ref[...], wb_ref[...])
    merged = jax.nn.sigmoid(ga_ref[...]) * a + jax.nn.sigmoid(gb_ref[...]) * b
    h_ref[...] = x_ref[...] + _dot(merged.astype(BF16), wo_ref[...])


def _merge(o_nsa, o_ret, zb, x, w_a, w_b, w_out, tm=256):
    m, d = x.shape
    resident = lambda shape: pl.BlockSpec(shape, lambda i: (0, 0), pipeline_mode=pl.Buffered(1))
    return pl.pallas_call(
        _merge_kernel,
        grid=(m // tm,),
        in_specs=[
            pl.BlockSpec((tm, d), lambda i: (i, 0)),
            pl.BlockSpec((tm, d), lambda i: (i, 0)),
            pl.BlockSpec((tm, d), lambda i: (i, B_GA // d)),
            pl.BlockSpec((tm, d), lambda i: (i, B_GB // d)),
            pl.BlockSpec((tm, d), lambda i: (i, 0)),
            resident(w_a.shape), resident(w_b.shape), resident(w_out.shape),
        ],
        out_specs=pl.BlockSpec((tm, d), lambda i: (i, 0)),
        out_shape=jax.ShapeDtypeStruct((m, d), F32),
        compiler_params=_params(("parallel",), 56),
        name="merge_out_proj",
    )(o_nsa, o_ret, zb, zb, x, w_a, w_b, w_out)


def _mem_kv_kernel(m_ref, nw_ref, wk_ref, wv_ref, k_ref, v_ref):
    mn = _rms(m_ref[...], nw_ref[...]).astype(BF16)
    k_ref[...] = _dot(mn, wk_ref[...]).astype(k_ref.dtype)
    v_ref[...] = _dot(mn, wv_ref[...]).astype(v_ref.dtype)


def _mem_kv(mem2, nw, wk, wv, tm=256):
    m, d = mem2.shape
    n = wk.shape[1]
    out = jax.ShapeDtypeStruct((m, n), BF16)
    return pl.pallas_call(
        _mem_kv_kernel,
        grid=(m // tm,),
        in_specs=[
            pl.BlockSpec((tm, d), lambda i: (i, 0)),
            pl.BlockSpec((1, d), lambda i: (0, 0)),
            pl.BlockSpec((d, n), lambda i: (0, 0)),
            pl.BlockSpec((d, n), lambda i: (0, 0)),
        ],
        out_specs=[pl.BlockSpec((tm, n), lambda i: (i, 0))] * 2,
        out_shape=[out, out],
        compiler_params=_params(("parallel",), 40),
        name="mem_kv_proj",
    )(mem2, nw.reshape(1, d), wk, wv)


def _cross_kernel(h_ref, xw_ref, mw_ref, wq_ref, kx_ref, vx_ref, wo_ref, h2_ref, nm_ref):
    h = h_ref[...]
    nx = _rms(h, xw_ref[...]).astype(BF16)
    qx = _dot(nx, wq_ref[...]).astype(BF16)
    outs = []
    for hh in range(X_HEADS):
        cols = slice(hh * X_DH, (hh + 1) * X_DH)
        s = _dot_nt(qx[:, cols], kx_ref[:, cols]) * (X_DH ** -0.5)
        e = jnp.exp(s - jnp.max(s, axis=-1, keepdims=True))
        p = e / jnp.sum(e, axis=-1, keepdims=True)
        outs.append(_dot(p.astype(BF16), vx_ref[:, cols]))
    ox = jnp.concatenate(outs, axis=-1).astype(BF16)
    h2 = h + _dot(ox, wo_ref[...])
    h2_ref[...] = h2
    nm_ref[...] = _rms(h2, mw_ref[...]).astype(nm_ref.dtype)


def _cross_attention(h1, x_norm_w, mlp_norm_w, wq, kx, vx, wo, seq, tm=512):
    m, d = h1.shape
    n = wq.shape[1]
    per_batch = seq // tm
    vec = lambda: pl.BlockSpec((1, d), lambda i: (0, 0))
    return pl.pallas_call(
        _cross_kernel,
        grid=(m // tm,),
        in_specs=[
            pl.BlockSpec((tm, d), lambda i: (i, 0)),
            vec(), vec(),
            pl.BlockSpec((d, n), lambda i: (0, 0)),
            pl.BlockSpec((MEM_LEN, n), lambda i: (i // per_batch, 0)),
            pl.BlockSpec((MEM_LEN, n), lambda i: (i // per_batch, 0)),
            pl.BlockSpec((n, d), lambda i: (0, 0)),
        ],
        out_specs=[pl.BlockSpec((tm, d), lambda i: (i, 0))] * 2,
        out_shape=[jax.ShapeDtypeStruct((m, d), F32), jax.ShapeDtypeStruct((m, d), BF16)],
        compiler_params=_params(("parallel",), 40),
        name="cross_attention",
    )(h1, x_norm_w.reshape(1, d), mlp_norm_w.reshape(1, d), wq, kx, vx, wo)


def _mlp_kernel(nm_ref, wu_ref, wd_ref, h_ref, fw_ref, o_ref):
    j = pl.program_id(1)

    @pl.when(j == 0)
    def _():
        o_ref[...] = jnp.zeros(o_ref.shape, F32)

    u = jnp.maximum(_dot(nm_ref[...], wu_ref[...]), 0.0)
    o_ref[...] += _dot((u * u).astype(BF16), wd_ref[...])

    @pl.when(j == pl.num_programs(1) - 1)
    def _():
        o_ref[...] = _rms(h_ref[...] + o_ref[...], fw_ref[...])


def _mlp(nm, w_up, w_down, h2, final_w, tm=512, tf=2048):
    m, d = nm.shape
    f = w_up.shape[1]
    return pl.pallas_call(
        _mlp_kernel,
        grid=(m // tm, f // tf),
        in_specs=[
            pl.BlockSpec((tm, d), lambda i, j: (i, 0)),
            pl.BlockSpec((d, tf), lambda i, j: (0, j)),
            pl.BlockSpec((tf, d), lambda i, j: (j, 0)),
            pl.BlockSpec((tm, d), lambda i, j: (i, 0)),
            pl.BlockSpec((1, d), lambda i, j: (0, 0)),
        ],
        out_specs=pl.BlockSpec((tm, d), lambda i, j: (i, 0)),
        out_shape=jax.ShapeDtypeStruct((m, d), F32),
        compiler_params=_params(("parallel", "arbitrary"), 60),
        name="mlp_final_norm",
    )(nm, w_up, w_down, h2, final_w.reshape(1, d))


def _layer(h, mem, attn_norm_w, w_in, cmp_pe_k, cmp_w1_k, cmp_w2_k, cmp_pe_v, cmp_w1_v, cmp_w2_v,
           w_a, ret_gn_w, w_b, w_out, x_norm_w, mem_norm_w, wq_x, wk_x, wv_x, wo_x,
           mlp_norm_w, w_up, w_down, out_norm_w, batch, seq):
    blk = W_IN_BLOCK
    kv_block = W_IN_KV // blk
    gate_shift = NSA_HEADS * 3

    wt = w_in.T
    skip_kv = lambda j: j + (j >= kv_block).astype(jnp.int32)
    kv, n, zg, wq_b, wk_b, wv_b, wo_b, c2k_b, c2v_b = _in_proj(
        h, wt, lambda j: kv_block, 1, BF16, regroup=CMP_STRIDE, norm_w=attn_norm_w,
        side_block=W_IN_NSA_GATE // LANES, casts=(wq_x, wk_x, wv_x, wo_x, cmp_w2_k, cmp_w2_v), name="in_proj_kv")
    a_steps = 4 * (h.shape[0] // 1024)
    b_first = W_IN_NSA_GATE // (B_WIDTH // a_steps)
    za, c1k_b, c1v_b, wt_b = _in_proj(
        n, wt, skip_kv, 4, BF16, casts=(cmp_w1_k, cmp_w1_v),
        lead_scale=(NSA_Q_SCALE, NSA_HEADS * NSA_DK // blk),
        slab=(wt, B_WIDTH, lambda s: s + b_first, gate_shift), name="in_proj_a")
    zb, wup_b, wdown_b = _proj(n, wt_b, F32, casts=(w_up, w_down), name="in_proj_b")

    assert seq // CMP_STRIDE == LANES
    rows_kv = NSA_GROUPS * batch * LANES
    k2 = kv.reshape(2 * rows_kv, CMP_STRIDE * NSA_DK)
    kc = _compress(k2, 0, rows_kv, cmp_pe_k.reshape(1, -1), c1k_b, c2k_b)
    vc = _compress(k2, rows_kv, rows_kv, cmp_pe_v.reshape(1, -1), c1v_b, c2v_b)

    o_nsa, wa_b, wb_b, wout_b = _nsa_attention(za, zg, kc, vc, batch, seq, casts=(w_a, w_b, w_out))
    o_ret = _retention(zb, ret_gn_w, batch, seq)
    h1 = _merge(o_nsa, o_ret, zb, h, wa_b, wb_b, wout_b)

    kx, vx = _mem_kv(mem.reshape(batch * MEM_LEN, D_MODEL), mem_norm_w, wk_b, wv_b)
    h2, nm = _cross_attention(h1, x_norm_w, mlp_norm_w, wq_b, kx, vx, wo_b, seq)
    return _mlp(nm, wup_b, wdown_b, h2, out_norm_w)


def kernel(x, mem, attn_norm_w, w_in, cmp_pe_k, cmp_w1_k, cmp_w2_k, cmp_pe_v, cmp_w1_v, cmp_w2_v, w_a, ret_gn_w,
           w_b, w_out, x_norm_w, mem_norm_w, wq_x, wk_x, wv_x, wo_x, mlp_norm_w, w_up, w_down, final_norm_w):
    batch, seq, d = x.shape
    depth = w_in.shape[0]
    assert depth == 1
    h = x.reshape(batch * seq, d)
    out = _layer(h, mem, attn_norm_w[0], w_in[0], cmp_pe_k[0], cmp_w1_k[0], cmp_w2_k[0],
                 cmp_pe_v[0], cmp_w1_v[0], cmp_w2_v[0], w_a[0], ret_gn_w[0], w_b[0], w_out[0],
                 x_norm_w[0], mem_norm_w[0], wq_x[0], wk_x[0], wv_x[0], wo_x[0],
                 mlp_norm_w[0], w_up[0], w_down[0], final_norm_w, batch, seq)
    return out.reshape(batch, seq, d)
```

```python
import functools

import jax
import jax.numpy as jnp
import numpy as np
from jax import lax
from jax.experimental import pallas as pl
from jax.experimental.pallas import tpu as pltpu

F32 = jnp.float32
BF16 = jnp.bfloat16

D_MODEL = 2048
MEM_LEN = 256
NSA_HEADS = 16
NSA_GROUPS = 4
NSA_REP = NSA_HEADS // NSA_GROUPS
NSA_DK = 128
NSA_DV = 128
CMP_LEN = 32
CMP_STRIDE = 16
CMP_HIDDEN = 1024
SEL_LEN = 64
SEL_TOPK = 16
WIN = 512
RET_HEADS = 8
RET_DK = 128
RET_DV = 256
RET_CHUNK = 128
ROPE_BASE = 10000.0
X_HEADS = 4
X_DH = 128
D_FF = 4 * D_MODEL
EPS = 1e-6
NEG = -1e30
LOG2E = 1.4426950408889634
NSA_Q_SCALE = NSA_DK ** -0.5 * LOG2E

LANES = 128
F32_SUBLANES = 8
MXU_ROWS = 256
V7X_VMEM_MIB = 64

W_IN_BLOCK = 1024
W_IN_KV = 2048
W_IN_NSA_GATE = 5120
A_QN = 0
A_KS = 2048
A_VS = 2560
A_KW = 3072
A_VW = 3584
B_QR = 0
B_KR = 1024
B_VR = 2048
B_GR = 4096
B_GA = 6144
B_GB = 8192
B_WIDTH = 10240

NT_DIMS = (((1,), (1,)), ((), ()))
TN_DIMS = (((0,), (0,)), ((), ()))


def _params(sem, vmem_mib):
    assert vmem_mib < V7X_VMEM_MIB
    return pltpu.CompilerParams(dimension_semantics=sem, vmem_limit_bytes=vmem_mib * 1024 * 1024)


def _rms(x, w):
    return x * lax.rsqrt(jnp.mean(x * x, axis=-1, keepdims=True) + EPS) * w


def _dot(a, b):
    return jnp.dot(a, b, preferred_element_type=F32)


def _dot_nt(a, b):
    return lax.dot_general(a, b, NT_DIMS, preferred_element_type=F32)


CAST_ROWS = 16
SLAB_TAIL_ROWS = 64


def _cast_rows(dst_ref, dst0, src_ref, src0, nrows):
    def body(r, carry):
        off = r * CAST_ROWS
        dst_ref[pl.ds(pl.multiple_of(dst0 + off, CAST_ROWS), CAST_ROWS), :] = (
            src_ref[pl.ds(pl.multiple_of(src0 + off, F32_SUBLANES), CAST_ROWS), :].astype(BF16))
        return carry

    lax.fori_loop(0, nrows // CAST_ROWS, body, 0)


def _slab_inputs(off):
    return 0 if off is None else (2 if off else 1)


def _convert_slab(in_refs, out_ref, off):
    main_ref = in_refs[0]
    keep = main_ref.shape[0] - off
    out_ref[:keep, :] = main_ref[off:, :].astype(BF16)
    if off:
        out_ref[keep:, :] = in_refs[1][:off, :].astype(BF16)


def _slab_specs(arr, n_rows, n_steps, step_of, block_of, off):
    rps = n_rows // n_steps
    cols = arr.shape[1]
    assert n_rows % n_steps == 0 and off % CAST_ROWS == 0 and (rps - off) % CAST_ROWS == 0
    ins = [pl.BlockSpec((rps, cols), lambda *g: (block_of(step_of(*g)), 0))]
    if off:
        assert rps % SLAB_TAIL_ROWS == 0 and off <= SLAB_TAIL_ROWS
        per = rps // SLAB_TAIL_ROWS
        ins.append(pl.BlockSpec((SLAB_TAIL_ROWS, cols), lambda *g: ((block_of(step_of(*g)) + 1) * per, 0)))
    out = pl.BlockSpec((rps, cols), lambda *g: (step_of(*g), 0))
    return ins, out, jax.ShapeDtypeStruct((n_rows, cols), BF16), [arr] * len(ins)


def _in_proj_kernel(*refs, shift, regroup, n_cast, norm, side, lead_scale, slab_off=None):
    n_in = 4 if norm else 3
    slab = slab_off is not None
    n_slab_in = _slab_inputs(slab_off)
    n_ref, w_ref, wnext_ref = refs[:3]
    cast_in = refs[n_in:n_in + n_cast]
    o_idx = n_in + n_cast + n_slab_in
    o_ref = refs[o_idx]
    n_out = o_idx + 1 + int(norm) + int(side)
    cast_out = refs[n_out:n_out + n_cast]
    wb_sc, *rest = refs[n_out + n_cast + int(slab):]
    tn = wb_sc.shape[0]
    for src, dst in zip(cast_in, cast_out):
        dst[...] = src[...].astype(dst.dtype)
    if slab:
        _convert_slab(refs[n_in + n_cast:o_idx], refs[n_out + n_cast], slab_off)

    @pl.when(pl.program_id(1) == 0)
    def _():
        _cast_rows(wb_sc, 0, w_ref, shift, tn - shift)
        if shift:
            _cast_rows(wb_sc, tn - shift, wnext_ref, 0, shift)
        if side:
            _cast_rows(rest[-1], 0, wnext_ref, 0, LANES)

    if norm:
        xn = _rms(n_ref[...], refs[3][...]).astype(BF16)
        refs[o_idx + 1][...] = xn
    else:
        xn = n_ref[...]
    if side:
        refs[n_out - 1][...] = _dot_nt(xn, rest[-1][...])
    if regroup:
        r_sc = rest[0]
        tm = xn.shape[0]
        pair = 2 * LANES
        for cc in range(tn // pair):
            res = _dot_nt(xn, wb_sc[cc * pair:(cc + 1) * pair, :])
            for c in (2 * cc, 2 * cc + 1):
                r_sc[c] = res[:, (c % 2) * LANES:(c % 2 + 1) * LANES]
                for j in range(regroup):
                    o_ref[c, :, j * LANES:(j + 1) * LANES] = (
                        r_sc[c, pl.ds(j, tm // regroup, stride=regroup), :].astype(o_ref.dtype))
    else:
        res = _dot_nt(xn, wb_sc[...])
        if lead_scale is not None:
            factor, blocks = lead_scale
            res = res * jnp.where(pl.program_id(0) < blocks, factor, 1.0)
        o_ref[...] = res.astype(o_ref.dtype)


def _in_proj(n, wt, src_block, n_blocks, out_dtype, *, shift=0, regroup=0, casts=(), norm_w=None,
             side_block=None, lead_scale=None, slab=None, tm=1024, tn=1024, name):
    m, k = n.shape
    per = tn // LANES
    m_tiles = m // tm
    norm = norm_w is not None
    side = side_block is not None
    assert shift % CAST_ROWS == 0 and shift <= LANES and not (norm and n_blocks > 1)
    assert not side or (shift == 0 and n_blocks == 1)
    scratch = [pltpu.VMEM((tn, k), BF16)]
    row_tile = pl.BlockSpec((tm, k), lambda j, i: (i, 0))
    once = {"pipeline_mode": pl.Buffered(1)} if n_blocks == 1 else {}
    if regroup:
        assert n_blocks == 1
        out_shape = jax.ShapeDtypeStruct((per, m // regroup, regroup * LANES), out_dtype)
        out_spec = pl.BlockSpec((per, tm // regroup, regroup * LANES), lambda j, i: (0, i, 0))
        scratch.append(pltpu.VMEM((per, tm, LANES), F32))
    else:
        out_shape = jax.ShapeDtypeStruct((m, n_blocks * tn), out_dtype)
        out_spec = pl.BlockSpec((tm, tn), lambda j, i: (i, j))

    cast_steps = 1 << ((n_blocks * m_tiles).bit_length() - 1)
    cast_specs = []
    for a in casts:
        rows = a.shape[0] // cast_steps
        assert a.shape[0] % cast_steps == 0 and rows % CAST_ROWS == 0
        cast_specs.append(pl.BlockSpec(
            (rows, a.shape[1]), lambda j, i: (jnp.minimum(j * m_tiles + i, cast_steps - 1), 0)))

    if side:
        scratch.append(pltpu.VMEM((LANES, k), BF16))
    next_rows = (lambda j, i: (side_block, 0)) if side else (lambda j, i: ((src_block(j) + 1) * per, 0))

    slab_in, slab_out_spec, slab_out_shape, slab_args, slab_off = [], [], [], [], None
    if slab is not None:
        arr, n_rows, block_of, slab_off = slab
        ins, out, shape, slab_args = _slab_specs(arr, n_rows, n_blocks * m_tiles, lambda j, i: j * m_tiles + i,
                                                 block_of, slab_off)
        slab_in, slab_out_spec, slab_out_shape = ins, [out], [shape]

    return pl.pallas_call(
        functools.partial(_in_proj_kernel, shift=shift, regroup=regroup, n_cast=len(casts), norm=norm, side=side,
                          lead_scale=lead_scale, slab_off=slab_off),
        grid=(n_blocks, m_tiles),
        in_specs=[
            row_tile,
            pl.BlockSpec((tn, k), lambda j, i: (src_block(j), 0), **once),
            pl.BlockSpec((LANES, k), next_rows, **once),
            *([pl.BlockSpec((1, k), lambda j, i: (0, 0))] if norm else []),
            *cast_specs,
            *slab_in,
        ],
        out_specs=[out_spec, *([row_tile] if norm else []),
                   *([pl.BlockSpec((tm, LANES), lambda j, i: (i, 0))] if side else []), *cast_specs,
                   *slab_out_spec],
        out_shape=[out_shape, *([jax.ShapeDtypeStruct((m, k), BF16)] if norm else []),
                   *([jax.ShapeDtypeStruct((m, LANES), F32)] if side else []),
                   *[jax.ShapeDtypeStruct(a.shape, BF16) for a in casts], *slab_out_shape],
        scratch_shapes=scratch,
        compiler_params=_params(("arbitrary", "arbitrary"), 60 if (norm or slab is not None) else 56),
        name=name,
    )(n, wt, wt, *([norm_w.reshape(1, k)] if norm else []), *casts, *slab_args)


def _proj_kernel(n_ref, w_ref, *refs, n_cast, slab_off, lead_scale):
    n_slab_in = _slab_inputs(slab_off)
    o_ref = refs[n_cast + n_slab_in]
    for src, dst in zip(refs[:n_cast], refs[n_cast + n_slab_in + 1:]):
        dst[...] = src[...].astype(dst.dtype)
    if slab_off is not None:
        _convert_slab(refs[n_cast:n_cast + n_slab_in], refs[-1], slab_off)
    res = _dot_nt(n_ref[...], w_ref[...])
    if lead_scale is not None:
        factor, blocks = lead_scale
        res = res * jnp.where(pl.program_id(0) < blocks, factor, 1.0)
    o_ref[...] = res.astype(o_ref.dtype)


def _proj(n, wt_bf16, out_dtype, *, casts=(), slab=None, lead_scale=None, tm=1024, tn=2048, name):
    m, k = n.shape
    n_cols = wt_bf16.shape[0]
    n_blocks, m_tiles = n_cols // tn, m // tm
    cast_steps = 1 << ((n_blocks * m_tiles).bit_length() - 1)
    cast_specs = []
    for a in casts:
        rows = a.shape[0] // cast_steps
        assert a.shape[0] % cast_steps == 0 and rows % CAST_ROWS == 0
        cast_specs.append(pl.BlockSpec(
            (rows, a.shape[1]), lambda j, i: (jnp.minimum(j * m_tiles + i, cast_steps - 1), 0)))
    slab_in, slab_out_spec, slab_out_shape, slab_args, slab_off = [], [], [], [], None
    if slab is not None:
        arr, n_rows, block_of, slab_off = slab
        ins, out, shape, slab_args = _slab_specs(arr, n_rows, n_blocks * m_tiles, lambda j, i: j * m_tiles + i,
                                                 block_of, slab_off)
        slab_in, slab_out_spec, slab_out_shape = ins, [out], [shape]
    return pl.pallas_call(
        functools.partial(_proj_kernel, n_cast=len(casts), slab_off=slab_off, lead_scale=lead_scale),
        grid=(n_blocks, m_tiles),
        in_specs=[pl.BlockSpec((tm, k), lambda j, i: (i, 0)), pl.BlockSpec((tn, k), lambda j, i: (j, 0)),
                  *cast_specs, *slab_in],
        out_specs=[pl.BlockSpec((tm, tn), lambda j, i: (i, j)), *cast_specs, *slab_out_spec],
        out_shape=[jax.ShapeDtypeStruct((m, n_cols), out_dtype),
                   *[jax.ShapeDtypeStruct(a.shape, BF16) for a in casts], *slab_out_shape],
        compiler_params=_params(("arbitrary", "arbitrary"), 58),
        name=name,
    )(n, wt_bf16, *casts, *slab_args)


def _compress_kernel(k2_ref, pe_ref, w1_ref, w2_ref, o_ref):
    rows, half = k2_ref.shape
    k2 = k2_ref[...].astype(F32)
    a_lo = (k2 + pe_ref[:, :half]).astype(BF16)
    a_hi = (k2 + pe_ref[:, half:]).astype(BF16)
    lo = _dot(a_lo, w1_ref[:half, :])
    hi = _dot(a_hi, w1_ref[half:, :])
    h = lo + pltpu.roll(hi, rows - 1, axis=0)
    act = h * jax.nn.sigmoid(h)
    o_ref[...] = _dot(act.astype(BF16), w2_ref[...]).astype(o_ref.dtype)


def _compress(k2, first_row, m, pe, w1, w2, rows=512):
    half = k2.shape[1]
    hid = w1.shape[1]
    dout = w2.shape[1]
    first = first_row // rows
    return pl.pallas_call(
        _compress_kernel,
        grid=(m // rows,),
        in_specs=[
            pl.BlockSpec((rows, half), lambda i: (first + i, 0)),
            pl.BlockSpec((1, 2 * half), lambda i: (0, 0)),
            pl.BlockSpec((2 * half, hid), lambda i: (0, 0)),
            pl.BlockSpec((hid, dout), lambda i: (0, 0)),
        ],
        out_specs=pl.BlockSpec((rows, dout), lambda i: (i, 0)),
        out_shape=jax.ShapeDtypeStruct((m, dout), BF16),
        compiler_params=_params(("parallel",), 40),
        name="nsa_compress",
    )(k2, pe, w1, w2)


def _nsa_kernel(*refs, seq, tq, kblk, gp, hm, n_cast):
    q_ref = refs[0]
    kc_refs = refs[1:1 + gp]
    vc_refs = refs[1 + gp:1 + 2 * gp]
    ks_ref, vs_ref, kw_ref, vw_ref, g_ref, e_ref = refs[1 + 2 * gp:7 + 2 * gp]
    cast_in = refs[7 + 2 * gp:7 + 2 * gp + n_cast]
    o_ref = refs[7 + 2 * gp + n_cast]
    cast_out = refs[8 + 2 * gp + n_cast:8 + 2 * gp + 2 * n_cast]
    s_sc, mx_sc, acc_sc, po_sc = refs[8 + 2 * gp + 2 * n_cast:]
    i = pl.program_id(2)
    t0 = i * tq
    rep = NSA_REP
    n_cmp = (seq - CMP_LEN) // CMP_STRIDE + 1
    n_sel = seq // SEL_LEN
    topk = min(SEL_TOPK, n_sel)
    groups = range(gp)

    q = q_ref[...]
    units = range(rep // hm)
    rows = lambda gg, r: slice((gg * rep + r) * tq, (gg * rep + r + 1) * tq)
    urows = lambda gg, u: slice((gg * rep + u * hm) * tq, (gg * rep + (u + 1) * hm) * tq)
    part = lambda x, h: x[h * tq:(h + 1) * tq]
    head_q = lambda gg, r: q[:, (gg * rep + r) * NSA_DK:(gg * rep + r + 1) * NSA_DK]
    unit_q = {(gg, u): jnp.concatenate([head_q(gg, u * hm + h) for h in range(hm)], axis=0)
              for gg in groups for u in units}
    group_cols = lambda ref, gg: ref.at[:, gg * LANES:(gg + 1) * LANES]

    def masked(s, mask1):
        return jnp.concatenate([jnp.where(mask1, part(s, h), NEG) for h in range(hm)], axis=0)
    tcol = t0 + lax.broadcasted_iota(jnp.int32, (tq, 1), 0)
    per_group = rep * 3
    g_sig = jax.nn.sigmoid(g_ref[...])
    first_group = pl.program_id(1) * gp
    gs = [pltpu.roll(g_sig, lax.rem(LANES - (first_group + gg) * per_group, LANES), axis=1) for gg in groups]
    gate = lambda gg, r, branch: gs[gg][:, 3 * r + branch:3 * r + branch + 1]

    def exp_rows(sm):
        return jnp.exp2(sm - jnp.max(sm, axis=-1, keepdims=True))

    def with_ones(v):
        return jnp.concatenate([v, jnp.ones(v.shape, v.dtype)], axis=1)

    def normalised(ev):
        return ev[:, :NSA_DV] / ev[:, NSA_DV:]

    c_idx = lax.broadcasted_iota(jnp.int32, (tq, LANES), 1)
    mask_c = ((c_idx * CMP_STRIDE + (CMP_LEN - 1)) <= tcol) & (c_idx < n_cmp)
    mask_cf = jnp.where(mask_c, 1.0, 0.0)
    o_cmp, psum = {}, []
    for gg in groups:
        kc = kc_refs[gg][...]
        vc = vc_refs[gg][...]
        tot = None
        for u in units:
            e = exp_rows(masked(_dot_nt(unit_q[gg, u], kc), mask_c))
            p = e / jnp.sum(e, axis=-1, keepdims=True)
            p = jnp.concatenate([part(p, h) * mask_cf for h in range(hm)], axis=0)
            o_cmp[gg, u] = _dot(p.astype(BF16), vc)
            for h in range(hm):
                tot = part(p, h) if tot is None else tot + part(p, h)
        psum.append(tot)

    wlen = WIN + tq
    start = pl.multiple_of(jnp.maximum(i - WIN // tq, 0) * tq, tq)
    dlt = tcol - (start + lax.broadcasted_iota(jnp.int32, (tq, wlen), 1))
    mask_w = (dlt >= 0) & (dlt < WIN)
    for gg in groups:
        kw = group_cols(kw_ref, gg)[pl.ds(start, wlen), :]
        vw1 = with_ones(group_cols(vw_ref, gg)[pl.ds(start, wlen), :])
        for u in units:
            e = exp_rows(masked(_dot_nt(unit_q[gg, u], kw), mask_w))
            o_win = normalised(_dot(e.astype(BF16), vw1))
            for h in range(hm):
                r = u * hm + h
                po_sc[rows(gg, r), :] = (gate(gg, r, 0) * part(o_cmp[gg, u], h)
                                         + gate(gg, r, 2) * part(o_win, h))

    jo = lax.broadcasted_iota(jnp.int32, (n_sel, LANES), 0)
    co = lax.broadcasted_iota(jnp.int32, (n_sel, LANES), 1)
    ov_t = jnp.where((co * CMP_STRIDE < jo * SEL_LEN + SEL_LEN) & (co * CMP_STRIDE + CMP_LEN > jo * SEL_LEN)
                     & (co < n_cmp), 1.0, 0.0).astype(BF16)
    j_idx = lax.broadcasted_iota(jnp.int32, (n_sel, tq), 0)
    cur = lax.shift_right_logical(t0 + lax.broadcasted_iota(jnp.int32, (n_sel, tq), 1), int(np.log2(SEL_LEN)))
    forced = (j_idx == 0) | (j_idx == cur) | (j_idx == cur - 1)
    future = j_idx > cur
    q_bias = {}
    for gg in groups:
        p_hi = psum[gg].astype(BF16)
        p_lo = (psum[gg] - p_hi.astype(F32)).astype(BF16)
        imp = _dot_nt(ov_t, p_hi) + _dot_nt(ov_t, p_lo)
        impm = jnp.where(forced, jnp.inf, jnp.where(future, -jnp.inf, imp))
        rank = jnp.zeros((n_sel, tq), F32)
        for ii in range(n_sel):
            row = impm[ii:ii + 1, :]
            beats = (row > impm) | ((row == impm) & (j_idx > ii))
            rank = rank + jnp.where(beats, 1.0, 0.0)
        bias_t = jnp.where(rank < topk, 0.0, NEG)
        bias = jnp.concatenate([bias_t, jnp.zeros((LANES - n_sel, tq), F32)], axis=0).T.astype(BF16)
        for u in units:
            q_bias[gg, u] = jnp.concatenate([unit_q[gg, u], jnp.concatenate([bias] * hm, axis=0)], axis=1)

    n_chunks = lax.div(t0 + tq + (kblk - 1), kblk)

    def lane_fold_max(x):
        out = x[:, :LANES]
        for c in range(1, kblk // LANES):
            out = jnp.maximum(out, x[:, c * LANES:(c + 1) * LANES])
        return out

    def chunk_start(kb):
        return kb * kblk if isinstance(kb, int) else pl.multiple_of(kb * kblk, kblk)

    def score_chunk(kb):
        k0 = chunk_start(kb)
        causal = (k0 + lax.broadcasted_iota(jnp.int32, (tq, kblk), 1)) <= tcol
        folds = []
        for gg in groups:
            k = jnp.concatenate([group_cols(ks_ref, gg)[pl.ds(k0, kblk), :], e_ref[kb]], axis=1)
            for u in units:
                sm_ = masked(_dot_nt(q_bias[gg, u], k), causal)
                s_sc[kb, urows(gg, u), :] = sm_
                folds.append(lane_fold_max(sm_))
        return jnp.concatenate(folds, axis=0)

    def score_pass(kb, carry):
        mx_sc[...] = jnp.maximum(mx_sc[...], score_chunk(kb))
        return carry

    mx_sc[...] = score_chunk(0)
    lax.fori_loop(1, n_chunks, score_pass, 0)
    m_sel = jnp.max(mx_sc[...], axis=-1, keepdims=True)
    for src, dst in zip(cast_in, cast_out):
        dst[...] = src[...].astype(dst.dtype)

    def value_chunk(kb):
        k0 = chunk_start(kb)
        pvs = []
        for gg in groups:
            v1 = with_ones(group_cols(vs_ref, gg)[pl.ds(k0, kblk), :])
            for u in units:
                pk = jnp.exp2(s_sc[kb, urows(gg, u), :] - m_sel[urows(gg, u)])
                pvs.append(_dot(pk.astype(BF16), v1))
        return jnp.concatenate(pvs, axis=0)

    def value_pass(kb, carry):
        acc_sc[...] += value_chunk(kb)
        return carry

    acc_sc[...] = value_chunk(0)
    lax.fori_loop(1, n_chunks, value_pass, 0)

    for gg in groups:
        for r in range(rep):
            o = po_sc[rows(gg, r), :] + gate(gg, r, 1) * normalised(acc_sc[rows(gg, r), :])
            col = (gg * rep + r) * NSA_DV
            o_ref[:, col:col + NSA_DV] = o.astype(o_ref.dtype)


def _nsa_attention(za, zg, kc, vc, batch, seq, casts=(), tq=256, kblk=512, gp=2):
    assert seq % kblk == 0 and seq >= WIN + tq and WIN % tq == 0 and NSA_GROUPS % gp == 0
    nq = seq // tq
    gw = gp * NSA_REP * NSA_DK
    streams = gp * NSA_REP * tq
    hm = max(1, MXU_ROWS // tq)
    assert NSA_REP % hm == 0
    kern = functools.partial(_nsa_kernel, seq=seq, tq=tq, kblk=kblk, gp=gp, hm=hm, n_cast=len(casts))
    n_g = NSA_GROUPS // gp
    steps = batch * n_g * nq
    cast_specs = []
    for a in casts:
        assert a.shape[0] % steps == 0 and (a.shape[0] // steps) % CAST_ROWS == 0
        cast_specs.append(pl.BlockSpec((a.shape[0] // steps, a.shape[1]),
                                       lambda b, g, i: ((b * n_g + g) * nq + i, 0)))
    key = np.arange(seq).reshape(seq // kblk, kblk, 1)
    expand = jnp.asarray(key // SEL_LEN == np.arange(LANES).reshape(1, 1, LANES), BF16)

    def kv_spec(off):
        return pl.BlockSpec((seq, gp * LANES), lambda b, g, i: (b, off // (gp * LANES) + g))

    def cmp_spec(gg):
        return pl.BlockSpec((LANES, NSA_DK), lambda b, g, i: ((g * gp + gg) * batch + b, 0))

    return pl.pallas_call(
        kern,
        grid=(batch, NSA_GROUPS // gp, nq),
        in_specs=[
            pl.BlockSpec((tq, gw), lambda b, g, i: (b * nq + i, A_QN // gw + g)),
            *[cmp_spec(gg) for gg in range(gp)],
            *[cmp_spec(gg) for gg in range(gp)],
            kv_spec(A_KS), kv_spec(A_VS), kv_spec(A_KW), kv_spec(A_VW),
            pl.BlockSpec((tq, LANES), lambda b, g, i: (b * nq + i, 0)),
            pl.BlockSpec(expand.shape, lambda b, g, i: (0, 0, 0)),
            *cast_specs,
        ],
        out_specs=[pl.BlockSpec((tq, gw), lambda b, g, i: (b * nq + i, g)), *cast_specs],
        out_shape=[jax.ShapeDtypeStruct((batch * seq, NSA_HEADS * NSA_DV), BF16),
                   *[jax.ShapeDtypeStruct(a.shape, BF16) for a in casts]],
        scratch_shapes=[
            pltpu.VMEM((seq // kblk, streams, kblk), F32),
            pltpu.VMEM((streams, LANES), F32),
            pltpu.VMEM((streams, 2 * NSA_DV), F32),
            pltpu.VMEM((streams, NSA_DV), F32),
        ],
        compiler_params=_params(("parallel", "parallel", "arbitrary"), 56),
        name="nsa_attention",
    )(za, *([kc] * gp), *([vc] * gp), za, za, za, za, zg, expand, *casts)


def _retention_kernel(q_ref, k_ref, v_ref, g_ref, cos_ref, sin_e_ref, sin_o_ref, dec_ref, wq_ref, wk_ref,
                      gc_ref, gn_ref, o_ref, st_ref):
    @pl.when(pl.program_id(1) == 0)
    def _():
        st_ref[...] = jnp.zeros(st_ref.shape, F32)

    c = RET_CHUNK
    for h in range(RET_HEADS):
        st = st_ref[h]
        wq = wq_ref[h]
        wq2 = jnp.concatenate([wq, wq], axis=1)
        cols = slice(h * RET_DV, (h + 1) * RET_DV)
        for sub in range(q_ref.shape[0] // c):
            rows = slice(sub * c, (sub + 1) * c)
            cos = cos_ref[rows, :]
            sin_e = sin_e_ref[rows, :]
            sin_o = sin_o_ref[rows, :]

            def rotate(x):
                return x * cos + pltpu.roll(x, RET_DK - 1, axis=1) * sin_e + pltpu.roll(x, 1, axis=1) * sin_o

            qf = rotate(q_ref[rows, h * RET_DK:(h + 1) * RET_DK])
            kf = rotate(k_ref[rows, h * RET_DK:(h + 1) * RET_DK]) * (RET_DK ** -0.5)
            qb = qf.astype(BF16)
            v = v_ref[rows, cols].astype(BF16)
            s = _dot_nt(qb, kf.astype(BF16)) * dec_ref[h]
            o = _dot(s.astype(BF16), v) + _dot(qb, st.astype(BF16)) * wq2
            st = st * gc_ref[h] + lax.dot_general((kf * wk_ref[h]).astype(BF16), v, TN_DIMS,
                                                  preferred_element_type=F32)
            mu = jnp.mean(o, axis=-1, keepdims=True)
            d = o - mu
            var = jnp.mean(d * d, axis=-1, keepdims=True)
            on = d * lax.rsqrt(var + EPS) * gn_ref[:, cols]
            gr = g_ref[rows, cols]
            o_ref[rows, cols] = (gr * jax.nn.sigmoid(gr) * on).astype(o_ref.dtype)
        st_ref[h] = st


def _retention(zb, gn_w, batch, seq, chunks_per_step=2):
    c = RET_CHUNK
    rows = chunks_per_step * c
    nc = seq // rows
    hq = RET_HEADS * RET_DK
    hv = RET_HEADS * RET_DV
    f32 = np.float32
    inv = f32(ROPE_BASE) ** (-np.arange(0, RET_DK, 2, dtype=f32) / f32(RET_DK))
    ang = np.arange(seq, dtype=f32)[:, None] * inv[None, :]
    zero = np.zeros_like(ang)
    pairs = lambda even, odd: np.stack([even, odd], axis=-1).reshape(seq, RET_DK)
    cos = pairs(np.cos(ang), np.cos(ang))
    sin_e = pairs(-np.sin(ang), zero)
    sin_o = pairs(zero, np.sin(ang))
    log_g = np.log1p(-np.exp2(f32(-5.0) - np.arange(RET_HEADS, dtype=f32)))
    idx = np.arange(c, dtype=f32)
    rel = idx[:, None] - idx[None, :]
    decay = np.where(rel >= 0, np.exp(log_g[:, None, None] * np.maximum(rel, f32(0.0))), f32(0.0)).astype(f32)
    lanes = lambda a: np.ascontiguousarray(np.broadcast_to(a[:, :, None], (RET_HEADS, c, RET_DK)), dtype=f32)
    w_k = lanes(np.exp(log_g[:, None] * (f32(c - 1) - idx)[None, :]))
    w_q = lanes(np.exp(log_g[:, None] * (idx + f32(1.0))[None, :]))
    g_chunk = np.broadcast_to(np.exp(log_g * f32(c))[:, None, None], (RET_HEADS, 1, RET_DV)).astype(f32)

    row = lambda b, n: b * nc + n
    return pl.pallas_call(
        _retention_kernel,
        grid=(batch, nc),
        in_specs=[
            pl.BlockSpec((rows, hq), lambda b, n: (row(b, n), B_QR // hq)),
            pl.BlockSpec((rows, hq), lambda b, n: (row(b, n), B_KR // hq)),
            pl.BlockSpec((rows, hv), lambda b, n: (row(b, n), B_VR // hv)),
            pl.BlockSpec((rows, hv), lambda b, n: (row(b, n), B_GR // hv)),
            pl.BlockSpec((rows, RET_DK), lambda b, n: (n, 0)),
            pl.BlockSpec((rows, RET_DK), lambda b, n: (n, 0)),
            pl.BlockSpec((rows, RET_DK), lambda b, n: (n, 0)),
            pl.BlockSpec((RET_HEADS, c, c), lambda b, n: (0, 0, 0)),
            pl.BlockSpec((RET_HEADS, c, RET_DK), lambda b, n: (0, 0, 0)),
            pl.BlockSpec((RET_HEADS, c, RET_DK), lambda b, n: (0, 0, 0)),
            pl.BlockSpec((RET_HEADS, 1, RET_DV), lambda b, n: (0, 0, 0)),
            pl.BlockSpec((1, hv), lambda b, n: (0, 0)),
        ],
        out_specs=pl.BlockSpec((rows, hv), lambda b, n: (row(b, n), 0)),
        out_shape=jax.ShapeDtypeStruct((batch * seq, hv), BF16),
        scratch_shapes=[pltpu.VMEM((RET_HEADS, RET_DK, RET_DV), F32)],
        compiler_params=_params(("parallel", "arbitrary"), 40),
        name="retention",
    )(zb, zb, zb, zb, cos, sin_e, sin_o, decay, w_q, w_k, g_chunk, gn_w.reshape(1, hv))


def _merge_kernel(on_ref, or_ref, ga_ref, gb_ref, x_ref, wa_ref, wb_ref, wo_ref, h_ref):
    a = _dot(on_ref[...], wa_ref[...])
    b = _dot(or_ref[...], wb_ref[...])
    merged = jax.nn.sigmoid(ga_ref[...]) * a + jax.nn.sigmoid(gb_ref[...]) * b
    h_ref[...] = x_ref[...] + _dot(merged.astype(BF16), wo_ref[...])


def _merge(o_nsa, o_ret, zb, x, w_a, w_b, w_out, tm=256):
    m, d = x.shape
    resident = lambda shape: pl.BlockSpec(shape, lambda i: (0, 0), pipeline_mode=pl.Buffered(1))
    return pl.pallas_call(
        _merge_kernel,
        grid=(m // tm,),
        in_specs=[
            pl.BlockSpec((tm, d), lambda i: (i, 0)),
            pl.BlockSpec((tm, d), lambda i: (i, 0)),
            pl.BlockSpec((tm, d), lambda i: (i, B_GA // d)),
            pl.BlockSpec((tm, d), lambda i: (i, B_GB // d)),
            pl.BlockSpec((tm, d), lambda i: (i, 0)),
            resident(w_a.shape), resident(w_b.shape), resident(w_out.shape),
        ],
        out_specs=pl.BlockSpec((tm, d), lambda i: (i, 0)),
        out_shape=jax.ShapeDtypeStruct((m, d), F32),
        compiler_params=_params(("parallel",), 56),
        name="merge_out_proj",
    )(o_nsa, o_ret, zb, zb, x, w_a, w_b, w_out)


def _mem_kv_kernel(m_ref, nw_ref, wk_ref, wv_ref, k_ref, v_ref):
    mn = _rms(m_ref[...], nw_ref[...]).astype(BF16)
    k_ref[...] = _dot(mn, wk_ref[...]).astype(k_ref.dtype)
    v_ref[...] = _dot(mn, wv_ref[...]).astype(v_ref.dtype)


def _mem_kv(mem2, nw, wk, wv, tm=256):
    m, d = mem2.shape
    n = wk.shape[1]
    out = jax.ShapeDtypeStruct((m, n), BF16)
    return pl.pallas_call(
        _mem_kv_kernel,
        grid=(m // tm,),
        in_specs=[
            pl.BlockSpec((tm, d), lambda i: (i, 0)),
            pl.BlockSpec((1, d), lambda i: (0, 0)),
            pl.BlockSpec((d, n), lambda i: (0, 0)),
            pl.BlockSpec((d, n), lambda i: (0, 0)),
        ],
        out_specs=[pl.BlockSpec((tm, n), lambda i: (i, 0))] * 2,
        out_shape=[out, out],
        compiler_params=_params(("parallel",), 40),
        name="mem_kv_proj",
    )(mem2, nw.reshape(1, d), wk, wv)


def _cross_kernel(h_ref, xw_ref, mw_ref, wq_ref, kx_ref, vx_ref, wo_ref, h2_ref, nm_ref):
    h = h_ref[...]
    nx = _rms(h, xw_ref[...]).astype(BF16)
    qx = (_dot(nx, wq_ref[...]) * (X_DH ** -0.5 * LOG2E)).astype(BF16)
    outs = []
    for hh in range(X_HEADS):
        cols = slice(hh * X_DH, (hh + 1) * X_DH)
        s = _dot_nt(qx[:, cols], kx_ref[:, cols])
        e = jnp.exp2(s - jnp.max(s, axis=-1, keepdims=True))
        vx = vx_ref[:, cols]
        ev = _dot(e.astype(BF16), jnp.concatenate([vx, jnp.ones(vx.shape, vx.dtype)], axis=1))
        outs.append(ev[:, :X_DH] / ev[:, X_DH:])
    ox = jnp.concatenate(outs, axis=-1).astype(BF16)
    h2 = h + _dot(ox, wo_ref[...])
    h2_ref[...] = h2
    nm_ref[...] = _rms(h2, mw_ref[...]).astype(nm_ref.dtype)


def _cross_attention(h1, x_norm_w, mlp_norm_w, wq, kx, vx, wo, seq, tm=512):
    m, d = h1.shape
    n = wq.shape[1]
    per_batch = seq // tm
    vec = lambda: pl.BlockSpec((1, d), lambda i: (0, 0))
    return pl.pallas_call(
        _cross_kernel,
        grid=(m // tm,),
        in_specs=[
            pl.BlockSpec((tm, d), lambda i: (i, 0)),
            vec(), vec(),
            pl.BlockSpec((d, n), lambda i: (0, 0)),
            pl.BlockSpec((MEM_LEN, n), lambda i: (i // per_batch, 0)),
            pl.BlockSpec((MEM_LEN, n), lambda i: (i // per_batch, 0)),
            pl.BlockSpec((n, d), lambda i: (0, 0)),
        ],
        out_specs=[pl.BlockSpec((tm, d), lambda i: (i, 0))] * 2,
        out_shape=[jax.ShapeDtypeStruct((m, d), F32), jax.ShapeDtypeStruct((m, d), BF16)],
        compiler_params=_params(("parallel",), 40),
        name="cross_attention",
    )(h1, x_norm_w.reshape(1, d), mlp_norm_w.reshape(1, d), wq, kx, vx, wo)


def _mlp_kernel(nm_ref, wu_ref, wd_ref, h_ref, fw_ref, o_ref):
    j = pl.program_id(1)

    @pl.when(j == 0)
    def _():
        o_ref[...] = jnp.zeros(o_ref.shape, F32)

    u = jnp.maximum(_dot(nm_ref[...], wu_ref[...]), 0.0)
    o_ref[...] += _dot((u * u).astype(BF16), wd_ref[...])

    @pl.when(j == pl.num_programs(1) - 1)
    def _():
        o_ref[...] = _rms(h_ref[...] + o_ref[...], fw_ref[...])


def _mlp(nm, w_up, w_down, h2, final_w, tm=512, tf=2048):
    m, d = nm.shape
    f = w_up.shape[1]
    return pl.pallas_call(
        _mlp_kernel,
        grid=(m // tm, f // tf),
        in_specs=[
            pl.BlockSpec((tm, d), lambda i, j: (i, 0)),
            pl.BlockSpec((d, tf), lambda i, j: (0, j)),
            pl.BlockSpec((tf, d), lambda i, j: (j, 0)),
            pl.BlockSpec((tm, d), lambda i, j: (i, 0)),
            pl.BlockSpec((1, d), lambda i, j: (0, 0)),
        ],
        out_specs=pl.BlockSpec((tm, d), lambda i, j: (i, 0)),
        out_shape=jax.ShapeDtypeStruct((m, d), F32),
        compiler_params=_params(("parallel", "arbitrary"), 60),
        name="mlp_final_norm",
    )(nm, w_up, w_down, h2, final_w.reshape(1, d))


def _layer(h, mem, attn_norm_w, w_in, cmp_pe_k, cmp_w1_k, cmp_w2_k, cmp_pe_v, cmp_w1_v, cmp_w2_v,
           w_a, ret_gn_w, w_b, w_out, x_norm_w, mem_norm_w, wq_x, wk_x, wv_x, wo_x,
           mlp_norm_w, w_up, w_down, out_norm_w, batch, seq):
    blk = W_IN_BLOCK
    kv_block = W_IN_KV // blk
    gate_shift = NSA_HEADS * 3

    wt = w_in.T
    skip_kv = lambda j: j + (j >= kv_block).astype(jnp.int32)
    kv, n, zg, wo_b, c2k_b, c2v_b = _in_proj(
        h, wt, lambda j: kv_block, 1, BF16, regroup=CMP_STRIDE, norm_w=attn_norm_w,
        side_block=W_IN_NSA_GATE // LANES, casts=(wo_x, cmp_w2_k, cmp_w2_v), name="in_proj_kv")
    a_steps = 4 * (h.shape[0] // 1024)
    b_first = W_IN_NSA_GATE // (B_WIDTH // a_steps)
    za, c1k_b, c1v_b, wt_b = _in_proj(
        n, wt, skip_kv, 4, BF16, casts=(cmp_w1_k, cmp_w1_v),
        lead_scale=(NSA_Q_SCALE, NSA_HEADS * NSA_DK // blk),
        slab=(wt, B_WIDTH, lambda s: s + b_first, gate_shift), name="in_proj_a")
    zb, wup_b, wdown_b = _proj(n, wt_b, F32, casts=(w_up, w_down), name="in_proj_b")

    assert seq // CMP_STRIDE == LANES
    rows_kv = NSA_GROUPS * batch * LANES
    k2 = kv.reshape(2 * rows_kv, CMP_STRIDE * NSA_DK)
    kc = _compress(k2, 0, rows_kv, cmp_pe_k.reshape(1, -1), c1k_b, c2k_b)
    vc = _compress(k2, rows_kv, rows_kv, cmp_pe_v.reshape(1, -1), c1v_b, c2v_b)

    o_nsa, wa_b, wb_b, wout_b, wq_b, wk_b, wv_b = _nsa_attention(
        za, zg, kc, vc, batch, seq, casts=(w_a, w_b, w_out, wq_x, wk_x, wv_x))
    o_ret = _retention(zb, ret_gn_w, batch, seq)
    h1 = _merge(o_nsa, o_ret, zb, h, wa_b, wb_b, wout_b)

    kx, vx = _mem_kv(mem.reshape(batch * MEM_LEN, D_MODEL), mem_norm_w, wk_b, wv_b)
    h2, nm = _cross_attention(h1, x_norm_w, mlp_norm_w, wq_b, kx, vx, wo_b, seq)
    return _mlp(nm, wup_b, wdown_b, h2, out_norm_w)


def kernel(x, mem, attn_norm_w, w_in, cmp_pe_k, cmp_w1_k, cmp_w2_k, cmp_pe_v, cmp_w1_v, cmp_w2_v, w_a, ret_gn_w,
           w_b, w_out, x_norm_w, mem_norm_w, wq_x, wk_x, wv_x, wo_x, mlp_norm_w, w_up, w_down, final_norm_w):
    batch, seq, d = x.shape
    depth = w_in.shape[0]
    assert depth == 1
    h = x.reshape(batch * seq, d)
    out = _layer(h, mem, attn_norm_w[0], w_in[0], cmp_pe_k[0], cmp_w1_k[0], cmp_w2_k[0],
                 cmp_pe_v[0], cmp_w1_v[0], cmp_w2_v[0], w_a[0], ret_gn_w[0], w_b[0], w_out[0],
                 x_norm_w[0], mem_norm_w[0], wq_x[0], wk_x[0], wv_x[0], wo_x[0],
                 mlp_norm_w[0], w_up[0], w_down[0], final_norm_w, batch, seq)
    return out.reshape(batch, seq, d)
```

```python
import functools

import jax
import jax.numpy as jnp
import numpy as np
from jax import lax
from jax.experimental import pallas as pl
from jax.experimental.pallas import tpu as pltpu

F32 = jnp.float32
BF16 = jnp.bfloat16

D_MODEL = 2048
MEM_LEN = 256
NSA_HEADS = 16
NSA_GROUPS = 4
NSA_REP = NSA_HEADS // NSA_GROUPS
NSA_DK = 128
NSA_DV = 128
CMP_LEN = 32
CMP_STRIDE = 16
CMP_HIDDEN = 1024
SEL_LEN = 64
SEL_TOPK = 16
WIN = 512
RET_HEADS = 8
RET_DK = 128
RET_DV = 256
RET_CHUNK = 128
ROPE_BASE = 10000.0
X_HEADS = 4
X_DH = 128
D_FF = 4 * D_MODEL
EPS = 1e-6
NEG = -1e30
LOG2E = 1.4426950408889634
NSA_Q_SCALE = NSA_DK ** -0.5 * LOG2E

LANES = 128
F32_SUBLANES = 8
MXU_ROWS = 256
V7X_VMEM_MIB = 64

W_IN_BLOCK = 1024
W_IN_KV = 2048
W_IN_NSA_GATE = 5120
A_QN = 0
A_KS = 2048
A_VS = 2560
A_KW = 3072
A_VW = 3584
B_QR = 0
B_KR = 1024
B_VR = 2048
B_GR = 4096
B_GA = 6144
B_GB = 8192
B_WIDTH = 10240

NT_DIMS = (((1,), (1,)), ((), ()))
TN_DIMS = (((0,), (0,)), ((), ()))


def _params(sem, vmem_mib):
    assert vmem_mib < V7X_VMEM_MIB
    return pltpu.CompilerParams(dimension_semantics=sem, vmem_limit_bytes=vmem_mib * 1024 * 1024)


def _rms(x, w):
    return x * lax.rsqrt(jnp.mean(x * x, axis=-1, keepdims=True) + EPS) * w


def _dot(a, b):
    return jnp.dot(a, b, preferred_element_type=F32)


def _dot_nt(a, b):
    return lax.dot_general(a, b, NT_DIMS, preferred_element_type=F32)


CAST_ROWS = 16
SLAB_TAIL_ROWS = 64


def _cast_rows(dst_ref, dst0, src_ref, src0, nrows):
    def body(r, carry):
        off = r * CAST_ROWS
        dst_ref[pl.ds(pl.multiple_of(dst0 + off, CAST_ROWS), CAST_ROWS), :] = (
            src_ref[pl.ds(pl.multiple_of(src0 + off, F32_SUBLANES), CAST_ROWS), :].astype(BF16))
        return carry

    lax.fori_loop(0, nrows // CAST_ROWS, body, 0)


def _slab_inputs(off):
    return 0 if off is None else (2 if off else 1)


def _convert_slab(in_refs, out_ref, off):
    main_ref = in_refs[0]
    keep = main_ref.shape[0] - off
    out_ref[:keep, :] = main_ref[off:, :].astype(BF16)
    if off:
        out_ref[keep:, :] = in_refs[1][:off, :].astype(BF16)


def _slab_specs(arr, n_rows, n_steps, step_of, block_of, off):
    rps = n_rows // n_steps
    cols = arr.shape[1]
    assert n_rows % n_steps == 0 and off % CAST_ROWS == 0 and (rps - off) % CAST_ROWS == 0
    ins = [pl.BlockSpec((rps, cols), lambda *g: (block_of(step_of(*g)), 0))]
    if off:
        assert rps % SLAB_TAIL_ROWS == 0 and off <= SLAB_TAIL_ROWS
        per = rps // SLAB_TAIL_ROWS
        ins.append(pl.BlockSpec((SLAB_TAIL_ROWS, cols), lambda *g: ((block_of(step_of(*g)) + 1) * per, 0)))
    out = pl.BlockSpec((rps, cols), lambda *g: (step_of(*g), 0))
    return ins, out, jax.ShapeDtypeStruct((n_rows, cols), BF16), [arr] * len(ins)


def _in_proj_kernel(*refs, shift, regroup, n_cast, norm, side, lead_scale, slab_off=None):
    n_in = 4 if norm else 3
    slab = slab_off is not None
    n_slab_in = _slab_inputs(slab_off)
    n_ref, w_ref, wnext_ref = refs[:3]
    cast_in = refs[n_in:n_in + n_cast]
    o_idx = n_in + n_cast + n_slab_in
    o_ref = refs[o_idx]
    n_out = o_idx + 1 + int(norm) + int(side)
    cast_out = refs[n_out:n_out + n_cast]
    wb_sc, *rest = refs[n_out + n_cast + int(slab):]
    tn = wb_sc.shape[0]
    for src, dst in zip(cast_in, cast_out):
        dst[...] = src[...].astype(dst.dtype)
    if slab:
        _convert_slab(refs[n_in + n_cast:o_idx], refs[n_out + n_cast], slab_off)

    @pl.when(pl.program_id(1) == 0)
    def _():
        _cast_rows(wb_sc, 0, w_ref, shift, tn - shift)
        if shift:
            _cast_rows(wb_sc, tn - shift, wnext_ref, 0, shift)
        if side:
            _cast_rows(rest[-1], 0, wnext_ref, 0, LANES)

    if norm:
        xn = _rms(n_ref[...], refs[3][...]).astype(BF16)
        refs[o_idx + 1][...] = xn
    else:
        xn = n_ref[...]
    if side:
        refs[n_out - 1][...] = _dot_nt(xn, rest[-1][...])
    if regroup:
        r_sc = rest[0]
        tm = xn.shape[0]
        pair = 2 * LANES
        for cc in range(tn // pair):
            res = _dot_nt(xn, wb_sc[cc * pair:(cc + 1) * pair, :])
            for c in (2 * cc, 2 * cc + 1):
                r_sc[c] = res[:, (c % 2) * LANES:(c % 2 + 1) * LANES]
                for j in range(regroup):
                    o_ref[c, :, j * LANES:(j + 1) * LANES] = (
                        r_sc[c, pl.ds(j, tm // regroup, stride=regroup), :].astype(o_ref.dtype))
    else:
        res = _dot_nt(xn, wb_sc[...])
        if lead_scale is not None:
            factor, blocks = lead_scale
            res = res * jnp.where(pl.program_id(0) < blocks, factor, 1.0)
        o_ref[...] = res.astype(o_ref.dtype)


def _in_proj(n, wt, src_block, n_blocks, out_dtype, *, shift=0, regroup=0, casts=(), norm_w=None,
             side_block=None, lead_scale=None, slab=None, tm=1024, tn=1024, name):
    m, k = n.shape
    per = tn // LANES
    m_tiles = m // tm
    norm = norm_w is not None
    side = side_block is not None
    assert shift % CAST_ROWS == 0 and shift <= LANES and not (norm and n_blocks > 1)
    assert not side or (shift == 0 and n_blocks == 1)
    scratch = [pltpu.VMEM((tn, k), BF16)]
    row_tile = pl.BlockSpec((tm, k), lambda j, i: (i, 0))
    once = {"pipeline_mode": pl.Buffered(1)} if n_blocks == 1 else {}
    if regroup:
        assert n_blocks == 1
        out_shape = jax.ShapeDtypeStruct((per, m // regroup, regroup * LANES), out_dtype)
        out_spec = pl.BlockSpec((per, tm // regroup, regroup * LANES), lambda j, i: (0, i, 0))
        scratch.append(pltpu.VMEM((per, tm, LANES), F32))
    else:
        out_shape = jax.ShapeDtypeStruct((m, n_blocks * tn), out_dtype)
        out_spec = pl.BlockSpec((tm, tn), lambda j, i: (i, j))

    cast_steps = 1 << ((n_blocks * m_tiles).bit_length() - 1)
    cast_specs = []
    for a in casts:
        rows = a.shape[0] // cast_steps
        assert a.shape[0] % cast_steps == 0 and rows % CAST_ROWS == 0
        cast_specs.append(pl.BlockSpec(
            (rows, a.shape[1]), lambda j, i: (jnp.minimum(j * m_tiles + i, cast_steps - 1), 0)))

    if side:
        scratch.append(pltpu.VMEM((LANES, k), BF16))
    next_rows = (lambda j, i: (side_block, 0)) if side else (lambda j, i: ((src_block(j) + 1) * per, 0))

    slab_in, slab_out_spec, slab_out_shape, slab_args, slab_off = [], [], [], [], None
    if slab is not None:
        arr, n_rows, block_of, slab_off = slab
        ins, out, shape, slab_args = _slab_specs(arr, n_rows, n_blocks * m_tiles, lambda j, i: j * m_tiles + i,
                                                 block_of, slab_off)
        slab_in, slab_out_spec, slab_out_shape = ins, [out], [shape]

    return pl.pallas_call(
        functools.partial(_in_proj_kernel, shift=shift, regroup=regroup, n_cast=len(casts), norm=norm, side=side,
                          lead_scale=lead_scale, slab_off=slab_off),
        grid=(n_blocks, m_tiles),
        in_specs=[
            row_tile,
            pl.BlockSpec((tn, k), lambda j, i: (src_block(j), 0), **once),
            pl.BlockSpec((LANES, k), next_rows, **once),
            *([pl.BlockSpec((1, k), lambda j, i: (0, 0))] if norm else []),
            *cast_specs,
            *slab_in,
        ],
        out_specs=[out_spec, *([row_tile] if norm else []),
                   *([pl.BlockSpec((tm, LANES), lambda j, i: (i, 0))] if side else []), *cast_specs,
                   *slab_out_spec],
        out_shape=[out_shape, *([jax.ShapeDtypeStruct((m, k), BF16)] if norm else []),
                   *([jax.ShapeDtypeStruct((m, LANES), F32)] if side else []),
                   *[jax.ShapeDtypeStruct(a.shape, BF16) for a in casts], *slab_out_shape],
        scratch_shapes=scratch,
        compiler_params=_params(("arbitrary", "arbitrary"), 60 if (norm or slab is not None) else 56),
        name=name,
    )(n, wt, wt, *([norm_w.reshape(1, k)] if norm else []), *casts, *slab_args)


def _proj_kernel(n_ref, w_ref, *refs, n_cast, slab_off, lead_scale):
    n_slab_in = _slab_inputs(slab_off)
    o_ref = refs[n_cast + n_slab_in]
    for src, dst in zip(refs[:n_cast], refs[n_cast + n_slab_in + 1:]):
        dst[...] = src[...].astype(dst.dtype)
    if slab_off is not None:
        _convert_slab(refs[n_cast:n_cast + n_slab_in], refs[-1], slab_off)
    res = _dot_nt(n_ref[...], w_ref[...])
    if lead_scale is not None:
        factor, blocks = lead_scale
        res = res * jnp.where(pl.program_id(0) < blocks, factor, 1.0)
    o_ref[...] = res.astype(o_ref.dtype)


def _proj(n, wt_bf16, out_dtype, *, casts=(), slab=None, lead_scale=None, tm=1024, tn=2048, name):
    m, k = n.shape
    n_cols = wt_bf16.shape[0]
    n_blocks, m_tiles = n_cols // tn, m // tm
    cast_steps = 1 << ((n_blocks * m_tiles).bit_length() - 1)
    cast_specs = []
    for a in casts:
        rows = a.shape[0] // cast_steps
        assert a.shape[0] % cast_steps == 0 and rows % CAST_ROWS == 0
        cast_specs.append(pl.BlockSpec(
            (rows, a.shape[1]), lambda j, i: (jnp.minimum(j * m_tiles + i, cast_steps - 1), 0)))
    slab_in, slab_out_spec, slab_out_shape, slab_args, slab_off = [], [], [], [], None
    if slab is not None:
        arr, n_rows, block_of, slab_off = slab
        ins, out, shape, slab_args = _slab_specs(arr, n_rows, n_blocks * m_tiles, lambda j, i: j * m_tiles + i,
                                                 block_of, slab_off)
        slab_in, slab_out_spec, slab_out_shape = ins, [out], [shape]
    return pl.pallas_call(
        functools.partial(_proj_kernel, n_cast=len(casts), slab_off=slab_off, lead_scale=lead_scale),
        grid=(n_blocks, m_tiles),
        in_specs=[pl.BlockSpec((tm, k), lambda j, i: (i, 0)), pl.BlockSpec((tn, k), lambda j, i: (j, 0)),
                  *cast_specs, *slab_in],
        out_specs=[pl.BlockSpec((tm, tn), lambda j, i: (i, j)), *cast_specs, *slab_out_spec],
        out_shape=[jax.ShapeDtypeStruct((m, n_cols), out_dtype),
                   *[jax.ShapeDtypeStruct(a.shape, BF16) for a in casts], *slab_out_shape],
        compiler_params=_params(("arbitrary", "arbitrary"), 58),
        name=name,
    )(n, wt_bf16, *casts, *slab_args)


def _compress_kernel(k2_ref, pe_ref, w1_ref, w2_ref, o_ref):
    rows, half = k2_ref.shape
    k2 = k2_ref[...].astype(F32)
    a_lo = (k2 + pe_ref[:, :half]).astype(BF16)
    a_hi = (k2 + pe_ref[:, half:]).astype(BF16)
    lo = _dot(a_lo, w1_ref[:half, :])
    hi = _dot(a_hi, w1_ref[half:, :])
    h = lo + pltpu.roll(hi, rows - 1, axis=0)
    act = h * jax.nn.sigmoid(h)
    o_ref[...] = _dot(act.astype(BF16), w2_ref[...]).astype(o_ref.dtype)


def _compress(k2, first_row, m, pe, w1, w2, rows=512):
    half = k2.shape[1]
    hid = w1.shape[1]
    dout = w2.shape[1]
    first = first_row // rows
    return pl.pallas_call(
        _compress_kernel,
        grid=(m // rows,),
        in_specs=[
            pl.BlockSpec((rows, half), lambda i: (first + i, 0)),
            pl.BlockSpec((1, 2 * half), lambda i: (0, 0)),
            pl.BlockSpec((2 * half, hid), lambda i: (0, 0)),
            pl.BlockSpec((hid, dout), lambda i: (0, 0)),
        ],
        out_specs=pl.BlockSpec((rows, dout), lambda i: (i, 0)),
        out_shape=jax.ShapeDtypeStruct((m, dout), BF16),
        compiler_params=_params(("parallel",), 40),
        name="nsa_compress",
    )(k2, pe, w1, w2)


def _nsa_kernel(*refs, seq, tq, kblk, gp, hm, n_cast):
    q_ref = refs[0]
    kc_refs = refs[1:1 + gp]
    vc_refs = refs[1 + gp:1 + 2 * gp]
    ks_ref, vs_ref, kw_ref, vw_ref, g_ref, e_ref = refs[1 + 2 * gp:7 + 2 * gp]
    cast_in = refs[7 + 2 * gp:7 + 2 * gp + n_cast]
    o_ref = refs[7 + 2 * gp + n_cast]
    cast_out = refs[8 + 2 * gp + n_cast:8 + 2 * gp + 2 * n_cast]
    s_sc, mx_sc, acc_sc, po_sc = refs[8 + 2 * gp + 2 * n_cast:]
    i = pl.program_id(2)
    t0 = i * tq
    rep = NSA_REP
    n_cmp = (seq - CMP_LEN) // CMP_STRIDE + 1
    n_sel = seq // SEL_LEN
    topk = min(SEL_TOPK, n_sel)
    groups = range(gp)

    q = q_ref[...]
    units = range(rep // hm)
    rows = lambda gg, r: slice((gg * rep + r) * tq, (gg * rep + r + 1) * tq)
    urows = lambda gg, u: slice((gg * rep + u * hm) * tq, (gg * rep + (u + 1) * hm) * tq)
    part = lambda x, h: x[h * tq:(h + 1) * tq]
    head_q = lambda gg, r: q[:, (gg * rep + r) * NSA_DK:(gg * rep + r + 1) * NSA_DK]
    unit_q = {(gg, u): jnp.concatenate([head_q(gg, u * hm + h) for h in range(hm)], axis=0)
              for gg in groups for u in units}
    group_cols = lambda ref, gg: ref.at[:, gg * LANES:(gg + 1) * LANES]

    def masked(s, mask1):
        return jnp.concatenate([jnp.where(mask1, part(s, h), NEG) for h in range(hm)], axis=0)
    tcol = t0 + lax.broadcasted_iota(jnp.int32, (tq, 1), 0)
    per_group = rep * 3
    g_sig = jax.nn.sigmoid(g_ref[...])
    first_group = pl.program_id(1) * gp
    gs = [pltpu.roll(g_sig, lax.rem(LANES - (first_group + gg) * per_group, LANES), axis=1) for gg in groups]
    gate = lambda gg, r, branch: gs[gg][:, 3 * r + branch:3 * r + branch + 1]

    def exp_rows(sm):
        return jnp.exp2(sm - jnp.max(sm, axis=-1, keepdims=True))

    def with_ones(v):
        return jnp.concatenate([v, jnp.ones(v.shape, v.dtype)], axis=1)

    def normalised(ev):
        return ev[:, :NSA_DV] / ev[:, NSA_DV:]

    c_idx = lax.broadcasted_iota(jnp.int32, (tq, LANES), 1)
    mask_c = ((c_idx * CMP_STRIDE + (CMP_LEN - 1)) <= tcol) & (c_idx < n_cmp)
    mask_cf = jnp.where(mask_c, 1.0, 0.0)
    o_cmp, psum = {}, []
    for gg in groups:
        kc = kc_refs[gg][...]
        vc = vc_refs[gg][...]
        tot = None
        for u in units:
            e = exp_rows(masked(_dot_nt(unit_q[gg, u], kc), mask_c))
            p = e / jnp.sum(e, axis=-1, keepdims=True)
            p = jnp.concatenate([part(p, h) * mask_cf for h in range(hm)], axis=0)
            o_cmp[gg, u] = _dot(p.astype(BF16), vc)
            for h in range(hm):
                tot = part(p, h) if tot is None else tot + part(p, h)
        psum.append(tot)

    wlen = WIN + tq
    start = pl.multiple_of(jnp.maximum(i - WIN // tq, 0) * tq, tq)
    dlt = tcol - (start + lax.broadcasted_iota(jnp.int32, (tq, wlen), 1))
    mask_w = (dlt >= 0) & (dlt < WIN)
    for gg in groups:
        kw = group_cols(kw_ref, gg)[pl.ds(start, wlen), :]
        vw1 = with_ones(group_cols(vw_ref, gg)[pl.ds(start, wlen), :])
        for u in units:
            e = exp_rows(masked(_dot_nt(unit_q[gg, u], kw), mask_w))
            o_win = normalised(_dot(e.astype(BF16), vw1))
            for h in range(hm):
                r = u * hm + h
                po_sc[rows(gg, r), :] = (gate(gg, r, 0) * part(o_cmp[gg, u], h)
                                         + gate(gg, r, 2) * part(o_win, h))

    jo = lax.broadcasted_iota(jnp.int32, (n_sel, LANES), 0)
    co = lax.broadcasted_iota(jnp.int32, (n_sel, LANES), 1)
    ov_t = jnp.where((co * CMP_STRIDE < jo * SEL_LEN + SEL_LEN) & (co * CMP_STRIDE + CMP_LEN > jo * SEL_LEN)
                     & (co < n_cmp), 1.0, 0.0).astype(BF16)
    j_idx = lax.broadcasted_iota(jnp.int32, (n_sel, tq), 0)
    cur = lax.shift_right_logical(t0 + lax.broadcasted_iota(jnp.int32, (n_sel, tq), 1), int(np.log2(SEL_LEN)))
    forced = (j_idx == 0) | (j_idx == cur) | (j_idx == cur - 1)
    future = j_idx > cur
    q_bias = {}
    for gg in groups:
        p_hi = psum[gg].astype(BF16)
        p_lo = (psum[gg] - p_hi.astype(F32)).astype(BF16)
        imp = _dot_nt(ov_t, p_hi) + _dot_nt(ov_t, p_lo)
        impm = jnp.where(forced, jnp.inf, jnp.where(future, -jnp.inf, imp))
        rank = jnp.zeros((n_sel, tq), F32)
        for ii in range(n_sel):
            row = impm[ii:ii + 1, :]
            beats = (row > impm) | ((row == impm) & (j_idx > ii))
            rank = rank + jnp.where(beats, 1.0, 0.0)
        bias_t = jnp.where(rank < topk, 0.0, NEG)
        bias = jnp.concatenate([bias_t, jnp.zeros((LANES - n_sel, tq), F32)], axis=0).T.astype(BF16)
        for u in units:
            q_bias[gg, u] = jnp.concatenate([unit_q[gg, u], jnp.concatenate([bias] * hm, axis=0)], axis=1)

    n_chunks = lax.div(t0 + tq + (kblk - 1), kblk)

    def lane_fold_max(x):
        out = x[:, :LANES]
        for c in range(1, kblk // LANES):
            out = jnp.maximum(out, x[:, c * LANES:(c + 1) * LANES])
        return out

    def chunk_start(kb):
        return kb * kblk if isinstance(kb, int) else pl.multiple_of(kb * kblk, kblk)

    def score_chunk(kb):
        k0 = chunk_start(kb)
        causal = (k0 + lax.broadcasted_iota(jnp.int32, (tq, kblk), 1)) <= tcol
        folds = []
        for gg in groups:
            k = jnp.concatenate([group_cols(ks_ref, gg)[pl.ds(k0, kblk), :], e_ref[kb]], axis=1)
            for u in units:
                sm_ = masked(_dot_nt(q_bias[gg, u], k), causal)
                s_sc[kb, urows(gg, u), :] = sm_
                folds.append(lane_fold_max(sm_))
        return jnp.concatenate(folds, axis=0)

    def score_pass(kb, carry):
        mx_sc[...] = jnp.maximum(mx_sc[...], score_chunk(kb))
        return carry

    mx_sc[...] = score_chunk(0)
    lax.fori_loop(1, n_chunks, score_pass, 0)
    m_sel = jnp.max(mx_sc[...], axis=-1, keepdims=True)
    for src, dst in zip(cast_in, cast_out):
        dst[...] = src[...].astype(dst.dtype)

    def value_chunk(kb):
        k0 = chunk_start(kb)
        pvs = []
        for gg in groups:
            v1 = with_ones(group_cols(vs_ref, gg)[pl.ds(k0, kblk), :])
            for u in units:
                pk = jnp.exp2(s_sc[kb, urows(gg, u), :] - m_sel[urows(gg, u)])
                pvs.append(_dot(pk.astype(BF16), v1))
        return jnp.concatenate(pvs, axis=0)

    def value_pass(kb, carry):
        acc_sc[...] += value_chunk(kb)
        return carry

    acc_sc[...] = value_chunk(0)
    lax.fori_loop(1, n_chunks, value_pass, 0)

    for gg in groups:
        for r in range(rep):
            o = po_sc[rows(gg, r), :] + gate(gg, r, 1) * normalised(acc_sc[rows(gg, r), :])
            col = (gg * rep + r) * NSA_DV
            o_ref[:, col:col + NSA_DV] = o.astype(o_ref.dtype)


def _nsa_attention(za, zg, kc, vc, batch, seq, casts=(), tq=256, kblk=512, gp=2):
    assert seq % kblk == 0 and seq >= WIN + tq and WIN % tq == 0 and NSA_GROUPS % gp == 0
    nq = seq // tq
    gw = gp * NSA_REP * NSA_DK
    streams = gp * NSA_REP * tq
    hm = max(1, MXU_ROWS // tq)
    assert NSA_REP % hm == 0
    kern = functools.partial(_nsa_kernel, seq=seq, tq=tq, kblk=kblk, gp=gp, hm=hm, n_cast=len(casts))
    n_g = NSA_GROUPS // gp
    steps = batch * n_g * nq
    cast_specs = []
    for a in casts:
        assert a.shape[0] % steps == 0 and (a.shape[0] // steps) % CAST_ROWS == 0
        cast_specs.append(pl.BlockSpec((a.shape[0] // steps, a.shape[1]),
                                       lambda b, g, i: ((b * n_g + g) * nq + i, 0)))
    key = np.arange(seq).reshape(seq // kblk, kblk, 1)
    expand = jnp.asarray(key // SEL_LEN == np.arange(LANES).reshape(1, 1, LANES), BF16)

    def kv_spec(off):
        return pl.BlockSpec((seq, gp * LANES), lambda b, g, i: (b, off // (gp * LANES) + g))

    def cmp_spec(gg):
        return pl.BlockSpec((LANES, NSA_DK), lambda b, g, i: ((g * gp + gg) * batch + b, 0))

    return pl.pallas_call(
        kern,
        grid=(batch, NSA_GROUPS // gp, nq),
        in_specs=[
            pl.BlockSpec((tq, gw), lambda b, g, i: (b * nq + i, A_QN // gw + g)),
            *[cmp_spec(gg) for gg in range(gp)],
            *[cmp_spec(gg) for gg in range(gp)],
            kv_spec(A_KS), kv_spec(A_VS), kv_spec(A_KW), kv_spec(A_VW),
            pl.BlockSpec((tq, LANES), lambda b, g, i: (b * nq + i, 0)),
            pl.BlockSpec(expand.shape, lambda b, g, i: (0, 0, 0)),
            *cast_specs,
        ],
        out_specs=[pl.BlockSpec((tq, gw), lambda b, g, i: (b * nq + i, g)), *cast_specs],
        out_shape=[jax.ShapeDtypeStruct((batch * seq, NSA_HEADS * NSA_DV), BF16),
                   *[jax.ShapeDtypeStruct(a.shape, BF16) for a in casts]],
        scratch_shapes=[
            pltpu.VMEM((seq // kblk, streams, kblk), F32),
            pltpu.VMEM((streams, LANES), F32),
            pltpu.VMEM((streams, 2 * NSA_DV), F32),
            pltpu.VMEM((streams, NSA_DV), F32),
        ],
        compiler_params=_params(("parallel", "parallel", "arbitrary"), 56),
        name="nsa_attention",
    )(za, *([kc] * gp), *([vc] * gp), za, za, za, za, zg, expand, *casts)


def _retention_kernel(q_ref, k_ref, v_ref, g_ref, cos_ref, sin_ref, dec_ref, wq_ref, wk_ref,
                      gc_ref, gn_ref, o_ref, st_ref, o_sc):
    @pl.when(pl.program_id(1) == 0)
    def _():
        st_ref[...] = jnp.zeros(st_ref.shape, F32)

    c = RET_CHUNK
    even = lax.broadcasted_iota(jnp.int32, (c, RET_DK), 1) % 2 == 0
    for h in range(RET_HEADS):
        st = st_ref[h]
        wq = wq_ref[h]
        wq2 = jnp.concatenate([wq, wq], axis=1)
        cols = slice(h * RET_DV, (h + 1) * RET_DV)
        for sub in range(q_ref.shape[0] // c):
            rows = slice(sub * c, (sub + 1) * c)
            cos = cos_ref[rows, :]
            sin = sin_ref[rows, :]

            def rotate(x):
                partner = jnp.where(even, pltpu.roll(x, RET_DK - 1, axis=1), pltpu.roll(x, 1, axis=1))
                return x * cos + partner * sin

            qf = rotate(q_ref[rows, h * RET_DK:(h + 1) * RET_DK])
            kf = rotate(k_ref[rows, h * RET_DK:(h + 1) * RET_DK])
            qb = qf.astype(BF16)
            v = v_ref[rows, cols].astype(BF16)
            s = _dot_nt(qb, kf.astype(BF16)) * dec_ref[h]
            o = _dot(s.astype(BF16), v) + _dot(qb, st.astype(BF16)) * wq2
            st = st * gc_ref[h] + lax.dot_general((kf * wk_ref[h]).astype(BF16), v, TN_DIMS,
                                                  preferred_element_type=F32)
            o_sc[rows, cols] = o
        st_ref[h] = st

    for h in range(RET_HEADS):
        cols = slice(h * RET_DV, (h + 1) * RET_DV)
        o = o_sc[:, cols]
        mu = jnp.mean(o, axis=-1, keepdims=True)
        d = o - mu
        var = jnp.mean(d * d, axis=-1, keepdims=True)
        on = d * lax.rsqrt(var + EPS) * gn_ref[:, cols]
        gr = g_ref[:, cols]
        o_ref[:, cols] = (gr * jax.nn.sigmoid(gr) * on).astype(o_ref.dtype)


def _retention(zb, gn_w, batch, seq, chunks_per_step=2):
    c = RET_CHUNK
    rows = chunks_per_step * c
    nc = seq // rows
    hq = RET_HEADS * RET_DK
    hv = RET_HEADS * RET_DV
    f32 = np.float32
    inv = f32(ROPE_BASE) ** (-np.arange(0, RET_DK, 2, dtype=f32) / f32(RET_DK))
    ang = np.arange(seq, dtype=f32)[:, None] * inv[None, :]
    pairs = lambda even, odd: np.stack([even, odd], axis=-1).reshape(seq, RET_DK)
    cos = pairs(np.cos(ang), np.cos(ang))
    sin = pairs(-np.sin(ang), np.sin(ang))
    log_g = np.log1p(-np.exp2(f32(-5.0) - np.arange(RET_HEADS, dtype=f32)))
    idx = np.arange(c, dtype=f32)
    rel = idx[:, None] - idx[None, :]
    k_scale = f32(RET_DK ** -0.5)
    decay = (np.where(rel >= 0, np.exp(log_g[:, None, None] * np.maximum(rel, f32(0.0))), f32(0.0))
             * k_scale).astype(f32)
    lanes = lambda a: np.ascontiguousarray(np.broadcast_to(a[:, :, None], (RET_HEADS, c, RET_DK)), dtype=f32)
    w_k = lanes(np.exp(log_g[:, None] * (f32(c - 1) - idx)[None, :]) * k_scale)
    w_q = lanes(np.exp(log_g[:, None] * (idx + f32(1.0))[None, :]))
    g_chunk = np.broadcast_to(np.exp(log_g * f32(c))[:, None, None], (RET_HEADS, 1, RET_DV)).astype(f32)

    row = lambda b, n: b * nc + n
    return pl.pallas_call(
        _retention_kernel,
        grid=(batch, nc),
        in_specs=[
            pl.BlockSpec((rows, hq), lambda b, n: (row(b, n), B_QR // hq)),
            pl.BlockSpec((rows, hq), lambda b, n: (row(b, n), B_KR // hq)),
            pl.BlockSpec((rows, hv), lambda b, n: (row(b, n), B_VR // hv)),
            pl.BlockSpec((rows, hv), lambda b, n: (row(b, n), B_GR // hv)),
            pl.BlockSpec((rows, RET_DK), lambda b, n: (n, 0)),
            pl.BlockSpec((rows, RET_DK), lambda b, n: (n, 0)),
            pl.BlockSpec((RET_HEADS, c, c), lambda b, n: (0, 0, 0)),
            pl.BlockSpec((RET_HEADS, c, RET_DK), lambda b, n: (0, 0, 0)),
            pl.BlockSpec((RET_HEADS, c, RET_DK), lambda b, n: (0, 0, 0)),
            pl.BlockSpec((RET_HEADS, 1, RET_DV), lambda b, n: (0, 0, 0)),
            pl.BlockSpec((1, hv), lambda b, n: (0, 0)),
        ],
        out_specs=pl.BlockSpec((rows, hv), lambda b, n: (row(b, n), 0)),
        out_shape=jax.ShapeDtypeStruct((batch * seq, hv), BF16),
        scratch_shapes=[pltpu.VMEM((RET_HEADS, RET_DK, RET_DV), F32),
                        pltpu.VMEM((rows, hv), F32)],
        compiler_params=_params(("parallel", "arbitrary"), 40),
        name="retention",
    )(zb, zb, zb, zb, cos, sin, decay, w_q, w_k, g_chunk, gn_w.reshape(1, hv))


def _merge_kernel(on_ref, or_ref, ga_ref, gb_ref, x_ref, wa_ref, wb_ref, wo_ref, h_ref):
    a = _dot(on_ref[...], wa_ref[...])
    b = _dot(or_ref[...], wb_ref[...])
    merged = jax.nn.sigmoid(ga_ref[...]) * a + jax.nn.sigmoid(gb_ref[...]) * b
    h_ref[...] = x_ref[...] + _dot(merged.astype(BF16), wo_ref[...])


def _merge(o_nsa, o_ret, zb, x, w_a, w_b, w_out, tm=256):
    m, d = x.shape
    resident = lambda shape: pl.BlockSpec(shape, lambda i: (0, 0), pipeline_mode=pl.Buffered(1))
    return pl.pallas_call(
        _merge_kernel,
        grid=(m // tm,),
        in_specs=[
            pl.BlockSpec((tm, d), lambda i: (i, 0)),
            pl.BlockSpec((tm, d), lambda i: (i, 0)),
            pl.BlockSpec((tm, d), lambda i: (i, B_GA // d)),
            pl.BlockSpec((tm, d), lambda i: (i, B_GB // d)),
            pl.BlockSpec((tm, d), lambda i: (i, 0)),
            resident(w_a.shape), resident(w_b.shape), resident(w_out.shape),
        ],
        out_specs=pl.BlockSpec((tm, d), lambda i: (i, 0)),
        out_shape=jax.ShapeDtypeStruct((m, d), F32),
        compiler_params=_params(("parallel",), 56),
        name="merge_out_proj",
    )(o_nsa, o_ret, zb, zb, x, w_a, w_b, w_out)


def _mem_kv_kernel(m_ref, nw_ref, wk_ref, wv_ref, k_ref, v_ref):
    mn = _rms(m_ref[...], nw_ref[...]).astype(BF16)
    k_ref[...] = _dot(mn, wk_ref[...]).astype(k_ref.dtype)
    v_ref[...] = _dot(mn, wv_ref[...]).astype(v_ref.dtype)


def _mem_kv(mem2, nw, wk, wv, tm=256):
    m, d = mem2.shape
    n = wk.shape[1]
    out = jax.ShapeDtypeStruct((m, n), BF16)
    return pl.pallas_call(
        _mem_kv_kernel,
        grid=(m // tm,),
        in_specs=[
            pl.BlockSpec((tm, d), lambda i: (i, 0)),
            pl.BlockSpec((1, d), lambda i: (0, 0)),
            pl.BlockSpec((d, n), lambda i: (0, 0)),
            pl.BlockSpec((d, n), lambda i: (0, 0)),
        ],
        out_specs=[pl.BlockSpec((tm, n), lambda i: (i, 0))] * 2,
        out_shape=[out, out],
        compiler_params=_params(("parallel",), 40),
        name="mem_kv_proj",
    )(mem2, nw.reshape(1, d), wk, wv)


def _cross_kernel(h_ref, xw_ref, mw_ref, wq_ref, kx_ref, vx_ref, wo_ref, h2_ref, nm_ref):
    h = h_ref[...]
    nx = _rms(h, xw_ref[...]).astype(BF16)
    qx = (_dot(nx, wq_ref[...]) * (X_DH ** -0.5 * LOG2E)).astype(BF16)
    outs = []
    for hh in range(X_HEADS):
        cols = slice(hh * X_DH, (hh + 1) * X_DH)
        s = _dot_nt(qx[:, cols], kx_ref[:, cols])
        e = jnp.exp2(s - jnp.max(s, axis=-1, keepdims=True))
        vx = vx_ref[:, cols]
        ev = _dot(e.astype(BF16), jnp.concatenate([vx, jnp.ones(vx.shape, vx.dtype)], axis=1))
        outs.append(ev[:, :X_DH] / ev[:, X_DH:])
    ox = jnp.concatenate(outs, axis=-1).astype(BF16)
    h2 = h + _dot(ox, wo_ref[...])
    h2_ref[...] = h2
    nm_ref[...] = _rms(h2, mw_ref[...]).astype(nm_ref.dtype)


def _cross_attention(h1, x_norm_w, mlp_norm_w, wq, kx, vx, wo, seq, tm=512):
    m, d = h1.shape
    n = wq.shape[1]
    per_batch = seq // tm
    vec = lambda: pl.BlockSpec((1, d), lambda i: (0, 0))
    return pl.pallas_call(
        _cross_kernel,
        grid=(m // tm,),
        in_specs=[
            pl.BlockSpec((tm, d), lambda i: (i, 0)),
            vec(), vec(),
            pl.BlockSpec((d, n), lambda i: (0, 0)),
            pl.BlockSpec((MEM_LEN, n), lambda i: (i // per_batch, 0)),
            pl.BlockSpec((MEM_LEN, n), lambda i: (i // per_batch, 0)),
            pl.BlockSpec((n, d), lambda i: (0, 0)),
        ],
        out_specs=[pl.BlockSpec((tm, d), lambda i: (i, 0))] * 2,
        out_shape=[jax.ShapeDtypeStruct((m, d), F32), jax.ShapeDtypeStruct((m, d), BF16)],
        compiler_params=_params(("parallel",), 40),
        name="cross_attention",
    )(h1, x_norm_w.reshape(1, d), mlp_norm_w.reshape(1, d), wq, kx, vx, wo)


def _mlp_kernel(nm_ref, wu_ref, wd_ref, h_ref, fw_ref, o_ref):
    j = pl.program_id(1)

    @pl.when(j == 0)
    def _():
        o_ref[...] = jnp.zeros(o_ref.shape, F32)

    u = jnp.maximum(_dot(nm_ref[...], wu_ref[...]), 0.0)
    o_ref[...] += _dot((u * u).astype(BF16), wd_ref[...])

    @pl.when(j == pl.num_programs(1) - 1)
    def _():
        o_ref[...] = _rms(h_ref[...] + o_ref[...], fw_ref[...])


def _mlp(nm, w_up, w_down, h2, final_w, tm=512, tf=2048):
    m, d = nm.shape
    f = w_up.shape[1]
    return pl.pallas_call(
        _mlp_kernel,
        grid=(m // tm, f // tf),
        in_specs=[
            pl.BlockSpec((tm, d), lambda i, j: (i, 0)),
            pl.BlockSpec((d, tf), lambda i, j: (0, j)),
            pl.BlockSpec((tf, d), lambda i, j: (j, 0)),
            pl.BlockSpec((tm, d), lambda i, j: (i, 0)),
            pl.BlockSpec((1, d), lambda i, j: (0, 0)),
        ],
        out_specs=pl.BlockSpec((tm, d), lambda i, j: (i, 0)),
        out_shape=jax.ShapeDtypeStruct((m, d), F32),
        compiler_params=_params(("parallel", "arbitrary"), 60),
        name="mlp_final_norm",
    )(nm, w_up, w_down, h2, final_w.reshape(1, d))


def _layer(h, mem, attn_norm_w, w_in, cmp_pe_k, cmp_w1_k, cmp_w2_k, cmp_pe_v, cmp_w1_v, cmp_w2_v,
           w_a, ret_gn_w, w_b, w_out, x_norm_w, mem_norm_w, wq_x, wk_x, wv_x, wo_x,
           mlp_norm_w, w_up, w_down, out_norm_w, batch, seq):
    blk = W_IN_BLOCK
    kv_block = W_IN_KV // blk
    gate_shift = NSA_HEADS * 3

    wt = w_in.T
    skip_kv = lambda j: j + (j >= kv_block).astype(jnp.int32)
    kv, n, zg, wo_b, c2k_b, c2v_b = _in_proj(
        h, wt, lambda j: kv_block, 1, BF16, regroup=CMP_STRIDE, norm_w=attn_norm_w,
        side_block=W_IN_NSA_GATE // LANES, casts=(wo_x, cmp_w2_k, cmp_w2_v), name="in_proj_kv")
    a_steps = 4 * (h.shape[0] // 1024)
    b_first = W_IN_NSA_GATE // (B_WIDTH // a_steps)
    za, c1k_b, c1v_b, wt_b = _in_proj(
        n, wt, skip_kv, 4, BF16, casts=(cmp_w1_k, cmp_w1_v),
        lead_scale=(NSA_Q_SCALE, NSA_HEADS * NSA_DK // blk),
        slab=(wt, B_WIDTH, lambda s: s + b_first, gate_shift), name="in_proj_a")
    zb, wup_b, wdown_b = _proj(n, wt_b, F32, casts=(w_up, w_down), name="in_proj_b")

    assert seq // CMP_STRIDE == LANES
    rows_kv = NSA_GROUPS * batch * LANES
    k2 = kv.reshape(2 * rows_kv, CMP_STRIDE * NSA_DK)
    kc = _compress(k2, 0, rows_kv, cmp_pe_k.reshape(1, -1), c1k_b, c2k_b)
    vc = _compress(k2, rows_kv, rows_kv, cmp_pe_v.reshape(1, -1), c1v_b, c2v_b)

    o_nsa, wa_b, wb_b, wout_b, wq_b, wk_b, wv_b = _nsa_attention(
        za, zg, kc, vc, batch, seq, casts=(w_a, w_b, w_out, wq_x, wk_x, wv_x))
    o_ret = _retention(zb, ret_gn_w, batch, seq)
    h1 = _merge(o_nsa, o_ret, zb, h, wa_b, wb_b, wout_b)

    kx, vx = _mem_kv(mem.reshape(batch * MEM_LEN, D_MODEL), mem_norm_w, wk_b, wv_b)
    h2, nm = _cross_attention(h1, x_norm_w, mlp_norm_w, wq_b, kx, vx, wo_b, seq)
    return _mlp(nm, wup_b, wdown_b, h2, out_norm_w)


def kernel(x, mem, attn_norm_w, w_in, cmp_pe_k, cmp_w1_k, cmp_w2_k, cmp_pe_v, cmp_w1_v, cmp_w2_v, w_a, ret_gn_w,
           w_b, w_out, x_norm_w, mem_norm_w, wq_x, wk_x, wv_x, wo_x, mlp_norm_w, w_up, w_down, final_norm_w):
    batch, seq, d = x.shape
    depth = w_in.shape[0]
    assert depth == 1
    h = x.reshape(batch * seq, d)
    out = _layer(h, mem, attn_norm_w[0], w_in[0], cmp_pe_k[0], cmp_w1_k[0], cmp_w2_k[0],
                 cmp_pe_v[0], cmp_w1_v[0], cmp_w2_v[0], w_a[0], ret_gn_w[0], w_b[0], w_out[0],
                 x_norm_w[0], mem_norm_w[0], wq_x[0], wk_x[0], wv_x[0], wo_x[0],
                 mlp_norm_w[0], w_up[0], w_down[0], final_norm_w, batch, seq)
    return out.reshape(batch, seq, d)
```

```python
import functools

import jax
import jax.numpy as jnp
import numpy as np
from jax import lax
from jax.experimental import pallas as pl
from jax.experimental.pallas import tpu as pltpu

F32 = jnp.float32
BF16 = jnp.bfloat16

D_MODEL = 2048
MEM_LEN = 256
NSA_HEADS = 16
NSA_GROUPS = 4
NSA_REP = NSA_HEADS // NSA_GROUPS
NSA_DK = 128
NSA_DV = 128
CMP_LEN = 32
CMP_STRIDE = 16
CMP_HIDDEN = 1024
SEL_LEN = 64
SEL_TOPK = 16
WIN = 512
RET_HEADS = 8
RET_DK = 128
RET_DV = 256
RET_CHUNK = 128
ROPE_BASE = 10000.0
X_HEADS = 4
X_DH = 128
D_FF = 4 * D_MODEL
EPS = 1e-6
NEG = -1e30
LOG2E = 1.4426950408889634
NSA_Q_SCALE = NSA_DK ** -0.5 * LOG2E

LANES = 128
F32_SUBLANES = 8
MXU_ROWS = 256
V7X_VMEM_MIB = 64

W_IN_BLOCK = 1024
W_IN_KV = 2048
W_IN_NSA_GATE = 5120
A_QN = 0
A_KS = 2048
A_VS = 2560
A_KW = 3072
A_VW = 3584
B_QR = 0
B_KR = 1024
B_VR = 2048
B_GR = 4096
B_GA = 6144
B_GB = 8192
B_WIDTH = 10240

NT_DIMS = (((1,), (1,)), ((), ()))
TN_DIMS = (((0,), (0,)), ((), ()))


def _params(sem, vmem_mib):
    assert vmem_mib < V7X_VMEM_MIB
    return pltpu.CompilerParams(dimension_semantics=sem, vmem_limit_bytes=vmem_mib * 1024 * 1024)


def _rms(x, w):
    return x * lax.rsqrt(jnp.mean(x * x, axis=-1, keepdims=True) + EPS) * w


def _dot(a, b):
    return jnp.dot(a, b, preferred_element_type=F32)


def _dot_nt(a, b):
    return lax.dot_general(a, b, NT_DIMS, preferred_element_type=F32)


CAST_ROWS = 16
SLAB_TAIL_ROWS = 64


def _cast_rows(dst_ref, dst0, src_ref, src0, nrows):
    def body(r, carry):
        off = r * CAST_ROWS
        dst_ref[pl.ds(pl.multiple_of(dst0 + off, CAST_ROWS), CAST_ROWS), :] = (
            src_ref[pl.ds(pl.multiple_of(src0 + off, F32_SUBLANES), CAST_ROWS), :].astype(BF16))
        return carry

    lax.fori_loop(0, nrows // CAST_ROWS, body, 0)


def _slab_inputs(off):
    return 0 if off is None else (2 if off else 1)


def _convert_slab(in_refs, out_ref, off):
    main_ref = in_refs[0]
    keep = main_ref.shape[0] - off
    out_ref[:keep, :] = main_ref[off:, :].astype(BF16)
    if off:
        out_ref[keep:, :] = in_refs[1][:off, :].astype(BF16)


def _slab_specs(arr, n_rows, n_steps, step_of, block_of, off):
    rps = n_rows // n_steps
    cols = arr.shape[1]
    assert n_rows % n_steps == 0 and off % CAST_ROWS == 0 and (rps - off) % CAST_ROWS == 0
    ins = [pl.BlockSpec((rps, cols), lambda *g: (block_of(step_of(*g)), 0))]
    if off:
        assert rps % SLAB_TAIL_ROWS == 0 and off <= SLAB_TAIL_ROWS
        per = rps // SLAB_TAIL_ROWS
        ins.append(pl.BlockSpec((SLAB_TAIL_ROWS, cols), lambda *g: ((block_of(step_of(*g)) + 1) * per, 0)))
    out = pl.BlockSpec((rps, cols), lambda *g: (step_of(*g), 0))
    return ins, out, jax.ShapeDtypeStruct((n_rows, cols), BF16), [arr] * len(ins)


def _in_proj_kernel(*refs, shift, regroup, n_cast, norm, side, lead_scale, slab_off=None):
    n_in = 4 if norm else 3
    slab = slab_off is not None
    n_slab_in = _slab_inputs(slab_off)
    n_ref, w_ref, wnext_ref = refs[:3]
    cast_in = refs[n_in:n_in + n_cast]
    o_idx = n_in + n_cast + n_slab_in
    o_ref = refs[o_idx]
    n_out = o_idx + 1 + int(norm) + int(side)
    cast_out = refs[n_out:n_out + n_cast]
    wb_sc, *rest = refs[n_out + n_cast + int(slab):]
    tn = wb_sc.shape[0]
    for src, dst in zip(cast_in, cast_out):
        dst[...] = src[...].astype(dst.dtype)
    if slab:
        _convert_slab(refs[n_in + n_cast:o_idx], refs[n_out + n_cast], slab_off)

    @pl.when(pl.program_id(1) == 0)
    def _():
        _cast_rows(wb_sc, 0, w_ref, shift, tn - shift)
        if shift:
            _cast_rows(wb_sc, tn - shift, wnext_ref, 0, shift)
        if side:
            _cast_rows(rest[-1], 0, wnext_ref, 0, LANES)

    if norm:
        xn = _rms(n_ref[...], refs[3][...]).astype(BF16)
        refs[o_idx + 1][...] = xn
    else:
        xn = n_ref[...]
    if side:
        refs[n_out - 1][...] = _dot_nt(xn, rest[-1][...])
    if regroup:
        r_sc = rest[0]
        tm = xn.shape[0]
        pair = 2 * LANES
        for cc in range(tn // pair):
            res = _dot_nt(xn, wb_sc[cc * pair:(cc + 1) * pair, :])
            for c in (2 * cc, 2 * cc + 1):
                r_sc[c] = res[:, (c % 2) * LANES:(c % 2 + 1) * LANES]
                for j in range(regroup):
                    o_ref[c, :, j * LANES:(j + 1) * LANES] = (
                        r_sc[c, pl.ds(j, tm // regroup, stride=regroup), :].astype(o_ref.dtype))
    else:
        res = _dot_nt(xn, wb_sc[...])
        if lead_scale is not None:
            factor, blocks = lead_scale
            res = res * jnp.where(pl.program_id(0) < blocks, factor, 1.0)
        o_ref[...] = res.astype(o_ref.dtype)


def _in_proj(n, wt, src_block, n_blocks, out_dtype, *, shift=0, regroup=0, casts=(), norm_w=None,
             side_block=None, lead_scale=None, slab=None, tm=1024, tn=1024, name):
    m, k = n.shape
    per = tn // LANES
    m_tiles = m // tm
    norm = norm_w is not None
    side = side_block is not None
    assert shift % CAST_ROWS == 0 and shift <= LANES and not (norm and n_blocks > 1)
    assert not side or (shift == 0 and n_blocks == 1)
    scratch = [pltpu.VMEM((tn, k), BF16)]
    row_tile = pl.BlockSpec((tm, k), lambda j, i: (i, 0))
    once = {"pipeline_mode": pl.Buffered(1)} if n_blocks == 1 else {}
    if regroup:
        assert n_blocks == 1
        out_shape = jax.ShapeDtypeStruct((per, m // regroup, regroup * LANES), out_dtype)
        out_spec = pl.BlockSpec((per, tm // regroup, regroup * LANES), lambda j, i: (0, i, 0))
        scratch.append(pltpu.VMEM((per, tm, LANES), F32))
    else:
        out_shape = jax.ShapeDtypeStruct((m, n_blocks * tn), out_dtype)
        out_spec = pl.BlockSpec((tm, tn), lambda j, i: (i, j))

    cast_steps = 1 << ((n_blocks * m_tiles).bit_length() - 1)
    cast_specs = []
    for a in casts:
        rows = a.shape[0] // cast_steps
        assert a.shape[0] % cast_steps == 0 and rows % CAST_ROWS == 0
        cast_specs.append(pl.BlockSpec(
            (rows, a.shape[1]), lambda j, i: (jnp.minimum(j * m_tiles + i, cast_steps - 1), 0)))

    if side:
        scratch.append(pltpu.VMEM((LANES, k), BF16))
    next_rows = (lambda j, i: (side_block, 0)) if side else (lambda j, i: ((src_block(j) + 1) * per, 0))

    slab_in, slab_out_spec, slab_out_shape, slab_args, slab_off = [], [], [], [], None
    if slab is not None:
        arr, n_rows, block_of, slab_off = slab
        ins, out, shape, slab_args = _slab_specs(arr, n_rows, n_blocks * m_tiles, lambda j, i: j * m_tiles + i,
                                                 block_of, slab_off)
        slab_in, slab_out_spec, slab_out_shape = ins, [out], [shape]

    return pl.pallas_call(
        functools.partial(_in_proj_kernel, shift=shift, regroup=regroup, n_cast=len(casts), norm=norm, side=side,
                          lead_scale=lead_scale, slab_off=slab_off),
        grid=(n_blocks, m_tiles),
        in_specs=[
            row_tile,
            pl.BlockSpec((tn, k), lambda j, i: (src_block(j), 0), **once),
            pl.BlockSpec((LANES, k), next_rows, **once),
            *([pl.BlockSpec((1, k), lambda j, i: (0, 0))] if norm else []),
            *cast_specs,
            *slab_in,
        ],
        out_specs=[out_spec, *([row_tile] if norm else []),
                   *([pl.BlockSpec((tm, LANES), lambda j, i: (i, 0))] if side else []), *cast_specs,
                   *slab_out_spec],
        out_shape=[out_shape, *([jax.ShapeDtypeStruct((m, k), BF16)] if norm else []),
                   *([jax.ShapeDtypeStruct((m, LANES), F32)] if side else []),
                   *[jax.ShapeDtypeStruct(a.shape, BF16) for a in casts], *slab_out_shape],
        scratch_shapes=scratch,
        compiler_params=_params(("arbitrary", "arbitrary"), 60 if (norm or slab is not None) else 56),
        name=name,
    )(n, wt, wt, *([norm_w.reshape(1, k)] if norm else []), *casts, *slab_args)


def _proj_kernel(n_ref, w_ref, *refs, n_cast, slab_off, lead_scale):
    n_slab_in = _slab_inputs(slab_off)
    o_ref = refs[n_cast + n_slab_in]
    for src, dst in zip(refs[:n_cast], refs[n_cast + n_slab_in + 1:]):
        dst[...] = src[...].astype(dst.dtype)
    if slab_off is not None:
        _convert_slab(refs[n_cast:n_cast + n_slab_in], refs[-1], slab_off)
    res = _dot_nt(n_ref[...], w_ref[...])
    if lead_scale is not None:
        factor, blocks = lead_scale
        res = res * jnp.where(pl.program_id(0) < blocks, factor, 1.0)
    o_ref[...] = res.astype(o_ref.dtype)


def _proj(n, wt_bf16, out_dtype, *, casts=(), slab=None, lead_scale=None, tm=1024, tn=2048, name):
    m, k = n.shape
    n_cols = wt_bf16.shape[0]
    n_blocks, m_tiles = n_cols // tn, m // tm
    cast_steps = 1 << ((n_blocks * m_tiles).bit_length() - 1)
    cast_specs = []
    for a in casts:
        rows = a.shape[0] // cast_steps
        assert a.shape[0] % cast_steps == 0 and rows % CAST_ROWS == 0
        cast_specs.append(pl.BlockSpec(
            (rows, a.shape[1]), lambda j, i: (jnp.minimum(j * m_tiles + i, cast_steps - 1), 0)))
    slab_in, slab_out_spec, slab_out_shape, slab_args, slab_off = [], [], [], [], None
    if slab is not None:
        arr, n_rows, block_of, slab_off = slab
        ins, out, shape, slab_args = _slab_specs(arr, n_rows, n_blocks * m_tiles, lambda j, i: j * m_tiles + i,
                                                 block_of, slab_off)
        slab_in, slab_out_spec, slab_out_shape = ins, [out], [shape]
    return pl.pallas_call(
        functools.partial(_proj_kernel, n_cast=len(casts), slab_off=slab_off, lead_scale=lead_scale),
        grid=(n_blocks, m_tiles),
        in_specs=[pl.BlockSpec((tm, k), lambda j, i: (i, 0)), pl.BlockSpec((tn, k), lambda j, i: (j, 0)),
                  *cast_specs, *slab_in],
        out_specs=[pl.BlockSpec((tm, tn), lambda j, i: (i, j)), *cast_specs, *slab_out_spec],
        out_shape=[jax.ShapeDtypeStruct((m, n_cols), out_dtype),
                   *[jax.ShapeDtypeStruct(a.shape, BF16) for a in casts], *slab_out_shape],
        compiler_params=_params(("arbitrary", "arbitrary"), 58),
        name=name,
    )(n, wt_bf16, *casts, *slab_args)


def _compress_kernel(k2_ref, pe_ref, w1_ref, w2_ref, o_ref):
    rows, half = k2_ref.shape
    k2 = k2_ref[...].astype(F32)
    a_lo = (k2 + pe_ref[:, :half]).astype(BF16)
    a_hi = (k2 + pe_ref[:, half:]).astype(BF16)
    lo = _dot(a_lo, w1_ref[:half, :])
    hi = _dot(a_hi, w1_ref[half:, :])
    h = lo + pltpu.roll(hi, rows - 1, axis=0)
    act = h * jax.nn.sigmoid(h)
    o_ref[...] = _dot(act.astype(BF16), w2_ref[...]).astype(o_ref.dtype)


def _compress(k2, first_row, m, pe, w1, w2, rows=512):
    half = k2.shape[1]
    hid = w1.shape[1]
    dout = w2.shape[1]
    first = first_row // rows
    return pl.pallas_call(
        _compress_kernel,
        grid=(m // rows,),
        in_specs=[
            pl.BlockSpec((rows, half), lambda i: (first + i, 0)),
            pl.BlockSpec((1, 2 * half), lambda i: (0, 0)),
            pl.BlockSpec((2 * half, hid), lambda i: (0, 0)),
            pl.BlockSpec((hid, dout), lambda i: (0, 0)),
        ],
        out_specs=pl.BlockSpec((rows, dout), lambda i: (i, 0)),
        out_shape=jax.ShapeDtypeStruct((m, dout), BF16),
        compiler_params=_params(("parallel",), 40),
        name="nsa_compress",
    )(k2, pe, w1, w2)


def _nsa_kernel(*refs, seq, tq, kblk, gp, hm, n_cast):
    q_ref = refs[0]
    kc_refs = refs[1:1 + gp]
    vc_refs = refs[1 + gp:1 + 2 * gp]
    ks_ref, vs_ref, kw_ref, vw_ref, g_ref, e_ref = refs[1 + 2 * gp:7 + 2 * gp]
    cast_in = refs[7 + 2 * gp:7 + 2 * gp + n_cast]
    o_ref = refs[7 + 2 * gp + n_cast]
    cast_out = refs[8 + 2 * gp + n_cast:8 + 2 * gp + 2 * n_cast]
    s_sc, mx_sc, acc_sc, po_sc = refs[8 + 2 * gp + 2 * n_cast:]
    i = pl.program_id(2)
    t0 = i * tq
    rep = NSA_REP
    n_cmp = (seq - CMP_LEN) // CMP_STRIDE + 1
    n_sel = seq // SEL_LEN
    topk = min(SEL_TOPK, n_sel)
    groups = range(gp)

    q = q_ref[...]
    units = range(rep // hm)
    rows = lambda gg, r: slice((gg * rep + r) * tq, (gg * rep + r + 1) * tq)
    urows = lambda gg, u: slice((gg * rep + u * hm) * tq, (gg * rep + (u + 1) * hm) * tq)
    part = lambda x, h: x[h * tq:(h + 1) * tq]
    head_q = lambda gg, r: q[:, (gg * rep + r) * NSA_DK:(gg * rep + r + 1) * NSA_DK]
    unit_q = {(gg, u): jnp.concatenate([head_q(gg, u * hm + h) for h in range(hm)], axis=0)
              for gg in groups for u in units}
    group_cols = lambda ref, gg: ref.at[:, gg * LANES:(gg + 1) * LANES]

    def masked(s, mask1):
        return jnp.concatenate([jnp.where(mask1, part(s, h), NEG) for h in range(hm)], axis=0)
    tcol = t0 + lax.broadcasted_iota(jnp.int32, (tq, 1), 0)
    per_group = rep * 3
    g_sig = jax.nn.sigmoid(g_ref[...])
    first_group = pl.program_id(1) * gp
    gs = [pltpu.roll(g_sig, lax.rem(LANES - (first_group + gg) * per_group, LANES), axis=1) for gg in groups]
    gate = lambda gg, r, branch: gs[gg][:, 3 * r + branch:3 * r + branch + 1]

    def exp_rows(sm):
        return jnp.exp2(sm - jnp.max(sm, axis=-1, keepdims=True))

    def with_ones(v):
        return jnp.concatenate([v, jnp.ones(v.shape, v.dtype)], axis=1)

    def normalised(ev):
        return ev[:, :NSA_DV] / ev[:, NSA_DV:]

    c_idx = lax.broadcasted_iota(jnp.int32, (tq, LANES), 1)
    mask_c = ((c_idx * CMP_STRIDE + (CMP_LEN - 1)) <= tcol) & (c_idx < n_cmp)
    mask_cf = jnp.where(mask_c, 1.0, 0.0)
    o_cmp, psum = {}, []
    for gg in groups:
        kc = kc_refs[gg][...]
        vc = vc_refs[gg][...]
        tot = None
        for u in units:
            e = exp_rows(masked(_dot_nt(unit_q[gg, u], kc), mask_c))
            p = e / jnp.sum(e, axis=-1, keepdims=True)
            p = jnp.concatenate([part(p, h) * mask_cf for h in range(hm)], axis=0)
            o_cmp[gg, u] = _dot(p.astype(BF16), vc)
            for h in range(hm):
                tot = part(p, h) if tot is None else tot + part(p, h)
        psum.append(tot)

    wlen = WIN + tq
    start = pl.multiple_of(jnp.maximum(i - WIN // tq, 0) * tq, tq)
    dlt = tcol - (start + lax.broadcasted_iota(jnp.int32, (tq, wlen), 1))
    mask_w = (dlt >= 0) & (dlt < WIN)
    for gg in groups:
        kw = group_cols(kw_ref, gg)[pl.ds(start, wlen), :]
        vw1 = with_ones(group_cols(vw_ref, gg)[pl.ds(start, wlen), :])
        for u in units:
            e = exp_rows(masked(_dot_nt(unit_q[gg, u], kw), mask_w))
            o_win = normalised(_dot(e.astype(BF16), vw1))
            for h in range(hm):
                r = u * hm + h
                po_sc[rows(gg, r), :] = (gate(gg, r, 0) * part(o_cmp[gg, u], h)
                                         + gate(gg, r, 2) * part(o_win, h))

    jo = lax.broadcasted_iota(jnp.int32, (n_sel, LANES), 0)
    co = lax.broadcasted_iota(jnp.int32, (n_sel, LANES), 1)
    ov_t = jnp.where((co * CMP_STRIDE < jo * SEL_LEN + SEL_LEN) & (co * CMP_STRIDE + CMP_LEN > jo * SEL_LEN)
                     & (co < n_cmp), 1.0, 0.0).astype(BF16)
    j_idx = lax.broadcasted_iota(jnp.int32, (n_sel, tq), 0)
    cur = lax.shift_right_logical(t0 + lax.broadcasted_iota(jnp.int32, (n_sel, tq), 1), int(np.log2(SEL_LEN)))
    forced = (j_idx == 0) | (j_idx == cur) | (j_idx == cur - 1)
    future = j_idx > cur
    q_bias = {}
    for gg in groups:
        p_hi = psum[gg].astype(BF16)
        p_lo = (psum[gg] - p_hi.astype(F32)).astype(BF16)
        imp = _dot_nt(ov_t, p_hi) + _dot_nt(ov_t, p_lo)
        impm = jnp.where(forced, jnp.inf, jnp.where(future, -jnp.inf, imp))
        rank = jnp.zeros((n_sel, tq), F32)
        for ii in range(n_sel):
            row = impm[ii:ii + 1, :]
            beats = (row > impm) | ((row == impm) & (j_idx > ii))
            rank = rank + jnp.where(beats, 1.0, 0.0)
        bias_t = jnp.where(rank < topk, 0.0, NEG)
        bias = jnp.concatenate([bias_t, jnp.zeros((LANES - n_sel, tq), F32)], axis=0).T.astype(BF16)
        for u in units:
            q_bias[gg, u] = jnp.concatenate([unit_q[gg, u], jnp.concatenate([bias] * hm, axis=0)], axis=1)

    n_chunks = lax.div(t0 + tq + (kblk - 1), kblk)

    def lane_fold_max(x):
        out = x[:, :LANES]
        for c in range(1, kblk // LANES):
            out = jnp.maximum(out, x[:, c * LANES:(c + 1) * LANES])
        return out

    def chunk_start(kb):
        return kb * kblk if isinstance(kb, int) else pl.multiple_of(kb * kblk, kblk)

    def score_chunk(kb):
        k0 = chunk_start(kb)
        causal = (k0 + lax.broadcasted_iota(jnp.int32, (tq, kblk), 1)) <= tcol
        folds = []
        for gg in groups:
            k = jnp.concatenate([group_cols(ks_ref, gg)[pl.ds(k0, kblk), :], e_ref[kb]], axis=1)
            for u in units:
                sm_ = masked(_dot_nt(q_bias[gg, u], k), causal)
                s_sc[kb, urows(gg, u), :] = sm_
                folds.append(lane_fold_max(sm_))
        return jnp.concatenate(folds, axis=0)

    def score_pass(kb, carry):
        mx_sc[...] = jnp.maximum(mx_sc[...], score_chunk(kb))
        return carry

    mx_sc[...] = score_chunk(0)
    lax.fori_loop(1, n_chunks, score_pass, 0)
    m_sel = jnp.max(mx_sc[...], axis=-1, keepdims=True)
    for src, dst in zip(cast_in, cast_out):
        dst[...] = src[...].astype(dst.dtype)

    def value_chunk(kb):
        k0 = chunk_start(kb)
        pvs = []
        for gg in groups:
            v1 = with_ones(group_cols(vs_ref, gg)[pl.ds(k0, kblk), :])
            for u in units:
                pk = jnp.exp2(s_sc[kb, urows(gg, u), :] - m_sel[urows(gg, u)])
                pvs.append(_dot(pk.astype(BF16), v1))
        return jnp.concatenate(pvs, axis=0)

    def value_pass(kb, carry):
        acc_sc[...] += value_chunk(kb)
        return carry

    acc_sc[...] = value_chunk(0)
    lax.fori_loop(1, n_chunks, value_pass, 0)

    for gg in groups:
        for r in range(rep):
            o = po_sc[rows(gg, r), :] + gate(gg, r, 1) * normalised(acc_sc[rows(gg, r), :])
            col = (gg * rep + r) * NSA_DV
            o_ref[:, col:col + NSA_DV] = o.astype(o_ref.dtype)


def _nsa_attention(za, zg, kc, vc, batch, seq, casts=(), tq=256, kblk=512, gp=2):
    assert seq % kblk == 0 and seq >= WIN + tq and WIN % tq == 0 and NSA_GROUPS % gp == 0
    nq = seq // tq
    gw = gp * NSA_REP * NSA_DK
    streams = gp * NSA_REP * tq
    hm = max(1, MXU_ROWS // tq)
    assert NSA_REP % hm == 0
    kern = functools.partial(_nsa_kernel, seq=seq, tq=tq, kblk=kblk, gp=gp, hm=hm, n_cast=len(casts))
    n_g = NSA_GROUPS // gp
    steps = batch * n_g * nq
    cast_specs = []
    for a in casts:
        assert a.shape[0] % steps == 0 and (a.shape[0] // steps) % CAST_ROWS == 0
        cast_specs.append(pl.BlockSpec((a.shape[0] // steps, a.shape[1]),
                                       lambda b, g, i: ((b * n_g + g) * nq + i, 0)))
    key = np.arange(seq).reshape(seq // kblk, kblk, 1)
    expand = jnp.asarray(key // SEL_LEN == np.arange(LANES).reshape(1, 1, LANES), BF16)

    def kv_spec(off):
        return pl.BlockSpec((seq, gp * LANES), lambda b, g, i: (b, off // (gp * LANES) + g))

    def cmp_spec(gg):
        return pl.BlockSpec((LANES, NSA_DK), lambda b, g, i: ((g * gp + gg) * batch + b, 0))

    return pl.pallas_call(
        kern,
        grid=(batch, NSA_GROUPS // gp, nq),
        in_specs=[
            pl.BlockSpec((tq, gw), lambda b, g, i: (b * nq + i, A_QN // gw + g)),
            *[cmp_spec(gg) for gg in range(gp)],
            *[cmp_spec(gg) for gg in range(gp)],
            kv_spec(A_KS), kv_spec(A_VS), kv_spec(A_KW), kv_spec(A_VW),
            pl.BlockSpec((tq, LANES), lambda b, g, i: (b * nq + i, 0)),
            pl.BlockSpec(expand.shape, lambda b, g, i: (0, 0, 0)),
            *cast_specs,
        ],
        out_specs=[pl.BlockSpec((tq, gw), lambda b, g, i: (b * nq + i, g)), *cast_specs],
        out_shape=[jax.ShapeDtypeStruct((batch * seq, NSA_HEADS * NSA_DV), BF16),
                   *[jax.ShapeDtypeStruct(a.shape, BF16) for a in casts]],
        scratch_shapes=[
            pltpu.VMEM((seq // kblk, streams, kblk), F32),
            pltpu.VMEM((streams, LANES), F32),
            pltpu.VMEM((streams, 2 * NSA_DV), F32),
            pltpu.VMEM((streams, NSA_DV), F32),
        ],
        compiler_params=_params(("parallel", "parallel", "arbitrary"), 56),
        name="nsa_attention",
    )(za, *([kc] * gp), *([vc] * gp), za, za, za, za, zg, expand, *casts)


def _retention_kernel(q_ref, k_ref, v_ref, g_ref, cos_ref, sin_ref, dec_ref, wq_ref, wk_ref,
                      gc_ref, gn_ref, o_ref, st_ref, o_sc):
    @pl.when(pl.program_id(1) == 0)
    def _():
        st_ref[...] = jnp.zeros(st_ref.shape, F32)

    c = RET_CHUNK
    even = lax.broadcasted_iota(jnp.int32, (c, RET_DK), 1) % 2 == 0
    for h in range(RET_HEADS):
        st = st_ref[h]
        wq = wq_ref[h]
        wq2 = jnp.concatenate([wq, wq], axis=1)
        cols = slice(h * RET_DV, (h + 1) * RET_DV)
        for sub in range(q_ref.shape[0] // c):
            rows = slice(sub * c, (sub + 1) * c)
            cos = cos_ref[rows, :]
            sin = sin_ref[rows, :]

            def rotate(x):
                partner = jnp.where(even, pltpu.roll(x, RET_DK - 1, axis=1), pltpu.roll(x, 1, axis=1))
                return x * cos + partner * sin

            qf = rotate(q_ref[rows, h * RET_DK:(h + 1) * RET_DK])
            kf = rotate(k_ref[rows, h * RET_DK:(h + 1) * RET_DK])
            qb = qf.astype(BF16)
            v = v_ref[rows, cols].astype(BF16)
            s = _dot_nt(qb, kf.astype(BF16)) * dec_ref[h]
            o = _dot(s.astype(BF16), v) + _dot(qb, st.astype(BF16)) * wq2
            st = st * gc_ref[h] + lax.dot_general((kf * wk_ref[h]).astype(BF16), v, TN_DIMS,
                                                  preferred_element_type=F32)
            o_sc[rows, cols] = o
        st_ref[h] = st

    for h in range(RET_HEADS):
        cols = slice(h * RET_DV, (h + 1) * RET_DV)
        o = o_sc[:, cols]
        mu = jnp.mean(o, axis=-1, keepdims=True)
        d = o - mu
        var = jnp.mean(d * d, axis=-1, keepdims=True)
        on = d * lax.rsqrt(var + EPS) * gn_ref[:, cols]
        gr = g_ref[:, cols]
        o_ref[:, cols] = (gr * jax.nn.sigmoid(gr) * on).astype(o_ref.dtype)


def _retention(zb, gn_w, batch, seq, chunks_per_step=2):
    c = RET_CHUNK
    rows = chunks_per_step * c
    nc = seq // rows
    hq = RET_HEADS * RET_DK
    hv = RET_HEADS * RET_DV
    f32 = np.float32
    inv = f32(ROPE_BASE) ** (-np.arange(0, RET_DK, 2, dtype=f32) / f32(RET_DK))
    ang = np.arange(seq, dtype=f32)[:, None] * inv[None, :]
    pairs = lambda even, odd: np.stack([even, odd], axis=-1).reshape(seq, RET_DK)
    cos = pairs(np.cos(ang), np.cos(ang))
    sin = pairs(-np.sin(ang), np.sin(ang))
    log_g = np.log1p(-np.exp2(f32(-5.0) - np.arange(RET_HEADS, dtype=f32)))
    idx = np.arange(c, dtype=f32)
    rel = idx[:, None] - idx[None, :]
    k_scale = f32(RET_DK ** -0.5)
    decay = (np.where(rel >= 0, np.exp(log_g[:, None, None] * np.maximum(rel, f32(0.0))), f32(0.0))
             * k_scale).astype(f32)
    lanes = lambda a: np.ascontiguousarray(np.broadcast_to(a[:, :, None], (RET_HEADS, c, RET_DK)), dtype=f32)
    w_k = lanes(np.exp(log_g[:, None] * (f32(c - 1) - idx)[None, :]) * k_scale)
    w_q = lanes(np.exp(log_g[:, None] * (idx + f32(1.0))[None, :]))
    g_chunk = np.broadcast_to(np.exp(log_g * f32(c))[:, None, None], (RET_HEADS, 1, RET_DV)).astype(f32)

    row = lambda b, n: b * nc + n
    return pl.pallas_call(
        _retention_kernel,
        grid=(batch, nc),
        in_specs=[
            pl.BlockSpec((rows, hq), lambda b, n: (row(b, n), B_QR // hq)),
            pl.BlockSpec((rows, hq), lambda b, n: (row(b, n), B_KR // hq)),
            pl.BlockSpec((rows, hv), lambda b, n: (row(b, n), B_VR // hv)),
            pl.BlockSpec((rows, hv), lambda b, n: (row(b, n), B_GR // hv)),
            pl.BlockSpec((rows, RET_DK), lambda b, n: (n, 0)),
            pl.BlockSpec((rows, RET_DK), lambda b, n: (n, 0)),
            pl.BlockSpec((RET_HEADS, c, c), lambda b, n: (0, 0, 0)),
            pl.BlockSpec((RET_HEADS, c, RET_DK), lambda b, n: (0, 0, 0)),
            pl.BlockSpec((RET_HEADS, c, RET_DK), lambda b, n: (0, 0, 0)),
            pl.BlockSpec((RET_HEADS, 1, RET_DV), lambda b, n: (0, 0, 0)),
            pl.BlockSpec((1, hv), lambda b, n: (0, 0)),
        ],
        out_specs=pl.BlockSpec((rows, hv), lambda b, n: (row(b, n), 0)),
        out_shape=jax.ShapeDtypeStruct((batch * seq, hv), BF16),
        scratch_shapes=[pltpu.VMEM((RET_HEADS, RET_DK, RET_DV), F32),
                        pltpu.VMEM((rows, hv), F32)],
        compiler_params=_params(("parallel", "arbitrary"), 40),
        name="retention",
    )(zb, zb, zb, zb, cos, sin, decay, w_q, w_k, g_chunk, gn_w.reshape(1, hv))


def _merge_kernel(on_ref, or_ref, ga_ref, gb_ref, x_ref, wa_ref, wb_ref, wo_ref, h_ref):
    a = _dot(on_ref[...], wa_ref[...])
    b = _dot(or_ref[...], wb_ref[...])
    merged = jax.nn.sigmoid(ga_ref[...]) * a + jax.nn.sigmoid(gb_ref[...]) * b
    h_ref[...] = x_ref[...] + _dot(merged.astype(BF16), wo_ref[...])


def _merge(o_nsa, o_ret, zb, x, w_a, w_b, w_out, tm=256):
    m, d = x.shape
    resident = lambda shape: pl.BlockSpec(shape, lambda i: (0, 0), pipeline_mode=pl.Buffered(1))
    return pl.pallas_call(
        _merge_kernel,
        grid=(m // tm,),
        in_specs=[
            pl.BlockSpec((tm, d), lambda i: (i, 0)),
            pl.BlockSpec((tm, d), lambda i: (i, 0)),
            pl.BlockSpec((tm, d), lambda i: (i, B_GA // d)),
            pl.BlockSpec((tm, d), lambda i: (i, B_GB // d)),
            pl.BlockSpec((tm, d), lambda i: (i, 0)),
            resident(w_a.shape), resident(w_b.shape), resident(w_out.shape),
        ],
        out_specs=pl.BlockSpec((tm, d), lambda i: (i, 0)),
        out_shape=jax.ShapeDtypeStruct((m, d), F32),
        compiler_params=_params(("parallel",), 56),
        name="merge_out_proj",
    )(o_nsa, o_ret, zb, zb, x, w_a, w_b, w_out)


def _mem_kv_kernel(m_ref, nw_ref, wk_ref, wv_ref, k_ref, v_ref):
    mn = _rms(m_ref[...], nw_ref[...]).astype(BF16)
    k_ref[...] = _dot(mn, wk_ref[...]).astype(k_ref.dtype)
    v_ref[...] = _dot(mn, wv_ref[...]).astype(v_ref.dtype)


def _mem_kv(mem2, nw, wk, wv, tm=256):
    m, d = mem2.shape
    n = wk.shape[1]
    out = jax.ShapeDtypeStruct((m, n), BF16)
    return pl.pallas_call(
        _mem_kv_kernel,
        grid=(m // tm,),
        in_specs=[
            pl.BlockSpec((tm, d), lambda i: (i, 0)),
            pl.BlockSpec((1, d), lambda i: (0, 0)),
            pl.BlockSpec((d, n), lambda i: (0, 0)),
            pl.BlockSpec((d, n), lambda i: (0, 0)),
        ],
        out_specs=[pl.BlockSpec((tm, n), lambda i: (i, 0))] * 2,
        out_shape=[out, out],
        compiler_params=_params(("parallel",), 40),
        name="mem_kv_proj",
    )(mem2, nw.reshape(1, d), wk, wv)


def _cross_kernel(h_ref, xw_ref, mw_ref, wq_ref, kx_ref, vx_ref, wo_ref, h2_ref, nm_ref):
    h = h_ref[...]
    nx = _rms(h, xw_ref[...]).astype(BF16)
    qx = (_dot(nx, wq_ref[...]) * (X_DH ** -0.5 * LOG2E)).astype(BF16)
    outs = []
    for hh in range(X_HEADS):
        cols = slice(hh * X_DH, (hh + 1) * X_DH)
        s = _dot_nt(qx[:, cols], kx_ref[:, cols])
        e = jnp.exp2(s - jnp.max(s, axis=-1, keepdims=True))
        vx = vx_ref[:, cols]
        ev = _dot(e.astype(BF16), jnp.concatenate([vx, jnp.ones(vx.shape, vx.dtype)], axis=1))
        outs.append(ev[:, :X_DH] / ev[:, X_DH:])
    ox = jnp.concatenate(outs, axis=-1).astype(BF16)
    h2 = h + _dot(ox, wo_ref[...])
    h2_ref[...] = h2
    nm_ref[...] = _rms(h2, mw_ref[...]).astype(nm_ref.dtype)


def _cross_attention(h1, x_norm_w, mlp_norm_w, wq, kx, vx, wo, seq, tm=512):
    m, d = h1.shape
    n = wq.shape[1]
    per_batch = seq // tm
    vec = lambda: pl.BlockSpec((1, d), lambda i: (0, 0))
    return pl.pallas_call(
        _cross_kernel,
        grid=(m // tm,),
        in_specs=[
            pl.BlockSpec((tm, d), lambda i: (i, 0)),
            vec(), vec(),
            pl.BlockSpec((d, n), lambda i: (0, 0)),
            pl.BlockSpec((MEM_LEN, n), lambda i: (i // per_batch, 0)),
            pl.BlockSpec((MEM_LEN, n), lambda i: (i // per_batch, 0)),
            pl.BlockSpec((n, d), lambda i: (0, 0)),
        ],
        out_specs=[pl.BlockSpec((tm, d), lambda i: (i, 0))] * 2,
        out_shape=[jax.ShapeDtypeStruct((m, d), F32), jax.ShapeDtypeStruct((m, d), BF16)],
        compiler_params=_params(("parallel",), 40),
        name="cross_attention",
    )(h1, x_norm_w.reshape(1, d), mlp_norm_w.reshape(1, d), wq, kx, vx, wo)


def _mlp_kernel(nm_ref, wu_ref, wd_ref, h_ref, fw_ref, o_ref):
    j = pl.program_id(1)

    @pl.when(j == 0)
    def _():
        o_ref[...] = jnp.zeros(o_ref.shape, F32)

    u = jnp.maximum(_dot(nm_ref[...], wu_ref[...]), 0.0)
    o_ref[...] += _dot((u * u).astype(BF16), wd_ref[...])

    @pl.when(j == pl.num_programs(1) - 1)
    def _():
        o_ref[...] = _rms(h_ref[...] + o_ref[...], fw_ref[...])


def _mlp(nm, w_up, w_down, h2, final_w, tm=512, tf=2048):
    m, d = nm.shape
    f = w_up.shape[1]
    return pl.pallas_call(
        _mlp_kernel,
        grid=(m // tm, f // tf),
        in_specs=[
            pl.BlockSpec((tm, d), lambda i, j: (i, 0)),
            pl.BlockSpec((d, tf), lambda i, j: (0, j)),
            pl.BlockSpec((tf, d), lambda i, j: (j, 0)),
            pl.BlockSpec((tm, d), lambda i, j: (i, 0)),
            pl.BlockSpec((1, d), lambda i, j: (0, 0)),
        ],
        out_specs=pl.BlockSpec((tm, d), lambda i, j: (i, 0)),
        out_shape=jax.ShapeDtypeStruct((m, d), F32),
        compiler_params=_params(("parallel", "arbitrary"), 60),
        name="mlp_final_norm",
    )(nm, w_up, w_down, h2, final_w.reshape(1, d))


def _layer(h, mem, attn_norm_w, w_in, cmp_pe_k, cmp_w1_k, cmp_w2_k, cmp_pe_v, cmp_w1_v, cmp_w2_v,
           w_a, ret_gn_w, w_b, w_out, x_norm_w, mem_norm_w, wq_x, wk_x, wv_x, wo_x,
           mlp_norm_w, w_up, w_down, out_norm_w, batch, seq):
    blk = W_IN_BLOCK
    kv_block = W_IN_KV // blk
    gate_shift = NSA_HEADS * 3

    wt = w_in.T
    skip_kv = lambda j: j + (j >= kv_block).astype(jnp.int32)
    kv, n, zg, wo_b, c2k_b, c2v_b = _in_proj(
        h, wt, lambda j: kv_block, 1, BF16, regroup=CMP_STRIDE, norm_w=attn_norm_w,
        side_block=W_IN_NSA_GATE // LANES, casts=(wo_x, cmp_w2_k, cmp_w2_v), name="in_proj_kv")
    a_steps = 4 * (h.shape[0] // 1024)
    b_first = W_IN_NSA_GATE // (B_WIDTH // a_steps)
    za, c1k_b, c1v_b, wt_b = _in_proj(
        n, wt, skip_kv, 4, BF16, casts=(cmp_w1_k, cmp_w1_v),
        lead_scale=(NSA_Q_SCALE, NSA_HEADS * NSA_DK // blk),
        slab=(wt, B_WIDTH, lambda s: s + b_first, gate_shift), name="in_proj_a")
    zb, = _proj(n, wt_b, F32, name="in_proj_b")

    assert seq // CMP_STRIDE == LANES
    rows_kv = NSA_GROUPS * batch * LANES
    k2 = kv.reshape(2 * rows_kv, CMP_STRIDE * NSA_DK)
    kc = _compress(k2, 0, rows_kv, cmp_pe_k.reshape(1, -1), c1k_b, c2k_b)
    vc = _compress(k2, rows_kv, rows_kv, cmp_pe_v.reshape(1, -1), c1v_b, c2v_b)

    o_nsa, wa_b, wb_b, wout_b, wq_b, wk_b, wv_b, wup_b, wdown_b = _nsa_attention(
        za, zg, kc, vc, batch, seq, casts=(w_a, w_b, w_out, wq_x, wk_x, wv_x, w_up, w_down))
    o_ret = _retention(zb, ret_gn_w, batch, seq)
    h1 = _merge(o_nsa, o_ret, zb, h, wa_b, wb_b, wout_b)

    kx, vx = _mem_kv(mem.reshape(batch * MEM_LEN, D_MODEL), mem_norm_w, wk_b, wv_b)
    h2, nm = _cross_attention(h1, x_norm_w, mlp_norm_w, wq_b, kx, vx, wo_b, seq)
    return _mlp(nm, wup_b, wdown_b, h2, out_norm_w)


def kernel(x, mem, attn_norm_w, w_in, cmp_pe_k, cmp_w1_k, cmp_w2_k, cmp_pe_v, cmp_w1_v, cmp_w2_v, w_a, ret_gn_w,
           w_b, w_out, x_norm_w, mem_norm_w, wq_x, wk_x, wv_x, wo_x, mlp_norm_w, w_up, w_down, final_norm_w):
    batch, seq, d = x.shape
    depth = w_in.shape[0]
    assert depth == 1
    h = x.reshape(batch * seq, d)
    out = _layer(h, mem, attn_norm_w[0], w_in[0], cmp_pe_k[0], cmp_w1_k[0], cmp_w2_k[0],
                 cmp_pe_v[0], cmp_w1_v[0], cmp_w2_v[0], w_a[0], ret_gn_w[0], w_b[0], w_out[0],
                 x_norm_w[0], mem_norm_w[0], wq_x[0], wk_x[0], wv_x[0], wo_x[0],
                 mlp_norm_w[0], w_up[0], w_down[0], final_norm_w, batch, seq)
    return out.reshape(batch, seq, d)
```

```python
import functools

import jax
import jax.numpy as jnp
import numpy as np
from jax import lax
from jax.experimental import pallas as pl
from jax.experimental.pallas import tpu as pltpu

F32 = jnp.float32
BF16 = jnp.bfloat16

D_MODEL = 2048
MEM_LEN = 256
NSA_HEADS = 16
NSA_GROUPS = 4
NSA_REP = NSA_HEADS // NSA_GROUPS
NSA_DK = 128
NSA_DV = 128
CMP_LEN = 32
CMP_STRIDE = 16
CMP_HIDDEN = 1024
SEL_LEN = 64
SEL_TOPK = 16
WIN = 512
RET_HEADS = 8
RET_DK = 128
RET_DV = 256
RET_CHUNK = 128
ROPE_BASE = 10000.0
X_HEADS = 4
X_DH = 128
D_FF = 4 * D_MODEL
EPS = 1e-6
NEG = -1e30
LOG2E = 1.4426950408889634
NSA_Q_SCALE = NSA_DK ** -0.5 * LOG2E

LANES = 128
F32_SUBLANES = 8
MXU_ROWS = 256
V7X_VMEM_MIB = 64

W_IN_BLOCK = 1024
W_IN_KV = 2048
W_IN_NSA_GATE = 5120
A_QN = 0
A_KS = 2048
A_VS = 2560
A_KW = 3072
A_VW = 3584
B_QR = 0
B_KR = 1024
B_VR = 2048
B_GR = 4096
B_GA = 6144
B_GB = 8192
B_WIDTH = 10240

NT_DIMS = (((1,), (1,)), ((), ()))
TN_DIMS = (((0,), (0,)), ((), ()))


def _params(sem, vmem_mib):
    assert vmem_mib < V7X_VMEM_MIB
    return pltpu.CompilerParams(dimension_semantics=sem, vmem_limit_bytes=vmem_mib * 1024 * 1024)


def _rms(x, w):
    return x * lax.rsqrt(jnp.mean(x * x, axis=-1, keepdims=True) + EPS) * w


def _dot(a, b):
    return jnp.dot(a, b, preferred_element_type=F32)


def _dot_nt(a, b):
    return lax.dot_general(a, b, NT_DIMS, preferred_element_type=F32)


CAST_ROWS = 16
SLAB_TAIL_ROWS = 64


def _cast_rows(dst_ref, dst0, src_ref, src0, nrows):
    def body(r, carry):
        off = r * CAST_ROWS
        dst_ref[pl.ds(pl.multiple_of(dst0 + off, CAST_ROWS), CAST_ROWS), :] = (
            src_ref[pl.ds(pl.multiple_of(src0 + off, F32_SUBLANES), CAST_ROWS), :].astype(BF16))
        return carry

    lax.fori_loop(0, nrows // CAST_ROWS, body, 0)


def _slab_inputs(off):
    return 0 if off is None else (2 if off else 1)


def _convert_slab(in_refs, out_ref, off):
    main_ref = in_refs[0]
    keep = main_ref.shape[0] - off
    out_ref[:keep, :] = main_ref[off:, :].astype(BF16)
    if off:
        out_ref[keep:, :] = in_refs[1][:off, :].astype(BF16)


def _slab_specs(arr, n_rows, n_steps, step_of, block_of, off):
    rps = n_rows // n_steps
    cols = arr.shape[1]
    assert n_rows % n_steps == 0 and off % CAST_ROWS == 0 and (rps - off) % CAST_ROWS == 0
    ins = [pl.BlockSpec((rps, cols), lambda *g: (block_of(step_of(*g)), 0))]
    if off:
        assert rps % SLAB_TAIL_ROWS == 0 and off <= SLAB_TAIL_ROWS
        per = rps // SLAB_TAIL_ROWS
        ins.append(pl.BlockSpec((SLAB_TAIL_ROWS, cols), lambda *g: ((block_of(step_of(*g)) + 1) * per, 0)))
    out = pl.BlockSpec((rps, cols), lambda *g: (step_of(*g), 0))
    return ins, out, jax.ShapeDtypeStruct((n_rows, cols), BF16), [arr] * len(ins)


def _in_proj_kernel(*refs, shift, regroup, n_cast, norm, side, lead_scale, slab_off=None):
    n_in = 4 if norm else 3
    slab = slab_off is not None
    n_slab_in = _slab_inputs(slab_off)
    n_ref, w_ref, wnext_ref = refs[:3]
    cast_in = refs[n_in:n_in + n_cast]
    o_idx = n_in + n_cast + n_slab_in
    o_ref = refs[o_idx]
    n_out = o_idx + 1 + int(norm) + int(side)
    cast_out = refs[n_out:n_out + n_cast]
    wb_sc, *rest = refs[n_out + n_cast + int(slab):]
    tn = wb_sc.shape[0]
    for src, dst in zip(cast_in, cast_out):
        dst[...] = src[...].astype(dst.dtype)
    if slab:
        _convert_slab(refs[n_in + n_cast:o_idx], refs[n_out + n_cast], slab_off)

    @pl.when(pl.program_id(1) == 0)
    def _():
        _cast_rows(wb_sc, 0, w_ref, shift, tn - shift)
        if shift:
            _cast_rows(wb_sc, tn - shift, wnext_ref, 0, shift)
        if side:
            _cast_rows(rest[-1], 0, wnext_ref, 0, LANES)

    if norm:
        xn = _rms(n_ref[...], refs[3][...]).astype(BF16)
        refs[o_idx + 1][...] = xn
    else:
        xn = n_ref[...]
    if side:
        refs[n_out - 1][...] = _dot_nt(xn, rest[-1][...])
    if regroup:
        r_sc = rest[0]
        tm = xn.shape[0]
        pair = 2 * LANES
        for cc in range(tn // pair):
            res = _dot_nt(xn, wb_sc[cc * pair:(cc + 1) * pair, :])
            for c in (2 * cc, 2 * cc + 1):
                r_sc[c] = res[:, (c % 2) * LANES:(c % 2 + 1) * LANES]
                for j in range(regroup):
                    o_ref[c, :, j * LANES:(j + 1) * LANES] = (
                        r_sc[c, pl.ds(j, tm // regroup, stride=regroup), :].astype(o_ref.dtype))
    else:
        res = _dot_nt(xn, wb_sc[...])
        if lead_scale is not None:
            factor, blocks = lead_scale
            res = res * jnp.where(pl.program_id(0) < blocks, factor, 1.0)
        o_ref[...] = res.astype(o_ref.dtype)


def _in_proj(n, wt, src_block, n_blocks, out_dtype, *, shift=0, regroup=0, casts=(), norm_w=None,
             side_block=None, lead_scale=None, slab=None, tm=1024, tn=1024, name):
    m, k = n.shape
    per = tn // LANES
    m_tiles = m // tm
    norm = norm_w is not None
    side = side_block is not None
    assert shift % CAST_ROWS == 0 and shift <= LANES and not (norm and n_blocks > 1)
    assert not side or (shift == 0 and n_blocks == 1)
    scratch = [pltpu.VMEM((tn, k), BF16)]
    row_tile = pl.BlockSpec((tm, k), lambda j, i: (i, 0))
    once = {"pipeline_mode": pl.Buffered(1)} if n_blocks == 1 else {}
    if regroup:
        assert n_blocks == 1
        out_shape = jax.ShapeDtypeStruct((per, m // regroup, regroup * LANES), out_dtype)
        out_spec = pl.BlockSpec((per, tm // regroup, regroup * LANES), lambda j, i: (0, i, 0))
        scratch.append(pltpu.VMEM((per, tm, LANES), F32))
    else:
        out_shape = jax.ShapeDtypeStruct((m, n_blocks * tn), out_dtype)
        out_spec = pl.BlockSpec((tm, tn), lambda j, i: (i, j))

    cast_steps = 1 << ((n_blocks * m_tiles).bit_length() - 1)
    cast_specs = []
    for a in casts:
        rows = a.shape[0] // cast_steps
        assert a.shape[0] % cast_steps == 0 and rows % CAST_ROWS == 0
        cast_specs.append(pl.BlockSpec(
            (rows, a.shape[1]), lambda j, i: (jnp.minimum(j * m_tiles + i, cast_steps - 1), 0)))

    if side:
        scratch.append(pltpu.VMEM((LANES, k), BF16))
    next_rows = (lambda j, i: (side_block, 0)) if side else (lambda j, i: ((src_block(j) + 1) * per, 0))

    slab_in, slab_out_spec, slab_out_shape, slab_args, slab_off = [], [], [], [], None
    if slab is not None:
        arr, n_rows, block_of, slab_off = slab
        ins, out, shape, slab_args = _slab_specs(arr, n_rows, n_blocks * m_tiles, lambda j, i: j * m_tiles + i,
                                                 block_of, slab_off)
        slab_in, slab_out_spec, slab_out_shape = ins, [out], [shape]

    return pl.pallas_call(
        functools.partial(_in_proj_kernel, shift=shift, regroup=regroup, n_cast=len(casts), norm=norm, side=side,
                          lead_scale=lead_scale, slab_off=slab_off),
        grid=(n_blocks, m_tiles),
        in_specs=[
            row_tile,
            pl.BlockSpec((tn, k), lambda j, i: (src_block(j), 0), **once),
            pl.BlockSpec((LANES, k), next_rows, **once),
            *([pl.BlockSpec((1, k), lambda j, i: (0, 0))] if norm else []),
            *cast_specs,
            *slab_in,
        ],
        out_specs=[out_spec, *([row_tile] if norm else []),
                   *([pl.BlockSpec((tm, LANES), lambda j, i: (i, 0))] if side else []), *cast_specs,
                   *slab_out_spec],
        out_shape=[out_shape, *([jax.ShapeDtypeStruct((m, k), BF16)] if norm else []),
                   *([jax.ShapeDtypeStruct((m, LANES), F32)] if side else []),
                   *[jax.ShapeDtypeStruct(a.shape, BF16) for a in casts], *slab_out_shape],
        scratch_shapes=scratch,
        compiler_params=_params(("arbitrary", "arbitrary"), 60 if (norm or slab is not None) else 56),
        name=name,
    )(n, wt, wt, *([norm_w.reshape(1, k)] if norm else []), *casts, *slab_args)


def _proj_kernel(n_ref, w_ref, *refs, n_cast, slab_off, lead_scale):
    n_slab_in = _slab_inputs(slab_off)
    o_ref = refs[n_cast + n_slab_in]
    for src, dst in zip(refs[:n_cast], refs[n_cast + n_slab_in + 1:]):
        dst[...] = src[...].astype(dst.dtype)
    if slab_off is not None:
        _convert_slab(refs[n_cast:n_cast + n_slab_in], refs[-1], slab_off)
    res = _dot_nt(n_ref[...], w_ref[...])
    if lead_scale is not None:
        factor, blocks = lead_scale
        res = res * jnp.where(pl.program_id(0) < blocks, factor, 1.0)
    o_ref[...] = res.astype(o_ref.dtype)


def _proj(n, wt_bf16, out_dtype, *, casts=(), slab=None, lead_scale=None, tm=1024, tn=2048, name):
    m, k = n.shape
    n_cols = wt_bf16.shape[0]
    n_blocks, m_tiles = n_cols // tn, m // tm
    cast_steps = 1 << ((n_blocks * m_tiles).bit_length() - 1)
    cast_specs = []
    for a in casts:
        rows = a.shape[0] // cast_steps
        assert a.shape[0] % cast_steps == 0 and rows % CAST_ROWS == 0
        cast_specs.append(pl.BlockSpec(
            (rows, a.shape[1]), lambda j, i: (jnp.minimum(j * m_tiles + i, cast_steps - 1), 0)))
    slab_in, slab_out_spec, slab_out_shape, slab_args, slab_off = [], [], [], [], None
    if slab is not None:
        arr, n_rows, block_of, slab_off = slab
        ins, out, shape, slab_args = _slab_specs(arr, n_rows, n_blocks * m_tiles, lambda j, i: j * m_tiles + i,
                                                 block_of, slab_off)
        slab_in, slab_out_spec, slab_out_shape = ins, [out], [shape]
    return pl.pallas_call(
        functools.partial(_proj_kernel, n_cast=len(casts), slab_off=slab_off, lead_scale=lead_scale),
        grid=(n_blocks, m_tiles),
        in_specs=[pl.BlockSpec((tm, k), lambda j, i: (i, 0)), pl.BlockSpec((tn, k), lambda j, i: (j, 0)),
                  *cast_specs, *slab_in],
        out_specs=[pl.BlockSpec((tm, tn), lambda j, i: (i, j)), *cast_specs, *slab_out_spec],
        out_shape=[jax.ShapeDtypeStruct((m, n_cols), out_dtype),
                   *[jax.ShapeDtypeStruct(a.shape, BF16) for a in casts], *slab_out_shape],
        compiler_params=_params(("arbitrary", "arbitrary"), 58),
        name=name,
    )(n, wt_bf16, *casts, *slab_args)


def _compress_kernel(k2_ref, pe_ref, w1_ref, w2_ref, o_ref):
    rows, half = k2_ref.shape
    k2 = k2_ref[...].astype(F32)
    a_lo = (k2 + pe_ref[:, :half]).astype(BF16)
    a_hi = (k2 + pe_ref[:, half:]).astype(BF16)
    lo = _dot(a_lo, w1_ref[:half, :])
    hi = _dot(a_hi, w1_ref[half:, :])
    h = lo + pltpu.roll(hi, rows - 1, axis=0)
    act = h * jax.nn.sigmoid(h)
    o_ref[...] = _dot(act.astype(BF16), w2_ref[...]).astype(o_ref.dtype)


def _compress(k2, first_row, m, pe, w1, w2, rows=512):
    half = k2.shape[1]
    hid = w1.shape[1]
    dout = w2.shape[1]
    first = first_row // rows
    return pl.pallas_call(
        _compress_kernel,
        grid=(m // rows,),
        in_specs=[
            pl.BlockSpec((rows, half), lambda i: (first + i, 0)),
            pl.BlockSpec((1, 2 * half), lambda i: (0, 0)),
            pl.BlockSpec((2 * half, hid), lambda i: (0, 0)),
            pl.BlockSpec((hid, dout), lambda i: (0, 0)),
        ],
        out_specs=pl.BlockSpec((rows, dout), lambda i: (i, 0)),
        out_shape=jax.ShapeDtypeStruct((m, dout), BF16),
        compiler_params=_params(("parallel",), 40),
        name="nsa_compress",
    )(k2, pe, w1, w2)


def _nsa_kernel(*refs, seq, tq, kblk, gp, hm, n_cast):
    q_ref = refs[0]
    kc_refs = refs[1:1 + gp]
    vc_refs = refs[1 + gp:1 + 2 * gp]
    ks_ref, vs_ref, kw_ref, vw_ref, g_ref, e_ref = refs[1 + 2 * gp:7 + 2 * gp]
    cast_in = refs[7 + 2 * gp:7 + 2 * gp + n_cast]
    o_ref = refs[7 + 2 * gp + n_cast]
    cast_out = refs[8 + 2 * gp + n_cast:8 + 2 * gp + 2 * n_cast]
    s_sc, mx_sc, acc_sc, po_sc = refs[8 + 2 * gp + 2 * n_cast:]
    i = pl.program_id(2)
    t0 = i * tq
    rep = NSA_REP
    n_cmp = (seq - CMP_LEN) // CMP_STRIDE + 1
    n_sel = seq // SEL_LEN
    topk = min(SEL_TOPK, n_sel)
    groups = range(gp)

    q = q_ref[...]
    units = range(rep // hm)
    rows = lambda gg, r: slice((gg * rep + r) * tq, (gg * rep + r + 1) * tq)
    urows = lambda gg, u: slice((gg * rep + u * hm) * tq, (gg * rep + (u + 1) * hm) * tq)
    part = lambda x, h: x[h * tq:(h + 1) * tq]
    head_q = lambda gg, r: q[:, (gg * rep + r) * NSA_DK:(gg * rep + r + 1) * NSA_DK]
    unit_q = {(gg, u): jnp.concatenate([head_q(gg, u * hm + h) for h in range(hm)], axis=0)
              for gg in groups for u in units}
    group_cols = lambda ref, gg: ref.at[:, gg * LANES:(gg + 1) * LANES]

    def masked(s, mask1):
        return jnp.concatenate([jnp.where(mask1, part(s, h), NEG) for h in range(hm)], axis=0)
    tcol = t0 + lax.broadcasted_iota(jnp.int32, (tq, 1), 0)
    per_group = rep * 3
    g_sig = jax.nn.sigmoid(g_ref[...])
    first_group = pl.program_id(1) * gp
    gs = [pltpu.roll(g_sig, lax.rem(LANES - (first_group + gg) * per_group, LANES), axis=1) for gg in groups]
    gate = lambda gg, r, branch: gs[gg][:, 3 * r + branch:3 * r + branch + 1]

    def exp_rows(sm):
        return jnp.exp2(sm - jnp.max(sm, axis=-1, keepdims=True))

    def with_ones(v):
        return jnp.concatenate([v, jnp.ones(v.shape, v.dtype)], axis=1)

    def normalised(ev):
        return ev[:, :NSA_DV] / ev[:, NSA_DV:]

    c_idx = lax.broadcasted_iota(jnp.int32, (tq, LANES), 1)
    mask_c = ((c_idx * CMP_STRIDE + (CMP_LEN - 1)) <= tcol) & (c_idx < n_cmp)
    mask_cf = jnp.where(mask_c, 1.0, 0.0)
    o_cmp, psum = {}, []
    for gg in groups:
        kc = kc_refs[gg][...]
        vc = vc_refs[gg][...]
        tot = None
        for u in units:
            e = exp_rows(masked(_dot_nt(unit_q[gg, u], kc), mask_c))
            p = e / jnp.sum(e, axis=-1, keepdims=True)
            p = jnp.concatenate([part(p, h) * mask_cf for h in range(hm)], axis=0)
            o_cmp[gg, u] = _dot(p.astype(BF16), vc)
            for h in range(hm):
                tot = part(p, h) if tot is None else tot + part(p, h)
        psum.append(tot)

    wlen = WIN + tq
    start = pl.multiple_of(jnp.maximum(i - WIN // tq, 0) * tq, tq)
    dlt = tcol - (start + lax.broadcasted_iota(jnp.int32, (tq, wlen), 1))
    mask_w = (dlt >= 0) & (dlt < WIN)
    for gg in groups:
        kw = group_cols(kw_ref, gg)[pl.ds(start, wlen), :]
        vw1 = with_ones(group_cols(vw_ref, gg)[pl.ds(start, wlen), :])
        for u in units:
            e = exp_rows(masked(_dot_nt(unit_q[gg, u], kw), mask_w))
            o_win = normalised(_dot(e.astype(BF16), vw1))
            for h in range(hm):
                r = u * hm + h
                po_sc[rows(gg, r), :] = (gate(gg, r, 0) * part(o_cmp[gg, u], h)
                                         + gate(gg, r, 2) * part(o_win, h))

    jo = lax.broadcasted_iota(jnp.int32, (n_sel, LANES), 0)
    co = lax.broadcasted_iota(jnp.int32, (n_sel, LANES), 1)
    ov_t = jnp.where((co * CMP_STRIDE < jo * SEL_LEN + SEL_LEN) & (co * CMP_STRIDE + CMP_LEN > jo * SEL_LEN)
                     & (co < n_cmp), 1.0, 0.0).astype(BF16)
    j_idx = lax.broadcasted_iota(jnp.int32, (n_sel, tq), 0)
    cur = lax.shift_right_logical(t0 + lax.broadcasted_iota(jnp.int32, (n_sel, tq), 1), int(np.log2(SEL_LEN)))
    forced = (j_idx == 0) | (j_idx == cur) | (j_idx == cur - 1)
    future = j_idx > cur
    q_bias = {}
    for gg in groups:
        p_hi = psum[gg].astype(BF16)
        p_lo = (psum[gg] - p_hi.astype(F32)).astype(BF16)
        imp = _dot_nt(ov_t, p_hi) + _dot_nt(ov_t, p_lo)
        impm = jnp.where(forced, jnp.inf, jnp.where(future, -jnp.inf, imp))
        rank = jnp.zeros((n_sel, tq), F32)
        for ii in range(n_sel):
            row = impm[ii:ii + 1, :]
            beats = (row > impm) | ((row == impm) & (j_idx > ii))
            rank = rank + jnp.where(beats, 1.0, 0.0)
        bias_t = jnp.where(rank < topk, 0.0, NEG)
        bias = jnp.concatenate([bias_t, jnp.zeros((LANES - n_sel, tq), F32)], axis=0).T.astype(BF16)
        for u in units:
            q_bias[gg, u] = jnp.concatenate([unit_q[gg, u], jnp.concatenate([bias] * hm, axis=0)], axis=1)

    n_chunks = lax.div(t0 + tq + (kblk - 1), kblk)

    def lane_fold_max(x):
        out = x[:, :LANES]
        for c in range(1, kblk // LANES):
            out = jnp.maximum(out, x[:, c * LANES:(c + 1) * LANES])
        return out

    def chunk_start(kb):
        return kb * kblk if isinstance(kb, int) else pl.multiple_of(kb * kblk, kblk)

    def score_chunk(kb):
        k0 = chunk_start(kb)
        causal = (k0 + lax.broadcasted_iota(jnp.int32, (tq, kblk), 1)) <= tcol
        folds = []
        for gg in groups:
            k = jnp.concatenate([group_cols(ks_ref, gg)[pl.ds(k0, kblk), :], e_ref[kb]], axis=1)
            for u in units:
                sm_ = masked(_dot_nt(q_bias[gg, u], k), causal)
                s_sc[kb, urows(gg, u), :] = sm_
                folds.append(lane_fold_max(sm_))
        return jnp.concatenate(folds, axis=0)

    def score_pass(kb, carry):
        mx_sc[...] = jnp.maximum(mx_sc[...], score_chunk(kb))
        return carry

    mx_sc[...] = score_chunk(0)
    lax.fori_loop(1, n_chunks, score_pass, 0)
    m_sel = jnp.max(mx_sc[...], axis=-1, keepdims=True)
    for src, dst in zip(cast_in, cast_out):
        dst[...] = src[...].astype(dst.dtype)

    def value_chunk(kb):
        k0 = chunk_start(kb)
        pvs = []
        for gg in groups:
            v1 = with_ones(group_cols(vs_ref, gg)[pl.ds(k0, kblk), :])
            for u in units:
                pk = jnp.exp2(s_sc[kb, urows(gg, u), :] - m_sel[urows(gg, u)])
                pvs.append(_dot(pk.astype(BF16), v1))
        return jnp.concatenate(pvs, axis=0)

    def value_pass(kb, carry):
        acc_sc[...] += value_chunk(kb)
        return carry

    acc_sc[...] = value_chunk(0)
    lax.fori_loop(1, n_chunks, value_pass, 0)

    for gg in groups:
        for r in range(rep):
            o = po_sc[rows(gg, r), :] + gate(gg, r, 1) * normalised(acc_sc[rows(gg, r), :])
            col = (gg * rep + r) * NSA_DV
            o_ref[:, col:col + NSA_DV] = o.astype(o_ref.dtype)


def _nsa_attention(za, zg, kc, vc, batch, seq, casts=(), tq=256, kblk=512, gp=2):
    assert seq % kblk == 0 and seq >= WIN + tq and WIN % tq == 0 and NSA_GROUPS % gp == 0
    nq = seq // tq
    gw = gp * NSA_REP * NSA_DK
    streams = gp * NSA_REP * tq
    hm = max(1, MXU_ROWS // tq)
    assert NSA_REP % hm == 0
    kern = functools.partial(_nsa_kernel, seq=seq, tq=tq, kblk=kblk, gp=gp, hm=hm, n_cast=len(casts))
    n_g = NSA_GROUPS // gp
    steps = batch * n_g * nq
    cast_specs = []
    for a in casts:
        assert a.shape[0] % steps == 0 and (a.shape[0] // steps) % CAST_ROWS == 0
        cast_specs.append(pl.BlockSpec((a.shape[0] // steps, a.shape[1]),
                                       lambda b, g, i: ((b * n_g + g) * nq + i, 0)))
    key = np.arange(seq).reshape(seq // kblk, kblk, 1)
    expand = jnp.asarray(key // SEL_LEN == np.arange(LANES).reshape(1, 1, LANES), BF16)

    def kv_spec(off):
        return pl.BlockSpec((seq, gp * LANES), lambda b, g, i: (b, off // (gp * LANES) + g))

    def cmp_spec(gg):
        return pl.BlockSpec((LANES, NSA_DK), lambda b, g, i: ((g * gp + gg) * batch + b, 0))

    return pl.pallas_call(
        kern,
        grid=(batch, NSA_GROUPS // gp, nq),
        in_specs=[
            pl.BlockSpec((tq, gw), lambda b, g, i: (b * nq + i, A_QN // gw + g)),
            *[cmp_spec(gg) for gg in range(gp)],
            *[cmp_spec(gg) for gg in range(gp)],
            kv_spec(A_KS), kv_spec(A_VS), kv_spec(A_KW), kv_spec(A_VW),
            pl.BlockSpec((tq, LANES), lambda b, g, i: (b * nq + i, 0)),
            pl.BlockSpec(expand.shape, lambda b, g, i: (0, 0, 0)),
            *cast_specs,
        ],
        out_specs=[pl.BlockSpec((tq, gw), lambda b, g, i: (b * nq + i, g)), *cast_specs],
        out_shape=[jax.ShapeDtypeStruct((batch * seq, NSA_HEADS * NSA_DV), BF16),
                   *[jax.ShapeDtypeStruct(a.shape, BF16) for a in casts]],
        scratch_shapes=[
            pltpu.VMEM((seq // kblk, streams, kblk), F32),
            pltpu.VMEM((streams, LANES), F32),
            pltpu.VMEM((streams, 2 * NSA_DV), F32),
            pltpu.VMEM((streams, NSA_DV), F32),
        ],
        compiler_params=_params(("parallel", "parallel", "arbitrary"), 56),
        name="nsa_attention",
    )(za, *([kc] * gp), *([vc] * gp), za, za, za, za, zg, expand, *casts)


def _retention_kernel(q_ref, k_ref, v_ref, g_ref, cos_ref, sin_ref, dec_ref, wq_ref, wk_ref,
                      gc_ref, gn_ref, o_ref, st_ref, o_sc):
    @pl.when(pl.program_id(1) == 0)
    def _():
        st_ref[...] = jnp.zeros(st_ref.shape, F32)

    c = RET_CHUNK
    even = lax.broadcasted_iota(jnp.int32, (c, RET_DK), 1) % 2 == 0
    for h in range(RET_HEADS):
        st = st_ref[h]
        wq = wq_ref[h]
        wq2 = jnp.concatenate([wq, wq], axis=1)
        cols = slice(h * RET_DV, (h + 1) * RET_DV)
        for sub in range(q_ref.shape[0] // c):
            rows = slice(sub * c, (sub + 1) * c)
            cos = cos_ref[rows, :]
            sin = sin_ref[rows, :]

            def rotate(x):
                partner = jnp.where(even, pltpu.roll(x, RET_DK - 1, axis=1), pltpu.roll(x, 1, axis=1))
                return x * cos + partner * sin

            qf = rotate(q_ref[rows, h * RET_DK:(h + 1) * RET_DK])
            kf = rotate(k_ref[rows, h * RET_DK:(h + 1) * RET_DK])
            qb = qf.astype(BF16)
            v = v_ref[rows, cols].astype(BF16)
            s = _dot_nt(qb, kf.astype(BF16)) * dec_ref[h]
            o = _dot(s.astype(BF16), v) + _dot(qb, st.astype(BF16)) * wq2
            st = st * gc_ref[h] + lax.dot_general((kf * wk_ref[h]).astype(BF16), v, TN_DIMS,
                                                  preferred_element_type=F32)
            o_sc[rows, cols] = o
        st_ref[h] = st

    for h in range(RET_HEADS):
        cols = slice(h * RET_DV, (h + 1) * RET_DV)
        o = o_sc[:, cols]
        mu = jnp.mean(o, axis=-1, keepdims=True)
        d = o - mu
        var = jnp.mean(d * d, axis=-1, keepdims=True)
        on = d * lax.rsqrt(var + EPS) * gn_ref[:, cols]
        gr = g_ref[:, cols]
        o_ref[:, cols] = (gr * jax.nn.sigmoid(gr) * on).astype(o_ref.dtype)


def _retention(zb, gn_w, batch, seq, chunks_per_step=2):
    c = RET_CHUNK
    rows = chunks_per_step * c
    nc = seq // rows
    hq = RET_HEADS * RET_DK
    hv = RET_HEADS * RET_DV
    f32 = np.float32
    inv = f32(ROPE_BASE) ** (-np.arange(0, RET_DK, 2, dtype=f32) / f32(RET_DK))
    ang = np.arange(seq, dtype=f32)[:, None] * inv[None, :]
    pairs = lambda even, odd: np.stack([even, odd], axis=-1).reshape(seq, RET_DK)
    cos = pairs(np.cos(ang), np.cos(ang))
    sin = pairs(-np.sin(ang), np.sin(ang))
    log_g = np.log1p(-np.exp2(f32(-5.0) - np.arange(RET_HEADS, dtype=f32)))
    idx = np.arange(c, dtype=f32)
    rel = idx[:, None] - idx[None, :]
    k_scale = f32(RET_DK ** -0.5)
    decay = (np.where(rel >= 0, np.exp(log_g[:, None, None] * np.maximum(rel, f32(0.0))), f32(0.0))
             * k_scale).astype(f32)
    lanes = lambda a: np.ascontiguousarray(np.broadcast_to(a[:, :, None], (RET_HEADS, c, RET_DK)), dtype=f32)
    w_k = lanes(np.exp(log_g[:, None] * (f32(c - 1) - idx)[None, :]) * k_scale)
    w_q = lanes(np.exp(log_g[:, None] * (idx + f32(1.0))[None, :]))
    g_chunk = np.broadcast_to(np.exp(log_g * f32(c))[:, None, None], (RET_HEADS, 1, RET_DV)).astype(f32)

    row = lambda b, n: b * nc + n
    return pl.pallas_call(
        _retention_kernel,
        grid=(batch, nc),
        in_specs=[
            pl.BlockSpec((rows, hq), lambda b, n: (row(b, n), B_QR // hq)),
            pl.BlockSpec((rows, hq), lambda b, n: (row(b, n), B_KR // hq)),
            pl.BlockSpec((rows, hv), lambda b, n: (row(b, n), B_VR // hv)),
            pl.BlockSpec((rows, hv), lambda b, n: (row(b, n), B_GR // hv)),
            pl.BlockSpec((rows, RET_DK), lambda b, n: (n, 0)),
            pl.BlockSpec((rows, RET_DK), lambda b, n: (n, 0)),
            pl.BlockSpec((RET_HEADS, c, c), lambda b, n: (0, 0, 0)),
            pl.BlockSpec((RET_HEADS, c, RET_DK), lambda b, n: (0, 0, 0)),
            pl.BlockSpec((RET_HEADS, c, RET_DK), lambda b, n: (0, 0, 0)),
            pl.BlockSpec((RET_HEADS, 1, RET_DV), lambda b, n: (0, 0, 0)),
            pl.BlockSpec((1, hv), lambda b, n: (0, 0)),
        ],
        out_specs=pl.BlockSpec((rows, hv), lambda b, n: (row(b, n), 0)),
        out_shape=jax.ShapeDtypeStruct((batch * seq, hv), BF16),
        scratch_shapes=[pltpu.VMEM((RET_HEADS, RET_DK, RET_DV), F32),
                        pltpu.VMEM((rows, hv), F32)],
        compiler_params=_params(("parallel", "arbitrary"), 40),
        name="retention",
    )(zb, zb, zb, zb, cos, sin, decay, w_q, w_k, g_chunk, gn_w.reshape(1, hv))


def _merge_kernel(on_ref, or_ref, ga_ref, gb_ref, x_ref, wa_ref, wb_ref, wo_ref, h_ref):
    a = _dot(on_ref[...], wa_ref[...])
    b = _dot(or_ref[...], wb_ref[...])
    merged = jax.nn.sigmoid(ga_ref[...]) * a + jax.nn.sigmoid(gb_ref[...]) * b
    h_ref[...] = x_ref[...] + _dot(merged.astype(BF16), wo_ref[...])


def _merge(o_nsa, o_ret, zb, x, w_a, w_b, w_out, tm=256):
    m, d = x.shape
    resident = lambda shape: pl.BlockSpec(shape, lambda i: (0, 0), pipeline_mode=pl.Buffered(1))
    return pl.pallas_call(
        _merge_kernel,
        grid=(m // tm,),
        in_specs=[
            pl.BlockSpec((tm, d), lambda i: (i, 0)),
            pl.BlockSpec((tm, d), lambda i: (i, 0)),
            pl.BlockSpec((tm, d), lambda i: (i, B_GA // d)),
            pl.BlockSpec((tm, d), lambda i: (i, B_GB // d)),
            pl.BlockSpec((tm, d), lambda i: (i, 0)),
            resident(w_a.shape), resident(w_b.shape), resident(w_out.shape),
        ],
        out_specs=pl.BlockSpec((tm, d), lambda i: (i, 0)),
        out_shape=jax.ShapeDtypeStruct((m, d), F32),
        compiler_params=_params(("parallel",), 56),
        name="merge_out_proj",
    )(o_nsa, o_ret, zb, zb, x, w_a, w_b, w_out)


def _mem_kv_kernel(m_ref, nw_ref, wk_ref, wv_ref, k_ref, v_ref):
    mn = _rms(m_ref[...], nw_ref[...]).astype(BF16)
    k_ref[...] = _dot(mn, wk_ref[...]).astype(k_ref.dtype)
    v_ref[...] = _dot(mn, wv_ref[...]).astype(v_ref.dtype)


def _mem_kv(mem2, nw, wk, wv, tm=256):
    m, d = mem2.shape
    n = wk.shape[1]
    out = jax.ShapeDtypeStruct((m, n), BF16)
    return pl.pallas_call(
        _mem_kv_kernel,
        grid=(m // tm,),
        in_specs=[
            pl.BlockSpec((tm, d), lambda i: (i, 0)),
            pl.BlockSpec((1, d), lambda i: (0, 0)),
            pl.BlockSpec((d, n), lambda i: (0, 0)),
            pl.BlockSpec((d, n), lambda i: (0, 0)),
        ],
        out_specs=[pl.BlockSpec((tm, n), lambda i: (i, 0))] * 2,
        out_shape=[out, out],
        compiler_params=_params(("parallel",), 40),
        name="mem_kv_proj",
    )(mem2, nw.reshape(1, d), wk, wv)


def _cross_kernel(h_ref, xw_ref, mw_ref, wq_ref, kx_ref, vx_ref, wo_ref, h2_ref, nm_ref):
    h = h_ref[...]
    nx = _rms(h, xw_ref[...]).astype(BF16)
    qx = (_dot(nx, wq_ref[...]) * (X_DH ** -0.5 * LOG2E)).astype(BF16)
    outs = []
    for hh in range(X_HEADS):
        cols = slice(hh * X_DH, (hh + 1) * X_DH)
        s = _dot_nt(qx[:, cols], kx_ref[:, cols])
        e = jnp.exp2(s - jnp.max(s, axis=-1, keepdims=True))
        vx = vx_ref[:, cols]
        ev = _dot(e.astype(BF16), jnp.concatenate([vx, jnp.ones(vx.shape, vx.dtype)], axis=1))
        outs.append(ev[:, :X_DH] / ev[:, X_DH:])
    ox = jnp.concatenate(outs, axis=-1).astype(BF16)
    h2 = h + _dot(ox, wo_ref[...])
    h2_ref[...] = h2
    nm_ref[...] = _rms(h2, mw_ref[...]).astype(nm_ref.dtype)


def _cross_attention(h1, x_norm_w, mlp_norm_w, wq, kx, vx, wo, seq, tm=512):
    m, d = h1.shape
    n = wq.shape[1]
    per_batch = seq // tm
    vec = lambda: pl.BlockSpec((1, d), lambda i: (0, 0))
    return pl.pallas_call(
        _cross_kernel,
        grid=(m // tm,),
        in_specs=[
            pl.BlockSpec((tm, d), lambda i: (i, 0)),
            vec(), vec(),
            pl.BlockSpec((d, n), lambda i: (0, 0)),
            pl.BlockSpec((MEM_LEN, n), lambda i: (i // per_batch, 0)),
            pl.BlockSpec((MEM_LEN, n), lambda i: (i // per_batch, 0)),
            pl.BlockSpec((n, d), lambda i: (0, 0)),
        ],
        out_specs=[pl.BlockSpec((tm, d), lambda i: (i, 0))] * 2,
        out_shape=[jax.ShapeDtypeStruct((m, d), F32), jax.ShapeDtypeStruct((m, d), BF16)],
        compiler_params=_params(("parallel",), 40),
        name="cross_attention",
    )(h1, x_norm_w.reshape(1, d), mlp_norm_w.reshape(1, d), wq, kx, vx, wo)


def _mlp_kernel(nm_ref, wu_ref, wd_ref, h_ref, fw_ref, o_ref):
    j = pl.program_id(1)

    @pl.when(j == 0)
    def _():
        o_ref[...] = jnp.zeros(o_ref.shape, F32)

    u = jnp.maximum(_dot(nm_ref[...], wu_ref[...]), 0.0)
    o_ref[...] += _dot((u * u).astype(BF16), wd_ref[...])

    @pl.when(j == pl.num_programs(1) - 1)
    def _():
        o_ref[...] = _rms(h_ref[...] + o_ref[...], fw_ref[...])


def _mlp(nm, w_up, w_down, h2, final_w, tm=512, tf=2048):
    m, d = nm.shape
    f = w_up.shape[1]
    return pl.pallas_call(
        _mlp_kernel,
        grid=(m // tm, f // tf),
        in_specs=[
            pl.BlockSpec((tm, d), lambda i, j: (i, 0)),
            pl.BlockSpec((d, tf), lambda i, j: (0, j)),
            pl.BlockSpec((tf, d), lambda i, j: (j, 0)),
            pl.BlockSpec((tm, d), lambda i, j: (i, 0)),
            pl.BlockSpec((1, d), lambda i, j: (0, 0)),
        ],
        out_specs=pl.BlockSpec((tm, d), lambda i, j: (i, 0)),
        out_shape=jax.ShapeDtypeStruct((m, d), F32),
        compiler_params=_params(("parallel", "arbitrary"), 60),
        name="mlp_final_norm",
    )(nm, w_up, w_down, h2, final_w.reshape(1, d))


def _layer(h, mem, attn_norm_w, w_in, cmp_pe_k, cmp_w1_k, cmp_w2_k, cmp_pe_v, cmp_w1_v, cmp_w2_v,
           w_a, ret_gn_w, w_b, w_out, x_norm_w, mem_norm_w, wq_x, wk_x, wv_x, wo_x,
           mlp_norm_w, w_up, w_down, out_norm_w, batch, seq):
    blk = W_IN_BLOCK
    kv_block = W_IN_KV // blk
    gate_shift = NSA_HEADS * 3

    wt = w_in.T
    skip_kv = lambda j: j + (j >= kv_block).astype(jnp.int32)
    kv, n, zg, wo_b, c2k_b, c2v_b = _in_proj(
        h, wt, lambda j: kv_block, 1, BF16, regroup=CMP_STRIDE, norm_w=attn_norm_w,
        side_block=W_IN_NSA_GATE // LANES, casts=(wo_x, cmp_w2_k, cmp_w2_v), name="in_proj_kv")
    a_steps = 4 * (h.shape[0] // 1024)
    b_first = W_IN_NSA_GATE // (B_WIDTH // a_steps)
    za, wt_b = _in_proj(
        n, wt, skip_kv, 4, BF16,
        lead_scale=(NSA_Q_SCALE, NSA_HEADS * NSA_DK // blk),
        slab=(wt, B_WIDTH, lambda s: s + b_first, gate_shift), name="in_proj_a")
    zb, c1k_b, c1v_b = _proj(n, wt_b, F32, casts=(cmp_w1_k, cmp_w1_v), name="in_proj_b")

    assert seq // CMP_STRIDE == LANES
    rows_kv = NSA_GROUPS * batch * LANES
    k2 = kv.reshape(2 * rows_kv, CMP_STRIDE * NSA_DK)
    kc = _compress(k2, 0, rows_kv, cmp_pe_k.reshape(1, -1), c1k_b, c2k_b)
    vc = _compress(k2, rows_kv, rows_kv, cmp_pe_v.reshape(1, -1), c1v_b, c2v_b)

    o_nsa, wa_b, wb_b, wout_b, wq_b, wk_b, wv_b, wup_b, wdown_b = _nsa_attention(
        za, zg, kc, vc, batch, seq, casts=(w_a, w_b, w_out, wq_x, wk_x, wv_x, w_up, w_down))
    o_ret = _retention(zb, ret_gn_w, batch, seq)
    h1 = _merge(o_nsa, o_ret, zb, h, wa_b, wb_b, wout_b)

    kx, vx = _mem_kv(mem.reshape(batch * MEM_LEN, D_MODEL), mem_norm_w, wk_b, wv_b)
    h2, nm = _cross_attention(h1, x_norm_w, mlp_norm_w, wq_b, kx, vx, wo_b, seq)
    return _mlp(nm, wup_b, wdown_b, h2, out_norm_w)


def kernel(x, mem, attn_norm_w, w_in, cmp_pe_k, cmp_w1_k, cmp_w2_k, cmp_pe_v, cmp_w1_v, cmp_w2_v, w_a, ret_gn_w,
           w_b, w_out, x_norm_w, mem_norm_w, wq_x, wk_x, wv_x, wo_x, mlp_norm_w, w_up, w_down, final_norm_w):
    batch, seq, d = x.shape
    depth = w_in.shape[0]
    assert depth == 1
    h = x.reshape(batch * seq, d)
    out = _layer(h, mem, attn_norm_w[0], w_in[0], cmp_pe_k[0], cmp_w1_k[0], cmp_w2_k[0],
                 cmp_pe_v[0], cmp_w1_v[0], cmp_w2_v[0], w_a[0], ret_gn_w[0], w_b[0], w_out[0],
                 x_norm_w[0], mem_norm_w[0], wq_x[0], wk_x[0], wv_x[0], wo_x[0],
                 mlp_norm_w[0], w_up[0], w_down[0], final_norm_w, batch, seq)
    return out.reshape(batch, seq, d)
```

```python
import functools

import jax
import jax.numpy as jnp
import numpy as np
from jax import lax
from jax.experimental import pallas as pl
from jax.experimental.pallas import tpu as pltpu

F32 = jnp.float32
BF16 = jnp.bfloat16

D_MODEL = 2048
MEM_LEN = 256
NSA_HEADS = 16
NSA_GROUPS = 4
NSA_REP = NSA_HEADS // NSA_GROUPS
NSA_DK = 128
NSA_DV = 128
CMP_LEN = 32
CMP_STRIDE = 16
CMP_HIDDEN = 1024
SEL_LEN = 64
SEL_TOPK = 16
WIN = 512
RET_HEADS = 8
RET_DK = 128
RET_DV = 256
RET_CHUNK = 128
ROPE_BASE = 10000.0
X_HEADS = 4
X_DH = 128
D_FF = 4 * D_MODEL
EPS = 1e-6
NEG = -1e30
LOG2E = 1.4426950408889634
NSA_Q_SCALE = NSA_DK ** -0.5 * LOG2E

LANES = 128
F32_SUBLANES = 8
MXU_ROWS = 256
V7X_VMEM_MIB = 64

W_IN_BLOCK = 1024
W_IN_KV = 2048
W_IN_NSA_GATE = 5120
A_QN = 0
A_KS = 2048
A_VS = 2560
A_KW = 3072
A_VW = 3584
B_QR = 0
B_KR = 1024
B_VR = 2048
B_GR = 4096
B_GA = 6144
B_GB = 8192
B_WIDTH = 10240

NT_DIMS = (((1,), (1,)), ((), ()))
TN_DIMS = (((0,), (0,)), ((), ()))


def _params(sem, vmem_mib):
    assert vmem_mib < V7X_VMEM_MIB
    return pltpu.CompilerParams(dimension_semantics=sem, vmem_limit_bytes=vmem_mib * 1024 * 1024)


def _rms(x, w):
    return x * lax.rsqrt(jnp.mean(x * x, axis=-1, keepdims=True) + EPS) * w


def _dot(a, b):
    return jnp.dot(a, b, preferred_element_type=F32)


def _dot_nt(a, b):
    return lax.dot_general(a, b, NT_DIMS, preferred_element_type=F32)


CAST_ROWS = 16
SLAB_TAIL_ROWS = 64


def _cast_rows(dst_ref, dst0, src_ref, src0, nrows):
    def body(r, carry):
        off = r * CAST_ROWS
        dst_ref[pl.ds(pl.multiple_of(dst0 + off, CAST_ROWS), CAST_ROWS), :] = (
            src_ref[pl.ds(pl.multiple_of(src0 + off, F32_SUBLANES), CAST_ROWS), :].astype(BF16))
        return carry

    lax.fori_loop(0, nrows // CAST_ROWS, body, 0)


def _slab_inputs(off):
    return 0 if off is None else (2 if off else 1)


def _convert_slab(in_refs, out_ref, off):
    main_ref = in_refs[0]
    keep = main_ref.shape[0] - off
    out_ref[:keep, :] = main_ref[off:, :].astype(BF16)
    if off:
        out_ref[keep:, :] = in_refs[1][:off, :].astype(BF16)


def _slab_specs(arr, n_rows, n_steps, step_of, block_of, off):
    rps = n_rows // n_steps
    cols = arr.shape[1]
    assert n_rows % n_steps == 0 and off % CAST_ROWS == 0 and (rps - off) % CAST_ROWS == 0
    ins = [pl.BlockSpec((rps, cols), lambda *g: (block_of(step_of(*g)), 0))]
    if off:
        assert rps % SLAB_TAIL_ROWS == 0 and off <= SLAB_TAIL_ROWS
        per = rps // SLAB_TAIL_ROWS
        ins.append(pl.BlockSpec((SLAB_TAIL_ROWS, cols), lambda *g: ((block_of(step_of(*g)) + 1) * per, 0)))
    out = pl.BlockSpec((rps, cols), lambda *g: (step_of(*g), 0))
    return ins, out, jax.ShapeDtypeStruct((n_rows, cols), BF16), [arr] * len(ins)


def _in_proj_kernel(*refs, shift, regroup, n_cast, norm, side, lead_scale, slab_off=None):
    n_in = 4 if norm else 3
    slab = slab_off is not None
    n_slab_in = _slab_inputs(slab_off)
    n_ref, w_ref, wnext_ref = refs[:3]
    cast_in = refs[n_in:n_in + n_cast]
    o_idx = n_in + n_cast + n_slab_in
    o_ref = refs[o_idx]
    n_out = o_idx + 1 + int(norm) + int(side)
    cast_out = refs[n_out:n_out + n_cast]
    wb_sc, *rest = refs[n_out + n_cast + int(slab):]
    tn = wb_sc.shape[0]
    for src, dst in zip(cast_in, cast_out):
        dst[...] = src[...].astype(dst.dtype)
    if slab:
        _convert_slab(refs[n_in + n_cast:o_idx], refs[n_out + n_cast], slab_off)

    @pl.when(pl.program_id(1) == 0)
    def _():
        _cast_rows(wb_sc, 0, w_ref, shift, tn - shift)
        if shift:
            _cast_rows(wb_sc, tn - shift, wnext_ref, 0, shift)
        if side:
            _cast_rows(rest[-1], 0, wnext_ref, 0, LANES)

    if norm:
        xn = _rms(n_ref[...], refs[3][...]).astype(BF16)
        refs[o_idx + 1][...] = xn
    else:
        xn = n_ref[...]
    if side:
        refs[n_out - 1][...] = _dot_nt(xn, rest[-1][...])
    if regroup:
        r_sc = rest[0]
        tm = xn.shape[0]
        pair = 2 * LANES
        for cc in range(tn // pair):
            res = _dot_nt(xn, wb_sc[cc * pair:(cc + 1) * pair, :])
            for c in (2 * cc, 2 * cc + 1):
                r_sc[c] = res[:, (c % 2) * LANES:(c % 2 + 1) * LANES]
                for j in range(regroup):
                    o_ref[c, :, j * LANES:(j + 1) * LANES] = (
                        r_sc[c, pl.ds(j, tm // regroup, stride=regroup), :].astype(o_ref.dtype))
    else:
        res = _dot_nt(xn, wb_sc[...])
        if lead_scale is not None:
            factor, blocks = lead_scale
            res = res * jnp.where(pl.program_id(0) < blocks, factor, 1.0)
        o_ref[...] = res.astype(o_ref.dtype)


def _in_proj(n, wt, src_block, n_blocks, out_dtype, *, shift=0, regroup=0, casts=(), norm_w=None,
             side_block=None, lead_scale=None, slab=None, tm=1024, tn=1024, name):
    m, k = n.shape
    per = tn // LANES
    m_tiles = m // tm
    norm = norm_w is not None
    side = side_block is not None
    assert shift % CAST_ROWS == 0 and shift <= LANES and not (norm and n_blocks > 1)
    assert not side or (shift == 0 and n_blocks == 1)
    scratch = [pltpu.VMEM((tn, k), BF16)]
    row_tile = pl.BlockSpec((tm, k), lambda j, i: (i, 0))
    once = {"pipeline_mode": pl.Buffered(1)} if n_blocks == 1 else {}
    if regroup:
        assert n_blocks == 1
        out_shape = jax.ShapeDtypeStruct((per, m // regroup, regroup * LANES), out_dtype)
        out_spec = pl.BlockSpec((per, tm // regroup, regroup * LANES), lambda j, i: (0, i, 0))
        scratch.append(pltpu.VMEM((per, tm, LANES), F32))
    else:
        out_shape = jax.ShapeDtypeStruct((m, n_blocks * tn), out_dtype)
        out_spec = pl.BlockSpec((tm, tn), lambda j, i: (i, j))

    cast_steps = 1 << ((n_blocks * m_tiles).bit_length() - 1)
    cast_specs = []
    for a in casts:
        rows = a.shape[0] // cast_steps
        assert a.shape[0] % cast_steps == 0 and rows % CAST_ROWS == 0
        cast_specs.append(pl.BlockSpec(
            (rows, a.shape[1]), lambda j, i: (jnp.minimum(j * m_tiles + i, cast_steps - 1), 0)))

    if side:
        scratch.append(pltpu.VMEM((LANES, k), BF16))
    next_rows = (lambda j, i: (side_block, 0)) if side else (lambda j, i: ((src_block(j) + 1) * per, 0))

    slab_in, slab_out_spec, slab_out_shape, slab_args, slab_off = [], [], [], [], None
    if slab is not None:
        arr, n_rows, block_of, slab_off = slab
        ins, out, shape, slab_args = _slab_specs(arr, n_rows, n_blocks * m_tiles, lambda j, i: j * m_tiles + i,
                                                 block_of, slab_off)
        slab_in, slab_out_spec, slab_out_shape = ins, [out], [shape]

    return pl.pallas_call(
        functools.partial(_in_proj_kernel, shift=shift, regroup=regroup, n_cast=len(casts), norm=norm, side=side,
                          lead_scale=lead_scale, slab_off=slab_off),
        grid=(n_blocks, m_tiles),
        in_specs=[
            row_tile,
            pl.BlockSpec((tn, k), lambda j, i: (src_block(j), 0), **once),
            pl.BlockSpec((LANES, k), next_rows, **once),
            *([pl.BlockSpec((1, k), lambda j, i: (0, 0))] if norm else []),
            *cast_specs,
            *slab_in,
        ],
        out_specs=[out_spec, *([row_tile] if norm else []),
                   *([pl.BlockSpec((tm, LANES), lambda j, i: (i, 0))] if side else []), *cast_specs,
                   *slab_out_spec],
        out_shape=[out_shape, *([jax.ShapeDtypeStruct((m, k), BF16)] if norm else []),
                   *([jax.ShapeDtypeStruct((m, LANES), F32)] if side else []),
                   *[jax.ShapeDtypeStruct(a.shape, BF16) for a in casts], *slab_out_shape],
        scratch_shapes=scratch,
        compiler_params=_params(("arbitrary", "arbitrary"), 60 if (norm or slab is not None) else 56),
        name=name,
    )(n, wt, wt, *([norm_w.reshape(1, k)] if norm else []), *casts, *slab_args)


def _proj_kernel(n_ref, w_ref, *refs, n_cast, slab_off, lead_scale):
    n_slab_in = _slab_inputs(slab_off)
    o_ref = refs[n_cast + n_slab_in]
    for src, dst in zip(refs[:n_cast], refs[n_cast + n_slab_in + 1:]):
        dst[...] = src[...].astype(dst.dtype)
    if slab_off is not None:
        _convert_slab(refs[n_cast:n_cast + n_slab_in], refs[-1], slab_off)
    res = _dot_nt(n_ref[...], w_ref[...])
    if lead_scale is not None:
        factor, blocks = lead_scale
        res = res * jnp.where(pl.program_id(0) < blocks, factor, 1.0)
    o_ref[...] = res.astype(o_ref.dtype)


def _proj(n, wt_bf16, out_dtype, *, casts=(), slab=None, lead_scale=None, tm=1024, tn=2048, name):
    m, k = n.shape
    n_cols = wt_bf16.shape[0]
    n_blocks, m_tiles = n_cols // tn, m // tm
    cast_steps = 1 << ((n_blocks * m_tiles).bit_length() - 1)
    cast_specs = []
    for a in casts:
        rows = a.shape[0] // cast_steps
        assert a.shape[0] % cast_steps == 0 and rows % CAST_ROWS == 0
        cast_specs.append(pl.BlockSpec(
            (rows, a.shape[1]), lambda j, i: (jnp.minimum(j * m_tiles + i, cast_steps - 1), 0)))
    slab_in, slab_out_spec, slab_out_shape, slab_args, slab_off = [], [], [], [], None
    if slab is not None:
        arr, n_rows, block_of, slab_off = slab
        ins, out, shape, slab_args = _slab_specs(arr, n_rows, n_blocks * m_tiles, lambda j, i: j * m_tiles + i,
                                                 block_of, slab_off)
        slab_in, slab_out_spec, slab_out_shape = ins, [out], [shape]
    return pl.pallas_call(
        functools.partial(_proj_kernel, n_cast=len(casts), slab_off=slab_off, lead_scale=lead_scale),
        grid=(n_blocks, m_tiles),
        in_specs=[pl.BlockSpec((tm, k), lambda j, i: (i, 0)), pl.BlockSpec((tn, k), lambda j, i: (j, 0)),
                  *cast_specs, *slab_in],
        out_specs=[pl.BlockSpec((tm, tn), lambda j, i: (i, j)), *cast_specs, *slab_out_spec],
        out_shape=[jax.ShapeDtypeStruct((m, n_cols), out_dtype),
                   *[jax.ShapeDtypeStruct(a.shape, BF16) for a in casts], *slab_out_shape],
        compiler_params=_params(("arbitrary", "arbitrary"), 58),
        name=name,
    )(n, wt_bf16, *casts, *slab_args)


def _compress_kernel(k2_ref, pe_ref, w1_ref, w2_ref, o_ref):
    rows, half = k2_ref.shape
    k2 = k2_ref[...].astype(F32)
    a_lo = (k2 + pe_ref[:, :half]).astype(BF16)
    a_hi = (k2 + pe_ref[:, half:]).astype(BF16)
    lo = _dot(a_lo, w1_ref[:half, :])
    hi = _dot(a_hi, w1_ref[half:, :])
    h = lo + pltpu.roll(hi, rows - 1, axis=0)
    act = h * jax.nn.sigmoid(h)
    o_ref[...] = _dot(act.astype(BF16), w2_ref[...]).astype(o_ref.dtype)


def _compress(k2, first_row, m, pe, w1, w2, rows=512):
    half = k2.shape[1]
    hid = w1.shape[1]
    dout = w2.shape[1]
    first = first_row // rows
    return pl.pallas_call(
        _compress_kernel,
        grid=(m // rows,),
        in_specs=[
            pl.BlockSpec((rows, half), lambda i: (first + i, 0)),
            pl.BlockSpec((1, 2 * half), lambda i: (0, 0)),
            pl.BlockSpec((2 * half, hid), lambda i: (0, 0)),
            pl.BlockSpec((hid, dout), lambda i: (0, 0)),
        ],
        out_specs=pl.BlockSpec((rows, dout), lambda i: (i, 0)),
        out_shape=jax.ShapeDtypeStruct((m, dout), BF16),
        compiler_params=_params(("parallel",), 40),
        name="nsa_compress",
    )(k2, pe, w1, w2)


def _nsa_kernel(*refs, seq, tq, kblk, gp, hm, n_cast):
    q_ref = refs[0]
    kc_refs = refs[1:1 + gp]
    vc_refs = refs[1 + gp:1 + 2 * gp]
    ks_ref, vs_ref, kw_ref, vw_ref, g_ref, e_ref = refs[1 + 2 * gp:7 + 2 * gp]
    cast_in = refs[7 + 2 * gp:7 + 2 * gp + n_cast]
    o_ref = refs[7 + 2 * gp + n_cast]
    cast_out = refs[8 + 2 * gp + n_cast:8 + 2 * gp + 2 * n_cast]
    s_sc, mx_sc, acc_sc, po_sc = refs[8 + 2 * gp + 2 * n_cast:]
    i = pl.program_id(2)
    t0 = i * tq
    rep = NSA_REP
    n_cmp = (seq - CMP_LEN) // CMP_STRIDE + 1
    n_sel = seq // SEL_LEN
    topk = min(SEL_TOPK, n_sel)
    groups = range(gp)

    q = q_ref[...]
    units = range(rep // hm)
    rows = lambda gg, r: slice((gg * rep + r) * tq, (gg * rep + r + 1) * tq)
    urows = lambda gg, u: slice((gg * rep + u * hm) * tq, (gg * rep + (u + 1) * hm) * tq)
    part = lambda x, h: x[h * tq:(h + 1) * tq]
    head_q = lambda gg, r: q[:, (gg * rep + r) * NSA_DK:(gg * rep + r + 1) * NSA_DK]
    unit_q = {(gg, u): jnp.concatenate([head_q(gg, u * hm + h) for h in range(hm)], axis=0)
              for gg in groups for u in units}
    group_cols = lambda ref, gg: ref.at[:, gg * LANES:(gg + 1) * LANES]

    def masked(s, mask1):
        return jnp.concatenate([jnp.where(mask1, part(s, h), NEG) for h in range(hm)], axis=0)
    tcol = t0 + lax.broadcasted_iota(jnp.int32, (tq, 1), 0)
    per_group = rep * 3
    g_sig = jax.nn.sigmoid(g_ref[...])
    first_group = pl.program_id(1) * gp
    gs = [pltpu.roll(g_sig, lax.rem(LANES - (first_group + gg) * per_group, LANES), axis=1) for gg in groups]
    gate = lambda gg, r, branch: gs[gg][:, 3 * r + branch:3 * r + branch + 1]

    def exp_rows(sm):
        return jnp.exp2(sm - jnp.max(sm, axis=-1, keepdims=True))

    def with_ones(v):
        return jnp.concatenate([v, jnp.ones(v.shape, v.dtype)], axis=1)

    def normalised(ev):
        return ev[:, :NSA_DV] / ev[:, NSA_DV:]

    c_idx = lax.broadcasted_iota(jnp.int32, (tq, LANES), 1)
    mask_c = ((c_idx * CMP_STRIDE + (CMP_LEN - 1)) <= tcol) & (c_idx < n_cmp)
    mask_cf = jnp.where(mask_c, 1.0, 0.0)
    o_cmp, psum = {}, []
    for gg in groups:
        kc = kc_refs[gg][...]
        vc = vc_refs[gg][...]
        tot = None
        for u in units:
            e = exp_rows(masked(_dot_nt(unit_q[gg, u], kc), mask_c))
            p = e / jnp.sum(e, axis=-1, keepdims=True)
            p = jnp.concatenate([part(p, h) * mask_cf for h in range(hm)], axis=0)
            o_cmp[gg, u] = _dot(p.astype(BF16), vc)
            for h in range(hm):
                tot = part(p, h) if tot is None else tot + part(p, h)
        psum.append(tot)

    wlen = WIN + tq
    start = pl.multiple_of(jnp.maximum(i - WIN // tq, 0) * tq, tq)
    dlt = tcol - (start + lax.broadcasted_iota(jnp.int32, (tq, wlen), 1))
    mask_w = (dlt >= 0) & (dlt < WIN)
    for gg in groups:
        kw = group_cols(kw_ref, gg)[pl.ds(start, wlen), :]
        vw1 = with_ones(group_cols(vw_ref, gg)[pl.ds(start, wlen), :])
        for u in units:
            e = exp_rows(masked(_dot_nt(unit_q[gg, u], kw), mask_w))
            o_win = normalised(_dot(e.astype(BF16), vw1))
            for h in range(hm):
                r = u * hm + h
                po_sc[rows(gg, r), :] = (gate(gg, r, 0) * part(o_cmp[gg, u], h)
                                         + gate(gg, r, 2) * part(o_win, h))

    jo = lax.broadcasted_iota(jnp.int32, (n_sel, LANES), 0)
    co = lax.broadcasted_iota(jnp.int32, (n_sel, LANES), 1)
    ov_t = jnp.where((co * CMP_STRIDE < jo * SEL_LEN + SEL_LEN) & (co * CMP_STRIDE + CMP_LEN > jo * SEL_LEN)
                     & (co < n_cmp), 1.0, 0.0).astype(BF16)
    j_idx = lax.broadcasted_iota(jnp.int32, (n_sel, tq), 0)
    cur = lax.shift_right_logical(t0 + lax.broadcasted_iota(jnp.int32, (n_sel, tq), 1), int(np.log2(SEL_LEN)))
    forced = (j_idx == 0) | (j_idx == cur) | (j_idx == cur - 1)
    future = j_idx > cur
    q_bias = {}
    for gg in groups:
        p_hi = psum[gg].astype(BF16)
        p_lo = (psum[gg] - p_hi.astype(F32)).astype(BF16)
        imp = _dot_nt(ov_t, p_hi) + _dot_nt(ov_t, p_lo)
        impm = jnp.where(forced, jnp.inf, jnp.where(future, -jnp.inf, imp))
        rank = jnp.zeros((n_sel, tq), F32)
        for ii in range(n_sel):
            row = impm[ii:ii + 1, :]
            beats = (row > impm) | ((row == impm) & (j_idx > ii))
            rank = rank + jnp.where(beats, 1.0, 0.0)
        bias_t = jnp.where(rank < topk, 0.0, NEG)
        bias = jnp.concatenate([bias_t, jnp.zeros((LANES - n_sel, tq), F32)], axis=0).T.astype(BF16)
        for u in units:
            q_bias[gg, u] = jnp.concatenate([unit_q[gg, u], jnp.concatenate([bias] * hm, axis=0)], axis=1)

    n_chunks = lax.div(t0 + tq + (kblk - 1), kblk)

    def lane_fold_max(x):
        out = x[:, :LANES]
        for c in range(1, kblk // LANES):
            out = jnp.maximum(out, x[:, c * LANES:(c + 1) * LANES])
        return out

    def chunk_start(kb):
        return kb * kblk if isinstance(kb, int) else pl.multiple_of(kb * kblk, kblk)

    def score_chunk(kb):
        k0 = chunk_start(kb)
        causal = (k0 + lax.broadcasted_iota(jnp.int32, (tq, kblk), 1)) <= tcol
        folds = []
        for gg in groups:
            k = jnp.concatenate([group_cols(ks_ref, gg)[pl.ds(k0, kblk), :], e_ref[kb]], axis=1)
            for u in units:
                sm_ = masked(_dot_nt(q_bias[gg, u], k), causal)
                s_sc[kb, urows(gg, u), :] = sm_
                folds.append(lane_fold_max(sm_))
        return jnp.concatenate(folds, axis=0)

    def score_pass(kb, carry):
        mx_sc[...] = jnp.maximum(mx_sc[...], score_chunk(kb))
        return carry

    mx_sc[...] = score_chunk(0)
    lax.fori_loop(1, n_chunks, score_pass, 0)
    m_sel = jnp.max(mx_sc[...], axis=-1, keepdims=True)
    for src, dst in zip(cast_in, cast_out):
        dst[...] = src[...].astype(dst.dtype)

    def value_chunk(kb):
        k0 = chunk_start(kb)
        pvs = []
        for gg in groups:
            v1 = with_ones(group_cols(vs_ref, gg)[pl.ds(k0, kblk), :])
            for u in units:
                pk = jnp.exp2(s_sc[kb, urows(gg, u), :] - m_sel[urows(gg, u)])
                pvs.append(_dot(pk.astype(BF16), v1))
        return jnp.concatenate(pvs, axis=0)

    def value_pass(kb, carry):
        acc_sc[...] += value_chunk(kb)
        return carry

    acc_sc[...] = value_chunk(0)
    lax.fori_loop(1, n_chunks, value_pass, 0)

    for gg in groups:
        for r in range(rep):
            o = po_sc[rows(gg, r), :] + gate(gg, r, 1) * normalised(acc_sc[rows(gg, r), :])
            col = (gg * rep + r) * NSA_DV
            o_ref[:, col:col + NSA_DV] = o.astype(o_ref.dtype)


def _nsa_attention(za, zg, kc, vc, batch, seq, casts=(), tq=256, kblk=512, gp=2):
    assert seq % kblk == 0 and seq >= WIN + tq and WIN % tq == 0 and NSA_GROUPS % gp == 0
    nq = seq // tq
    gw = gp * NSA_REP * NSA_DK
    streams = gp * NSA_REP * tq
    hm = max(1, MXU_ROWS // tq)
    assert NSA_REP % hm == 0
    kern = functools.partial(_nsa_kernel, seq=seq, tq=tq, kblk=kblk, gp=gp, hm=hm, n_cast=len(casts))
    n_g = NSA_GROUPS // gp
    steps = batch * n_g * nq
    cast_specs = []
    for a in casts:
        assert a.shape[0] % steps == 0 and (a.shape[0] // steps) % CAST_ROWS == 0
        cast_specs.append(pl.BlockSpec((a.shape[0] // steps, a.shape[1]),
                                       lambda b, g, i: ((b * n_g + g) * nq + i, 0)))
    key = np.arange(seq).reshape(seq // kblk, kblk, 1)
    expand = jnp.asarray(key // SEL_LEN == np.arange(LANES).reshape(1, 1, LANES), BF16)

    def kv_spec(off):
        return pl.BlockSpec((seq, gp * LANES), lambda b, g, i: (b, off // (gp * LANES) + g))

    def cmp_spec(gg):
        return pl.BlockSpec((LANES, NSA_DK), lambda b, g, i: ((g * gp + gg) * batch + b, 0))

    return pl.pallas_call(
        kern,
        grid=(batch, NSA_GROUPS // gp, nq),
        in_specs=[
            pl.BlockSpec((tq, gw), lambda b, g, i: (b * nq + i, A_QN // gw + g)),
            *[cmp_spec(gg) for gg in range(gp)],
            *[cmp_spec(gg) for gg in range(gp)],
            kv_spec(A_KS), kv_spec(A_VS), kv_spec(A_KW), kv_spec(A_VW),
            pl.BlockSpec((tq, LANES), lambda b, g, i: (b * nq + i, 0)),
            pl.BlockSpec(expand.shape, lambda b, g, i: (0, 0, 0)),
            *cast_specs,
        ],
        out_specs=[pl.BlockSpec((tq, gw), lambda b, g, i: (b * nq + i, g)), *cast_specs],
        out_shape=[jax.ShapeDtypeStruct((batch * seq, NSA_HEADS * NSA_DV), BF16),
                   *[jax.ShapeDtypeStruct(a.shape, BF16) for a in casts]],
        scratch_shapes=[
            pltpu.VMEM((seq // kblk, streams, kblk), F32),
            pltpu.VMEM((streams, LANES), F32),
            pltpu.VMEM((streams, 2 * NSA_DV), F32),
            pltpu.VMEM((streams, NSA_DV), F32),
        ],
        compiler_params=_params(("parallel", "parallel", "arbitrary"), 56),
        name="nsa_attention",
    )(za, *([kc] * gp), *([vc] * gp), za, za, za, za, zg, expand, *casts)


def _retention_kernel(q_ref, k_ref, v_ref, g_ref, cos_ref, sin_ref, dec_ref, wq_ref, wk_ref,
                      gc_ref, gn_ref, o_ref, st_ref, o_sc):
    @pl.when(pl.program_id(1) == 0)
    def _():
        st_ref[...] = jnp.zeros(st_ref.shape, F32)

    c = RET_CHUNK
    even = lax.broadcasted_iota(jnp.int32, (c, RET_DK), 1) % 2 == 0
    for h in range(RET_HEADS):
        st = st_ref[h]
        wq = wq_ref[h]
        wq2 = jnp.concatenate([wq, wq], axis=1)
        cols = slice(h * RET_DV, (h + 1) * RET_DV)
        for sub in range(q_ref.shape[0] // c):
            rows = slice(sub * c, (sub + 1) * c)
            cos = cos_ref[rows, :]
            sin = sin_ref[rows, :]

            def rotate(x):
                partner = jnp.where(even, pltpu.roll(x, RET_DK - 1, axis=1), pltpu.roll(x, 1, axis=1))
                return x * cos + partner * sin

            qf = rotate(q_ref[rows, h * RET_DK:(h + 1) * RET_DK])
            kf = rotate(k_ref[rows, h * RET_DK:(h + 1) * RET_DK])
            qb = qf.astype(BF16)
            v = v_ref[rows, cols].astype(BF16)
            s = _dot_nt(qb, kf.astype(BF16)) * dec_ref[h]
            o = _dot(s.astype(BF16), v) + _dot(qb, st.astype(BF16)) * wq2
            st = st * gc_ref[h] + lax.dot_general((kf * wk_ref[h]).astype(BF16), v, TN_DIMS,
                                                  preferred_element_type=F32)
            o_sc[rows, cols] = o
        st_ref[h] = st

    for h in range(RET_HEADS):
        cols = slice(h * RET_DV, (h + 1) * RET_DV)
        o = o_sc[:, cols]
        mu = jnp.mean(o, axis=-1, keepdims=True)
        d = o - mu
        var = jnp.mean(d * d, axis=-1, keepdims=True)
        on = d * lax.rsqrt(var + EPS) * gn_ref[:, cols]
        gr = g_ref[:, cols]
        o_ref[:, cols] = (gr * jax.nn.sigmoid(gr) * on).astype(o_ref.dtype)


def _retention(zb, gn_w, batch, seq, chunks_per_step=4):
    c = RET_CHUNK
    rows = chunks_per_step * c
    nc = seq // rows
    hq = RET_HEADS * RET_DK
    hv = RET_HEADS * RET_DV
    f32 = np.float32
    inv = f32(ROPE_BASE) ** (-np.arange(0, RET_DK, 2, dtype=f32) / f32(RET_DK))
    ang = np.arange(seq, dtype=f32)[:, None] * inv[None, :]
    pairs = lambda even, odd: np.stack([even, odd], axis=-1).reshape(seq, RET_DK)
    cos = pairs(np.cos(ang), np.cos(ang))
    sin = pairs(-np.sin(ang), np.sin(ang))
    log_g = np.log1p(-np.exp2(f32(-5.0) - np.arange(RET_HEADS, dtype=f32)))
    idx = np.arange(c, dtype=f32)
    rel = idx[:, None] - idx[None, :]
    k_scale = f32(RET_DK ** -0.5)
    decay = (np.where(rel >= 0, np.exp(log_g[:, None, None] * np.maximum(rel, f32(0.0))), f32(0.0))
             * k_scale).astype(f32)
    lanes = lambda a: np.ascontiguousarray(np.broadcast_to(a[:, :, None], (RET_HEADS, c, RET_DK)), dtype=f32)
    w_k = lanes(np.exp(log_g[:, None] * (f32(c - 1) - idx)[None, :]) * k_scale)
    w_q = lanes(np.exp(log_g[:, None] * (idx + f32(1.0))[None, :]))
    g_chunk = np.broadcast_to(np.exp(log_g * f32(c))[:, None, None], (RET_HEADS, 1, RET_DV)).astype(f32)

    row = lambda b, n: b * nc + n
    return pl.pallas_call(
        _retention_kernel,
        grid=(batch, nc),
        in_specs=[
            pl.BlockSpec((rows, hq), lambda b, n: (row(b, n), B_QR // hq)),
            pl.BlockSpec((rows, hq), lambda b, n: (row(b, n), B_KR // hq)),
            pl.BlockSpec((rows, hv), lambda b, n: (row(b, n), B_VR // hv)),
            pl.BlockSpec((rows, hv), lambda b, n: (row(b, n), B_GR // hv)),
            pl.BlockSpec((rows, RET_DK), lambda b, n: (n, 0)),
            pl.BlockSpec((rows, RET_DK), lambda b, n: (n, 0)),
            pl.BlockSpec((RET_HEADS, c, c), lambda b, n: (0, 0, 0)),
            pl.BlockSpec((RET_HEADS, c, RET_DK), lambda b, n: (0, 0, 0)),
            pl.BlockSpec((RET_HEADS, c, RET_DK), lambda b, n: (0, 0, 0)),
            pl.BlockSpec((RET_HEADS, 1, RET_DV), lambda b, n: (0, 0, 0)),
            pl.BlockSpec((1, hv), lambda b, n: (0, 0)),
        ],
        out_specs=pl.BlockSpec((rows, hv), lambda b, n: (row(b, n), 0)),
        out_shape=jax.ShapeDtypeStruct((batch * seq, hv), BF16),
        scratch_shapes=[pltpu.VMEM((RET_HEADS, RET_DK, RET_DV), F32),
                        pltpu.VMEM((rows, hv), F32)],
        compiler_params=_params(("parallel", "arbitrary"), 40),
        name="retention",
    )(zb, zb, zb, zb, cos, sin, decay, w_q, w_k, g_chunk, gn_w.reshape(1, hv))


def _merge_kernel(on_ref, or_ref, ga_ref, gb_ref, x_ref, wa_ref, wb_ref, wo_ref, h_ref):
    a = _dot(on_ref[...], wa_ref[...])
    b = _dot(or_ref[...], wb_ref[...])
    merged = jax.nn.sigmoid(ga_ref[...]) * a + jax.nn.sigmoid(gb_ref[...]) * b
    h_ref[...] = x_ref[...] + _dot(merged.astype(BF16), wo_ref[...])


def _merge(o_nsa, o_ret, zb, x, w_a, w_b, w_out, tm=256):
    m, d = x.shape
    resident = lambda shape: pl.BlockSpec(shape, lambda i: (0, 0), pipeline_mode=pl.Buffered(1))
    return pl.pallas_call(
        _merge_kernel,
        grid=(m // tm,),
        in_specs=[
            pl.BlockSpec((tm, d), lambda i: (i, 0)),
            pl.BlockSpec((tm, d), lambda i: (i, 0)),
            pl.BlockSpec((tm, d), lambda i: (i, B_GA // d)),
            pl.BlockSpec((tm, d), lambda i: (i, B_GB // d)),
            pl.BlockSpec((tm, d), lambda i: (i, 0)),
            resident(w_a.shape), resident(w_b.shape), resident(w_out.shape),
        ],
        out_specs=pl.BlockSpec((tm, d), lambda i: (i, 0)),
        out_shape=jax.ShapeDtypeStruct((m, d), F32),
        compiler_params=_params(("parallel",), 56),
        name="merge_out_proj",
    )(o_nsa, o_ret, zb, zb, x, w_a, w_b, w_out)


def _mem_kv_kernel(m_ref, nw_ref, wk_ref, wv_ref, k_ref, v_ref):
    mn = _rms(m_ref[...], nw_ref[...]).astype(BF16)
    k_ref[...] = _dot(mn, wk_ref[...]).astype(k_ref.dtype)
    v_ref[...] = _dot(mn, wv_ref[...]).astype(v_ref.dtype)


def _mem_kv(mem2, nw, wk, wv, tm=256):
    m, d = mem2.shape
    n = wk.shape[1]
    out = jax.ShapeDtypeStruct((m, n), BF16)
    return pl.pallas_call(
        _mem_kv_kernel,
        grid=(m // tm,),
        in_specs=[
            pl.BlockSpec((tm, d), lambda i: (i, 0)),
            pl.BlockSpec((1, d), lambda i: (0, 0)),
            pl.BlockSpec((d, n), lambda i: (0, 0)),
            pl.BlockSpec((d, n), lambda i: (0, 0)),
        ],
        out_specs=[pl.BlockSpec((tm, n), lambda i: (i, 0))] * 2,
        out_shape=[out, out],
        compiler_params=_params(("parallel",), 40),
        name="mem_kv_proj",
    )(mem2, nw.reshape(1, d), wk, wv)


def _cross_kernel(h_ref, xw_ref, mw_ref, wq_ref, kx_ref, vx_ref, wo_ref, h2_ref, nm_ref):
    h = h_ref[...]
    nx = _rms(h, xw_ref[...]).astype(BF16)
    qx = (_dot(nx, wq_ref[...]) * (X_DH ** -0.5 * LOG2E)).astype(BF16)
    outs = []
    for hh in range(X_HEADS):
        cols = slice(hh * X_DH, (hh + 1) * X_DH)
        s = _dot_nt(qx[:, cols], kx_ref[:, cols])
        e = jnp.exp2(s - jnp.max(s, axis=-1, keepdims=True))
        vx = vx_ref[:, cols]
        ev = _dot(e.astype(BF16), jnp.concatenate([vx, jnp.ones(vx.shape, vx.dtype)], axis=1))
        outs.append(ev[:, :X_DH] / ev[:, X_DH:])
    ox = jnp.concatenate(outs, axis=-1).astype(BF16)
    h2 = h + _dot(ox, wo_ref[...])
    h2_ref[...] = h2
    nm_ref[...] = _rms(h2, mw_ref[...]).astype(nm_ref.dtype)


def _cross_attention(h1, x_norm_w, mlp_norm_w, wq, kx, vx, wo, seq, tm=512):
    m, d = h1.shape
    n = wq.shape[1]
    per_batch = seq // tm
    vec = lambda: pl.BlockSpec((1, d), lambda i: (0, 0))
    return pl.pallas_call(
        _cross_kernel,
        grid=(m // tm,),
        in_specs=[
            pl.BlockSpec((tm, d), lambda i: (i, 0)),
            vec(), vec(),
            pl.BlockSpec((d, n), lambda i: (0, 0)),
            pl.BlockSpec((MEM_LEN, n), lambda i: (i // per_batch, 0)),
            pl.BlockSpec((MEM_LEN, n), lambda i: (i // per_batch, 0)),
            pl.BlockSpec((n, d), lambda i: (0, 0)),
        ],
        out_specs=[pl.BlockSpec((tm, d), lambda i: (i, 0))] * 2,
        out_shape=[jax.ShapeDtypeStruct((m, d), F32), jax.ShapeDtypeStruct((m, d), BF16)],
        compiler_params=_params(("parallel",), 40),
        name="cross_attention",
    )(h1, x_norm_w.reshape(1, d), mlp_norm_w.reshape(1, d), wq, kx, vx, wo)


def _mlp_kernel(nm_ref, wu_ref, wd_ref, h_ref, fw_ref, o_ref):
    j = pl.program_id(1)

    @pl.when(j == 0)
    def _():
        o_ref[...] = jnp.zeros(o_ref.shape, F32)

    u = jnp.maximum(_dot(nm_ref[...], wu_ref[...]), 0.0)
    o_ref[...] += _dot((u * u).astype(BF16), wd_ref[...])

    @pl.when(j == pl.num_programs(1) - 1)
    def _():
        o_ref[...] = _rms(h_ref[...] + o_ref[...], fw_ref[...])


def _mlp(nm, w_up, w_down, h2, final_w, tm=512, tf=2048):
    m, d = nm.shape
    f = w_up.shape[1]
    return pl.pallas_call(
        _mlp_kernel,
        grid=(m // tm, f // tf),
        in_specs=[
            pl.BlockSpec((tm, d), lambda i, j: (i, 0)),
            pl.BlockSpec((d, tf), lambda i, j: (0, j)),
            pl.BlockSpec((tf, d), lambda i, j: (j, 0)),
            pl.BlockSpec((tm, d), lambda i, j: (i, 0)),
            pl.BlockSpec((1, d), lambda i, j: (0, 0)),
        ],
        out_specs=pl.BlockSpec((tm, d), lambda i, j: (i, 0)),
        out_shape=jax.ShapeDtypeStruct((m, d), F32),
        compiler_params=_params(("parallel", "arbitrary"), 60),
        name="mlp_final_norm",
    )(nm, w_up, w_down, h2, final_w.reshape(1, d))


def _layer(h, mem, attn_norm_w, w_in, cmp_pe_k, cmp_w1_k, cmp_w2_k, cmp_pe_v, cmp_w1_v, cmp_w2_v,
           w_a, ret_gn_w, w_b, w_out, x_norm_w, mem_norm_w, wq_x, wk_x, wv_x, wo_x,
           mlp_norm_w, w_up, w_down, out_norm_w, batch, seq):
    blk = W_IN_BLOCK
    kv_block = W_IN_KV // blk
    gate_shift = NSA_HEADS * 3

    wt = w_in.T
    skip_kv = lambda j: j + (j >= kv_block).astype(jnp.int32)
    kv, n, zg, wo_b, c2k_b, c2v_b = _in_proj(
        h, wt, lambda j: kv_block, 1, BF16, regroup=CMP_STRIDE, norm_w=attn_norm_w,
        side_block=W_IN_NSA_GATE // LANES, casts=(wo_x, cmp_w2_k, cmp_w2_v), name="in_proj_kv")
    a_steps = 4 * (h.shape[0] // 1024)
    b_first = W_IN_NSA_GATE // (B_WIDTH // a_steps)
    za, c1k_b, c1v_b, wt_b = _in_proj(
        n, wt, skip_kv, 4, BF16, casts=(cmp_w1_k, cmp_w1_v),
        lead_scale=(NSA_Q_SCALE, NSA_HEADS * NSA_DK // blk),
        slab=(wt, B_WIDTH, lambda s: s + b_first, gate_shift), name="in_proj_a")
    zb, = _proj(n, wt_b, F32, name="in_proj_b")

    assert seq // CMP_STRIDE == LANES
    rows_kv = NSA_GROUPS * batch * LANES
    k2 = kv.reshape(2 * rows_kv, CMP_STRIDE * NSA_DK)
    kc = _compress(k2, 0, rows_kv, cmp_pe_k.reshape(1, -1), c1k_b, c2k_b)
    vc = _compress(k2, rows_kv, rows_kv, cmp_pe_v.reshape(1, -1), c1v_b, c2v_b)

    o_nsa, wa_b, wb_b, wout_b, wq_b, wk_b, wv_b, wup_b, wdown_b = _nsa_attention(
        za, zg, kc, vc, batch, seq, casts=(w_a, w_b, w_out, wq_x, wk_x, wv_x, w_up, w_down))
    o_ret = _retention(zb, ret_gn_w, batch, seq)
    h1 = _merge(o_nsa, o_ret, zb, h, wa_b, wb_b, wout_b)

    kx, vx = _mem_kv(mem.reshape(batch * MEM_LEN, D_MODEL), mem_norm_w, wk_b, wv_b)
    h2, nm = _cross_attention(h1, x_norm_w, mlp_norm_w, wq_b, kx, vx, wo_b, seq)
    return _mlp(nm, wup_b, wdown_b, h2, out_norm_w)


def kernel(x, mem, attn_norm_w, w_in, cmp_pe_k, cmp_w1_k, cmp_w2_k, cmp_pe_v, cmp_w1_v, cmp_w2_v, w_a, ret_gn_w,
           w_b, w_out, x_norm_w, mem_norm_w, wq_x, wk_x, wv_x, wo_x, mlp_norm_w, w_up, w_down, final_norm_w):
    batch, seq, d = x.shape
    depth = w_in.shape[0]
    assert depth == 1
    h = x.reshape(batch * seq, d)
    out = _layer(h, mem, attn_norm_w[0], w_in[0], cmp_pe_k[0], cmp_w1_k[0], cmp_w2_k[0],
                 cmp_pe_v[0], cmp_w1_v[0], cmp_w2_v[0], w_a[0], ret_gn_w[0], w_b[0], w_out[0],
                 x_norm_w[0], mem_norm_w[0], wq_x[0], wk_x[0], wv_x[0], wo_x[0],
                 mlp_norm_w[0], w_up[0], w_down[0], final_norm_w, batch, seq)
    return out.reshape(batch, seq, d)
```
